```python
import math
import jax
import jax.numpy as jnp
from jax import lax
import numpy as np

D_MODEL = 2048
BATCH = 4
SEQ = 2048
DEPTH = 2
DEC_BATCH = 8
DEC_SEQ = 8
PAST_LEN = 16384
PAGE_SIZE = 128

CONV_W = D_MODEL // 4
NSA_W = D_MODEL // 2
MLSTM_W = D_MODEL // 4
MIX_W = CONV_W + NSA_W + MLSTM_W
CONV_K = 3
HEAD_DIM = 64
NSA_HEADS = NSA_W // HEAD_DIM
NSA_KV_HEADS = 4
NSA_GROUP = NSA_HEADS // NSA_KV_HEADS
CMP_STRIDE = 16
CMP_BLOCK = 2 * CMP_STRIDE
SEL_BLOCK = 64
SEL_TOPN = 16
WINDOW = 512
SEL_QBLOCK = 64
WIN_QBLOCK = 128
MLSTM_HEADS = 4
MLSTM_DH = MLSTM_W // MLSTM_HEADS
MLSTM_CHUNK = 64
PEER_HEADS = 8
PEER_NKEYS = 128
PEER_EXPERTS = PEER_NKEYS * PEER_NKEYS
PEER_DKEY = 256
PEER_TOPK = 16
PEER_TBLOCK = 128
RMS_EPS = 1e-6
MASK_BIG = 1e9
SPLIT_SIZES = (CONV_W, CONV_W, CONV_W, NSA_W, 6 * NSA_KV_HEADS * HEAD_DIM, 3 * NSA_HEADS,
               MLSTM_W, MLSTM_W, MLSTM_W, MLSTM_W, MLSTM_HEADS, MLSTM_HEADS)
IN_COLS = sum(SPLIT_SIZES)

kernel_name = 'hymba_conv_nsa_mlstm_peer_step'


def rmsnorm(x, g):
    x32 = x.astype(jnp.float32)
    r = x32 * lax.rsqrt(jnp.mean(x32 * x32, -1, keepdims=True) + RMS_EPS)
    return (r * g.astype(jnp.float32)).astype(x.dtype)


def masked_softmax(s, mask):
    s = jnp.where(mask, s.astype(jnp.float32), -jnp.inf)
    m = jnp.max(s, -1, keepdims=True)
    m = jnp.where(jnp.isfinite(m), m, 0.0)
    e = jnp.where(mask, jnp.exp(s - m), 0.0)
    return e / jnp.maximum(jnp.sum(e, -1, keepdims=True), 1e-30)


def short_conv(u, buf, w):
    ext = jnp.concatenate([buf.astype(u.dtype), u], axis=1)
    y = lax.conv_general_dilated(ext, w[:, None, :].astype(u.dtype), (1,), 'VALID',
                                 dimension_numbers=('NWC', 'WIO', 'NWC'),
                                 feature_group_count=u.shape[-1])
    return y, ext[:, -(CONV_K - 1):]


def mlstm_chunkwise(q, k, v, i_pre, logf, C0, n0, m0):
    B, S, H, DH = q.shape
    L = MLSTM_CHUNK if S % MLSTM_CHUNK == 0 else S
    nc = S // L

    def chunks(a):
        a = a.astype(jnp.float32).reshape((B, nc, L) + a.shape[2:])
        return jnp.swapaxes(jnp.moveaxis(a, 1, 0), 2, 3)

    qc, kc, vc = chunks(q), chunks(k) * (DH ** -0.5), chunks(v)
    ic, fc = chunks(i_pre), chunks(logf)
    tril = jnp.tril(jnp.ones((L, L), bool))

    def step(carry, inp):
        C, n, m = carry
        qq, kk, vv, ii, ff = inp
        b = jnp.cumsum(ff, -1)
        dmat = jnp.where(tril, b[..., :, None] - b[..., None, :] + ii[..., None, :], -jnp.inf)
        inter = b + m[..., None]
        mt = jnp.maximum(inter, jnp.max(dmat, -1))
        w = jnp.exp(dmat - mt[..., None])
        a = jnp.exp(inter - mt)
        wqk = w * jnp.einsum('bhtd,bhsd->bhts', qq, kk)
        num = a[..., None] * jnp.einsum('bhtd,bhde->bhte', qq, C) + jnp.einsum('bhts,bhse->bhte', wqk, vv)
        den = a * jnp.einsum('bhtd,bhd->bht', qq, n) + jnp.sum(wqk, -1)
        h = num / jnp.maximum(jnp.abs(den), jnp.exp(-mt))[..., None]
        m_new = mt[..., -1]
        wl = jnp.exp(b[..., -1:] - b + ii - m_new[..., None])
        decay = jnp.exp(b[..., -1] + m - m_new)
        C_new = decay[..., None, None] * C + jnp.einsum('bhs,bhsd,bhse->bhde', wl, kk, vv)
        n_new = decay[..., None] * n + jnp.einsum('bhs,bhsd->bhd', wl, kk)
        return (C_new, n_new, m_new), h

    init = (C0.astype(jnp.float32), n0.astype(jnp.float32), m0.astype(jnp.float32))
    (C, n, m), hs = lax.scan(step, init, (qc, kc, vc, ic, fc))
    h = jnp.moveaxis(jnp.swapaxes(hs, 2, 3), 0, 1).reshape(B, S, H, DH)
    return h, C, n, m


def compress_kv(k):
    B, Lp = k.shape[:2]
    sub = jnp.mean(k.reshape((B, Lp // CMP_STRIDE, CMP_STRIDE) + k.shape[2:]), axis=2)
    nxt = jnp.concatenate([sub[:, 1:], jnp.zeros_like(sub[:, :1])], axis=1)
    return 0.5 * (sub + nxt)


def nsa_global_block(q, q_pos, kc, vc, kb, vb, n_sel):
    B, Q = q.shape[:2]
    scale = HEAD_DIM ** -0.5
    NC = kc.shape[1]
    NB = kb.shape[2]
    c_end = jnp.arange(NC) * CMP_STRIDE + CMP_BLOCK - 1
    c_mask = c_end[None, :] <= q_pos[:, None]
    p = masked_softmax(jnp.einsum('bqkgd,bckd->bkgqc', q, kc) * scale, c_mask)
    o_cmp = jnp.einsum('bkgqc,bckd->bqkgd', p.astype(vc.dtype), vc)
    imp = jnp.sum(p, axis=2).reshape(B, NSA_KV_HEADS, Q, NB, SEL_BLOCK // CMP_STRIDE).sum(-1)
    blk = jnp.arange(NB)[None, :]
    qblk = (q_pos // SEL_BLOCK)[:, None]
    valid = blk * SEL_BLOCK <= q_pos[:, None]
    forced = (blk == 0) | (blk == qblk) | (blk == qblk - 1)
    score = jnp.where(forced, MASK_BIG, jnp.where(valid, imp, -MASK_BIG))
    top, idx = lax.top_k(score, n_sel)
    sel_ok = top > -0.5 * MASK_BIG
    bi = jnp.arange(B)[:, None, None, None]
    hi = jnp.arange(NSA_KV_HEADS)[None, :, None, None]
    kg = kb[bi, hi, idx]
    vg = vb[bi, hi, idx]
    kpos = idx[..., None] * SEL_BLOCK + jnp.arange(SEL_BLOCK)
    k_mask = (kpos <= q_pos[None, None, :, None, None]) & sel_ok[..., None]
    M = n_sel * SEL_BLOCK
    s2 = jnp.einsum('bqkgd,bkqnjd->bkqgnj', q, kg) * scale
    p2 = masked_softmax(s2.reshape(B, NSA_KV_HEADS, Q, NSA_GROUP, M),
                        k_mask.reshape(B, NSA_KV_HEADS, Q, 1, M))
    o_sel = jnp.einsum('bkqgm,bkqmd->bqkgd', p2.astype(vg.dtype),
                       vg.reshape(B, NSA_KV_HEADS, Q, M, HEAD_DIM))
    return o_cmp, o_sel


def nsa_global(q, kvs, q_pos):
    B, L = kvs.shape[:2]
    Lp = -(-L // SEL_BLOCK) * SEL_BLOCK
    kvs = jnp.pad(kvs, ((0, 0), (0, Lp - L), (0, 0), (0, 0), (0, 0)))
    kc, vc = compress_kv(kvs[:, :, 0]), compress_kv(kvs[:, :, 1])
    NB = Lp // SEL_BLOCK

    def blocks(a):
        return a.reshape(B, NB, SEL_BLOCK, NSA_KV_HEADS, HEAD_DIM).transpose(0, 3, 1, 2, 4)

    kb, vb = blocks(kvs[:, :, 2]), blocks(kvs[:, :, 3])
    n_sel = min(SEL_TOPN, NB)
    Q = q.shape[1]
    if Q % SEL_QBLOCK == 0:
        nq = Q // SEL_QBLOCK
        qs = jnp.moveaxis(q.reshape((B, nq, SEL_QBLOCK) + q.shape[2:]), 1, 0)
        ps = q_pos.reshape(nq, SEL_QBLOCK)
        oc, osel = lax.map(lambda a: nsa_global_block(a[0], a[1], kc, vc, kb, vb, n_sel), (qs, ps))
        return jnp.moveaxis(oc, 0, 1).reshape(q.shape), jnp.moveaxis(osel, 0, 1).reshape(q.shape)
    return nsa_global_block(q, q_pos, kc, vc, kb, vb, n_sel)


def window_prompt(q, k, v):
    B, S = q.shape[:2]
    QB = WIN_QBLOCK if S % WIN_QBLOCK == 0 else S
    nq = S // QB
    span = WINDOW + QB
    pad = ((0, 0), (WINDOW, 0), (0, 0), (0, 0))
    kp, vp = jnp.pad(k, pad), jnp.pad(v, pad)
    kidx = jnp.arange(nq)[:, None] * QB + jnp.arange(span)[None, :]
    kpos = kidx - WINDOW
    qpos = jnp.arange(nq)[:, None] * QB + jnp.arange(QB)[None, :]
    kb, vb = kp[:, kidx], vp[:, kidx]
    qb = q.reshape(B, nq, QB, NSA_KV_HEADS, NSA_GROUP, HEAD_DIM)
    diff = qpos[:, :, None] - kpos[:, None, :]
    mask = (diff >= 0) & (diff < WINDOW) & (kpos[:, None, :] >= 0)
    s = jnp.einsum('bnqkgd,bnjkd->bnkgqj', qb, kb) * (HEAD_DIM ** -0.5)
    p = masked_softmax(s, mask[None, :, None, None])
    o = jnp.einsum('bnkgqj,bnjkd->bnqkgd', p.astype(vb.dtype), vb)
    return o.reshape(q.shape)


def window_cached(q, k_all, v_all, q_pos, k_pos):
    diff = q_pos[:, None] - k_pos[None, :]
    mask = (diff >= 0) & (diff < WINDOW)
    s = jnp.einsum('bqkgd,bjkd->bkgqj', q, k_all) * (HEAD_DIM ** -0.5)
    p = masked_softmax(s, mask)
    return jnp.einsum('bkgqj,bjkd->bqkgd', p.astype(v_all.dtype), v_all)


def nsa_prompt(q, kv):
    B, S = q.shape[:2]
    rows = kv[:, :, :2].reshape(B, S, 4, NSA_KV_HEADS, HEAD_DIM)
    o_cmp, o_sel = nsa_global(q, rows, jnp.arange(S))
    o_win = window_prompt(q, kv[:, :, 2, 0], kv[:, :, 2, 1])
    win_state = kv[:, S - min(WINDOW, S):, 2]
    return o_cmp, o_sel, o_win, rows, win_state


def make_nsa_sample(cache_nsa_kv, layer, page_table, win_buf):
    def nsa_sample(q, kv):
        B, S = q.shape[:2]
        past_len = page_table.shape[1] * cache_nsa_kv.shape[2]
        rows = kv[:, :, :2].reshape(B, S, 4, NSA_KV_HEADS, HEAD_DIM)
        past = cache_nsa_kv[layer, page_table].reshape(B, past_len, 4, NSA_KV_HEADS, HEAD_DIM)
        q_pos = past_len + jnp.arange(S)
        o_cmp, o_sel = nsa_global(q, jnp.concatenate([past, rows.astype(past.dtype)], axis=1), q_pos)
        wb = win_buf.shape[1]
        w_all = jnp.concatenate([win_buf, kv[:, :, 2].astype(win_buf.dtype)], axis=1)
        k_pos = past_len - wb + jnp.arange(wb + S)
        o_win = window_cached(q, w_all[:, :, 0], w_all[:, :, 1], q_pos, k_pos)
        return o_cmp, o_sel, o_win, rows, w_all[:, S:]
    return nsa_sample


def peer_block(x, wq, subkeys, u, v):
    T = x.shape[0]
    q = (x @ wq).reshape(T, PEER_HEADS, 2, PEER_DKEY // 2)
    s = jnp.einsum('thcd,hcnd->thcn', q, subkeys).astype(jnp.float32)
    s1, i1 = lax.top_k(s[:, :, 0], PEER_TOPK)
    s2, i2 = lax.top_k(s[:, :, 1], PEER_TOPK)
    cand = (s1[..., :, None] + s2[..., None, :]).reshape(T, PEER_HEADS, PEER_TOPK * PEER_TOPK)
    cidx = (i1[..., :, None] * PEER_NKEYS + i2[..., None, :]).reshape(T, PEER_HEADS, PEER_TOPK * PEER_TOPK)
    top, pos = lax.top_k(cand, PEER_TOPK)
    eidx = jnp.take_along_axis(cidx, pos, -1)
    g = jax.nn.softmax(top, -1)
    act = jax.nn.gelu(jnp.einsum('td,thkd->thk', x, u[eidx]).astype(jnp.float32))
    return jnp.einsum('thk,thkd->td', (g * act).astype(x.dtype), v[eidx])


def peer(x, wq, subkeys, u, v):
    B, S, D = x.shape
    T = B * S
    TB = PEER_TBLOCK if T % PEER_TBLOCK == 0 else T
    out = lax.map(lambda xb: peer_block(xb, wq, subkeys, u, v), x.reshape(T // TB, TB, D))
    return out.reshape(B, S, D)


def hybrid_layer(x, lw, conv_buf, C0, n0, m0, nsa_fn):
    (norm1_g, w_in, conv_w, gate_b, mnorm_g, w_out, norm2_g, pwq, psub, pu, pv) = lw
    B, S, _ = x.shape
    z = rmsnorm(x, norm1_g) @ w_in
    offs = np.cumsum(SPLIT_SIZES)[:-1].tolist()
    (cb, cc, ch, nq, nkv, ngate, mq, mk, mv, mo, mi, mf) = jnp.split(z, offs, axis=-1)
    u, conv_new = short_conv(cc * ch, conv_buf, conv_w)
    y_a = cb * u
    q = nq.reshape(B, S, NSA_KV_HEADS, NSA_GROUP, HEAD_DIM)
    kv = nkv.reshape(B, S, 3, 2, NSA_KV_HEADS, HEAD_DIM)
    o_cmp, o_sel, o_win, nsa_rows, win_state = nsa_fn(q, kv)
    g = jax.nn.sigmoid(ngate.astype(jnp.float32)).reshape(B, S, NSA_KV_HEADS, NSA_GROUP, 3).astype(x.dtype)
    y_b = (g[..., 0:1] * o_cmp + g[..., 1:2] * o_sel + g[..., 2:3] * o_win).reshape(B, S, NSA_W)
    hd = lambda a: a.reshape(B, S, MLSTM_HEADS, MLSTM_DH)
    i_pre = mi.astype(jnp.float32) + gate_b[0].astype(jnp.float32)
    logf = jax.nn.log_sigmoid(mf.astype(jnp.float32) + gate_b[1].astype(jnp.float32))
    h, C, n, m = mlstm_chunkwise(hd(mq), hd(mk), hd(mv), i_pre, logf, C0, n0, m0)
    h = h * lax.rsqrt(jnp.mean(h * h, -1, keepdims=True) + RMS_EPS)
    y_c = (jax.nn.sigmoid(mo.astype(jnp.float32)) * h.reshape(B, S, MLSTM_W)
           * mnorm_g.astype(jnp.float32)).astype(x.dtype)
    x = x + jnp.concatenate([y_a, y_b, y_c], axis=-1) @ w_out
    x = x + peer(rmsnorm(x, norm2_g), pwq, psub, pu, pv)
    return x, (nsa_rows, win_state, conv_new, C, n, m)


def setup_inputs(seed: int = 0) -> dict:
    key = jax.random.key(seed)
    ks = jax.random.split(key, 24)
    f32 = jnp.float32
    n_pages = PAST_LEN // PAGE_SIZE
    n_phys = (5 * DEC_BATCH * n_pages + 3) // 4
    win_buf = min(WINDOW, PAST_LEN)
    nrm = lambda k, shape, s: s * jax.random.normal(k, shape, f32)
    page_table = jax.random.permutation(ks[0], n_phys)[: DEC_BATCH * n_pages].reshape(DEC_BATCH, n_pages).astype(jnp.int32)
    gate_b = jnp.stack([nrm(ks[1], (DEPTH, MLSTM_HEADS), 0.1),
                        3.0 + nrm(ks[2], (DEPTH, MLSTM_HEADS), 0.5)], axis=1)
    return {
        'x_prompt': nrm(ks[3], (BATCH, SEQ, D_MODEL), 1.0),
        'x_sample': nrm(ks[4], (DEC_BATCH, DEC_SEQ, D_MODEL), 1.0),
        'cache_nsa_kv': nrm(ks[5], (DEPTH, n_phys, PAGE_SIZE, 4, NSA_KV_HEADS, HEAD_DIM), 1.0),
        'state_win_kv': nrm(ks[6], (DEPTH, DEC_BATCH, win_buf, 2, NSA_KV_HEADS, HEAD_DIM), 1.0),
        'state_conv': nrm(ks[7], (DEPTH, DEC_BATCH, CONV_K - 1, CONV_W), 1.0),
        'state_mlstm_C': nrm(ks[8], (DEPTH, DEC_BATCH, MLSTM_HEADS, MLSTM_DH, MLSTM_DH), 0.3),
        'state_mlstm_n': nrm(ks[9], (DEPTH, DEC_BATCH, MLSTM_HEADS, MLSTM_DH), 0.3),
        'state_mlstm_m': nrm(ks[10], (DEPTH, DEC_BATCH, MLSTM_HEADS), 0.5),
        'page_table': page_table,
        'norm1_g': 1.0 + nrm(ks[11], (DEPTH, D_MODEL), 0.1),
        'w_in': nrm(ks[12], (DEPTH, D_MODEL, IN_COLS), D_MODEL ** -0.5),
        'conv_w': nrm(ks[13], (DEPTH, CONV_K, CONV_W), CONV_K ** -0.5),
        'mlstm_gate_b': gate_b,
        'mlstm_norm_g': 1.0 + nrm(ks[14], (DEPTH, MLSTM_W), 0.1),
        'w_out': nrm(ks[15], (DEPTH, MIX_W, D_MODEL), MIX_W ** -0.5),
        'norm2_g': 1.0 + nrm(ks[16], (DEPTH, D_MODEL), 0.1),
        'peer_wq': nrm(ks[17], (DEPTH, D_MODEL, PEER_HEADS * PEER_DKEY), D_MODEL ** -0.5),
        'peer_subkeys': nrm(ks[18], (DEPTH, PEER_HEADS, 2, PEER_NKEYS, PEER_DKEY // 2), (PEER_DKEY // 2) ** -0.5),
        'peer_u': nrm(ks[19], (DEPTH, PEER_EXPERTS, D_MODEL), D_MODEL ** -0.5),
        'peer_v': nrm(ks[20], (DEPTH, PEER_EXPERTS, D_MODEL), PEER_HEADS ** -0.5),
        'final_norm_g': 1.0 + nrm(ks[21], (D_MODEL,), 0.1),
    }


def reference(x_prompt, x_sample, cache_nsa_kv, state_win_kv, state_conv, state_mlstm_C, state_mlstm_n,
              state_mlstm_m, page_table, norm1_g, w_in, conv_w, mlstm_gate_b, mlstm_norm_g, w_out, norm2_g,
              peer_wq, peer_subkeys, peer_u, peer_v, final_norm_g):
    xp, xs = x_prompt, x_sample
    bp = xp.shape[0]
    p_st, s_st = [], []
    for l in range(DEPTH):
        lw = (norm1_g[l], w_in[l], conv_w[l], mlstm_gate_b[l], mlstm_norm_g[l], w_out[l], norm2_g[l],
              peer_wq[l], peer_subkeys[l], peer_u[l], peer_v[l])
        xp, st = hybrid_layer(xp, lw,
                              jnp.zeros((bp, CONV_K - 1, CONV_W), xp.dtype),
                              jnp.zeros((bp, MLSTM_HEADS, MLSTM_DH, MLSTM_DH), jnp.float32),
                              jnp.zeros((bp, MLSTM_HEADS, MLSTM_DH), jnp.float32),
                              jnp.zeros((bp, MLSTM_HEADS), jnp.float32),
                              nsa_prompt)
        p_st.append(st)
        xs, st = hybrid_layer(xs, lw, state_conv[l], state_mlstm_C[l], state_mlstm_n[l], state_mlstm_m[l],
                              make_nsa_sample(cache_nsa_kv, l, page_table, state_win_kv[l]))
        s_st.append(st)
    p_rows, p_win, p_conv, p_C, p_n, p_m = [jnp.stack(a) for a in zip(*p_st)]
    s_rows, s_win, s_conv, s_C, s_n, s_m = [jnp.stack(a) for a in zip(*s_st)]
    y_prompt = rmsnorm(xp, final_norm_g)
    y_sample = rmsnorm(xs, final_norm_g)
    return (y_prompt, y_sample, p_rows, p_win, p_conv, p_C, p_n, p_m, s_rows, s_win, s_conv, s_C, s_n, s_m)
```

```python
import functools

import numpy as np
import jax
import jax.numpy as jnp
from jax import lax
from jax.experimental import pallas as pl
from jax.experimental.pallas import tpu as pltpu

F32 = jnp.float32
BF16 = jnp.bfloat16
HI = lax.Precision.HIGHEST

RMS_EPS = 1e-6
HEAD_DIM = 64
NSA_KV_HEADS = 4
NSA_GROUP = 4
CMP_STRIDE = 16
SEL_BLOCK = 64
SEL_TOPN = 16
WINDOW = 512
MLSTM_HEADS = 4
MLSTM_DH = 128
MLSTM_CHUNK = 64
PEER_HEADS = 8
PEER_NKEYS = 128
PEER_TOPK = 16
MASK_BIG = 1e9
NEG = -1e30

LANES = 128
VMEM_LIMIT = 56 * 1024 * 1024

NT = (((1,), (1,)), ((), ()))
TN = (((0,), (0,)), ((), ()))


def _cparams(sem):
    return pltpu.CompilerParams(dimension_semantics=sem, vmem_limit_bytes=VMEM_LIMIT)


def _div(x, d):
    assert d & (d - 1) == 0
    return lax.shift_right_arithmetic(x, jnp.int32(d.bit_length() - 1))


def _masked_softmax(s, mask):
    s = jnp.where(mask, s, NEG)
    m = jnp.max(s, -1, keepdims=True)
    e = jnp.where(mask, jnp.exp(s - m), 0.0)
    d = jnp.maximum(jnp.sum(e, -1, keepdims=True), 1e-30)
    return e * (1.0 / d)


def _rms_proj_kernel(x_ref, g_ref, w_ref, *rest, emit_xn):
    if emit_xn:
        o_ref, xo_ref, xn_ref = rest
    else:
        o_ref, xn_ref = rest

    @pl.when(pl.program_id(1) == 0)
    def _():
        x = x_ref[...]
        r = x * lax.rsqrt(jnp.mean(x * x, -1, keepdims=True) + RMS_EPS)
        xn = (r * g_ref[...]).astype(BF16)
        xn_ref[...] = xn
        if emit_xn:
            xo_ref[...] = xn

    o_ref[...] = jnp.dot(xn_ref[...], w_ref[...], preferred_element_type=F32).astype(o_ref.dtype)


def _rms_proj(x, g, w, tm, tn, out_dtype, emit_xn=False):
    T, D = x.shape
    N = w.shape[1]
    out_shape = [jax.ShapeDtypeStruct((T, N), out_dtype)]
    out_specs = [pl.BlockSpec((tm, tn), lambda i, j: (i, j))]
    if emit_xn:
        out_shape.append(jax.ShapeDtypeStruct((T, D), BF16))
        out_specs.append(pl.BlockSpec((tm, D), lambda i, j: (i, 0)))
    res = pl.pallas_call(
        functools.partial(_rms_proj_kernel, emit_xn=emit_xn),
        out_shape=out_shape,
        grid=(T // tm, N // tn),
        in_specs=[pl.BlockSpec((tm, D), lambda i, j: (i, 0)),
                  pl.BlockSpec((1, D), lambda i, j: (0, 0)),
                  pl.BlockSpec((D, tn), lambda i, j: (0, j))],
        out_specs=out_specs,
        scratch_shapes=[pltpu.VMEM((tm, D), BF16)],
        compiler_params=_cparams(("parallel", "arbitrary")),
        name="rms_proj",
    )(x, g.reshape(1, D), w)
    return res if emit_xn else res[0]


def _conv_kernel(cb_ref, cc_ref, ch_ref, buf_ref, w_ref, y_ref, new_ref, ext_ref, *, S):
    C = cb_ref.shape[-1]
    u = cc_ref[...] * ch_ref[...]
    ext_ref[0:8, :] = jnp.zeros((8, C), F32)
    ext_ref[6:8, :] = buf_ref[0]
    ext_ref[8:8 + S, :] = u
    w = w_ref[...]
    y = w[0:1] * ext_ref[6:6 + S, :] + w[1:2] * ext_ref[7:7 + S, :] + w[2:3] * u
    y_ref[...] = cb_ref[...] * y
    new_ref[0] = u[S - 2:S]


def _conv(z, row0, B, S, buf, w):
    C = w.shape[1]
    rb0 = row0 // S
    return pl.pallas_call(
        functools.partial(_conv_kernel, S=S),
        out_shape=[jax.ShapeDtypeStruct((B * S, C), F32), jax.ShapeDtypeStruct((B, 2, C), F32)],
        grid=(B,),
        in_specs=[pl.BlockSpec((S, C), lambda b: (rb0 + b, 0)),
                  pl.BlockSpec((S, C), lambda b: (rb0 + b, 1)),
                  pl.BlockSpec((S, C), lambda b: (rb0 + b, 2)),
                  pl.BlockSpec((1, 2, C), lambda b: (b, 0, 0)),
                  pl.BlockSpec((3, C), lambda b: (0, 0))],
        out_specs=[pl.BlockSpec((S, C), lambda b: (b, 0)),
                   pl.BlockSpec((1, 2, C), lambda b: (b, 0, 0))],
        scratch_shapes=[pltpu.VMEM((S + 8, C), F32)],
        compiler_params=_cparams(("parallel",)),
        name="short_conv",
    )(z, z, z, buf, w)


def _topn_rank_select(score, n_sel):
    NB = score.shape[1]
    blk = lax.broadcasted_iota(jnp.int32, score.shape, 1)
    rank = jnp.zeros(score.shape, F32)
    for i in range(NB):
        ci = score[:, i:i + 1]
        beats = jnp.where(ci > score, 1.0, jnp.where((ci == score) & (blk > i), 1.0, 0.0))
        rank = rank + beats
    return rank < n_sel


def _nsa_prompt_kernel(q_ref, kcs_ref, vcs_ref, ks_ref, vs_ref, kw_ref, vw_ref, gt_ref,
                       y_ref, kc_ref, vc_ref, *, S, tq):
    kvh = pl.program_id(1)
    qi = pl.program_id(2)
    G = NSA_GROUP
    NC = S // CMP_STRIDE
    NB = S // SEL_BLOCK
    n_sel = min(SEL_TOPN, NB)
    span = min(WINDOW + tq, S)

    @pl.when(qi == 0)
    def _():
        j = lax.broadcasted_iota(jnp.int32, (NC, S), 0)
        r = lax.broadcasted_iota(jnp.int32, (NC, S), 1)
        lo = j * CMP_STRIDE
        pool = jnp.where((r >= lo) & (r < lo + 2 * CMP_STRIDE), 0.5 / CMP_STRIDE, 0.0).astype(F32)
        kc_ref[...] = jnp.dot(pool, kcs_ref[0, 0], precision=HI, preferred_element_type=F32).astype(BF16)
        vc_ref[...] = jnp.dot(pool, vcs_ref[0, 0], precision=HI, preferred_element_type=F32).astype(BF16)

    t0 = qi * tq
    qs = q_ref[0, 0].reshape(G * tq, HEAD_DIM) * jnp.asarray(HEAD_DIM ** -0.5, BF16)
    row = lax.broadcasted_iota(jnp.int32, (G * tq, 1), 0)
    pos = t0 + (row & (tq - 1))
    posq = t0 + lax.broadcasted_iota(jnp.int32, (tq, 1), 0)

    s = lax.dot_general(qs, kc_ref[...], NT, preferred_element_type=F32)
    c_end = lax.broadcasted_iota(jnp.int32, (1, NC), 1) * CMP_STRIDE + (2 * CMP_STRIDE - 1)
    p = _masked_softmax(s, c_end <= pos)
    o_cmp = jnp.dot(p.astype(BF16), vc_ref[...], preferred_element_type=F32)

    imp = p[0:tq]
    for g in range(1, G):
        imp = imp + p[g * tq:(g + 1) * tq]
    per = SEL_BLOCK // CMP_STRIDE
    e4 = jnp.where(_div(lax.broadcasted_iota(jnp.int32, (NC, NB), 0), per)
                   == lax.broadcasted_iota(jnp.int32, (NC, NB), 1), 1.0, 0.0).astype(F32)
    impb = jnp.dot(imp, e4, precision=HI, preferred_element_type=F32)
    blk = lax.broadcasted_iota(jnp.int32, (1, NB), 1)
    qblk = _div(posq, SEL_BLOCK)
    valid = blk * SEL_BLOCK <= posq
    forced = (blk == 0) | (blk == qblk) | (blk == qblk - 1)
    score = jnp.where(forced, MASK_BIG, jnp.where(valid, impb, -MASK_BIG))
    sel = _topn_rank_select(score, n_sel) & (score > -0.5 * MASK_BIG)
    eb = jnp.where(_div(lax.broadcasted_iota(jnp.int32, (NB, S), 1), SEL_BLOCK)
                   == lax.broadcasted_iota(jnp.int32, (NB, S), 0), 1.0, 0.0).astype(BF16)
    selk = jnp.dot(jnp.where(sel, 1.0, 0.0).astype(BF16), eb, preferred_element_type=F32)
    kpos = lax.broadcasted_iota(jnp.int32, (1, S), 1)
    sel_mask = (selk > 0.5) & (kpos <= posq)

    start = pl.multiple_of(jnp.maximum(t0 + tq - span, 0), tq)
    kw = kw_ref[0, 0, pl.ds(start, span), :]
    vw = vw_ref[0, 0, pl.ds(start, span), :]
    diff = posq - (start + lax.broadcasted_iota(jnp.int32, (1, span), 1))
    win_mask = (diff >= 0) & (diff < WINDOW)

    sg = jax.nn.sigmoid(gt_ref[...])
    pick = jnp.where(lax.broadcasted_iota(jnp.int32, (LANES, LANES), 0)
                     == lax.broadcasted_iota(jnp.int32, (LANES, LANES), 1) + kvh * (3 * G), 1.0, 0.0).astype(F32)
    g12 = jnp.dot(sg, pick, precision=HI, preferred_element_type=F32)

    outs = []
    for g in range(G):
        qg = qs[g * tq:(g + 1) * tq]
        s2 = lax.dot_general(qg, ks_ref[0, 0], NT, preferred_element_type=F32)
        p2 = _masked_softmax(s2, sel_mask)
        o_sel = jnp.dot(p2.astype(BF16), vs_ref[0, 0], preferred_element_type=F32)
        s3 = lax.dot_general(qg, kw, NT, preferred_element_type=F32)
        p3 = _masked_softmax(s3, win_mask)
        o_win = jnp.dot(p3.astype(BF16), vw, preferred_element_type=F32)
        outs.append(g12[:, 3 * g:3 * g + 1] * o_cmp[g * tq:(g + 1) * tq]
                    + g12[:, 3 * g + 1:3 * g + 2] * o_sel
                    + g12[:, 3 * g + 2:3 * g + 3] * o_win)
    y_ref[...] = jnp.concatenate(outs, axis=-1)


def _nsa_prompt(q_hm, kvc_hm, kvsw_hm, z, gate_cb, B, S, tq=128):
    nq = S // tq
    KVH, G, HD = NSA_KV_HEADS, NSA_GROUP, HEAD_DIM
    NC = S // CMP_STRIDE

    def kv_spec(i):
        return pl.BlockSpec((None, 1, 1, S, HD), lambda b, h, t, i=i: (i, b, h, 0, 0))

    return pl.pallas_call(
        functools.partial(_nsa_prompt_kernel, S=S, tq=tq),
        out_shape=jax.ShapeDtypeStruct((B * S, KVH * G * HD), F32),
        grid=(B, KVH, nq),
        in_specs=[pl.BlockSpec((1, 1, G, tq, HD), lambda b, h, t: (b, h, 0, t, 0)),
                  kv_spec(0), kv_spec(1), kv_spec(0), kv_spec(1), kv_spec(2), kv_spec(3),
                  pl.BlockSpec((tq, LANES), lambda b, h, t: (b * nq + t, gate_cb))],
        out_specs=pl.BlockSpec((tq, G * HD), lambda b, h, t: (b * nq + t, h)),
        scratch_shapes=[pltpu.VMEM((NC, HD), BF16), pltpu.VMEM((NC, HD), BF16)],
        compiler_params=_cparams(("parallel", "parallel", "arbitrary")),
        name="nsa_prompt",
    )(q_hm, kvc_hm, kvc_hm, kvsw_hm, kvsw_hm, kvsw_hm, kvsw_hm, z)


def _log_sigmoid(x):
    return jnp.minimum(x, 0.0) - jnp.log(1.0 + jnp.exp(-jnp.abs(x)))


def _mlstm_kernel(gb_ref, q_ref, k_ref, v_ref, o_ref, gt_ref, mg_ref, c0_ref, n0_ref, m0_ref,
                  y_ref, cn_ref, nn_ref, mn_ref, c_s, n_s, m_s, *, L, Lb, gi, gf):
    H, DH = MLSTM_HEADS, MLSTM_DH
    c = pl.program_id(1)

    @pl.when(c == 0)
    def _():
        c_s[...] = c0_ref[0]
        n_s[...] = n0_ref[0]
        m_s[...] = m0_ref[0]

    def padrows(a):
        if Lb == L:
            return a
        return jnp.concatenate([a, jnp.zeros((L - Lb, a.shape[1]), a.dtype)], axis=0)

    lane = lax.broadcasted_iota(jnp.int32, (1, LANES), 1)
    bias = jnp.zeros((1, LANES), F32)
    for h in range(H):
        bias = bias + jnp.where(lane == gi + h, gb_ref[0, h], 0.0) + jnp.where(lane == gf + h, gb_ref[1, h], 0.0)
    is_f = (lane >= gf) & (lane < gf + H)
    pre = padrows(gt_ref[...]) + bias
    gate = jnp.where(is_f, _log_sigmoid(pre), pre)
    if Lb != L:
        live = lax.broadcasted_iota(jnp.int32, (L, 1), 0) < Lb
        gate = jnp.where(live, gate, jnp.where(is_f, 0.0, NEG))
    rr = lax.broadcasted_iota(jnp.int32, (L, L), 0)
    cc = lax.broadcasted_iota(jnp.int32, (L, L), 1)
    tril = rr >= cc
    bcum = jnp.dot(jnp.where(tril, 1.0, 0.0).astype(F32), gate, precision=HI, preferred_element_type=F32)
    e8 = jnp.where(lax.broadcasted_iota(jnp.int32, (8, LANES), 1)
                   == lax.broadcasted_iota(jnp.int32, (8, LANES), 0) + gi, 1.0, 0.0).astype(F32)
    rg = lax.dot_general(e8, gate, NT, precision=HI, preferred_element_type=F32)
    rb = lax.dot_general(e8, bcum, NT, precision=HI, preferred_element_type=F32)

    q = padrows(q_ref[...])
    k = padrows(k_ref[...])
    v = padrows(v_ref[...])
    og = padrows(o_ref[...])
    ys = []
    for h in range(H):
        sl = slice(h * DH, (h + 1) * DH)
        qq = q[:, sl]
        kk = k[:, sl] * (DH ** -0.5)
        vv = v[:, sl]
        qb, kb, vb = qq.astype(BF16), kk.astype(BF16), vv.astype(BF16)
        b_col = bcum[:, gf + h:gf + h + 1]
        b_row = rb[H + h:H + h + 1, :]
        i_row = rg[h:h + 1, :]
        i_col = gate[:, gi + h:gi + h + 1]
        m_prev = m_s[h][:, 0:1]
        cmat = c_s[h]
        n_row = n_s[h]
        dmat = jnp.where(tril, b_col - b_row + i_row, NEG)
        inter = b_col + m_prev
        mt = jnp.maximum(inter, jnp.max(dmat, -1, keepdims=True))
        w = jnp.exp(dmat - mt)
        a = jnp.exp(inter - mt)
        wqk = w * lax.dot_general(qb, kb, NT, preferred_element_type=F32)
        num = (a * jnp.dot(qb, cmat.astype(BF16), preferred_element_type=F32)
               + jnp.dot(wqk.astype(BF16), vb, preferred_element_type=F32))
        den = a * jnp.sum(qq * n_row, -1, keepdims=True) + jnp.sum(wqk, -1, keepdims=True)
        hh = num * (1.0 / jnp.maximum(jnp.abs(den), jnp.exp(-mt)))
        b_last = b_col[L - 1:L]
        m_new = mt[L - 1:L]
        wl = jnp.exp(b_last - b_col + i_col - m_new)
        decay = jnp.exp(b_last + m_prev - m_new)
        kw = wl * kk
        c_s[h] = decay * cmat + lax.dot_general(kw.astype(BF16), vb, TN, preferred_element_type=F32)
        n_s[h] = decay * n_row + jnp.sum(kw, 0, keepdims=True)
        m_s[h] = jnp.broadcast_to(m_new, (1, LANES))
        hn = hh * lax.rsqrt(jnp.mean(hh * hh, -1, keepdims=True) + RMS_EPS)
        ys.append(jax.nn.sigmoid(og[:, sl]) * hn * mg_ref[:, sl])
    y = jnp.concatenate(ys, axis=-1)
    y_ref[...] = y[0:Lb]

    @pl.when(c == pl.num_programs(1) - 1)
    def _():
        cn_ref[0] = c_s[...]
        nn_ref[0] = n_s[...]
        mn_ref[0] = m_s[...]


def _mlstm(z, row0, B, S, col_q, gate_cb, gi, gf, gate_b, mnorm_g, C0, n0, m0):
    H, DH = MLSTM_HEADS, MLSTM_DH
    W = H * DH
    L = MLSTM_CHUNK
    Lb = L if S % L == 0 else S
    assert Lb <= L
    nc = S // Lb
    rb0 = row0 // Lb
    n0 = n0.reshape(B, H, 1, DH)
    m0 = jnp.broadcast_to(m0.reshape(B, H, 1, 1), (B, H, 1, LANES))

    def zspec(cb, width):
        return pl.BlockSpec((Lb, width), lambda b, c, cb=cb: (rb0 + b * nc + c, cb))

    y, C, n, m = pl.pallas_call(
        functools.partial(_mlstm_kernel, L=L, Lb=Lb, gi=gi, gf=gf),
        out_shape=[jax.ShapeDtypeStruct((B * S, W), F32),
                   jax.ShapeDtypeStruct((B, H, DH, DH), F32),
                   jax.ShapeDtypeStruct((B, H, 1, DH), F32),
                   jax.ShapeDtypeStruct((B, H, 1, LANES), F32)],
        grid=(B, nc),
        in_specs=[pl.BlockSpec(memory_space=pltpu.SMEM),
                  zspec(col_q, W), zspec(col_q + 1, W), zspec(col_q + 2, W), zspec(col_q + 3, W),
                  zspec(gate_cb, LANES),
                  pl.BlockSpec((1, W), lambda b, c: (0, 0)),
                  pl.BlockSpec((1, H, DH, DH), lambda b, c: (b, 0, 0, 0)),
                  pl.BlockSpec((1, H, 1, DH), lambda b, c: (b, 0, 0, 0)),
                  pl.BlockSpec((1, H, 1, LANES), lambda b, c: (b, 0, 0, 0))],
        out_specs=[pl.BlockSpec((Lb, W), lambda b, c: (b * nc + c, 0)),
                   pl.BlockSpec((1, H, DH, DH), lambda b, c: (b, 0, 0, 0)),
                   pl.BlockSpec((1, H, 1, DH), lambda b, c: (b, 0, 0, 0)),
                   pl.BlockSpec((1, H, 1, LANES), lambda b, c: (b, 0, 0, 0))],
        scratch_shapes=[pltpu.VMEM((H, DH, DH), F32), pltpu.VMEM((H, 1, DH), F32),
                        pltpu.VMEM((H, 1, LANES), F32)],
        compiler_params=_cparams(("parallel", "arbitrary")),
        name="mlstm",
    )(gate_b, z, z, z, z, z, mnorm_g.reshape(1, W), C0, n0, m0)
    return y, C, n.reshape(B, H, DH), m[:, :, 0, 0]


PAGES_PER_STEP = 8


def _lane_extract_topn(score, n_sel, floor):
    lane = lax.broadcasted_iota(jnp.int32, score.shape, 1).astype(F32)
    sel = jnp.zeros(score.shape, F32)
    sc = score
    for _ in range(n_sel):
        m = jnp.max(sc, -1, keepdims=True)
        idx = jnp.min(jnp.where(sc == m, lane, float(score.shape[1])), -1, keepdims=True)
        hit = lane == idx
        sel = jnp.where(hit & (m > floor), 1.0, sel)
        sc = jnp.where(hit, -jnp.inf, sc)
    return sel


def _s1_kernel(pt_ref, *refs, past, S, ncp, nbp):
    pgs = refs[:PAGES_PER_STEP]
    tail_ref, q_ref, ocmp_ref, sel_ref, sub_ref = refs[PAGES_PER_STEP:]
    P = pgs[0].shape[2]
    j = pl.program_id(1)
    nfull = pl.num_programs(1) - 1
    KVH, G, HD = NSA_KV_HEADS, NSA_GROUP, HEAD_DIM
    per_page = P // CMP_STRIDE
    rows_step = PAGES_PER_STEP * per_page

    @pl.when(j == 0)
    def _():
        sub_ref[...] = jnp.zeros(sub_ref.shape, F32)

    def pool(n):
        return jnp.where(_div(lax.broadcasted_iota(jnp.int32, (n // CMP_STRIDE, n), 1), CMP_STRIDE)
                         == lax.broadcasted_iota(jnp.int32, (n // CMP_STRIDE, n), 0),
                         1.0 / CMP_STRIDE, 0.0).astype(F32)

    @pl.when(j < nfull)
    def _():
        pages = jnp.concatenate([r[0, 0] for r in pgs], axis=0)
        sub = jnp.dot(pool(PAGES_PER_STEP * P), pages, precision=HI, preferred_element_type=F32)
        sub_ref[pl.ds(pl.multiple_of(j * rows_step, rows_step), rows_step), :] = sub

    @pl.when(j == nfull)
    def _():
        base = (past // CMP_STRIDE)
        sub_ref[base:base + per_page, :] = jnp.dot(pool(P), tail_ref[0], precision=HI,
                                                   preferred_element_type=F32)
        R = G * S
        row = lax.broadcasted_iota(jnp.int32, (R, 1), 0)
        pos = past + (row & (S - 1))
        posq = past + lax.broadcasted_iota(jnp.int32, (S, 1), 0)
        c_end = lax.broadcasted_iota(jnp.int32, (1, ncp), 1) * CMP_STRIDE + (2 * CMP_STRIDE - 1)
        cmask = c_end <= pos
        per = SEL_BLOCK // CMP_STRIDE
        e4 = jnp.where(_div(lax.broadcasted_iota(jnp.int32, (ncp, nbp), 0), per)
                       == lax.broadcasted_iota(jnp.int32, (ncp, nbp), 1), 1.0, 0.0).astype(F32)
        blk = lax.broadcasted_iota(jnp.int32, (1, nbp), 1)
        qblk = _div(posq, SEL_BLOCK)
        valid = blk * SEL_BLOCK <= posq
        forced = (blk == 0) | (blk == qblk) | (blk == qblk - 1)
        for h in range(KVH):
            ksl = slice(h * HD, (h + 1) * HD)
            vsl = slice((KVH + h) * HD, (KVH + h + 1) * HD)
            kc = 0.5 * (sub_ref[0:ncp, ksl] + sub_ref[1:ncp + 1, ksl])
            vc = 0.5 * (sub_ref[0:ncp, vsl] + sub_ref[1:ncp + 1, vsl])
            qs = q_ref[0, h] * jnp.asarray(HD ** -0.5, BF16)
            s = lax.dot_general(qs, kc.astype(BF16), NT, preferred_element_type=F32)
            p = _masked_softmax(s, cmask)
            ocmp_ref[0, h] = jnp.dot(p.astype(BF16), vc.astype(BF16), preferred_element_type=F32)
            imp = p[0:S]
            for g in range(1, G):
                imp = imp + p[g * S:(g + 1) * S]
            impb = jnp.dot(imp, e4, precision=HI, preferred_element_type=F32)
            score = jnp.where(forced, MASK_BIG, jnp.where(valid, impb, -MASK_BIG))
            sel_ref[0, h] = _lane_extract_topn(score, SEL_TOPN, -0.5 * MASK_BIG)


def _s2_kernel(pt_ref, *refs, past, S, nbp):
    pgs = refs[:PAGES_PER_STEP]
    tail_ref, q_ref, sel_ref, osel_ref, m_s, l_s, acc_s = refs[PAGES_PER_STEP:]
    P = pgs[0].shape[2]
    j = pl.program_id(1)
    nfull = pl.num_programs(1) - 1
    KVH, G, HD = NSA_KV_HEADS, NSA_GROUP, HEAD_DIM
    R = G * S

    @pl.when(j == 0)
    def _():
        m_s[...] = jnp.full(m_s.shape, NEG, F32)
        l_s[...] = jnp.zeros(l_s.shape, F32)
        acc_s[...] = jnp.zeros(acc_s.shape, F32)

    row = lax.broadcasted_iota(jnp.int32, (R, 1), 0)
    pos = past + (row & (S - 1))

    def process(kv, kpos0):
        n = kv.shape[0]
        kvb = kv.astype(BF16)
        kpos = kpos0 + lax.broadcasted_iota(jnp.int32, (1, n), 1)
        kblk = _div(kpos, SEL_BLOCK)
        esel = jnp.where(lax.broadcasted_iota(jnp.int32, (nbp, n), 0) == kblk, 1.0, 0.0).astype(BF16)
        causal = kpos <= pos
        for h in range(KVH):
            kh = kvb[:, h * HD:(h + 1) * HD]
            vh = kvb[:, (KVH + h) * HD:(KVH + h + 1) * HD]
            qs = q_ref[0, h] * jnp.asarray(HD ** -0.5, BF16)
            s = lax.dot_general(qs, kh, NT, preferred_element_type=F32)
            mk = jnp.dot(sel_ref[0, h].astype(BF16), esel, preferred_element_type=F32)
            mask = (jnp.concatenate([mk] * G, axis=0) > 0.5) & causal
            sm = jnp.where(mask, s, NEG)
            m_old = m_s[h]
            m_new = jnp.maximum(m_old, jnp.max(sm, -1, keepdims=True))
            alpha = jnp.exp(m_old - m_new)
            e = jnp.where(mask, jnp.exp(sm - m_new), 0.0)
            l_s[h] = alpha * l_s[h] + jnp.sum(e, -1, keepdims=True)
            acc_s[h] = alpha * acc_s[h] + jnp.dot(e.astype(BF16), vh, preferred_element_type=F32)
            m_s[h] = m_new

    @pl.when(j < nfull)
    def _():
        process(jnp.concatenate([r[0, 0] for r in pgs], axis=0), j * (PAGES_PER_STEP * P))

    @pl.when(j == nfull)
    def _():
        process(tail_ref[0], past)
        for h in range(KVH):
            osel_ref[0, h] = acc_s[h] * (1.0 / jnp.maximum(l_s[h], 1e-30))


def _page_specs(layer, half, n_pages, P, width):
    def spec(r):
        return pl.BlockSpec(
            (1, 1, P, width),
            lambda b, j, pt, r=r: (layer, pt[b, jnp.minimum(j * PAGES_PER_STEP + r, n_pages - 1)], 0, half))
    return [spec(r) for r in range(PAGES_PER_STEP)]


def _nsa_sample_global(cache4, layer, page_table, tail, q_s, S):
    B, n_pages = page_table.shape
    P = cache4.shape[2]
    KVH, G, HD = NSA_KV_HEADS, NSA_GROUP, HEAD_DIM
    half = 2 * KVH * HD
    past = n_pages * P
    assert n_pages % PAGES_PER_STEP == 0 and S & (S - 1) == 0 and S <= SEL_BLOCK
    nsteps = n_pages // PAGES_PER_STEP + 1
    ncp = -(-(past + P) // CMP_STRIDE // LANES) * LANES
    nbp = -(-(past + P) // SEL_BLOCK // LANES) * LANES
    R = G * S
    qspec = pl.BlockSpec((1, KVH, R, HD), lambda b, j, pt: (b, 0, 0, 0))
    o_cmp, sel = pl.pallas_call(
        functools.partial(_s1_kernel, past=past, S=S, ncp=ncp, nbp=nbp),
        out_shape=[jax.ShapeDtypeStruct((B, KVH, R, HD), F32), jax.ShapeDtypeStruct((B, KVH, S, nbp), F32)],
        grid_spec=pltpu.PrefetchScalarGridSpec(
            num_scalar_prefetch=1, grid=(B, nsteps),
            in_specs=_page_specs(layer, 0, n_pages, P, half)
            + [pl.BlockSpec((1, P, half), lambda b, j, pt: (b, 0, 0)), qspec],
            out_specs=[pl.BlockSpec((1, KVH, R, HD), lambda b, j, pt: (b, 0, 0, 0)),
                       pl.BlockSpec((1, KVH, S, nbp), lambda b, j, pt: (b, 0, 0, 0))],
            scratch_shapes=[pltpu.VMEM((ncp + 8, half), F32)]),
        compiler_params=_cparams(("parallel", "arbitrary")),
        name="nsa_sample_cmp",
    )(page_table, *([cache4] * PAGES_PER_STEP), tail, q_s)
    o_sel = pl.pallas_call(
        functools.partial(_s2_kernel, past=past, S=S, nbp=nbp),
        out_shape=jax.ShapeDtypeStruct((B, KVH, R, HD), F32),
        grid_spec=pltpu.PrefetchScalarGridSpec(
            num_scalar_prefetch=1, grid=(B, nsteps),
            in_specs=_page_specs(layer, 1, n_pages, P, half)
            + [pl.BlockSpec((1, P, half), lambda b, j, pt: (b, 0, 1)), qspec,
               pl.BlockSpec((1, KVH, S, nbp), lambda b, j, pt: (b, 0, 0, 0))],
            out_specs=pl.BlockSpec((1, KVH, R, HD), lambda b, j, pt: (b, 0, 0, 0)),
            scratch_shapes=[pltpu.VMEM((KVH, R, 1), F32), pltpu.VMEM((KVH, R, 1), F32),
                            pltpu.VMEM((KVH, R, HD), F32)]),
        compiler_params=_cparams(("parallel", "arbitrary")),
        name="nsa_sample_sel",
    )(page_table, *([cache4] * PAGES_PER_STEP), tail, q_s, sel)
    return o_cmp, o_sel


def _s3_kernel(q_ref, kw_ref, vw_ref, ocmp_ref, osel_ref, gt_ref, y_ref, *, past, S):
    KVH, G, HD = NSA_KV_HEADS, NSA_GROUP, HEAD_DIM
    R = G * S
    nk = kw_ref.shape[1]
    row = lax.broadcasted_iota(jnp.int32, (R, 1), 0)
    pos = past + (row & (S - 1))
    kpos = past - (nk - S) + lax.broadcasted_iota(jnp.int32, (1, nk), 1)
    diff = pos - kpos
    wmask = (diff >= 0) & (diff < WINDOW)
    sg = jax.nn.sigmoid(gt_ref[...])
    kw = kw_ref[0].astype(BF16)
    vw = vw_ref[0].astype(BF16)
    outs = []
    for h in range(KVH):
        qs = q_ref[0, h] * jnp.asarray(HD ** -0.5, BF16)
        s = lax.dot_general(qs, kw[:, h * HD:(h + 1) * HD], NT, preferred_element_type=F32)
        p = _masked_softmax(s, wmask)
        o_win = jnp.dot(p.astype(BF16), vw[:, h * HD:(h + 1) * HD], preferred_element_type=F32)
        oc = ocmp_ref[0, h]
        osl = osel_ref[0, h]
        for g in range(G):
            c0 = (h * G + g) * 3
            rs = slice(g * S, (g + 1) * S)
            outs.append(sg[:, c0:c0 + 1] * oc[rs] + sg[:, c0 + 1:c0 + 2] * osl[rs]
                        + sg[:, c0 + 2:c0 + 3] * o_win[rs])
    y_ref[...] = jnp.concatenate(outs, axis=-1)


def _nsa_sample_combine(q_s, kwin, vwin, o_cmp, o_sel, z, row0, gate_cb, past, S):
    B, KVH, R, HD = q_s.shape
    nk = kwin.shape[1]
    W = KVH * HD
    ospec = pl.BlockSpec((1, KVH, R, HD), lambda b: (b, 0, 0, 0))
    return pl.pallas_call(
        functools.partial(_s3_kernel, past=past, S=S),
        out_shape=jax.ShapeDtypeStruct((B * S, KVH * NSA_GROUP * HD), F32),
        grid=(B,),
        in_specs=[ospec,
                  pl.BlockSpec((1, nk, W), lambda b: (b, 0, 0)),
                  pl.BlockSpec((1, nk, W), lambda b: (b, 0, 0)),
                  ospec, ospec,
                  pl.BlockSpec((S, LANES), lambda b: (row0 // S + b, gate_cb))],
        out_specs=pl.BlockSpec((S, KVH * NSA_GROUP * HD), lambda b: (b, 0)),
        compiler_params=_cparams(("parallel",)),
        name="nsa_sample_win",
    )(q_s, kwin, vwin, o_cmp, o_sel, z)


def _out_proj_kernel(x_ref, y_ref, w_ref, o_ref):
    o_ref[...] = x_ref[...] + jnp.dot(y_ref[...].astype(BF16), w_ref[...], preferred_element_type=F32)


def _out_proj(x, y, w, tm):
    T, D = x.shape
    K = y.shape[1]
    return pl.pallas_call(
        _out_proj_kernel,
        out_shape=jax.ShapeDtypeStruct((T, D), F32),
        grid=(T // tm,),
        in_specs=[pl.BlockSpec((tm, D), lambda i: (i, 0)),
                  pl.BlockSpec((tm, K), lambda i: (i, 0)),
                  pl.BlockSpec((K, D), lambda i: (0, 0))],
        out_specs=pl.BlockSpec((tm, D), lambda i: (i, 0)),
        compiler_params=_cparams(("parallel",)),
        name="out_proj",
    )(x, y, w)


def _structural_pairs(k):
    return [(i, j) for i in range(k) for j in range(k) if (i + 1) * (j + 1) <= k]


def _router_kernel(q_ref, wb_ref, n_out, e1_out, r2_out, e2_out, s_ref, cur_ref, rk_ref, tmp_ref):
    H, NK, K = PEER_HEADS, PEER_NKEYS, PEER_TOPK
    Tt = q_ref.shape[0]
    half = q_ref.shape[1] // 2
    key = lax.broadcasted_iota(jnp.int32, (NK, H, Tt), 0).astype(F32)
    tops = []
    for c in range(2):
        s = lax.dot_general(wb_ref[c], q_ref[:, c * half:(c + 1) * half], NT, preferred_element_type=F32)
        s_ref[c] = s.reshape(NK, H, Tt)
        cur_ref[...] = s_ref[c]
        rk_ref[c] = jnp.full((NK, H, Tt), float(NK), F32)
        vals = []
        for k in range(K):
            cur = cur_ref[...]
            m = jnp.max(cur, axis=0)
            idx = jnp.min(jnp.where(cur == m[None], key, float(NK)), axis=0)
            hit = key == idx[None]
            rk_ref[c] = jnp.where(hit, float(k), rk_ref[c])
            cur_ref[...] = jnp.where(hit, -jnp.inf, cur)
            vals.append(m)
        tops.append(vals)
    v1, v2 = tops

    pairs = _structural_pairs(K)
    cand = [v1[i] + v2[j] for (i, j) in pairs]
    n = len(pairs)
    rank = []
    for p in range(n):
        rank.append(jnp.zeros((H, Tt), F32))
    for p in range(n):
        ip, jp = pairs[p]
        for q in range(p + 1, n):
            iq, jq = pairs[q]
            if ip <= iq and jp <= jq:
                rank[q] = rank[q] + 1.0
            else:
                b = jnp.where(cand[p] >= cand[q], 1.0, 0.0)
                rank[q] = rank[q] + b
                rank[p] = rank[p] + (1.0 - b)
    sel = [jnp.where(r < K, 1.0, 0.0) for r in rank]
    e1 = [jnp.exp(v1[i] - v1[0]) for i in range(K)]
    e2 = [jnp.exp(v2[j] - v2[0]) for j in range(K)]
    cnt = [jnp.zeros((H, Tt), F32) for _ in range(K)]
    zsum = jnp.zeros((H, Tt), F32)
    for p, (i, j) in enumerate(pairs):
        cnt[i] = cnt[i] + sel[p]
        zsum = zsum + sel[p] * (e1[i] * e2[j])
    inv_z = 1.0 / zsum

    rk1 = rk_ref[0]
    nk = jnp.zeros((NK, H, Tt), F32)
    for i in range(K):
        nk = jnp.where(rk1 == float(i), cnt[i][None], nk)
    outs = ((n_out, nk),
            (e1_out, jnp.exp(s_ref[0] - v1[0][None]) * inv_z[None]),
            (r2_out, rk_ref[1]),
            (e2_out, jnp.exp(s_ref[1] - v2[0][None])))
    for ref, val in outs:
        tmp_ref[...] = val.reshape(NK * H, Tt)
        for h in range(H):
            ref[h] = tmp_ref[pl.ds(h, NK, stride=H), :]


def _router(q, wb, tt):
    T = q.shape[0]
    H, NK = PEER_HEADS, PEER_NKEYS
    shp = jax.ShapeDtypeStruct((H, NK, T), F32)
    ospec = pl.BlockSpec((H, NK, tt), lambda i: (0, 0, i))
    return pl.pallas_call(
        _router_kernel,
        out_shape=[shp, shp, shp, shp],
        grid=(T // tt,),
        in_specs=[pl.BlockSpec((tt, q.shape[1]), lambda i: (i, 0)),
                  pl.BlockSpec(wb.shape, lambda i: (0, 0, 0))],
        out_specs=[ospec, ospec, ospec, ospec],
        scratch_shapes=[pltpu.VMEM((2, NK, H, tt), F32), pltpu.VMEM((NK, H, tt), F32),
                        pltpu.VMEM((2, NK, H, tt), F32), pltpu.VMEM((NK * H, tt), F32)],
        compiler_params=_cparams(("parallel",)),
        name="peer_router",
    )(q, wb)


def _gelu_tanh(x):
    return 0.5 * x * (1.0 + jnp.tanh(np.sqrt(2.0 / np.pi) * (x + 0.044715 * (x * x * x))))


def _peer_dense_kernel(xn_ref, u_ref, v_ref, n_ref, e1_ref, r2_ref, e2_ref, x1_ref, g_ref, o_ref, *, final_norm):
    H, NK = PEER_HEADS, PEER_NKEYS
    j = pl.program_id(1)
    na = u_ref.shape[0] // NK

    @pl.when(j == 0)
    def _():
        o_ref[...] = x1_ref[...]

    act = lax.dot_general(u_ref[...], xn_ref[...], NT, preferred_element_type=F32)
    rows = []
    for a in range(na):
        gate = None
        for h in range(H):
            t = jnp.where(r2_ref[h] < n_ref[h, 0, a:a + 1, :], e1_ref[h, 0, a:a + 1, :] * e2_ref[h], 0.0)
            gate = t if gate is None else gate + t
        rows.append(_gelu_tanh(act[a * NK:(a + 1) * NK]) * gate)
    w = jnp.concatenate(rows, axis=0).T.astype(BF16)
    o_ref[...] += jnp.dot(w, v_ref[...], preferred_element_type=F32)

    if final_norm:
        @pl.when(j == pl.num_programs(1) - 1)
        def _():
            x = o_ref[...]
            o_ref[...] = x * lax.rsqrt(jnp.mean(x * x, -1, keepdims=True) + RMS_EPS) * g_ref[...]


def _peer_dense(xn, u, v, nk, e1, r2, e2, x1, g, tt, te, final_norm):
    T, D = xn.shape
    E = u.shape[0]
    H, NK = PEER_HEADS, PEER_NKEYS
    na = te // NK
    nk4 = nk.reshape(H, NK // na, na, T)
    e14 = e1.reshape(H, NK // na, na, T)
    tab_a = pl.BlockSpec((H, 1, na, tt), lambda i, j: (0, j, 0, i))
    tab_b = pl.BlockSpec((H, NK, tt), lambda i, j: (0, 0, i))
    return pl.pallas_call(
        functools.partial(_peer_dense_kernel, final_norm=final_norm),
        out_shape=jax.ShapeDtypeStruct((T, D), F32),
        grid=(T // tt, E // te),
        in_specs=[pl.BlockSpec((tt, D), lambda i, j: (i, 0)),
                  pl.BlockSpec((te, D), lambda i, j: (j, 0)),
                  pl.BlockSpec((te, D), lambda i, j: (j, 0)),
                  tab_a, tab_a, tab_b, tab_b,
                  pl.BlockSpec((tt, D), lambda i, j: (i, 0)),
                  pl.BlockSpec((1, D), lambda i, j: (0, 0))],
        out_specs=pl.BlockSpec((tt, D), lambda i, j: (i, 0)),
        compiler_params=_cparams(("parallel", "arbitrary")),
        name="peer_dense",
    )(xn, u, v, nk4, e14, r2, e2, x1, g.reshape(1, D))


T_TILE = 384
ROUTER_TILE = 128
EXPERT_TILE = 512


def kernel(x_prompt, x_sample, cache_nsa_kv, state_win_kv, state_conv, state_mlstm_C, state_mlstm_n,
           state_mlstm_m, page_table, norm1_g, w_in, conv_w, mlstm_gate_b, mlstm_norm_g, w_out, norm2_g,
           peer_wq, peer_subkeys, peer_u, peer_v, final_norm_g):
    BP, SP, D = x_prompt.shape
    BS, SS, _ = x_sample.shape
    depth = w_in.shape[0]
    KVH, G, HD = NSA_KV_HEADS, NSA_GROUP, HEAD_DIM
    H, DH = MLSTM_HEADS, MLSTM_DH
    CW = conv_w.shape[2]
    NW = KVH * G * HD
    KVW = 6 * KVH * HD
    NG = 3 * KVH * G
    MW = H * DH
    TP, TS = BP * SP, BS * SS
    T = TP + TS
    tile = T_TILE * 2
    TPAD = -(-T // tile) * tile
    n_pages = page_table.shape[1]
    P = cache_nsa_kv.shape[2]
    past = n_pages * P
    wb_rows = state_win_kv.shape[2]

    c_q = 3 * CW
    c_kv = c_q + NW
    c_m = c_kv + KVW
    c_g = c_m + 4 * MW
    ZW = c_g + LANES
    assert CW % LANES == 0 and c_m % MW == 0 and c_g % LANES == 0 and NG + 2 * H <= LANES
    gate_cb = c_g // LANES
    gi, gf = NG, NG + H
    o_gate = 3 * CW + NW + KVW

    x = jnp.concatenate([x_prompt.reshape(TP, D), x_sample.reshape(TS, D),
                         jnp.zeros((TPAD - T, D), F32)], axis=0)
    cache4 = cache_nsa_kv.reshape(depth, cache_nsa_kv.shape[1], P, 4 * KVH * HD)
    eye_h = jnp.eye(PEER_HEADS, dtype=F32)
    dk = peer_subkeys.shape[-1]

    p_st, s_st = [], []
    for l in range(depth):
        wi = w_in[l]
        w_perm = jnp.concatenate(
            [wi[:, :o_gate], wi[:, o_gate + NG:o_gate + NG + 4 * MW], wi[:, o_gate:o_gate + NG],
             wi[:, o_gate + NG + 4 * MW:], jnp.zeros((D, ZW - wi.shape[1]), F32)], axis=1).astype(BF16)
        z = _rms_proj(x, norm1_g[l], w_perm, T_TILE, ZW // 7, F32)

        ya_p, conv_p = _conv(z, 0, BP, SP, jnp.zeros((BP, 2, CW), F32), conv_w[l])
        ya_s, conv_s = _conv(z, TP, BS, SS, state_conv[l], conv_w[l])

        zq = z[:TP, c_q:c_q + NW].reshape(BP, SP, KVH, G, HD)
        q_hm = zq.transpose(0, 2, 3, 1, 4).astype(BF16)
        zkv = z[:TP, c_kv:c_kv + KVW].reshape(BP, SP, 3, 2, KVH, HD)
        kv_hm = zkv.transpose(2, 3, 0, 4, 1, 5)
        kvc_hm = kv_hm[0]
        kvsw_hm = kv_hm[1:].reshape(4, BP, KVH, SP, HD).astype(BF16)
        yb_p = _nsa_prompt(q_hm, kvc_hm, kvsw_hm, z, gate_cb, BP, SP)
        rows_p = zkv[:, :, :2].reshape(BP, SP, 4, KVH, HD)
        win_p = zkv[:, SP - min(WINDOW, SP):, 2]

        zs = z[TP:T]
        q_s = zs[:, c_q:c_q + NW].reshape(BS, SS, KVH, G, HD).transpose(0, 2, 3, 1, 4)
        q_s = q_s.reshape(BS, KVH, G * SS, HD).astype(BF16)
        zkv_s = zs[:, c_kv:c_kv + KVW].reshape(BS, SS, 3, 2, KVH, HD)
        rows_s = zkv_s[:, :, :2].reshape(BS, SS, 4, KVH, HD)
        tail = jnp.pad(rows_s.reshape(BS, SS, 4 * KVH * HD), ((0, 0), (0, P - SS), (0, 0)))
        o_cmp, o_sel = _nsa_sample_global(cache4, l, page_table, tail, q_s, SS)
        w_all = jnp.concatenate([state_win_kv[l], zkv_s[:, :, 2]], axis=1)
        kwin = w_all[:, :, 0].reshape(BS, wb_rows + SS, KVH * HD)
        vwin = w_all[:, :, 1].reshape(BS, wb_rows + SS, KVH * HD)
        yb_s = _nsa_sample_combine(q_s, kwin, vwin, o_cmp, o_sel, z, TP, gate_cb, past, SS)
        win_s = w_all[:, SS:]

        yc_p, C_p, n_p, m_p = _mlstm(z, 0, BP, SP, c_m // MW, gate_cb, gi, gf, mlstm_gate_b[l],
                                     mlstm_norm_g[l], jnp.zeros((BP, H, DH, DH), F32),
                                     jnp.zeros((BP, H, DH), F32), jnp.zeros((BP, H), F32))
        yc_s, C_s, n_s, m_s = _mlstm(z, TP, BS, SS, c_m // MW, gate_cb, gi, gf, mlstm_gate_b[l],
                                     mlstm_norm_g[l], state_mlstm_C[l], state_mlstm_n[l], state_mlstm_m[l])

        ymix = jnp.concatenate(
            [jnp.concatenate([ya_p, yb_p, yc_p], axis=1), jnp.concatenate([ya_s, yb_s, yc_s], axis=1),
             jnp.zeros((TPAD - T, D), F32)], axis=0)
        x1 = _out_proj(x, ymix, w_out[l].astype(BF16), T_TILE)

        nh = PEER_HEADS
        wq_perm = peer_wq[l].reshape(D, nh, 2, dk).transpose(0, 2, 1, 3).reshape(D, 2 * nh * dk).astype(BF16)
        q_peer, xn2 = _rms_proj(x1, norm2_g[l], wq_perm, T_TILE, nh * dk, BF16, emit_xn=True)
        wb = jnp.einsum('hcnd,hg->cnhgd', peer_subkeys[l], eye_h).reshape(2, PEER_NKEYS * nh, nh * dk).astype(BF16)
        nk, e1, r2, e2 = _router(q_peer, wb, ROUTER_TILE)
        x = _peer_dense(xn2, peer_u[l].astype(BF16), peer_v[l].astype(BF16), nk, e1, r2, e2, x1,
                        final_norm_g, T_TILE, EXPERT_TILE, final_norm=(l == depth - 1))

        p_st.append((rows_p, win_p, conv_p, C_p, n_p, m_p))
        s_st.append((rows_s, win_s, conv_s, C_s, n_s, m_s))

    p_rows, p_win, p_conv, p_C, p_n, p_m = [jnp.stack(a) for a in zip(*p_st)]
    s_rows, s_win, s_conv, s_C, s_n, s_m = [jnp.stack(a) for a in zip(*s_st)]
    y_prompt = x[:TP].reshape(BP, SP, D)
    y_sample = x[TP:T].reshape(BS, SS, D)
    return (y_prompt, y_sample, p_rows, p_win, p_conv, p_C, p_n, p_m, s_rows, s_win, s_conv, s_C, s_n, s_m)
```

```python
import functools

import numpy as np
import jax
import jax.numpy as jnp
from jax import lax
from jax.experimental import pallas as pl
from jax.experimental.pallas import tpu as pltpu

F32 = jnp.float32
BF16 = jnp.bfloat16
HI = lax.Precision.HIGHEST

RMS_EPS = 1e-6
HEAD_DIM = 64
NSA_KV_HEADS = 4
NSA_GROUP = 4
CMP_STRIDE = 16
SEL_BLOCK = 64
SEL_TOPN = 16
WINDOW = 512
MLSTM_HEADS = 4
MLSTM_DH = 128
MLSTM_CHUNK = 64
PEER_HEADS = 8
PEER_NKEYS = 128
PEER_TOPK = 16
MASK_BIG = 1e9
NEG = -1e30

LANES = 128
VMEM_LIMIT = 56 * 1024 * 1024

NT = (((1,), (1,)), ((), ()))
TN = (((0,), (0,)), ((), ()))


def _cparams(sem):
    return pltpu.CompilerParams(dimension_semantics=sem, vmem_limit_bytes=VMEM_LIMIT)


def _div(x, d):
    assert d & (d - 1) == 0
    return lax.shift_right_arithmetic(x, jnp.int32(d.bit_length() - 1))


def _masked_softmax(s, mask):
    s = jnp.where(mask, s, NEG)
    m = jnp.max(s, -1, keepdims=True)
    e = jnp.where(mask, jnp.exp(s - m), 0.0)
    d = jnp.maximum(jnp.sum(e, -1, keepdims=True), 1e-30)
    return e * (1.0 / d)


def _rms_proj_kernel(x_ref, g_ref, w_ref, *rest, emit_xn):
    if emit_xn:
        o_ref, xo_ref, xn_ref = rest
    else:
        o_ref, xn_ref = rest

    @pl.when(pl.program_id(1) == 0)
    def _():
        x = x_ref[...]
        r = x * lax.rsqrt(jnp.mean(x * x, -1, keepdims=True) + RMS_EPS)
        xn = (r * g_ref[...]).astype(BF16)
        xn_ref[...] = xn
        if emit_xn:
            xo_ref[...] = xn

    o_ref[...] = jnp.dot(xn_ref[...], w_ref[...], preferred_element_type=F32).astype(o_ref.dtype)


def _rms_proj(x, g, w, tm, tn, out_dtype, emit_xn=False):
    T, D = x.shape
    N = w.shape[1]
    out_shape = [jax.ShapeDtypeStruct((T, N), out_dtype)]
    out_specs = [pl.BlockSpec((tm, tn), lambda i, j: (i, j))]
    if emit_xn:
        out_shape.append(jax.ShapeDtypeStruct((T, D), BF16))
        out_specs.append(pl.BlockSpec((tm, D), lambda i, j: (i, 0)))
    res = pl.pallas_call(
        functools.partial(_rms_proj_kernel, emit_xn=emit_xn),
        out_shape=out_shape,
        grid=(T // tm, N // tn),
        in_specs=[pl.BlockSpec((tm, D), lambda i, j: (i, 0)),
                  pl.BlockSpec((1, D), lambda i, j: (0, 0)),
                  pl.BlockSpec((D, tn), lambda i, j: (0, j))],
        out_specs=out_specs,
        scratch_shapes=[pltpu.VMEM((tm, D), BF16)],
        compiler_params=_cparams(("parallel", "arbitrary")),
        name="rms_proj",
    )(x, g.reshape(1, D), w)
    return res if emit_xn else res[0]


def _conv_kernel(cb_ref, cc_ref, ch_ref, buf_ref, w_ref, y_ref, new_ref, ext_ref, *, S):
    C = cb_ref.shape[-1]
    u = cc_ref[...] * ch_ref[...]
    ext_ref[0:8, :] = jnp.zeros((8, C), F32)
    ext_ref[6:8, :] = buf_ref[0]
    ext_ref[8:8 + S, :] = u
    w = w_ref[...]
    y = w[0:1] * ext_ref[6:6 + S, :] + w[1:2] * ext_ref[7:7 + S, :] + w[2:3] * u
    y_ref[...] = cb_ref[...] * y
    new_ref[0] = u[S - 2:S]


def _conv(z, row0, B, S, buf, w):
    C = w.shape[1]
    rb0 = row0 // S
    return pl.pallas_call(
        functools.partial(_conv_kernel, S=S),
        out_shape=[jax.ShapeDtypeStruct((B * S, C), F32), jax.ShapeDtypeStruct((B, 2, C), F32)],
        grid=(B,),
        in_specs=[pl.BlockSpec((S, C), lambda b: (rb0 + b, 0)),
                  pl.BlockSpec((S, C), lambda b: (rb0 + b, 1)),
                  pl.BlockSpec((S, C), lambda b: (rb0 + b, 2)),
                  pl.BlockSpec((1, 2, C), lambda b: (b, 0, 0)),
                  pl.BlockSpec((3, C), lambda b: (0, 0))],
        out_specs=[pl.BlockSpec((S, C), lambda b: (b, 0)),
                   pl.BlockSpec((1, 2, C), lambda b: (b, 0, 0))],
        scratch_shapes=[pltpu.VMEM((S + 8, C), F32)],
        compiler_params=_cparams(("parallel",)),
        name="short_conv",
    )(z, z, z, buf, w)


def _topn_rank_select(score, n_sel):
    NB = score.shape[1]
    blk = lax.broadcasted_iota(jnp.int32, score.shape, 1)
    rank = jnp.zeros(score.shape, F32)
    for i in range(NB):
        ci = score[:, i:i + 1]
        beats = jnp.where(ci > score, 1.0, jnp.where((ci == score) & (blk > i), 1.0, 0.0))
        rank = rank + beats
    return rank < n_sel


SEL_KCHUNK = 512


def _bias_softmax_pv(q, k, v, bias):
    s = lax.dot_general(q, k, NT, preferred_element_type=F32) + bias
    e = jnp.exp(s - jnp.max(s, -1, keepdims=True))
    o = jnp.dot(e.astype(BF16), v, preferred_element_type=F32)
    return o * (1.0 / jnp.sum(e, -1, keepdims=True))


def _nsa_prompt_kernel(q_ref, kcs_ref, vcs_ref, ks_ref, vs_ref, kw_ref, vw_ref, gt_ref,
                       y_ref, kc_ref, vc_ref, osel_s, *, S, tq):
    kvh = pl.program_id(1)
    qi = pl.program_id(2)
    G = NSA_GROUP
    NC = S // CMP_STRIDE
    NB = S // SEL_BLOCK
    n_sel = min(SEL_TOPN, NB)
    span = min(WINDOW + tq, S)

    @pl.when(qi == 0)
    def _():
        j = lax.broadcasted_iota(jnp.int32, (NC, S), 0)
        r = lax.broadcasted_iota(jnp.int32, (NC, S), 1)
        lo = j * CMP_STRIDE
        pool = jnp.where((r >= lo) & (r < lo + 2 * CMP_STRIDE), 0.5 / CMP_STRIDE, 0.0).astype(F32)
        kc_ref[...] = jnp.dot(pool, kcs_ref[0, 0], precision=HI, preferred_element_type=F32).astype(BF16)
        vc_ref[...] = jnp.dot(pool, vcs_ref[0, 0], precision=HI, preferred_element_type=F32).astype(BF16)

    t0 = qi * tq
    qs = q_ref[0, 0].reshape(G * tq, HEAD_DIM) * jnp.asarray(HEAD_DIM ** -0.5, BF16)
    row = lax.broadcasted_iota(jnp.int32, (G * tq, 1), 0)
    pos = t0 + (row & (tq - 1))
    posq = t0 + lax.broadcasted_iota(jnp.int32, (tq, 1), 0)

    s = lax.dot_general(qs, kc_ref[...], NT, preferred_element_type=F32)
    c_end = lax.broadcasted_iota(jnp.int32, (1, NC), 1) * CMP_STRIDE + (2 * CMP_STRIDE - 1)
    p = _masked_softmax(s, c_end <= pos)
    o_cmp = jnp.dot(p.astype(BF16), vc_ref[...], preferred_element_type=F32)

    imp = p[0:tq]
    for g in range(1, G):
        imp = imp + p[g * tq:(g + 1) * tq]
    per = SEL_BLOCK // CMP_STRIDE
    e4 = jnp.where(_div(lax.broadcasted_iota(jnp.int32, (NC, NB), 0), per)
                   == lax.broadcasted_iota(jnp.int32, (NC, NB), 1), 1.0, 0.0).astype(F32)
    impb = jnp.dot(imp, e4, precision=HI, preferred_element_type=F32)
    blk = lax.broadcasted_iota(jnp.int32, (1, NB), 1)
    qblk = _div(posq, SEL_BLOCK)
    valid = blk * SEL_BLOCK <= posq
    forced = (blk == 0) | (blk == qblk) | (blk == qblk - 1)
    score = jnp.where(forced, MASK_BIG, jnp.where(valid, impb, -MASK_BIG))
    sel = _topn_rank_select(score, n_sel) & (score > -0.5 * MASK_BIG)
    sel01 = jnp.where(sel, 1.0, 0.0).astype(BF16)

    def sel_branch(klen):
        eb = jnp.where(_div(lax.broadcasted_iota(jnp.int32, (NB, klen), 1), SEL_BLOCK)
                       == lax.broadcasted_iota(jnp.int32, (NB, klen), 0), 1.0, 0.0).astype(BF16)
        selk = jnp.dot(sel01, eb, preferred_element_type=F32)
        kpos = lax.broadcasted_iota(jnp.int32, (1, klen), 1)
        bias = jnp.where((selk > 0.5) & (kpos <= posq), 0.0, NEG)
        ks = ks_ref[0, 0, 0:klen, :]
        vs = vs_ref[0, 0, 0:klen, :]
        for g in range(G):
            osel_s[g * tq:(g + 1) * tq, :] = _bias_softmax_pv(qs[g * tq:(g + 1) * tq], ks, vs, bias)

    nvar = -(-S // SEL_KCHUNK)
    per_var = SEL_KCHUNK // tq
    for v in range(nvar):
        @pl.when(qi // per_var == v)
        def _(v=v):
            sel_branch(min(S, (v + 1) * SEL_KCHUNK))

    start = pl.multiple_of(jnp.maximum(t0 + tq - span, 0), tq)
    kw = kw_ref[0, 0, pl.ds(start, span), :]
    vw = vw_ref[0, 0, pl.ds(start, span), :]
    diff = posq - (start + lax.broadcasted_iota(jnp.int32, (1, span), 1))
    win_bias = jnp.where((diff >= 0) & (diff < WINDOW), 0.0, NEG)

    sg = jax.nn.sigmoid(gt_ref[...])
    pick = jnp.where(lax.broadcasted_iota(jnp.int32, (LANES, LANES), 0)
                     == lax.broadcasted_iota(jnp.int32, (LANES, LANES), 1) + kvh * (3 * G), 1.0, 0.0).astype(F32)
    g12 = jnp.dot(sg, pick, precision=HI, preferred_element_type=F32)

    outs = []
    for g in range(G):
        rows = slice(g * tq, (g + 1) * tq)
        o_win = _bias_softmax_pv(qs[rows], kw, vw, win_bias)
        outs.append(g12[:, 3 * g:3 * g + 1] * o_cmp[rows]
                    + g12[:, 3 * g + 1:3 * g + 2] * osel_s[rows, :]
                    + g12[:, 3 * g + 2:3 * g + 3] * o_win)
    y_ref[...] = jnp.concatenate(outs, axis=-1)


def _nsa_prompt(q_hm, kvc_hm, kvsw_hm, z, gate_cb, B, S, tq=128):
    nq = S // tq
    KVH, G, HD = NSA_KV_HEADS, NSA_GROUP, HEAD_DIM
    NC = S // CMP_STRIDE

    def kv_spec(i):
        return pl.BlockSpec((None, 1, 1, S, HD), lambda b, h, t, i=i: (i, b, h, 0, 0))

    return pl.pallas_call(
        functools.partial(_nsa_prompt_kernel, S=S, tq=tq),
        out_shape=jax.ShapeDtypeStruct((B * S, KVH * G * HD), F32),
        grid=(B, KVH, nq),
        in_specs=[pl.BlockSpec((1, 1, G, tq, HD), lambda b, h, t: (b, h, 0, t, 0)),
                  kv_spec(0), kv_spec(1), kv_spec(0), kv_spec(1), kv_spec(2), kv_spec(3),
                  pl.BlockSpec((tq, LANES), lambda b, h, t: (b * nq + t, gate_cb))],
        out_specs=pl.BlockSpec((tq, G * HD), lambda b, h, t: (b * nq + t, h)),
        scratch_shapes=[pltpu.VMEM((NC, HD), BF16), pltpu.VMEM((NC, HD), BF16),
                        pltpu.VMEM((G * tq, HD), F32)],
        compiler_params=_cparams(("parallel", "parallel", "arbitrary")),
        name="nsa_prompt",
    )(q_hm, kvc_hm, kvc_hm, kvsw_hm, kvsw_hm, kvsw_hm, kvsw_hm, z)


def _log_sigmoid(x):
    return jnp.minimum(x, 0.0) - jnp.log(1.0 + jnp.exp(-jnp.abs(x)))


def _mlstm_kernel(gb_ref, q_ref, k_ref, v_ref, o_ref, gt_ref, mg_ref, c0_ref, n0_ref, m0_ref,
                  y_ref, cn_ref, nn_ref, mn_ref, c_s, n_s, m_s, *, L, Lb, gi, gf):
    H, DH = MLSTM_HEADS, MLSTM_DH
    c = pl.program_id(1)

    @pl.when(c == 0)
    def _():
        c_s[...] = c0_ref[0]
        n_s[...] = n0_ref[0]
        m_s[...] = m0_ref[0]

    def padrows(a):
        if Lb == L:
            return a
        return jnp.concatenate([a, jnp.zeros((L - Lb, a.shape[1]), a.dtype)], axis=0)

    lane = lax.broadcasted_iota(jnp.int32, (1, LANES), 1)
    bias = jnp.zeros((1, LANES), F32)
    for h in range(H):
        bias = bias + jnp.where(lane == gi + h, gb_ref[0, h], 0.0) + jnp.where(lane == gf + h, gb_ref[1, h], 0.0)
    is_f = (lane >= gf) & (lane < gf + H)
    pre = padrows(gt_ref[...]) + bias
    gate = jnp.where(is_f, _log_sigmoid(pre), pre)
    if Lb != L:
        live = lax.broadcasted_iota(jnp.int32, (L, 1), 0) < Lb
        gate = jnp.where(live, gate, jnp.where(is_f, 0.0, NEG))
    rr = lax.broadcasted_iota(jnp.int32, (L, L), 0)
    cc = lax.broadcasted_iota(jnp.int32, (L, L), 1)
    tril = rr >= cc
    bcum = jnp.dot(jnp.where(tril, 1.0, 0.0).astype(F32), gate, precision=HI, preferred_element_type=F32)
    e8 = jnp.where(lax.broadcasted_iota(jnp.int32, (8, LANES), 1)
                   == lax.broadcasted_iota(jnp.int32, (8, LANES), 0) + gi, 1.0, 0.0).astype(F32)
    rg = lax.dot_general(e8, gate, NT, precision=HI, preferred_element_type=F32)
    rb = lax.dot_general(e8, bcum, NT, precision=HI, preferred_element_type=F32)

    q = padrows(q_ref[...])
    k = padrows(k_ref[...])
    v = padrows(v_ref[...])
    og = padrows(o_ref[...])
    ys = []
    for h in range(H):
        sl = slice(h * DH, (h + 1) * DH)
        qq = q[:, sl]
        kk = k[:, sl] * (DH ** -0.5)
        vv = v[:, sl]
        qb, kb, vb = qq.astype(BF16), kk.astype(BF16), vv.astype(BF16)
        b_col = bcum[:, gf + h:gf + h + 1]
        b_row = rb[H + h:H + h + 1, :]
        i_row = rg[h:h + 1, :]
        i_col = gate[:, gi + h:gi + h + 1]
        m_prev = m_s[h][:, 0:1]
        cmat = c_s[h]
        n_row = n_s[h]
        dmat = jnp.where(tril, b_col - b_row + i_row, NEG)
        inter = b_col + m_prev
        mt = jnp.maximum(inter, jnp.max(dmat, -1, keepdims=True))
        w = jnp.exp(dmat - mt)
        a = jnp.exp(inter - mt)
        wqk = w * lax.dot_general(qb, kb, NT, preferred_element_type=F32)
        num = (a * jnp.dot(qb, cmat.astype(BF16), preferred_element_type=F32)
               + jnp.dot(wqk.astype(BF16), vb, preferred_element_type=F32))
        den = a * jnp.sum(qq * n_row, -1, keepdims=True) + jnp.sum(wqk, -1, keepdims=True)
        hh = num * (1.0 / jnp.maximum(jnp.abs(den), jnp.exp(-mt)))
        b_last = b_col[L - 1:L]
        m_new = mt[L - 1:L]
        wl = jnp.exp(b_last - b_col + i_col - m_new)
        decay = jnp.exp(b_last + m_prev - m_new)
        kw = wl * kk
        c_s[h] = decay * cmat + lax.dot_general(kw.astype(BF16), vb, TN, preferred_element_type=F32)
        n_s[h] = decay * n_row + jnp.sum(kw, 0, keepdims=True)
        m_s[h] = jnp.broadcast_to(m_new, (1, LANES))
        hn = hh * lax.rsqrt(jnp.mean(hh * hh, -1, keepdims=True) + RMS_EPS)
        ys.append(jax.nn.sigmoid(og[:, sl]) * hn * mg_ref[:, sl])
    y = jnp.concatenate(ys, axis=-1)
    y_ref[...] = y[0:Lb]

    @pl.when(c == pl.num_programs(1) - 1)
    def _():
        cn_ref[0] = c_s[...]
        nn_ref[0] = n_s[...]
        mn_ref[0] = m_s[...]


def _mlstm(z, row0, B, S, col_q, gate_cb, gi, gf, gate_b, mnorm_g, C0, n0, m0):
    H, DH = MLSTM_HEADS, MLSTM_DH
    W = H * DH
    L = MLSTM_CHUNK
    Lb = L if S % L == 0 else S
    assert Lb <= L
    nc = S // Lb
    rb0 = row0 // Lb
    n0 = n0.reshape(B, H, 1, DH)
    m0 = jnp.broadcast_to(m0.reshape(B, H, 1, 1), (B, H, 1, LANES))

    def zspec(cb, width):
        return pl.BlockSpec((Lb, width), lambda b, c, cb=cb: (rb0 + b * nc + c, cb))

    y, C, n, m = pl.pallas_call(
        functools.partial(_mlstm_kernel, L=L, Lb=Lb, gi=gi, gf=gf),
        out_shape=[jax.ShapeDtypeStruct((B * S, W), F32),
                   jax.ShapeDtypeStruct((B, H, DH, DH), F32),
                   jax.ShapeDtypeStruct((B, H, 1, DH), F32),
                   jax.ShapeDtypeStruct((B, H, 1, LANES), F32)],
        grid=(B, nc),
        in_specs=[pl.BlockSpec(memory_space=pltpu.SMEM),
                  zspec(col_q, W), zspec(col_q + 1, W), zspec(col_q + 2, W), zspec(col_q + 3, W),
                  zspec(gate_cb, LANES),
                  pl.BlockSpec((1, W), lambda b, c: (0, 0)),
                  pl.BlockSpec((1, H, DH, DH), lambda b, c: (b, 0, 0, 0)),
                  pl.BlockSpec((1, H, 1, DH), lambda b, c: (b, 0, 0, 0)),
                  pl.BlockSpec((1, H, 1, LANES), lambda b, c: (b, 0, 0, 0))],
        out_specs=[pl.BlockSpec((Lb, W), lambda b, c: (b * nc + c, 0)),
                   pl.BlockSpec((1, H, DH, DH), lambda b, c: (b, 0, 0, 0)),
                   pl.BlockSpec((1, H, 1, DH), lambda b, c: (b, 0, 0, 0)),
                   pl.BlockSpec((1, H, 1, LANES), lambda b, c: (b, 0, 0, 0))],
        scratch_shapes=[pltpu.VMEM((H, DH, DH), F32), pltpu.VMEM((H, 1, DH), F32),
                        pltpu.VMEM((H, 1, LANES), F32)],
        compiler_params=_cparams(("parallel", "arbitrary")),
        name="mlstm",
    )(gate_b, z, z, z, z, z, mnorm_g.reshape(1, W), C0, n0, m0)
    return y, C, n.reshape(B, H, DH), m[:, :, 0, 0]


PAGES_PER_STEP = 16


def _lane_extract_topn(score, n_sel, floor):
    lane = lax.broadcasted_iota(jnp.int32, score.shape, 1).astype(F32)
    sel = jnp.zeros(score.shape, F32)
    sc = score
    for _ in range(n_sel):
        m = jnp.max(sc, -1, keepdims=True)
        idx = jnp.min(jnp.where(sc == m, lane, float(score.shape[1])), -1, keepdims=True)
        hit = lane == idx
        sel = jnp.where(hit & (m > floor), 1.0, sel)
        sc = jnp.where(hit, -jnp.inf, sc)
    return sel


def _s1_kernel(pt_ref, *refs, past, S, ncp, nbp):
    pgs = refs[:PAGES_PER_STEP]
    tail_ref, q_ref, ocmp_ref, sel_ref, sub_ref = refs[PAGES_PER_STEP:]
    P = pgs[0].shape[-1]
    j = pl.program_id(1)
    nfull = pl.num_programs(1) - 1
    KVH, G, HD = NSA_KV_HEADS, NSA_GROUP, HEAD_DIM
    cols_step = PAGES_PER_STEP * P // CMP_STRIDE
    assert cols_step == LANES

    @pl.when(j == 0)
    def _():
        sub_ref[...] = jnp.zeros(sub_ref.shape, F32)

    def pool_t(n):
        return jnp.where(_div(lax.broadcasted_iota(jnp.int32, (n, LANES), 0), CMP_STRIDE)
                         == lax.broadcasted_iota(jnp.int32, (n, LANES), 1),
                         1.0 / CMP_STRIDE, 0.0).astype(F32)

    @pl.when(j < nfull)
    def _():
        pt = pool_t(PAGES_PER_STEP * P)
        col = pl.ds(pl.multiple_of(j * cols_step, cols_step), cols_step)
        for k in range(2):
            for h in range(KVH):
                xt = jnp.concatenate([r[0, 0, k, h] for r in pgs], axis=1)
                sub_ref[k, h, :, col] = jnp.dot(xt, pt, precision=HI, preferred_element_type=F32)

    @pl.when(j == nfull)
    def _():
        base = past // CMP_STRIDE
        pt = pool_t(P)
        for k in range(2):
            for h in range(KVH):
                sub_ref[k, h, :, base:base + LANES] = jnp.dot(tail_ref[0, k, h], pt, precision=HI,
                                                              preferred_element_type=F32)
        R = G * S
        row = lax.broadcasted_iota(jnp.int32, (R, 1), 0)
        pos = past + (row & (S - 1))
        posq = past + lax.broadcasted_iota(jnp.int32, (S, 1), 0)
        c_end = lax.broadcasted_iota(jnp.int32, (1, ncp), 1) * CMP_STRIDE + (2 * CMP_STRIDE - 1)
        cmask = c_end <= pos
        per = SEL_BLOCK // CMP_STRIDE
        e4 = jnp.where(_div(lax.broadcasted_iota(jnp.int32, (ncp, nbp), 0), per)
                       == lax.broadcasted_iota(jnp.int32, (ncp, nbp), 1), 1.0, 0.0).astype(F32)
        blk = lax.broadcasted_iota(jnp.int32, (1, nbp), 1)
        qblk = _div(posq, SEL_BLOCK)
        valid = blk * SEL_BLOCK <= posq
        forced = (blk == 0) | (blk == qblk) | (blk == qblk - 1)
        for h in range(KVH):
            kct = 0.5 * (sub_ref[0, h, :, 0:ncp] + sub_ref[0, h, :, 1:ncp + 1])
            vct = 0.5 * (sub_ref[1, h, :, 0:ncp] + sub_ref[1, h, :, 1:ncp + 1])
            qs = q_ref[0, h] * jnp.asarray(HD ** -0.5, BF16)
            s = jnp.dot(qs, kct.astype(BF16), preferred_element_type=F32)
            p = _masked_softmax(s, cmask)
            ocmp_ref[0, h] = lax.dot_general(p.astype(BF16), vct.astype(BF16), NT, preferred_element_type=F32)
            imp = p[0:S]
            for g in range(1, G):
                imp = imp + p[g * S:(g + 1) * S]
            impb = jnp.dot(imp, e4, precision=HI, preferred_element_type=F32)
            score = jnp.where(forced, MASK_BIG, jnp.where(valid, impb, -MASK_BIG))
            sel_ref[0, h] = _lane_extract_topn(score, SEL_TOPN, -0.5 * MASK_BIG)


def _s2_kernel(pt_ref, *refs, past, S, nbp):
    pgs = refs[:PAGES_PER_STEP]
    tail_ref, q_ref, sel_ref, osel_ref, m_s, l_s, acc_s = refs[PAGES_PER_STEP:]
    P = pgs[0].shape[-1]
    j = pl.program_id(1)
    nfull = pl.num_programs(1) - 1
    KVH, G, HD = NSA_KV_HEADS, NSA_GROUP, HEAD_DIM
    R = G * S

    @pl.when(j == 0)
    def _():
        m_s[...] = jnp.full(m_s.shape, NEG, F32)
        l_s[...] = jnp.zeros(l_s.shape, F32)
        acc_s[...] = jnp.zeros(acc_s.shape, F32)

    row = lax.broadcasted_iota(jnp.int32, (R, 1), 0)
    pos = past + (row & (S - 1))

    def process(kt_of, vt_of, n, kpos0):
        kpos = kpos0 + lax.broadcasted_iota(jnp.int32, (1, n), 1)
        kblk = _div(kpos, SEL_BLOCK)
        esel = jnp.where(lax.broadcasted_iota(jnp.int32, (nbp, n), 0) == kblk, 1.0, 0.0).astype(BF16)
        causal = kpos <= pos
        for h in range(KVH):
            kt = kt_of(h).astype(BF16)
            vt = vt_of(h).astype(BF16)
            qs = q_ref[0, h] * jnp.asarray(HD ** -0.5, BF16)
            s = jnp.dot(qs, kt, preferred_element_type=F32)
            mk = jnp.dot(sel_ref[0, h].astype(BF16), esel, preferred_element_type=F32)
            mask = (jnp.concatenate([mk] * G, axis=0) > 0.5) & causal
            sm = jnp.where(mask, s, NEG)
            m_old = m_s[h]
            m_new = jnp.maximum(m_old, jnp.max(sm, -1, keepdims=True))
            alpha = jnp.exp(m_old - m_new)
            e = jnp.where(mask, jnp.exp(sm - m_new), 0.0)
            l_s[h] = alpha * l_s[h] + jnp.sum(e, -1, keepdims=True)
            acc_s[h] = alpha * acc_s[h] + lax.dot_general(e.astype(BF16), vt, NT, preferred_element_type=F32)
            m_s[h] = m_new

    @pl.when(j < nfull)
    def _():
        process(lambda h: jnp.concatenate([r[0, 0, 0, h] for r in pgs], axis=1),
                lambda h: jnp.concatenate([r[0, 0, 1, h] for r in pgs], axis=1),
                PAGES_PER_STEP * P, j * (PAGES_PER_STEP * P))

    @pl.when(j == nfull)
    def _():
        process(lambda h: tail_ref[0, 0, h], lambda h: tail_ref[0, 1, h], P, past)
        for h in range(KVH):
            osel_ref[0, h] = acc_s[h] * (1.0 / jnp.maximum(l_s[h], 1e-30))


def _page_specs(layer, half, n_pages, P):
    def spec(r):
        return pl.BlockSpec(
            (1, 1, 2, NSA_KV_HEADS, HEAD_DIM, P),
            lambda b, j, pt, r=r: (layer, pt[b, jnp.minimum(j * PAGES_PER_STEP + r, n_pages - 1)],
                                   half, 0, 0, 0))
    return [spec(r) for r in range(PAGES_PER_STEP)]


def _nsa_sample_global(cache_t, layer, page_table, tail_t, q_s, S):
    B, n_pages = page_table.shape
    P = cache_t.shape[-1]
    KVH, G, HD = NSA_KV_HEADS, NSA_GROUP, HEAD_DIM
    past = n_pages * P
    assert n_pages % PAGES_PER_STEP == 0 and S & (S - 1) == 0 and S <= SEL_BLOCK
    nsteps = n_pages // PAGES_PER_STEP + 1
    ncp = -(-(past + P) // CMP_STRIDE // LANES) * LANES
    nbp = -(-(past + P) // SEL_BLOCK // LANES) * LANES
    R = G * S
    qspec = pl.BlockSpec((1, KVH, R, HD), lambda b, j, pt: (b, 0, 0, 0))
    o_cmp, sel = pl.pallas_call(
        functools.partial(_s1_kernel, past=past, S=S, ncp=ncp, nbp=nbp),
        out_shape=[jax.ShapeDtypeStruct((B, KVH, R, HD), F32), jax.ShapeDtypeStruct((B, KVH, S, nbp), F32)],
        grid_spec=pltpu.PrefetchScalarGridSpec(
            num_scalar_prefetch=1, grid=(B, nsteps),
            in_specs=_page_specs(layer, 0, n_pages, P)
            + [pl.BlockSpec((1, 2, KVH, HD, P), lambda b, j, pt: (b, 0, 0, 0, 0)), qspec],
            out_specs=[pl.BlockSpec((1, KVH, R, HD), lambda b, j, pt: (b, 0, 0, 0)),
                       pl.BlockSpec((1, KVH, S, nbp), lambda b, j, pt: (b, 0, 0, 0))],
            scratch_shapes=[pltpu.VMEM((2, KVH, HD, ncp + LANES), F32)]),
        compiler_params=_cparams(("parallel", "arbitrary")),
        name="nsa_sample_cmp",
    )(page_table, *([cache_t] * PAGES_PER_STEP), tail_t, q_s)
    o_sel = pl.pallas_call(
        functools.partial(_s2_kernel, past=past, S=S, nbp=nbp),
        out_shape=jax.ShapeDtypeStruct((B, KVH, R, HD), F32),
        grid_spec=pltpu.PrefetchScalarGridSpec(
            num_scalar_prefetch=1, grid=(B, nsteps),
            in_specs=_page_specs(layer, 1, n_pages, P)
            + [pl.BlockSpec((1, 2, KVH, HD, P), lambda b, j, pt: (b, 1, 0, 0, 0)), qspec,
               pl.BlockSpec((1, KVH, S, nbp), lambda b, j, pt: (b, 0, 0, 0))],
            out_specs=pl.BlockSpec((1, KVH, R, HD), lambda b, j, pt: (b, 0, 0, 0)),
            scratch_shapes=[pltpu.VMEM((KVH, R, 1), F32), pltpu.VMEM((KVH, R, 1), F32),
                            pltpu.VMEM((KVH, R, HD), F32)]),
        compiler_params=_cparams(("parallel", "arbitrary")),
        name="nsa_sample_sel",
    )(page_table, *([cache_t] * PAGES_PER_STEP), tail_t, q_s, sel)
    return o_cmp, o_sel


def _s3_kernel(q_ref, kw_ref, vw_ref, ocmp_ref, osel_ref, gt_ref, y_ref, *, past, S):
    KVH, G, HD = NSA_KV_HEADS, NSA_GROUP, HEAD_DIM
    R = G * S
    nk = kw_ref.shape[1]
    row = lax.broadcasted_iota(jnp.int32, (R, 1), 0)
    pos = past + (row & (S - 1))
    kpos = past - (nk - S) + lax.broadcasted_iota(jnp.int32, (1, nk), 1)
    diff = pos - kpos
    wmask = (diff >= 0) & (diff < WINDOW)
    sg = jax.nn.sigmoid(gt_ref[...])
    kw = kw_ref[0].astype(BF16)
    vw = vw_ref[0].astype(BF16)
    outs = []
    for h in range(KVH):
        qs = q_ref[0, h] * jnp.asarray(HD ** -0.5, BF16)
        s = lax.dot_general(qs, kw[:, h * HD:(h + 1) * HD], NT, preferred_element_type=F32)
        p = _masked_softmax(s, wmask)
        o_win = jnp.dot(p.astype(BF16), vw[:, h * HD:(h + 1) * HD], preferred_element_type=F32)
        oc = ocmp_ref[0, h]
        osl = osel_ref[0, h]
        for g in range(G):
            c0 = (h * G + g) * 3
            rs = slice(g * S, (g + 1) * S)
            outs.append(sg[:, c0:c0 + 1] * oc[rs] + sg[:, c0 + 1:c0 + 2] * osl[rs]
                        + sg[:, c0 + 2:c0 + 3] * o_win[rs])
    y_ref[...] = jnp.concatenate(outs, axis=-1)


def _nsa_sample_combine(q_s, kwin, vwin, o_cmp, o_sel, z, row0, gate_cb, past, S):
    B, KVH, R, HD = q_s.shape
    nk = kwin.shape[1]
    W = KVH * HD
    ospec = pl.BlockSpec((1, KVH, R, HD), lambda b: (b, 0, 0, 0))
    return pl.pallas_call(
        functools.partial(_s3_kernel, past=past, S=S),
        out_shape=jax.ShapeDtypeStruct((B * S, KVH * NSA_GROUP * HD), F32),
        grid=(B,),
        in_specs=[ospec,
                  pl.BlockSpec((1, nk, W), lambda b: (b, 0, 0)),
                  pl.BlockSpec((1, nk, W), lambda b: (b, 0, 0)),
                  ospec, ospec,
                  pl.BlockSpec((S, LANES), lambda b: (row0 // S + b, gate_cb))],
        out_specs=pl.BlockSpec((S, KVH * NSA_GROUP * HD), lambda b: (b, 0)),
        compiler_params=_cparams(("parallel",)),
        name="nsa_sample_win",
    )(q_s, kwin, vwin, o_cmp, o_sel, z)


def _out_proj_kernel(x_ref, y_ref, w_ref, o_ref):
    o_ref[...] = x_ref[...] + jnp.dot(y_ref[...].astype(BF16), w_ref[...], preferred_element_type=F32)


def _out_proj(x, y, w, tm):
    T, D = x.shape
    K = y.shape[1]
    return pl.pallas_call(
        _out_proj_kernel,
        out_shape=jax.ShapeDtypeStruct((T, D), F32),
        grid=(T // tm,),
        in_specs=[pl.BlockSpec((tm, D), lambda i: (i, 0)),
                  pl.BlockSpec((tm, K), lambda i: (i, 0)),
                  pl.BlockSpec((K, D), lambda i: (0, 0))],
        out_specs=pl.BlockSpec((tm, D), lambda i: (i, 0)),
        compiler_params=_cparams(("parallel",)),
        name="out_proj",
    )(x, y, w)


def _structural_pairs(k):
    return [(i, j) for i in range(k) for j in range(k) if (i + 1) * (j + 1) <= k]


def _router_kernel(q_ref, wb_ref, n_out, e1_out, r2_out, e2_out, s_ref, cur_ref, rk_ref, tmp_ref):
    H, NK, K = PEER_HEADS, PEER_NKEYS, PEER_TOPK
    Tt = q_ref.shape[0]
    half = q_ref.shape[1] // 2
    key = lax.broadcasted_iota(jnp.int32, (NK, H, Tt), 0).astype(F32)
    tops = []
    for c in range(2):
        s = lax.dot_general(wb_ref[c], q_ref[:, c * half:(c + 1) * half], NT, preferred_element_type=F32)
        s_ref[c] = s.reshape(NK, H, Tt)
        cur_ref[...] = s_ref[c]
        rk_ref[c] = jnp.full((NK, H, Tt), float(NK), F32)
        vals = []
        for k in range(K):
            cur = cur_ref[...]
            m = jnp.max(cur, axis=0)
            idx = jnp.min(jnp.where(cur == m[None], key, float(NK)), axis=0)
            hit = key == idx[None]
            rk_ref[c] = jnp.where(hit, float(k), rk_ref[c])
            cur_ref[...] = jnp.where(hit, -jnp.inf, cur)
            vals.append(m)
        tops.append(vals)
    v1, v2 = tops

    pairs = _structural_pairs(K)
    cand = [v1[i] + v2[j] for (i, j) in pairs]
    n = len(pairs)
    rank = []
    for p in range(n):
        rank.append(jnp.zeros((H, Tt), F32))
    for p in range(n):
        ip, jp = pairs[p]
        for q in range(p + 1, n):
            iq, jq = pairs[q]
            if ip <= iq and jp <= jq:
                rank[q] = rank[q] + 1.0
            else:
                b = jnp.where(cand[p] >= cand[q], 1.0, 0.0)
                rank[q] = rank[q] + b
                rank[p] = rank[p] + (1.0 - b)
    sel = [jnp.where(r < K, 1.0, 0.0) for r in rank]
    e1 = [jnp.exp(v1[i] - v1[0]) for i in range(K)]
    e2 = [jnp.exp(v2[j] - v2[0]) for j in range(K)]
    cnt = [jnp.zeros((H, Tt), F32) for _ in range(K)]
    zsum = jnp.zeros((H, Tt), F32)
    for p, (i, j) in enumerate(pairs):
        cnt[i] = cnt[i] + sel[p]
        zsum = zsum + sel[p] * (e1[i] * e2[j])
    inv_z = 1.0 / zsum

    rk1 = rk_ref[0]
    nk = jnp.zeros((NK, H, Tt), F32)
    for i in range(K):
        nk = jnp.where(rk1 == float(i), cnt[i][None], nk)
    outs = ((n_out, nk),
            (e1_out, jnp.exp(s_ref[0] - v1[0][None]) * inv_z[None]),
            (r2_out, rk_ref[1]),
            (e2_out, jnp.exp(s_ref[1] - v2[0][None])))
    for ref, val in outs:
        tmp_ref[...] = val.reshape(NK * H, Tt)
        for h in range(H):
            ref[h] = tmp_ref[pl.ds(h, NK, stride=H), :].astype(ref.dtype)


def _router(q, wb, tt):
    T = q.shape[0]
    H, NK = PEER_HEADS, PEER_NKEYS
    shp = jax.ShapeDtypeStruct((H, NK, T), F32)
    shp_b = jax.ShapeDtypeStruct((H, NK, T), BF16)
    ospec = pl.BlockSpec((H, NK, tt), lambda i: (0, 0, i))
    return pl.pallas_call(
        _router_kernel,
        out_shape=[shp, shp, shp_b, shp_b],
        grid=(T // tt,),
        in_specs=[pl.BlockSpec((tt, q.shape[1]), lambda i: (i, 0)),
                  pl.BlockSpec(wb.shape, lambda i: (0, 0, 0))],
        out_specs=[ospec, ospec, ospec, ospec],
        scratch_shapes=[pltpu.VMEM((2, NK, H, tt), F32), pltpu.VMEM((NK, H, tt), F32),
                        pltpu.VMEM((2, NK, H, tt), F32), pltpu.VMEM((NK * H, tt), F32)],
        compiler_params=_cparams(("parallel",)),
        name="peer_router",
    )(q, wb)


def _gelu_tanh(x):
    return 0.5 * x * (1.0 + jnp.tanh(np.sqrt(2.0 / np.pi) * (x + 0.044715 * (x * x * x))))


PEER_ACHUNK = 2


def _peer_dense_kernel(xn_ref, u_ref, v_ref, n_ref, e1_ref, r2_ref, e2_ref, x1_ref, g_ref, o_ref, *, final_norm):
    H, NK = PEER_HEADS, PEER_NKEYS
    j = pl.program_id(1)
    na = u_ref.shape[0] // NK
    tt = xn_ref.shape[0]

    @pl.when(j == 0)
    def _():
        o_ref[...] = x1_ref[...]

    xn = xn_ref[...]
    parts = []
    for c in range(na // PEER_ACHUNK):
        rows = slice(c * PEER_ACHUNK * NK, (c + 1) * PEER_ACHUNK * NK)
        act = lax.dot_general(u_ref[rows, :], xn, NT, preferred_element_type=F32)
        gates = []
        for a in range(c * PEER_ACHUNK, (c + 1) * PEER_ACHUNK):
            row = pl.ds(j * na + a, 1)
            gate = None
            for h in range(H):
                t = jnp.where(r2_ref[h] < n_ref[h, row, :].astype(BF16),
                              e1_ref[h, row, :].astype(BF16) * e2_ref[h], jnp.zeros((), BF16))
                gate = t if gate is None else gate + t
            gates.append(gate)
        parts.append(_gelu_tanh(act).astype(BF16) * jnp.concatenate(gates, axis=0))
    wt = jnp.concatenate(parts, axis=0)
    o_ref[...] += lax.dot_general(wt, v_ref[...], TN, preferred_element_type=F32)

    if final_norm:
        @pl.when(j == pl.num_programs(1) - 1)
        def _():
            x = o_ref[...]
            o_ref[...] = x * lax.rsqrt(jnp.mean(x * x, -1, keepdims=True) + RMS_EPS) * g_ref[...]


def _peer_dense(xn, u, v, nk, e1, r2, e2, x1, g, tt, te, final_norm):
    T, D = xn.shape
    E = u.shape[0]
    H, NK = PEER_HEADS, PEER_NKEYS
    once = pl.Buffered(1)
    tab = pl.BlockSpec((H, NK, tt), lambda i, j: (0, 0, i), pipeline_mode=once)
    return pl.pallas_call(
        functools.partial(_peer_dense_kernel, final_norm=final_norm),
        out_shape=jax.ShapeDtypeStruct((T, D), F32),
        grid=(T // tt, E // te),
        in_specs=[pl.BlockSpec((tt, D), lambda i, j: (i, 0), pipeline_mode=once),
                  pl.BlockSpec((te, D), lambda i, j: (j, 0)),
                  pl.BlockSpec((te, D), lambda i, j: (j, 0)),
                  tab, tab, tab, tab,
                  pl.BlockSpec((tt, D), lambda i, j: (i, 0), pipeline_mode=once),
                  pl.BlockSpec((1, D), lambda i, j: (0, 0))],
        out_specs=pl.BlockSpec((tt, D), lambda i, j: (i, 0)),
        compiler_params=_cparams(("parallel", "arbitrary")),
        name="peer_dense",
    )(xn, u, v, nk, e1, r2, e2, x1, g.reshape(1, D))


T_TILE = 384
ROUTER_TILE = 128
PEER_T_TILE = 384
PAD_TILE = 768
EXPERT_TILE = 1024


def kernel(x_prompt, x_sample, cache_nsa_kv, state_win_kv, state_conv, state_mlstm_C, state_mlstm_n,
           state_mlstm_m, page_table, norm1_g, w_in, conv_w, mlstm_gate_b, mlstm_norm_g, w_out, norm2_g,
           peer_wq, peer_subkeys, peer_u, peer_v, final_norm_g):
    BP, SP, D = x_prompt.shape
    BS, SS, _ = x_sample.shape
    depth = w_in.shape[0]
    KVH, G, HD = NSA_KV_HEADS, NSA_GROUP, HEAD_DIM
    H, DH = MLSTM_HEADS, MLSTM_DH
    CW = conv_w.shape[2]
    NW = KVH * G * HD
    KVW = 6 * KVH * HD
    NG = 3 * KVH * G
    MW = H * DH
    TP, TS = BP * SP, BS * SS
    T = TP + TS
    assert PAD_TILE % T_TILE == 0 and PAD_TILE % ROUTER_TILE == 0 and PAD_TILE % PEER_T_TILE == 0
    TPAD = -(-T // PAD_TILE) * PAD_TILE
    n_pages = page_table.shape[1]
    P = cache_nsa_kv.shape[2]
    past = n_pages * P
    wb_rows = state_win_kv.shape[2]

    c_q = 3 * CW
    c_kv = c_q + NW
    c_m = c_kv + KVW
    c_g = c_m + 4 * MW
    ZW = c_g + LANES
    assert CW % LANES == 0 and c_m % MW == 0 and c_g % LANES == 0 and NG + 2 * H <= LANES
    gate_cb = c_g // LANES
    gi, gf = NG, NG + H
    o_gate = 3 * CW + NW + KVW

    x = jnp.concatenate([x_prompt.reshape(TP, D), x_sample.reshape(TS, D),
                         jnp.zeros((TPAD - T, D), F32)], axis=0)
    cache_t = jnp.transpose(cache_nsa_kv, (0, 1, 3, 4, 5, 2))
    eye_h = jnp.eye(PEER_HEADS, dtype=F32)
    dk = peer_subkeys.shape[-1]

    p_st, s_st = [], []
    for l in range(depth):
        wi = w_in[l]
        w_perm = jnp.concatenate(
            [wi[:, :o_gate], wi[:, o_gate + NG:o_gate + NG + 4 * MW], wi[:, o_gate:o_gate + NG],
             wi[:, o_gate + NG + 4 * MW:], jnp.zeros((D, ZW - wi.shape[1]), F32)], axis=1).astype(BF16)
        z = _rms_proj(x, norm1_g[l], w_perm, T_TILE, ZW // 7, F32)

        ya_p, conv_p = _conv(z, 0, BP, SP, jnp.zeros((BP, 2, CW), F32), conv_w[l])
        ya_s, conv_s = _conv(z, TP, BS, SS, state_conv[l], conv_w[l])

        zq = z[:TP, c_q:c_q + NW].reshape(BP, SP, KVH, G, HD)
        q_hm = zq.transpose(0, 2, 3, 1, 4).astype(BF16)
        zkv = z[:TP, c_kv:c_kv + KVW].reshape(BP, SP, 3, 2, KVH, HD)
        kv_hm = zkv.transpose(2, 3, 0, 4, 1, 5)
        kvc_hm = kv_hm[0]
        kvsw_hm = kv_hm[1:].reshape(4, BP, KVH, SP, HD).astype(BF16)
        yb_p = _nsa_prompt(q_hm, kvc_hm, kvsw_hm, z, gate_cb, BP, SP)
        rows_p = zkv[:, :, :2].reshape(BP, SP, 4, KVH, HD)
        win_p = zkv[:, SP - min(WINDOW, SP):, 2]

        zs = z[TP:T]
        q_s = zs[:, c_q:c_q + NW].reshape(BS, SS, KVH, G, HD).transpose(0, 2, 3, 1, 4)
        q_s = q_s.reshape(BS, KVH, G * SS, HD).astype(BF16)
        zkv_s = zs[:, c_kv:c_kv + KVW].reshape(BS, SS, 3, 2, KVH, HD)
        rows_s = zkv_s[:, :, :2].reshape(BS, SS, 4, KVH, HD)
        tail_t = jnp.pad(rows_s.transpose(0, 2, 3, 4, 1), ((0, 0), (0, 0), (0, 0), (0, 0), (0, P - SS)))
        o_cmp, o_sel = _nsa_sample_global(cache_t, l, page_table, tail_t, q_s, SS)
        w_all = jnp.concatenate([state_win_kv[l], zkv_s[:, :, 2]], axis=1)
        kwin = w_all[:, :, 0].reshape(BS, wb_rows + SS, KVH * HD)
        vwin = w_all[:, :, 1].reshape(BS, wb_rows + SS, KVH * HD)
        yb_s = _nsa_sample_combine(q_s, kwin, vwin, o_cmp, o_sel, z, TP, gate_cb, past, SS)
        win_s = w_all[:, SS:]

        yc_p, C_p, n_p, m_p = _mlstm(z, 0, BP, SP, c_m // MW, gate_cb, gi, gf, mlstm_gate_b[l],
                                     mlstm_norm_g[l], jnp.zeros((BP, H, DH, DH), F32),
                                     jnp.zeros((BP, H, DH), F32), jnp.zeros((BP, H), F32))
        yc_s, C_s, n_s, m_s = _mlstm(z, TP, BS, SS, c_m // MW, gate_cb, gi, gf, mlstm_gate_b[l],
                                     mlstm_norm_g[l], state_mlstm_C[l], state_mlstm_n[l], state_mlstm_m[l])

        ymix = jnp.concatenate(
            [jnp.concatenate([ya_p, yb_p, yc_p], axis=1), jnp.concatenate([ya_s, yb_s, yc_s], axis=1),
             jnp.zeros((TPAD - T, D), F32)], axis=0)
        x1 = _out_proj(x, ymix, w_out[l].astype(BF16), T_TILE)

        nh = PEER_HEADS
        wq_perm = peer_wq[l].reshape(D, nh, 2, dk).transpose(0, 2, 1, 3).reshape(D, 2 * nh * dk).astype(BF16)
        q_peer, xn2 = _rms_proj(x1, norm2_g[l], wq_perm, T_TILE, nh * dk, BF16, emit_xn=True)
        wb = jnp.einsum('hcnd,hg->cnhgd', peer_subkeys[l], eye_h).reshape(2, PEER_NKEYS * nh, nh * dk).astype(BF16)
        nk, e1, r2, e2 = _router(q_peer, wb, ROUTER_TILE)
        x = _peer_dense(xn2, peer_u[l].astype(BF16), peer_v[l].astype(BF16), nk, e1, r2, e2, x1,
                        final_norm_g, PEER_T_TILE, EXPERT_TILE, final_norm=(l == depth - 1))

        p_st.append((rows_p, win_p, conv_p, C_p, n_p, m_p))
        s_st.append((rows_s, win_s, conv_s, C_s, n_s, m_s))

    p_rows, p_win, p_conv, p_C, p_n, p_m = [jnp.stack(a) for a in zip(*p_st)]
    s_rows, s_win, s_conv, s_C, s_n, s_m = [jnp.stack(a) for a in zip(*s_st)]
    y_prompt = x[:TP].reshape(BP, SP, D)
    y_sample = x[TP:T].reshape(BS, SS, D)
    return (y_prompt, y_sample, p_rows, p_win, p_conv, p_C, p_n, p_m, s_rows, s_win, s_conv, s_C, s_n, s_m)
```

```python
import functools

import numpy as np
import jax
import jax.numpy as jnp
from jax import lax
from jax.experimental import pallas as pl
from jax.experimental.pallas import tpu as pltpu

F32 = jnp.float32
BF16 = jnp.bfloat16
HI = lax.Precision.HIGHEST

RMS_EPS = 1e-6
HEAD_DIM = 64
NSA_KV_HEADS = 4
NSA_GROUP = 4
CMP_STRIDE = 16
SEL_BLOCK = 64
SEL_TOPN = 16
WINDOW = 512
MLSTM_HEADS = 4
MLSTM_DH = 128
MLSTM_CHUNK = 64
PEER_HEADS = 8
PEER_NKEYS = 128
PEER_TOPK = 16
MASK_BIG = 1e9
NEG = -1e30

LANES = 128
VMEM_LIMIT = 56 * 1024 * 1024

NT = (((1,), (1,)), ((), ()))
TN = (((0,), (0,)), ((), ()))


def _cparams(sem):
    return pltpu.CompilerParams(dimension_semantics=sem, vmem_limit_bytes=VMEM_LIMIT)


def _div(x, d):
    assert d & (d - 1) == 0
    return lax.shift_right_arithmetic(x, jnp.int32(d.bit_length() - 1))


def _pool_dot(pool, x, x_is_lhs):
    hi = x.astype(BF16)
    lo = (x - hi.astype(F32)).astype(BF16)
    if x_is_lhs:
        return (jnp.dot(hi, pool, preferred_element_type=F32) + jnp.dot(lo, pool, preferred_element_type=F32))
    return (jnp.dot(pool, hi, preferred_element_type=F32) + jnp.dot(pool, lo, preferred_element_type=F32))


def _masked_softmax(s, mask):
    s = jnp.where(mask, s, NEG)
    m = jnp.max(s, -1, keepdims=True)
    e = jnp.where(mask, jnp.exp(s - m), 0.0)
    d = jnp.maximum(jnp.sum(e, -1, keepdims=True), 1e-30)
    return e * (1.0 / d)


def _rms_proj_kernel(x_ref, g_ref, w_ref, *rest, emit_xn):
    if emit_xn:
        o_ref, xo_ref, xn_ref = rest
    else:
        o_ref, xn_ref = rest

    @pl.when(pl.program_id(1) == 0)
    def _():
        x = x_ref[...]
        r = x * lax.rsqrt(jnp.mean(x * x, -1, keepdims=True) + RMS_EPS)
        xn = (r * g_ref[...]).astype(BF16)
        xn_ref[...] = xn
        if emit_xn:
            xo_ref[...] = xn

    o_ref[...] = jnp.dot(xn_ref[...], w_ref[...], preferred_element_type=F32).astype(o_ref.dtype)


def _rms_proj(x, g, w, tm, tn, out_dtype, emit_xn=False):
    T, D = x.shape
    N = w.shape[1]
    out_shape = [jax.ShapeDtypeStruct((T, N), out_dtype)]
    out_specs = [pl.BlockSpec((tm, tn), lambda i, j: (i, j))]
    if emit_xn:
        out_shape.append(jax.ShapeDtypeStruct((T, D), BF16))
        out_specs.append(pl.BlockSpec((tm, D), lambda i, j: (i, 0)))
    res = pl.pallas_call(
        functools.partial(_rms_proj_kernel, emit_xn=emit_xn),
        out_shape=out_shape,
        grid=(T // tm, N // tn),
        in_specs=[pl.BlockSpec((tm, D), lambda i, j: (i, 0)),
                  pl.BlockSpec((1, D), lambda i, j: (0, 0)),
                  pl.BlockSpec((D, tn), lambda i, j: (0, j))],
        out_specs=out_specs,
        scratch_shapes=[pltpu.VMEM((tm, D), BF16)],
        compiler_params=_cparams(("parallel", "arbitrary")),
        name="rms_proj",
    )(x, g.reshape(1, D), w)
    return res if emit_xn else res[0]


def _conv_kernel(cb_ref, cc_ref, ch_ref, buf_ref, w_ref, y_ref, new_ref, ext_ref, *, S):
    C = cb_ref.shape[-1]
    u = cc_ref[...] * ch_ref[...]
    ext_ref[0:8, :] = jnp.zeros((8, C), F32)
    ext_ref[6:8, :] = buf_ref[0]
    ext_ref[8:8 + S, :] = u
    w = w_ref[...]
    y = w[0:1] * ext_ref[6:6 + S, :] + w[1:2] * ext_ref[7:7 + S, :] + w[2:3] * u
    y_ref[...] = cb_ref[...] * y
    new_ref[0] = u[S - 2:S]


def _conv(z, row0, B, S, buf, w):
    C = w.shape[1]
    rb0 = row0 // S
    return pl.pallas_call(
        functools.partial(_conv_kernel, S=S),
        out_shape=[jax.ShapeDtypeStruct((B * S, C), F32), jax.ShapeDtypeStruct((B, 2, C), F32)],
        grid=(B,),
        in_specs=[pl.BlockSpec((S, C), lambda b: (rb0 + b, 0)),
                  pl.BlockSpec((S, C), lambda b: (rb0 + b, 1)),
                  pl.BlockSpec((S, C), lambda b: (rb0 + b, 2)),
                  pl.BlockSpec((1, 2, C), lambda b: (b, 0, 0)),
                  pl.BlockSpec((3, C), lambda b: (0, 0))],
        out_specs=[pl.BlockSpec((S, C), lambda b: (b, 0)),
                   pl.BlockSpec((1, 2, C), lambda b: (b, 0, 0))],
        scratch_shapes=[pltpu.VMEM((S + 8, C), F32)],
        compiler_params=_cparams(("parallel",)),
        name="short_conv",
    )(z, z, z, buf, w)


def _topn_rank_select(score, n_sel):
    NB = score.shape[1]
    blk = lax.broadcasted_iota(jnp.int32, score.shape, 1)
    rank = jnp.zeros(score.shape, F32)
    for i in range(NB):
        ci = score[:, i:i + 1]
        beats = jnp.where(ci > score, 1.0, jnp.where((ci == score) & (blk > i), 1.0, 0.0))
        rank = rank + beats
    return rank < n_sel


SEL_KCHUNK = 512


def _bias_softmax_pv(q, k, v, bias):
    s = lax.dot_general(q, k, NT, preferred_element_type=F32) + bias
    e = jnp.exp(s - jnp.max(s, -1, keepdims=True))
    o = jnp.dot(e.astype(BF16), v, preferred_element_type=F32)
    return o * (1.0 / jnp.sum(e, -1, keepdims=True))


def _nsa_prompt_kernel(q_ref, kcs_ref, vcs_ref, ks_ref, vs_ref, kw_ref, vw_ref, gt_ref,
                       y_ref, kc_ref, vc_ref, osel_s, *, S, tq):
    kvh = pl.program_id(1)
    qi = pl.program_id(2)
    G = NSA_GROUP
    NC = S // CMP_STRIDE
    NB = S // SEL_BLOCK
    n_sel = min(SEL_TOPN, NB)
    span = min(WINDOW + tq, S)

    @pl.when(qi == 0)
    def _():
        j = lax.broadcasted_iota(jnp.int32, (NC, S), 0)
        r = lax.broadcasted_iota(jnp.int32, (NC, S), 1)
        lo = j * CMP_STRIDE
        pool = jnp.where((r >= lo) & (r < lo + 2 * CMP_STRIDE), 0.5 / CMP_STRIDE, 0.0).astype(BF16)
        kc_ref[...] = _pool_dot(pool, kcs_ref[0, 0], False).astype(BF16)
        vc_ref[...] = _pool_dot(pool, vcs_ref[0, 0], False).astype(BF16)

    t0 = qi * tq
    qs = q_ref[0, 0].reshape(G * tq, HEAD_DIM) * jnp.asarray(HEAD_DIM ** -0.5, BF16)
    row = lax.broadcasted_iota(jnp.int32, (G * tq, 1), 0)
    pos = t0 + (row & (tq - 1))
    posq = t0 + lax.broadcasted_iota(jnp.int32, (tq, 1), 0)

    s = lax.dot_general(qs, kc_ref[...], NT, preferred_element_type=F32)
    c_end = lax.broadcasted_iota(jnp.int32, (1, NC), 1) * CMP_STRIDE + (2 * CMP_STRIDE - 1)
    p = _masked_softmax(s, c_end <= pos)
    o_cmp = jnp.dot(p.astype(BF16), vc_ref[...], preferred_element_type=F32)

    imp = p[0:tq]
    for g in range(1, G):
        imp = imp + p[g * tq:(g + 1) * tq]
    per = SEL_BLOCK // CMP_STRIDE
    e4 = jnp.where(_div(lax.broadcasted_iota(jnp.int32, (NC, NB), 0), per)
                   == lax.broadcasted_iota(jnp.int32, (NC, NB), 1), 1.0, 0.0).astype(F32)
    impb = jnp.dot(imp, e4, precision=HI, preferred_element_type=F32)
    blk = lax.broadcasted_iota(jnp.int32, (1, NB), 1)
    qblk = _div(posq, SEL_BLOCK)
    valid = blk * SEL_BLOCK <= posq
    forced = (blk == 0) | (blk == qblk) | (blk == qblk - 1)
    score = jnp.where(forced, MASK_BIG, jnp.where(valid, impb, -MASK_BIG))
    sel = _topn_rank_select(score, n_sel) & (score > -0.5 * MASK_BIG)
    sel01 = jnp.where(sel, 1.0, 0.0).astype(BF16)

    def sel_branch(klen):
        eb = jnp.where(_div(lax.broadcasted_iota(jnp.int32, (NB, klen), 1), SEL_BLOCK)
                       == lax.broadcasted_iota(jnp.int32, (NB, klen), 0), 1.0, 0.0).astype(BF16)
        selk = jnp.dot(sel01, eb, preferred_element_type=F32)
        kpos = lax.broadcasted_iota(jnp.int32, (1, klen), 1)
        bias = jnp.where((selk > 0.5) & (kpos <= posq), 0.0, NEG)
        ks = ks_ref[0, 0, 0:klen, :]
        vs = vs_ref[0, 0, 0:klen, :]
        for g in range(G):
            osel_s[g * tq:(g + 1) * tq, :] = _bias_softmax_pv(qs[g * tq:(g + 1) * tq], ks, vs, bias)

    nvar = -(-S // SEL_KCHUNK)
    per_var = SEL_KCHUNK // tq
    for v in range(nvar):
        @pl.when(qi // per_var == v)
        def _(v=v):
            sel_branch(min(S, (v + 1) * SEL_KCHUNK))

    start = pl.multiple_of(jnp.maximum(t0 + tq - span, 0), tq)
    kw = kw_ref[0, 0, pl.ds(start, span), :]
    vw = vw_ref[0, 0, pl.ds(start, span), :]
    diff = posq - (start + lax.broadcasted_iota(jnp.int32, (1, span), 1))
    win_bias = jnp.where((diff >= 0) & (diff < WINDOW), 0.0, NEG)

    sg = jax.nn.sigmoid(gt_ref[...])
    pick = jnp.where(lax.broadcasted_iota(jnp.int32, (LANES, LANES), 0)
                     == lax.broadcasted_iota(jnp.int32, (LANES, LANES), 1) + kvh * (3 * G), 1.0, 0.0).astype(F32)
    g12 = jnp.dot(sg, pick, precision=HI, preferred_element_type=F32)

    outs = []
    for g in range(G):
        rows = slice(g * tq, (g + 1) * tq)
        o_win = _bias_softmax_pv(qs[rows], kw, vw, win_bias)
        outs.append(g12[:, 3 * g:3 * g + 1] * o_cmp[rows]
                    + g12[:, 3 * g + 1:3 * g + 2] * osel_s[rows, :]
                    + g12[:, 3 * g + 2:3 * g + 3] * o_win)
    y_ref[...] = jnp.concatenate(outs, axis=-1)


def _nsa_prompt(q_hm, kvc_hm, kvsw_hm, z, gate_cb, B, S, tq=256):
    nq = S // tq
    KVH, G, HD = NSA_KV_HEADS, NSA_GROUP, HEAD_DIM
    NC = S // CMP_STRIDE

    def kv_spec(i):
        return pl.BlockSpec((None, 1, 1, S, HD), lambda b, h, t, i=i: (i, b, h, 0, 0))

    return pl.pallas_call(
        functools.partial(_nsa_prompt_kernel, S=S, tq=tq),
        out_shape=jax.ShapeDtypeStruct((B * S, KVH * G * HD), F32),
        grid=(B, KVH, nq),
        in_specs=[pl.BlockSpec((1, 1, G, tq, HD), lambda b, h, t: (b, h, 0, t, 0)),
                  kv_spec(0), kv_spec(1), kv_spec(0), kv_spec(1), kv_spec(2), kv_spec(3),
                  pl.BlockSpec((tq, LANES), lambda b, h, t: (b * nq + t, gate_cb))],
        out_specs=pl.BlockSpec((tq, G * HD), lambda b, h, t: (b * nq + t, h)),
        scratch_shapes=[pltpu.VMEM((NC, HD), BF16), pltpu.VMEM((NC, HD), BF16),
                        pltpu.VMEM((G * tq, HD), F32)],
        compiler_params=_cparams(("parallel", "parallel", "arbitrary")),
        name="nsa_prompt",
    )(q_hm, kvc_hm, kvc_hm, kvsw_hm, kvsw_hm, kvsw_hm, kvsw_hm, z)


def _log_sigmoid(x):
    return jnp.minimum(x, 0.0) - jnp.log(1.0 + jnp.exp(-jnp.abs(x)))


def _mlstm_kernel(gb_ref, q_ref, k_ref, v_ref, o_ref, gt_ref, mg_ref, c0_ref, n0_ref, m0_ref,
                  y_ref, cn_ref, nn_ref, mn_ref, c_s, n_s, m_s, *, L, Lb, gi, gf):
    H, DH = MLSTM_HEADS, MLSTM_DH
    c = pl.program_id(1)

    @pl.when(c == 0)
    def _():
        c_s[...] = c0_ref[0]
        n_s[...] = n0_ref[0]
        m_s[...] = m0_ref[0]

    def padrows(a):
        if Lb == L:
            return a
        return jnp.concatenate([a, jnp.zeros((L - Lb, a.shape[1]), a.dtype)], axis=0)

    lane = lax.broadcasted_iota(jnp.int32, (1, LANES), 1)
    bias = jnp.zeros((1, LANES), F32)
    for h in range(H):
        bias = bias + jnp.where(lane == gi + h, gb_ref[0, h], 0.0) + jnp.where(lane == gf + h, gb_ref[1, h], 0.0)
    is_f = (lane >= gf) & (lane < gf + H)
    pre = padrows(gt_ref[...]) + bias
    gate = jnp.where(is_f, _log_sigmoid(pre), pre)
    if Lb != L:
        live = lax.broadcasted_iota(jnp.int32, (L, 1), 0) < Lb
        gate = jnp.where(live, gate, jnp.where(is_f, 0.0, NEG))
    rr = lax.broadcasted_iota(jnp.int32, (L, L), 0)
    cc = lax.broadcasted_iota(jnp.int32, (L, L), 1)
    tril = rr >= cc
    bcum = jnp.dot(jnp.where(tril, 1.0, 0.0).astype(F32), gate, precision=HI, preferred_element_type=F32)
    e8 = jnp.where(lax.broadcasted_iota(jnp.int32, (8, LANES), 1)
                   == lax.broadcasted_iota(jnp.int32, (8, LANES), 0) + gi, 1.0, 0.0).astype(F32)
    rg = lax.dot_general(e8, gate, NT, precision=HI, preferred_element_type=F32)
    rb = lax.dot_general(e8, bcum, NT, precision=HI, preferred_element_type=F32)

    q = padrows(q_ref[...])
    k = padrows(k_ref[...])
    v = padrows(v_ref[...])
    og = padrows(o_ref[...])
    ys = []
    for h in range(H):
        sl = slice(h * DH, (h + 1) * DH)
        qq = q[:, sl]
        kk = k[:, sl] * (DH ** -0.5)
        vv = v[:, sl]
        qb, kb, vb = qq.astype(BF16), kk.astype(BF16), vv.astype(BF16)
        b_col = bcum[:, gf + h:gf + h + 1]
        b_row = rb[H + h:H + h + 1, :]
        i_row = rg[h:h + 1, :]
        i_col = gate[:, gi + h:gi + h + 1]
        m_prev = m_s[h][:, 0:1]
        cmat = c_s[h]
        n_row = n_s[h]
        dmat = jnp.where(tril, b_col - b_row + i_row, NEG)
        inter = b_col + m_prev
        mt = jnp.maximum(inter, jnp.max(dmat, -1, keepdims=True))
        w = jnp.exp(dmat - mt)
        a = jnp.exp(inter - mt)
        wqk = w * lax.dot_general(qb, kb, NT, preferred_element_type=F32)
        num = (a * jnp.dot(qb, cmat.astype(BF16), preferred_element_type=F32)
               + jnp.dot(wqk.astype(BF16), vb, preferred_element_type=F32))
        den = a * jnp.sum(qq * n_row, -1, keepdims=True) + jnp.sum(wqk, -1, keepdims=True)
        hh = num * (1.0 / jnp.maximum(jnp.abs(den), jnp.exp(-mt)))
        b_last = b_col[L - 1:L]
        m_new = mt[L - 1:L]
        wl = jnp.exp(b_last - b_col + i_col - m_new)
        decay = jnp.exp(b_last + m_prev - m_new)
        kw = wl * kk
        c_s[h] = decay * cmat + lax.dot_general(kw.astype(BF16), vb, TN, preferred_element_type=F32)
        n_s[h] = decay * n_row + jnp.sum(kw, 0, keepdims=True)
        m_s[h] = jnp.broadcast_to(m_new, (1, LANES))
        hn = hh * lax.rsqrt(jnp.mean(hh * hh, -1, keepdims=True) + RMS_EPS)
        ys.append(jax.nn.sigmoid(og[:, sl]) * hn * mg_ref[:, sl])
    y = jnp.concatenate(ys, axis=-1)
    y_ref[...] = y[0:Lb]

    @pl.when(c == pl.num_programs(1) - 1)
    def _():
        cn_ref[0] = c_s[...]
        nn_ref[0] = n_s[...]
        mn_ref[0] = m_s[...]


def _mlstm(z, row0, B, S, col_q, gate_cb, gi, gf, gate_b, mnorm_g, C0, n0, m0):
    H, DH = MLSTM_HEADS, MLSTM_DH
    W = H * DH
    L = MLSTM_CHUNK
    Lb = L if S % L == 0 else S
    assert Lb <= L
    nc = S // Lb
    rb0 = row0 // Lb
    n0 = n0.reshape(B, H, 1, DH)
    m0 = jnp.broadcast_to(m0.reshape(B, H, 1, 1), (B, H, 1, LANES))

    def zspec(cb, width):
        return pl.BlockSpec((Lb, width), lambda b, c, cb=cb: (rb0 + b * nc + c, cb))

    y, C, n, m = pl.pallas_call(
        functools.partial(_mlstm_kernel, L=L, Lb=Lb, gi=gi, gf=gf),
        out_shape=[jax.ShapeDtypeStruct((B * S, W), F32),
                   jax.ShapeDtypeStruct((B, H, DH, DH), F32),
                   jax.ShapeDtypeStruct((B, H, 1, DH), F32),
                   jax.ShapeDtypeStruct((B, H, 1, LANES), F32)],
        grid=(B, nc),
        in_specs=[pl.BlockSpec(memory_space=pltpu.SMEM),
                  zspec(col_q, W), zspec(col_q + 1, W), zspec(col_q + 2, W), zspec(col_q + 3, W),
                  zspec(gate_cb, LANES),
                  pl.BlockSpec((1, W), lambda b, c: (0, 0)),
                  pl.BlockSpec((1, H, DH, DH), lambda b, c: (b, 0, 0, 0)),
                  pl.BlockSpec((1, H, 1, DH), lambda b, c: (b, 0, 0, 0)),
                  pl.BlockSpec((1, H, 1, LANES), lambda b, c: (b, 0, 0, 0))],
        out_specs=[pl.BlockSpec((Lb, W), lambda b, c: (b * nc + c, 0)),
                   pl.BlockSpec((1, H, DH, DH), lambda b, c: (b, 0, 0, 0)),
                   pl.BlockSpec((1, H, 1, DH), lambda b, c: (b, 0, 0, 0)),
                   pl.BlockSpec((1, H, 1, LANES), lambda b, c: (b, 0, 0, 0))],
        scratch_shapes=[pltpu.VMEM((H, DH, DH), F32), pltpu.VMEM((H, 1, DH), F32),
                        pltpu.VMEM((H, 1, LANES), F32)],
        compiler_params=_cparams(("parallel", "arbitrary")),
        name="mlstm",
    )(gate_b, z, z, z, z, z, mnorm_g.reshape(1, W), C0, n0, m0)
    return y, C, n.reshape(B, H, DH), m[:, :, 0, 0]


PAGES_PER_STEP = 16


def _lane_extract_topn(score, n_sel, floor):
    lane = lax.broadcasted_iota(jnp.int32, score.shape, 1).astype(F32)
    sel = jnp.zeros(score.shape, F32)
    sc = score
    for _ in range(n_sel):
        m = jnp.max(sc, -1, keepdims=True)
        idx = jnp.min(jnp.where(sc == m, lane, float(score.shape[1])), -1, keepdims=True)
        hit = lane == idx
        sel = jnp.where(hit & (m > floor), 1.0, sel)
        sc = jnp.where(hit, -jnp.inf, sc)
    return sel


def _s1_kernel(pt_ref, *refs, past, S, ncp, nbp):
    pgs = refs[:PAGES_PER_STEP]
    tail_ref, q_ref, ocmp_ref, sel_ref, sub_ref = refs[PAGES_PER_STEP:]
    P = pgs[0].shape[-1]
    j = pl.program_id(1)
    nfull = pl.num_programs(1) - 1
    KVH, G, HD = NSA_KV_HEADS, NSA_GROUP, HEAD_DIM
    cols_step = PAGES_PER_STEP * P // CMP_STRIDE
    assert cols_step == LANES

    @pl.when(j == 0)
    def _():
        sub_ref[...] = jnp.zeros(sub_ref.shape, F32)

    def pool_t(n):
        return jnp.where(_div(lax.broadcasted_iota(jnp.int32, (n, LANES), 0), CMP_STRIDE)
                         == lax.broadcasted_iota(jnp.int32, (n, LANES), 1),
                         1.0 / CMP_STRIDE, 0.0).astype(BF16)

    @pl.when(j < nfull)
    def _():
        pt = pool_t(PAGES_PER_STEP * P)
        col = pl.ds(pl.multiple_of(j * cols_step, cols_step), cols_step)
        for k in range(2):
            for h in range(KVH):
                xt = jnp.concatenate([r[0, 0, k, h] for r in pgs], axis=1)
                sub_ref[k, h, :, col] = _pool_dot(pt, xt, True)

    @pl.when(j == nfull)
    def _():
        base = past // CMP_STRIDE
        pt = pool_t(P)
        for k in range(2):
            for h in range(KVH):
                sub_ref[k, h, :, base:base + LANES] = _pool_dot(pt, tail_ref[0, k, h], True)
        R = G * S
        row = lax.broadcasted_iota(jnp.int32, (R, 1), 0)
        pos = past + (row & (S - 1))
        posq = past + lax.broadcasted_iota(jnp.int32, (S, 1), 0)
        c_end = lax.broadcasted_iota(jnp.int32, (1, ncp), 1) * CMP_STRIDE + (2 * CMP_STRIDE - 1)
        cmask = c_end <= pos
        per = SEL_BLOCK // CMP_STRIDE
        e4 = jnp.where(_div(lax.broadcasted_iota(jnp.int32, (ncp, nbp), 0), per)
                       == lax.broadcasted_iota(jnp.int32, (ncp, nbp), 1), 1.0, 0.0).astype(F32)
        blk = lax.broadcasted_iota(jnp.int32, (1, nbp), 1)
        qblk = _div(posq, SEL_BLOCK)
        valid = blk * SEL_BLOCK <= posq
        forced = (blk == 0) | (blk == qblk) | (blk == qblk - 1)
        for h in range(KVH):
            kct = 0.5 * (sub_ref[0, h, :, 0:ncp] + sub_ref[0, h, :, 1:ncp + 1])
            vct = 0.5 * (sub_ref[1, h, :, 0:ncp] + sub_ref[1, h, :, 1:ncp + 1])
            qs = q_ref[0, h] * jnp.asarray(HD ** -0.5, BF16)
            s = jnp.dot(qs, kct.astype(BF16), preferred_element_type=F32)
            p = _masked_softmax(s, cmask)
            ocmp_ref[0, h] = lax.dot_general(p.astype(BF16), vct.astype(BF16), NT, preferred_element_type=F32)
            imp = p[0:S]
            for g in range(1, G):
                imp = imp + p[g * S:(g + 1) * S]
            impb = jnp.dot(imp, e4, precision=HI, preferred_element_type=F32)
            score = jnp.where(forced, MASK_BIG, jnp.where(valid, impb, -MASK_BIG))
            sel_ref[0, h] = _lane_extract_topn(score, SEL_TOPN, -0.5 * MASK_BIG)


def _s2_kernel(pt_ref, *refs, past, S, nbp):
    pgs = refs[:PAGES_PER_STEP]
    tail_ref, q_ref, sel_ref, osel_ref, m_s, l_s, acc_s = refs[PAGES_PER_STEP:]
    P = pgs[0].shape[-1]
    j = pl.program_id(1)
    nfull = pl.num_programs(1) - 1
    KVH, G, HD = NSA_KV_HEADS, NSA_GROUP, HEAD_DIM
    R = G * S

    @pl.when(j == 0)
    def _():
        m_s[...] = jnp.full(m_s.shape, NEG, F32)
        l_s[...] = jnp.zeros(l_s.shape, F32)
        acc_s[...] = jnp.zeros(acc_s.shape, F32)

    row = lax.broadcasted_iota(jnp.int32, (R, 1), 0)
    pos = past + (row & (S - 1))

    def process(kt_of, vt_of, n, kpos0):
        kpos = kpos0 + lax.broadcasted_iota(jnp.int32, (1, n), 1)
        kblk = _div(kpos, SEL_BLOCK)
        esel = jnp.where(lax.broadcasted_iota(jnp.int32, (nbp, n), 0) == kblk, 1.0, 0.0).astype(BF16)
        causal = kpos <= pos
        for h in range(KVH):
            kt = kt_of(h).astype(BF16)
            vt = vt_of(h).astype(BF16)
            qs = q_ref[0, h] * jnp.asarray(HD ** -0.5, BF16)
            s = jnp.dot(qs, kt, preferred_element_type=F32)
            mk = jnp.dot(sel_ref[0, h].astype(BF16), esel, preferred_element_type=F32)
            mask = (jnp.concatenate([mk] * G, axis=0) > 0.5) & causal
            sm = jnp.where(mask, s, NEG)
            m_old = m_s[h]
            m_new = jnp.maximum(m_old, jnp.max(sm, -1, keepdims=True))
            alpha = jnp.exp(m_old - m_new)
            e = jnp.where(mask, jnp.exp(sm - m_new), 0.0)
            l_s[h] = alpha * l_s[h] + jnp.sum(e, -1, keepdims=True)
            acc_s[h] = alpha * acc_s[h] + lax.dot_general(e.astype(BF16), vt, NT, preferred_element_type=F32)
            m_s[h] = m_new

    @pl.when(j < nfull)
    def _():
        process(lambda h: jnp.concatenate([r[0, 0, 0, h] for r in pgs], axis=1),
                lambda h: jnp.concatenate([r[0, 0, 1, h] for r in pgs], axis=1),
                PAGES_PER_STEP * P, j * (PAGES_PER_STEP * P))

    @pl.when(j == nfull)
    def _():
        process(lambda h: tail_ref[0, 0, h], lambda h: tail_ref[0, 1, h], P, past)
        for h in range(KVH):
            osel_ref[0, h] = acc_s[h] * (1.0 / jnp.maximum(l_s[h], 1e-30))


def _page_specs(layer, half, n_pages, P):
    def spec(r):
        return pl.BlockSpec(
            (1, 1, 2, NSA_KV_HEADS, HEAD_DIM, P),
            lambda b, j, pt, r=r: (layer, pt[b, jnp.minimum(j * PAGES_PER_STEP + r, n_pages - 1)],
                                   half, 0, 0, 0))
    return [spec(r) for r in range(PAGES_PER_STEP)]


def _nsa_sample_global(cache_t, layer, page_table, tail_t, q_s, S):
    B, n_pages = page_table.shape
    P = cache_t.shape[-1]
    KVH, G, HD = NSA_KV_HEADS, NSA_GROUP, HEAD_DIM
    past = n_pages * P
    assert n_pages % PAGES_PER_STEP == 0 and S & (S - 1) == 0 and S <= SEL_BLOCK
    nsteps = n_pages // PAGES_PER_STEP + 1
    ncp = -(-(past + P) // CMP_STRIDE // LANES) * LANES
    nbp = -(-(past + P) // SEL_BLOCK // LANES) * LANES
    R = G * S
    qspec = pl.BlockSpec((1, KVH, R, HD), lambda b, j, pt: (b, 0, 0, 0))
    o_cmp, sel = pl.pallas_call(
        functools.partial(_s1_kernel, past=past, S=S, ncp=ncp, nbp=nbp),
        out_shape=[jax.ShapeDtypeStruct((B, KVH, R, HD), F32), jax.ShapeDtypeStruct((B, KVH, S, nbp), F32)],
        grid_spec=pltpu.PrefetchScalarGridSpec(
            num_scalar_prefetch=1, grid=(B, nsteps),
            in_specs=_page_specs(layer, 0, n_pages, P)
            + [pl.BlockSpec((1, 2, KVH, HD, P), lambda b, j, pt: (b, 0, 0, 0, 0)), qspec],
            out_specs=[pl.BlockSpec((1, KVH, R, HD), lambda b, j, pt: (b, 0, 0, 0)),
                       pl.BlockSpec((1, KVH, S, nbp), lambda b, j, pt: (b, 0, 0, 0))],
            scratch_shapes=[pltpu.VMEM((2, KVH, HD, ncp + LANES), F32)]),
        compiler_params=_cparams(("parallel", "arbitrary")),
        name="nsa_sample_cmp",
    )(page_table, *([cache_t] * PAGES_PER_STEP), tail_t, q_s)
    o_sel = pl.pallas_call(
        functools.partial(_s2_kernel, past=past, S=S, nbp=nbp),
        out_shape=jax.ShapeDtypeStruct((B, KVH, R, HD), F32),
        grid_spec=pltpu.PrefetchScalarGridSpec(
            num_scalar_prefetch=1, grid=(B, nsteps),
            in_specs=_page_specs(layer, 1, n_pages, P)
            + [pl.BlockSpec((1, 2, KVH, HD, P), lambda b, j, pt: (b, 1, 0, 0, 0)), qspec,
               pl.BlockSpec((1, KVH, S, nbp), lambda b, j, pt: (b, 0, 0, 0))],
            out_specs=pl.BlockSpec((1, KVH, R, HD), lambda b, j, pt: (b, 0, 0, 0)),
            scratch_shapes=[pltpu.VMEM((KVH, R, 1), F32), pltpu.VMEM((KVH, R, 1), F32),
                            pltpu.VMEM((KVH, R, HD), F32)]),
        compiler_params=_cparams(("parallel", "arbitrary")),
        name="nsa_sample_sel",
    )(page_table, *([cache_t] * PAGES_PER_STEP), tail_t, q_s, sel)
    return o_cmp, o_sel


def _s3_kernel(q_ref, kw_ref, vw_ref, ocmp_ref, osel_ref, gt_ref, y_ref, *, past, S):
    KVH, G, HD = NSA_KV_HEADS, NSA_GROUP, HEAD_DIM
    R = G * S
    nk = kw_ref.shape[1]
    row = lax.broadcasted_iota(jnp.int32, (R, 1), 0)
    pos = past + (row & (S - 1))
    kpos = past - (nk - S) + lax.broadcasted_iota(jnp.int32, (1, nk), 1)
    diff = pos - kpos
    wmask = (diff >= 0) & (diff < WINDOW)
    sg = jax.nn.sigmoid(gt_ref[...])
    kw = kw_ref[0].astype(BF16)
    vw = vw_ref[0].astype(BF16)
    outs = []
    for h in range(KVH):
        qs = q_ref[0, h] * jnp.asarray(HD ** -0.5, BF16)
        s = lax.dot_general(qs, kw[:, h * HD:(h + 1) * HD], NT, preferred_element_type=F32)
        p = _masked_softmax(s, wmask)
        o_win = jnp.dot(p.astype(BF16), vw[:, h * HD:(h + 1) * HD], preferred_element_type=F32)
        oc = ocmp_ref[0, h]
        osl = osel_ref[0, h]
        for g in range(G):
            c0 = (h * G + g) * 3
            rs = slice(g * S, (g + 1) * S)
            outs.append(sg[:, c0:c0 + 1] * oc[rs] + sg[:, c0 + 1:c0 + 2] * osl[rs]
                        + sg[:, c0 + 2:c0 + 3] * o_win[rs])
    y_ref[...] = jnp.concatenate(outs, axis=-1)


def _nsa_sample_combine(q_s, kwin, vwin, o_cmp, o_sel, z, row0, gate_cb, past, S):
    B, KVH, R, HD = q_s.shape
    nk = kwin.shape[1]
    W = KVH * HD
    ospec = pl.BlockSpec((1, KVH, R, HD), lambda b: (b, 0, 0, 0))
    return pl.pallas_call(
        functools.partial(_s3_kernel, past=past, S=S),
        out_shape=jax.ShapeDtypeStruct((B * S, KVH * NSA_GROUP * HD), F32),
        grid=(B,),
        in_specs=[ospec,
                  pl.BlockSpec((1, nk, W), lambda b: (b, 0, 0)),
                  pl.BlockSpec((1, nk, W), lambda b: (b, 0, 0)),
                  ospec, ospec,
                  pl.BlockSpec((S, LANES), lambda b: (row0 // S + b, gate_cb))],
        out_specs=pl.BlockSpec((S, KVH * NSA_GROUP * HD), lambda b: (b, 0)),
        compiler_params=_cparams(("parallel",)),
        name="nsa_sample_win",
    )(q_s, kwin, vwin, o_cmp, o_sel, z)


def _out_proj_kernel(x_ref, y_ref, w_ref, o_ref):
    o_ref[...] = x_ref[...] + jnp.dot(y_ref[...].astype(BF16), w_ref[...], preferred_element_type=F32)


def _out_proj(x, y, w, tm):
    T, D = x.shape
    K = y.shape[1]
    return pl.pallas_call(
        _out_proj_kernel,
        out_shape=jax.ShapeDtypeStruct((T, D), F32),
        grid=(T // tm,),
        in_specs=[pl.BlockSpec((tm, D), lambda i: (i, 0)),
                  pl.BlockSpec((tm, K), lambda i: (i, 0)),
                  pl.BlockSpec((K, D), lambda i: (0, 0))],
        out_specs=pl.BlockSpec((tm, D), lambda i: (i, 0)),
        compiler_params=_cparams(("parallel",)),
        name="out_proj",
    )(x, y, w)


def _structural_pairs(k):
    return [(i, j) for i in range(k) for j in range(k) if (i + 1) * (j + 1) <= k]


RANK_MARK = 2.0 ** 126


def _extract_topk(cur_ref, top_ref, c, exact):
    NK, K = PEER_NKEYS, PEER_TOPK
    for k in range(K):
        cur = cur_ref[c]
        m = jnp.max(cur, axis=0)
        if exact:
            key = lax.broadcasted_iota(jnp.int32, cur.shape, 0).astype(F32)
            idx = jnp.min(jnp.where(cur == m[None], key, float(NK)), axis=0)
            hit = key == idx[None]
        else:
            hit = cur == m[None]
        cur_ref[c] = jnp.where(hit, -RANK_MARK * (1.0 + k / 32.0), cur)
        top_ref[c, k] = m


def _router_kernel(q_ref, wb_ref, n_out, e1_out, r2_out, e2_out, s_ref, cur_ref, top_ref, tmp_ref):
    H, NK, K = PEER_HEADS, PEER_NKEYS, PEER_TOPK
    Tt = q_ref.shape[0]
    half = q_ref.shape[1] // 2
    for c in range(2):
        s = lax.dot_general(wb_ref[c], q_ref[:, c * half:(c + 1) * half], NT, preferred_element_type=F32)
        s_ref[c] = s.reshape(NK, H, Tt)
        cur_ref[c] = s_ref[c]
        _extract_topk(cur_ref, top_ref, c, exact=False)

    marked = jnp.sum(jnp.where(cur_ref[...] < -0.5 * RANK_MARK, 1.0, 0.0), axis=1)
    tied = jnp.max(jnp.where(marked != float(K), 1.0, 0.0)) > 0.0

    @pl.when(tied)
    def _():
        for c in range(2):
            cur_ref[c] = s_ref[c]
            _extract_topk(cur_ref, top_ref, c, exact=True)

    v1 = [top_ref[0, k] for k in range(K)]
    v2 = [top_ref[1, k] for k in range(K)]

    pairs = _structural_pairs(K)
    cand = [v1[i] + v2[j] for (i, j) in pairs]
    n = len(pairs)
    rank = []
    for p in range(n):
        rank.append(jnp.zeros((H, Tt), F32))
    for p in range(n):
        ip, jp = pairs[p]
        for q in range(p + 1, n):
            iq, jq = pairs[q]
            if ip <= iq and jp <= jq:
                rank[q] = rank[q] + 1.0
            else:
                b = jnp.where(cand[p] >= cand[q], 1.0, 0.0)
                rank[q] = rank[q] + b
                rank[p] = rank[p] + (1.0 - b)
    sel = [jnp.where(r < K, 1.0, 0.0) for r in rank]
    e1 = [jnp.exp(v1[i] - v1[0]) for i in range(K)]
    e2 = [jnp.exp(v2[j] - v2[0]) for j in range(K)]
    cnt = [jnp.zeros((H, Tt), F32) for _ in range(K)]
    zsum = jnp.zeros((H, Tt), F32)
    for p, (i, j) in enumerate(pairs):
        cnt[i] = cnt[i] + sel[p]
        zsum = zsum + sel[p] * (e1[i] * e2[j])
    inv_z = 1.0 / zsum

    def rank_of(c):
        cur = cur_ref[c]
        return jnp.where(cur < -0.5 * RANK_MARK, (cur * (-1.0 / RANK_MARK) - 1.0) * 32.0, float(NK))

    rk1 = rank_of(0)
    nk = jnp.zeros((NK, H, Tt), F32)
    for i in range(K):
        nk = jnp.where(rk1 == float(i), cnt[i][None], nk)
    outs = ((n_out, nk),
            (e1_out, jnp.exp(s_ref[0] - v1[0][None]) * inv_z[None]),
            (r2_out, rank_of(1)),
            (e2_out, jnp.exp(s_ref[1] - v2[0][None])))
    for ref, val in outs:
        tmp_ref[...] = val.reshape(NK * H, Tt)
        for h in range(H):
            ref[h] = tmp_ref[pl.ds(h, NK, stride=H), :].astype(ref.dtype)


def _router(q, wb, tt):
    T = q.shape[0]
    H, NK = PEER_HEADS, PEER_NKEYS
    shp = jax.ShapeDtypeStruct((H, NK, T), F32)
    shp_b = jax.ShapeDtypeStruct((H, NK, T), BF16)
    ospec = pl.BlockSpec((H, NK, tt), lambda i: (0, 0, i))
    return pl.pallas_call(
        _router_kernel,
        out_shape=[shp, shp, shp_b, shp_b],
        grid=(T // tt,),
        in_specs=[pl.BlockSpec((tt, q.shape[1]), lambda i: (i, 0)),
                  pl.BlockSpec(wb.shape, lambda i: (0, 0, 0))],
        out_specs=[ospec, ospec, ospec, ospec],
        scratch_shapes=[pltpu.VMEM((2, NK, H, tt), F32), pltpu.VMEM((2, NK, H, tt), F32),
                        pltpu.VMEM((2, PEER_TOPK, H, tt), F32), pltpu.VMEM((NK * H, tt), F32)],
        compiler_params=_cparams(("parallel",)),
        name="peer_router",
    )(q, wb)


def _gelu_tanh(x):
    return 0.5 * x * (1.0 + jnp.tanh(np.sqrt(2.0 / np.pi) * (x + 0.044715 * (x * x * x))))


PEER_ACHUNK = 2


def _peer_dense_kernel(xn_ref, u_ref, v_ref, n_ref, e1_ref, r2_ref, e2_ref, x1_ref, g_ref, o_ref, *, final_norm):
    H, NK = PEER_HEADS, PEER_NKEYS
    j = pl.program_id(1)
    na = u_ref.shape[0] // NK
    tt = xn_ref.shape[0]

    @pl.when(j == 0)
    def _():
        o_ref[...] = x1_ref[...]

    xn = xn_ref[...]
    parts = []
    for c in range(na // PEER_ACHUNK):
        rows = slice(c * PEER_ACHUNK * NK, (c + 1) * PEER_ACHUNK * NK)
        act = lax.dot_general(u_ref[rows, :], xn, NT, preferred_element_type=F32)
        gates = []
        for a in range(c * PEER_ACHUNK, (c + 1) * PEER_ACHUNK):
            row = pl.ds(j * na + a, 1)
            gate = None
            for h in range(H):
                t = jnp.where(r2_ref[h] < n_ref[h, row, :].astype(BF16),
                              e1_ref[h, row, :].astype(BF16) * e2_ref[h], jnp.zeros((), BF16))
                gate = t if gate is None else gate + t
            gates.append(gate)
        parts.append(_gelu_tanh(act).astype(BF16) * jnp.concatenate(gates, axis=0))
    wt = jnp.concatenate(parts, axis=0)
    o_ref[...] += lax.dot_general(wt, v_ref[...], TN, preferred_element_type=F32)

    if final_norm:
        @pl.when(j == pl.num_programs(1) - 1)
        def _():
            x = o_ref[...]
            o_ref[...] = x * lax.rsqrt(jnp.mean(x * x, -1, keepdims=True) + RMS_EPS) * g_ref[...]


def _peer_dense(xn, u, v, nk, e1, r2, e2, x1, g, tt, te, final_norm):
    T, D = xn.shape
    E = u.shape[0]
    H, NK = PEER_HEADS, PEER_NKEYS
    once = pl.Buffered(1)
    tab = pl.BlockSpec((H, NK, tt), lambda i, j: (0, 0, i), pipeline_mode=once)
    return pl.pallas_call(
        functools.partial(_peer_dense_kernel, final_norm=final_norm),
        out_shape=jax.ShapeDtypeStruct((T, D), F32),
        grid=(T // tt, E // te),
        in_specs=[pl.BlockSpec((tt, D), lambda i, j: (i, 0), pipeline_mode=once),
                  pl.BlockSpec((te, D), lambda i, j: (j, 0)),
                  pl.BlockSpec((te, D), lambda i, j: (j, 0)),
                  tab, tab, tab, tab,
                  pl.BlockSpec((tt, D), lambda i, j: (i, 0), pipeline_mode=once),
                  pl.BlockSpec((1, D), lambda i, j: (0, 0))],
        out_specs=pl.BlockSpec((tt, D), lambda i, j: (i, 0)),
        compiler_params=_cparams(("parallel", "arbitrary")),
        name="peer_dense",
    )(xn, u, v, nk, e1, r2, e2, x1, g.reshape(1, D))


T_TILE = 384
ROUTER_TILE = 128
PEER_T_TILE = 384
PAD_TILE = 768
Z_COL_TILE = 1280
EXPERT_TILE = 1024


def kernel(x_prompt, x_sample, cache_nsa_kv, state_win_kv, state_conv, state_mlstm_C, state_mlstm_n,
           state_mlstm_m, page_table, norm1_g, w_in, conv_w, mlstm_gate_b, mlstm_norm_g, w_out, norm2_g,
           peer_wq, peer_subkeys, peer_u, peer_v, final_norm_g):
    BP, SP, D = x_prompt.shape
    BS, SS, _ = x_sample.shape
    depth = w_in.shape[0]
    KVH, G, HD = NSA_KV_HEADS, NSA_GROUP, HEAD_DIM
    H, DH = MLSTM_HEADS, MLSTM_DH
    CW = conv_w.shape[2]
    NW = KVH * G * HD
    KVW = 6 * KVH * HD
    NG = 3 * KVH * G
    MW = H * DH
    TP, TS = BP * SP, BS * SS
    T = TP + TS
    assert PAD_TILE % T_TILE == 0 and PAD_TILE % ROUTER_TILE == 0 and PAD_TILE % PEER_T_TILE == 0
    TPAD = -(-T // PAD_TILE) * PAD_TILE
    n_pages = page_table.shape[1]
    P = cache_nsa_kv.shape[2]
    past = n_pages * P
    wb_rows = state_win_kv.shape[2]

    c_q = 3 * CW
    c_kv = c_q + NW
    c_m = c_kv + KVW
    c_g = c_m + 4 * MW
    ZW = -(-(c_g + LANES) // Z_COL_TILE) * Z_COL_TILE
    assert CW % LANES == 0 and c_m % MW == 0 and c_g % LANES == 0 and NG + 2 * H <= LANES
    gate_cb = c_g // LANES
    gi, gf = NG, NG + H
    o_gate = 3 * CW + NW + KVW

    x = jnp.concatenate([x_prompt.reshape(TP, D), x_sample.reshape(TS, D),
                         jnp.zeros((TPAD - T, D), F32)], axis=0)
    cache_t = jnp.transpose(cache_nsa_kv, (0, 1, 3, 4, 5, 2))
    eye_h = jnp.eye(PEER_HEADS, dtype=F32)
    dk = peer_subkeys.shape[-1]

    p_st, s_st = [], []
    for l in range(depth):
        wi = w_in[l]
        w_perm = jnp.concatenate(
            [wi[:, :o_gate], wi[:, o_gate + NG:o_gate + NG + 4 * MW], wi[:, o_gate:o_gate + NG],
             wi[:, o_gate + NG + 4 * MW:], jnp.zeros((D, ZW - wi.shape[1]), F32)], axis=1).astype(BF16)
        z = _rms_proj(x, norm1_g[l], w_perm, PAD_TILE, Z_COL_TILE, F32)

        ya_p, conv_p = _conv(z, 0, BP, SP, jnp.zeros((BP, 2, CW), F32), conv_w[l])
        ya_s, conv_s = _conv(z, TP, BS, SS, state_conv[l], conv_w[l])

        zq = z[:TP, c_q:c_q + NW].reshape(BP, SP, KVH, G, HD)
        q_hm = zq.transpose(0, 2, 3, 1, 4).astype(BF16)
        zkv = z[:TP, c_kv:c_kv + KVW].reshape(BP, SP, 3, 2, KVH, HD)
        kv_hm = zkv.transpose(2, 3, 0, 4, 1, 5)
        kvc_hm = kv_hm[0]
        kvsw_hm = kv_hm[1:].reshape(4, BP, KVH, SP, HD).astype(BF16)
        yb_p = _nsa_prompt(q_hm, kvc_hm, kvsw_hm, z, gate_cb, BP, SP)
        rows_p = zkv[:, :, :2].reshape(BP, SP, 4, KVH, HD)
        win_p = zkv[:, SP - min(WINDOW, SP):, 2]

        zs = z[TP:T]
        q_s = zs[:, c_q:c_q + NW].reshape(BS, SS, KVH, G, HD).transpose(0, 2, 3, 1, 4)
        q_s = q_s.reshape(BS, KVH, G * SS, HD).astype(BF16)
        zkv_s = zs[:, c_kv:c_kv + KVW].reshape(BS, SS, 3, 2, KVH, HD)
        rows_s = zkv_s[:, :, :2].reshape(BS, SS, 4, KVH, HD)
        tail_t = jnp.pad(rows_s.transpose(0, 2, 3, 4, 1), ((0, 0), (0, 0), (0, 0), (0, 0), (0, P - SS)))
        o_cmp, o_sel = _nsa_sample_global(cache_t, l, page_table, tail_t, q_s, SS)
        w_all = jnp.concatenate([state_win_kv[l], zkv_s[:, :, 2]], axis=1)
        kwin = w_all[:, :, 0].reshape(BS, wb_rows + SS, KVH * HD)
        vwin = w_all[:, :, 1].reshape(BS, wb_rows + SS, KVH * HD)
        yb_s = _nsa_sample_combine(q_s, kwin, vwin, o_cmp, o_sel, z, TP, gate_cb, past, SS)
        win_s = w_all[:, SS:]

        yc_p, C_p, n_p, m_p = _mlstm(z, 0, BP, SP, c_m // MW, gate_cb, gi, gf, mlstm_gate_b[l],
                                     mlstm_norm_g[l], jnp.zeros((BP, H, DH, DH), F32),
                                     jnp.zeros((BP, H, DH), F32), jnp.zeros((BP, H), F32))
        yc_s, C_s, n_s, m_s = _mlstm(z, TP, BS, SS, c_m // MW, gate_cb, gi, gf, mlstm_gate_b[l],
                                     mlstm_norm_g[l], state_mlstm_C[l], state_mlstm_n[l], state_mlstm_m[l])

        ymix = jnp.concatenate(
            [jnp.concatenate([ya_p, yb_p, yc_p], axis=1), jnp.concatenate([ya_s, yb_s, yc_s], axis=1),
             jnp.zeros((TPAD - T, D), F32)], axis=0)
        x1 = _out_proj(x, ymix, w_out[l].astype(BF16), T_TILE)

        nh = PEER_HEADS
        wq_perm = peer_wq[l].reshape(D, nh, 2, dk).transpose(0, 2, 1, 3).reshape(D, 2 * nh * dk).astype(BF16)
        q_peer, xn2 = _rms_proj(x1, norm2_g[l], wq_perm, T_TILE, nh * dk, BF16, emit_xn=True)
        wb = jnp.einsum('hcnd,hg->cnhgd', peer_subkeys[l], eye_h).reshape(2, PEER_NKEYS * nh, nh * dk).astype(BF16)
        nk, e1, r2, e2 = _router(q_peer, wb, ROUTER_TILE)
        x = _peer_dense(xn2, peer_u[l].astype(BF16), peer_v[l].astype(BF16), nk, e1, r2, e2, x1,
                        final_norm_g, PEER_T_TILE, EXPERT_TILE, final_norm=(l == depth - 1))

        p_st.append((rows_p, win_p, conv_p, C_p, n_p, m_p))
        s_st.append((rows_s, win_s, conv_s, C_s, n_s, m_s))

    p_rows, p_win, p_conv, p_C, p_n, p_m = [jnp.stack(a) for a in zip(*p_st)]
    s_rows, s_win, s_conv, s_C, s_n, s_m = [jnp.stack(a) for a in zip(*s_st)]
    y_prompt = x[:TP].reshape(BP, SP, D)
    y_sample = x[TP:T].reshape(BS, SS, D)
    return (y_prompt, y_sample, p_rows, p_win, p_conv, p_C, p_n, p_m, s_rows, s_win, s_conv, s_C, s_n, s_m)
```

```python
import functools

import numpy as np
import jax
import jax.numpy as jnp
from jax import lax
from jax.experimental import pallas as pl
from jax.experimental.pallas import tpu as pltpu

F32 = jnp.float32
BF16 = jnp.bfloat16
HI = lax.Precision.HIGHEST

RMS_EPS = 1e-6
HEAD_DIM = 64
NSA_KV_HEADS = 4
NSA_GROUP = 4
CMP_STRIDE = 16
SEL_BLOCK = 64
SEL_TOPN = 16
WINDOW = 512
MLSTM_HEADS = 4
MLSTM_DH = 128
MLSTM_CHUNK = 64
PEER_HEADS = 8
PEER_NKEYS = 128
PEER_TOPK = 16
MASK_BIG = 1e9
NEG = -1e30

LANES = 128
VMEM_LIMIT = 56 * 1024 * 1024

NT = (((1,), (1,)), ((), ()))
TN = (((0,), (0,)), ((), ()))


def _cparams(sem):
    return pltpu.CompilerParams(dimension_semantics=sem, vmem_limit_bytes=VMEM_LIMIT)


def _div(x, d):
    assert d & (d - 1) == 0
    return lax.shift_right_arithmetic(x, jnp.int32(d.bit_length() - 1))


def _pool_dot(pool, x, x_is_lhs):
    hi = x.astype(BF16)
    lo = (x - hi.astype(F32)).astype(BF16)
    if x_is_lhs:
        return (jnp.dot(hi, pool, preferred_element_type=F32) + jnp.dot(lo, pool, preferred_element_type=F32))
    return (jnp.dot(pool, hi, preferred_element_type=F32) + jnp.dot(pool, lo, preferred_element_type=F32))


def _masked_softmax(s, mask):
    s = jnp.where(mask, s, NEG)
    m = jnp.max(s, -1, keepdims=True)
    e = jnp.where(mask, jnp.exp(s - m), 0.0)
    d = jnp.maximum(jnp.sum(e, -1, keepdims=True), 1e-30)
    return e * (1.0 / d)


def _rms_proj_kernel(x_ref, g_ref, w_ref, *rest, emit_xn):
    if emit_xn:
        o_ref, xo_ref, xn_ref = rest
    else:
        o_ref, xn_ref = rest

    @pl.when(pl.program_id(1) == 0)
    def _():
        x = x_ref[...]
        r = x * lax.rsqrt(jnp.mean(x * x, -1, keepdims=True) + RMS_EPS)
        xn = (r * g_ref[...]).astype(BF16)
        xn_ref[...] = xn
        if emit_xn:
            xo_ref[...] = xn

    o_ref[...] = jnp.dot(xn_ref[...], w_ref[...], preferred_element_type=F32).astype(o_ref.dtype)


def _rms_proj(x, g, w, tm, tn, out_dtype, emit_xn=False):
    T, D = x.shape
    N = w.shape[1]
    out_shape = [jax.ShapeDtypeStruct((T, N), out_dtype)]
    out_specs = [pl.BlockSpec((tm, tn), lambda i, j: (i, j))]
    if emit_xn:
        out_shape.append(jax.ShapeDtypeStruct((T, D), BF16))
        out_specs.append(pl.BlockSpec((tm, D), lambda i, j: (i, 0)))
    res = pl.pallas_call(
        functools.partial(_rms_proj_kernel, emit_xn=emit_xn),
        out_shape=out_shape,
        grid=(T // tm, N // tn),
        in_specs=[pl.BlockSpec((tm, D), lambda i, j: (i, 0)),
                  pl.BlockSpec((1, D), lambda i, j: (0, 0)),
                  pl.BlockSpec((D, tn), lambda i, j: (0, j))],
        out_specs=out_specs,
        scratch_shapes=[pltpu.VMEM((tm, D), BF16)],
        compiler_params=_cparams(("parallel", "arbitrary")),
        name="rms_proj",
    )(x, g.reshape(1, D), w)
    return res if emit_xn else res[0]


def _conv_kernel(cb_ref, cc_ref, ch_ref, buf_ref, w_ref, y_ref, new_ref, ext_ref, *, S):
    C = cb_ref.shape[-1]
    u = cc_ref[...] * ch_ref[...]
    ext_ref[0:8, :] = jnp.zeros((8, C), F32)
    ext_ref[6:8, :] = buf_ref[0]
    ext_ref[8:8 + S, :] = u
    w = w_ref[...]
    y = w[0:1] * ext_ref[6:6 + S, :] + w[1:2] * ext_ref[7:7 + S, :] + w[2:3] * u
    y_ref[...] = cb_ref[...] * y
    new_ref[0] = u[S - 2:S]


def _conv(z, row0, B, S, buf, w):
    C = w.shape[1]
    rb0 = row0 // S
    return pl.pallas_call(
        functools.partial(_conv_kernel, S=S),
        out_shape=[jax.ShapeDtypeStruct((B * S, C), F32), jax.ShapeDtypeStruct((B, 2, C), F32)],
        grid=(B,),
        in_specs=[pl.BlockSpec((S, C), lambda b: (rb0 + b, 0)),
                  pl.BlockSpec((S, C), lambda b: (rb0 + b, 1)),
                  pl.BlockSpec((S, C), lambda b: (rb0 + b, 2)),
                  pl.BlockSpec((1, 2, C), lambda b: (b, 0, 0)),
                  pl.BlockSpec((3, C), lambda b: (0, 0))],
        out_specs=[pl.BlockSpec((S, C), lambda b: (b, 0)),
                   pl.BlockSpec((1, 2, C), lambda b: (b, 0, 0))],
        scratch_shapes=[pltpu.VMEM((S + 8, C), F32)],
        compiler_params=_cparams(("parallel",)),
        name="short_conv",
    )(z, z, z, buf, w)


def _topn_rank_select(score, n_sel):
    NB = score.shape[1]
    blk = lax.broadcasted_iota(jnp.int32, score.shape, 1)
    rank = jnp.zeros(score.shape, F32)
    for i in range(NB):
        ci = score[:, i:i + 1]
        beats = jnp.where(ci > score, 1.0, jnp.where((ci == score) & (blk > i), 1.0, 0.0))
        rank = rank + beats
    return rank < n_sel


SEL_KCHUNK = 512


def _bias_softmax_pv(q, k, v, bias):
    s = lax.dot_general(q, k, NT, preferred_element_type=F32) + bias
    e = jnp.exp(s - jnp.max(s, -1, keepdims=True))
    o = jnp.dot(e.astype(BF16), v, preferred_element_type=F32)
    return o * (1.0 / jnp.sum(e, -1, keepdims=True))


NSA_HPS = 2


def _nsa_prompt_kernel(q_ref, kcs_ref, vcs_ref, kss_ref, vss_ref, kws_ref, vws_ref, gt_ref,
                       y_ref, kc_ref, vc_ref, ks_ref, vs_ref, kw_ref, vw_ref, osel_s, *, S, tq):
    pair = pl.program_id(1)
    qi = pl.program_id(2)
    G, HD = NSA_GROUP, HEAD_DIM
    NC = S // CMP_STRIDE
    NB = S // SEL_BLOCK
    n_sel = min(SEL_TOPN, NB)
    span = min(WINDOW + tq, S)

    @pl.when(qi == 0)
    def _():
        j = lax.broadcasted_iota(jnp.int32, (NC, S), 0)
        r = lax.broadcasted_iota(jnp.int32, (NC, S), 1)
        lo = j * CMP_STRIDE
        pool = jnp.where((r >= lo) & (r < lo + 2 * CMP_STRIDE), 0.5 / CMP_STRIDE, 0.0).astype(BF16)
        kcp = _pool_dot(pool, kcs_ref[...], False).astype(BF16)
        vcp = _pool_dot(pool, vcs_ref[...], False).astype(BF16)
        for hh in range(NSA_HPS):
            cols = slice(hh * HD, (hh + 1) * HD)
            kc_ref[hh] = kcp[:, cols]
            vc_ref[hh] = vcp[:, cols]
            ks_ref[hh] = kss_ref[:, cols].astype(BF16)
            vs_ref[hh] = vss_ref[:, cols].astype(BF16)
            kw_ref[hh] = kws_ref[:, cols].astype(BF16)
            vw_ref[hh] = vws_ref[:, cols].astype(BF16)

    t0 = qi * tq
    row = lax.broadcasted_iota(jnp.int32, (G * tq, 1), 0)
    pos = t0 + (row & (tq - 1))
    posq = t0 + lax.broadcasted_iota(jnp.int32, (tq, 1), 0)
    c_end = lax.broadcasted_iota(jnp.int32, (1, NC), 1) * CMP_STRIDE + (2 * CMP_STRIDE - 1)
    cmask = c_end <= pos
    per = SEL_BLOCK // CMP_STRIDE
    e4 = jnp.where(_div(lax.broadcasted_iota(jnp.int32, (NC, NB), 0), per)
                   == lax.broadcasted_iota(jnp.int32, (NC, NB), 1), 1.0, 0.0).astype(F32)
    blk = lax.broadcasted_iota(jnp.int32, (1, NB), 1)
    qblk = _div(posq, SEL_BLOCK)
    valid = blk * SEL_BLOCK <= posq
    forced = (blk == 0) | (blk == qblk) | (blk == qblk - 1)

    qss, o_cmps, sel01s = [], [], []
    for hh in range(NSA_HPS):
        qh = q_ref[:, hh * G * HD:(hh + 1) * G * HD] * (HD ** -0.5)
        qs = jnp.concatenate([qh[:, g * HD:(g + 1) * HD] for g in range(G)], axis=0).astype(BF16)
        s = lax.dot_general(qs, kc_ref[hh], NT, preferred_element_type=F32)
        p = _masked_softmax(s, cmask)
        o_cmps.append(jnp.dot(p.astype(BF16), vc_ref[hh], preferred_element_type=F32))
        imp = p[0:tq]
        for g in range(1, G):
            imp = imp + p[g * tq:(g + 1) * tq]
        impb = jnp.dot(imp, e4, precision=HI, preferred_element_type=F32)
        score = jnp.where(forced, MASK_BIG, jnp.where(valid, impb, -MASK_BIG))
        sel = _topn_rank_select(score, n_sel) & (score > -0.5 * MASK_BIG)
        sel01s.append(jnp.where(sel, 1.0, 0.0).astype(BF16))
        qss.append(qs)

    def sel_branch(klen):
        eb = jnp.where(_div(lax.broadcasted_iota(jnp.int32, (NB, klen), 1), SEL_BLOCK)
                       == lax.broadcasted_iota(jnp.int32, (NB, klen), 0), 1.0, 0.0).astype(BF16)
        kpos = lax.broadcasted_iota(jnp.int32, (1, klen), 1)
        causal = kpos <= posq
        for hh in range(NSA_HPS):
            selk = jnp.dot(sel01s[hh], eb, preferred_element_type=F32)
            bias = jnp.where((selk > 0.5) & causal, 0.0, NEG)
            ks = ks_ref[hh, 0:klen, :]
            vs = vs_ref[hh, 0:klen, :]
            for g in range(G):
                osel_s[hh, g * tq:(g + 1) * tq, :] = _bias_softmax_pv(qss[hh][g * tq:(g + 1) * tq], ks, vs, bias)

    nvar = -(-S // SEL_KCHUNK)
    per_var = SEL_KCHUNK // tq
    for v in range(nvar):
        @pl.when(qi // per_var == v)
        def _(v=v):
            sel_branch(min(S, (v + 1) * SEL_KCHUNK))

    start = pl.multiple_of(jnp.maximum(t0 + tq - span, 0), tq)
    diff = posq - (start + lax.broadcasted_iota(jnp.int32, (1, span), 1))
    win_bias = jnp.where((diff >= 0) & (diff < WINDOW), 0.0, NEG)

    sg = jax.nn.sigmoid(gt_ref[...])
    outs = []
    for hh in range(NSA_HPS):
        kvh = pair * NSA_HPS + hh
        pick = jnp.where(lax.broadcasted_iota(jnp.int32, (LANES, LANES), 0)
                         == lax.broadcasted_iota(jnp.int32, (LANES, LANES), 1) + kvh * (3 * G),
                         1.0, 0.0).astype(F32)
        g12 = jnp.dot(sg, pick, precision=HI, preferred_element_type=F32)
        kw = kw_ref[hh, pl.ds(start, span), :]
        vw = vw_ref[hh, pl.ds(start, span), :]
        for g in range(G):
            rows = slice(g * tq, (g + 1) * tq)
            o_win = _bias_softmax_pv(qss[hh][rows], kw, vw, win_bias)
            outs.append(g12[:, 3 * g:3 * g + 1] * o_cmps[hh][rows]
                        + g12[:, 3 * g + 1:3 * g + 2] * osel_s[hh, rows, :]
                        + g12[:, 3 * g + 2:3 * g + 3] * o_win)
    y_ref[...] = jnp.concatenate(outs, axis=-1)


def _nsa_prompt(z, c_q, c_kv, gate_cb, B, S, tq=256):
    nq = S // tq
    KVH, G, HD = NSA_KV_HEADS, NSA_GROUP, HEAD_DIM
    NC = S // CMP_STRIDE
    qw = NSA_HPS * G * HD
    assert NSA_HPS * HD == LANES and c_q % qw == 0 and c_kv % LANES == 0

    def kv_spec(i):
        cb = (c_kv + i * KVH * HD) // LANES
        return pl.BlockSpec((S, LANES), lambda b, p, t, cb=cb: (b, cb + p))

    head_kv = pltpu.VMEM((NSA_HPS, S, HD), BF16)
    return pl.pallas_call(
        functools.partial(_nsa_prompt_kernel, S=S, tq=tq),
        out_shape=jax.ShapeDtypeStruct((B * S, KVH * G * HD), F32),
        grid=(B, KVH // NSA_HPS, nq),
        in_specs=[pl.BlockSpec((tq, qw), lambda b, p, t: (b * nq + t, c_q // qw + p)),
                  kv_spec(0), kv_spec(1), kv_spec(2), kv_spec(3), kv_spec(4), kv_spec(5),
                  pl.BlockSpec((tq, LANES), lambda b, p, t: (b * nq + t, gate_cb))],
        out_specs=pl.BlockSpec((tq, qw), lambda b, p, t: (b * nq + t, p)),
        scratch_shapes=[pltpu.VMEM((NSA_HPS, NC, HD), BF16), pltpu.VMEM((NSA_HPS, NC, HD), BF16),
                        head_kv, head_kv, head_kv, head_kv,
                        pltpu.VMEM((NSA_HPS, G * tq, HD), F32)],
        compiler_params=_cparams(("parallel", "parallel", "arbitrary")),
        name="nsa_prompt",
    )(z, z, z, z, z, z, z, z)


def _log_sigmoid(x):
    return jnp.minimum(x, 0.0) - jnp.log(1.0 + jnp.exp(-jnp.abs(x)))


def _mlstm_kernel(gb_ref, q_ref, k_ref, v_ref, o_ref, gt_ref, mg_ref, c0_ref, n0_ref, m0_ref,
                  y_ref, cn_ref, nn_ref, mn_ref, c_s, n_s, m_s, *, L, Lb, gi, gf):
    H, DH = MLSTM_HEADS, MLSTM_DH
    c = pl.program_id(1)

    @pl.when(c == 0)
    def _():
        c_s[...] = c0_ref[0]
        n_s[...] = n0_ref[0]
        m_s[...] = m0_ref[0]

    def padrows(a):
        if Lb == L:
            return a
        return jnp.concatenate([a, jnp.zeros((L - Lb, a.shape[1]), a.dtype)], axis=0)

    lane = lax.broadcasted_iota(jnp.int32, (1, LANES), 1)
    bias = jnp.zeros((1, LANES), F32)
    for h in range(H):
        bias = bias + jnp.where(lane == gi + h, gb_ref[0, h], 0.0) + jnp.where(lane == gf + h, gb_ref[1, h], 0.0)
    is_f = (lane >= gf) & (lane < gf + H)
    pre = padrows(gt_ref[...]) + bias
    gate = jnp.where(is_f, _log_sigmoid(pre), pre)
    if Lb != L:
        live = lax.broadcasted_iota(jnp.int32, (L, 1), 0) < Lb
        gate = jnp.where(live, gate, jnp.where(is_f, 0.0, NEG))
    rr = lax.broadcasted_iota(jnp.int32, (L, L), 0)
    cc = lax.broadcasted_iota(jnp.int32, (L, L), 1)
    tril = rr >= cc
    bcum = jnp.dot(jnp.where(tril, 1.0, 0.0).astype(F32), gate, precision=HI, preferred_element_type=F32)
    e8 = jnp.where(lax.broadcasted_iota(jnp.int32, (8, LANES), 1)
                   == lax.broadcasted_iota(jnp.int32, (8, LANES), 0) + gi, 1.0, 0.0).astype(F32)
    rg = lax.dot_general(e8, gate, NT, precision=HI, preferred_element_type=F32)
    rb = lax.dot_general(e8, bcum, NT, precision=HI, preferred_element_type=F32)

    q = padrows(q_ref[...])
    k = padrows(k_ref[...])
    v = padrows(v_ref[...])
    og = padrows(o_ref[...])
    ys = []
    for h in range(H):
        sl = slice(h * DH, (h + 1) * DH)
        qq = q[:, sl]
        kk = k[:, sl] * (DH ** -0.5)
        vv = v[:, sl]
        qb, kb, vb = qq.astype(BF16), kk.astype(BF16), vv.astype(BF16)
        b_col = bcum[:, gf + h:gf + h + 1]
        b_row = rb[H + h:H + h + 1, :]
        i_row = rg[h:h + 1, :]
        i_col = gate[:, gi + h:gi + h + 1]
        m_prev = m_s[h][:, 0:1]
        cmat = c_s[h]
        n_row = n_s[h]
        dmat = jnp.where(tril, b_col - b_row + i_row, NEG)
        inter = b_col + m_prev
        mt = jnp.maximum(inter, jnp.max(dmat, -1, keepdims=True))
        w = jnp.exp(dmat - mt)
        a = jnp.exp(inter - mt)
        wqk = w * lax.dot_general(qb, kb, NT, preferred_element_type=F32)
        num = (a * jnp.dot(qb, cmat.astype(BF16), preferred_element_type=F32)
               + jnp.dot(wqk.astype(BF16), vb, preferred_element_type=F32))
        den = a * jnp.sum(qq * n_row, -1, keepdims=True) + jnp.sum(wqk, -1, keepdims=True)
        hh = num * (1.0 / jnp.maximum(jnp.abs(den), jnp.exp(-mt)))
        b_last = b_col[L - 1:L]
        m_new = mt[L - 1:L]
        wl = jnp.exp(b_last - b_col + i_col - m_new)
        decay = jnp.exp(b_last + m_prev - m_new)
        kw = wl * kk
        c_s[h] = decay * cmat + lax.dot_general(kw.astype(BF16), vb, TN, preferred_element_type=F32)
        n_s[h] = decay * n_row + jnp.sum(kw, 0, keepdims=True)
        m_s[h] = jnp.broadcast_to(m_new, (1, LANES))
        hn = hh * lax.rsqrt(jnp.mean(hh * hh, -1, keepdims=True) + RMS_EPS)
        ys.append(jax.nn.sigmoid(og[:, sl]) * hn * mg_ref[:, sl])
    y = jnp.concatenate(ys, axis=-1)
    y_ref[...] = y[0:Lb]

    @pl.when(c == pl.num_programs(1) - 1)
    def _():
        cn_ref[0] = c_s[...]
        nn_ref[0] = n_s[...]
        mn_ref[0] = m_s[...]


def _mlstm(z, row0, B, S, col_q, gate_cb, gi, gf, gate_b, mnorm_g, C0, n0, m0):
    H, DH = MLSTM_HEADS, MLSTM_DH
    W = H * DH
    L = MLSTM_CHUNK
    Lb = L if S % L == 0 else S
    assert Lb <= L
    nc = S // Lb
    rb0 = row0 // Lb
    n0 = n0.reshape(B, H, 1, DH)
    m0 = jnp.broadcast_to(m0.reshape(B, H, 1, 1), (B, H, 1, LANES))

    def zspec(cb, width):
        return pl.BlockSpec((Lb, width), lambda b, c, cb=cb: (rb0 + b * nc + c, cb))

    y, C, n, m = pl.pallas_call(
        functools.partial(_mlstm_kernel, L=L, Lb=Lb, gi=gi, gf=gf),
        out_shape=[jax.ShapeDtypeStruct((B * S, W), F32),
                   jax.ShapeDtypeStruct((B, H, DH, DH), F32),
                   jax.ShapeDtypeStruct((B, H, 1, DH), F32),
                   jax.ShapeDtypeStruct((B, H, 1, LANES), F32)],
        grid=(B, nc),
        in_specs=[pl.BlockSpec(memory_space=pltpu.SMEM),
                  zspec(col_q, W), zspec(col_q + 1, W), zspec(col_q + 2, W), zspec(col_q + 3, W),
                  zspec(gate_cb, LANES),
                  pl.BlockSpec((1, W), lambda b, c: (0, 0)),
                  pl.BlockSpec((1, H, DH, DH), lambda b, c: (b, 0, 0, 0)),
                  pl.BlockSpec((1, H, 1, DH), lambda b, c: (b, 0, 0, 0)),
                  pl.BlockSpec((1, H, 1, LANES), lambda b, c: (b, 0, 0, 0))],
        out_specs=[pl.BlockSpec((Lb, W), lambda b, c: (b * nc + c, 0)),
                   pl.BlockSpec((1, H, DH, DH), lambda b, c: (b, 0, 0, 0)),
                   pl.BlockSpec((1, H, 1, DH), lambda b, c: (b, 0, 0, 0)),
                   pl.BlockSpec((1, H, 1, LANES), lambda b, c: (b, 0, 0, 0))],
        scratch_shapes=[pltpu.VMEM((H, DH, DH), F32), pltpu.VMEM((H, 1, DH), F32),
                        pltpu.VMEM((H, 1, LANES), F32)],
        compiler_params=_cparams(("parallel", "arbitrary")),
        name="mlstm",
    )(gate_b, z, z, z, z, z, mnorm_g.reshape(1, W), C0, n0, m0)
    return y, C, n.reshape(B, H, DH), m[:, :, 0, 0]


PAGES_PER_STEP = 16


def _lane_extract_topn(score, n_sel, floor):
    lane = lax.broadcasted_iota(jnp.int32, score.shape, 1).astype(F32)
    sel = jnp.zeros(score.shape, F32)
    sc = score
    for _ in range(n_sel):
        m = jnp.max(sc, -1, keepdims=True)
        idx = jnp.min(jnp.where(sc == m, lane, float(score.shape[1])), -1, keepdims=True)
        hit = lane == idx
        sel = jnp.where(hit & (m > floor), 1.0, sel)
        sc = jnp.where(hit, -jnp.inf, sc)
    return sel


def _s1_kernel(pt_ref, *refs, past, S, ncp, nbp):
    pgs = refs[:PAGES_PER_STEP]
    tail_ref, q_ref, ocmp_ref, sel_ref, sub_ref = refs[PAGES_PER_STEP:]
    P = pgs[0].shape[-1]
    j = pl.program_id(1)
    nfull = pl.num_programs(1) - 1
    KVH, G, HD = NSA_KV_HEADS, NSA_GROUP, HEAD_DIM
    cols_step = PAGES_PER_STEP * P // CMP_STRIDE
    assert cols_step == LANES

    @pl.when(j == 0)
    def _():
        sub_ref[...] = jnp.zeros(sub_ref.shape, F32)

    def pool_t(n):
        return jnp.where(_div(lax.broadcasted_iota(jnp.int32, (n, LANES), 0), CMP_STRIDE)
                         == lax.broadcasted_iota(jnp.int32, (n, LANES), 1),
                         1.0 / CMP_STRIDE, 0.0).astype(BF16)

    @pl.when(j < nfull)
    def _():
        pt = pool_t(PAGES_PER_STEP * P)
        col = pl.ds(pl.multiple_of(j * cols_step, cols_step), cols_step)
        for k in range(2):
            for h in range(KVH):
                xt = jnp.concatenate([r[0, 0, k, h] for r in pgs], axis=1)
                sub_ref[k, h, :, col] = _pool_dot(pt, xt, True)

    @pl.when(j == nfull)
    def _():
        base = past // CMP_STRIDE
        pt = pool_t(P)
        for k in range(2):
            for h in range(KVH):
                sub_ref[k, h, :, base:base + LANES] = _pool_dot(pt, tail_ref[0, k, h], True)
        R = G * S
        row = lax.broadcasted_iota(jnp.int32, (R, 1), 0)
        pos = past + (row & (S - 1))
        posq = past + lax.broadcasted_iota(jnp.int32, (S, 1), 0)
        c_end = lax.broadcasted_iota(jnp.int32, (1, ncp), 1) * CMP_STRIDE + (2 * CMP_STRIDE - 1)
        cmask = c_end <= pos
        per = SEL_BLOCK // CMP_STRIDE
        e4 = jnp.where(_div(lax.broadcasted_iota(jnp.int32, (ncp, nbp), 0), per)
                       == lax.broadcasted_iota(jnp.int32, (ncp, nbp), 1), 1.0, 0.0).astype(F32)
        blk = lax.broadcasted_iota(jnp.int32, (1, nbp), 1)
        qblk = _div(posq, SEL_BLOCK)
        valid = blk * SEL_BLOCK <= posq
        forced = (blk == 0) | (blk == qblk) | (blk == qblk - 1)
        for h in range(KVH):
            kct = 0.5 * (sub_ref[0, h, :, 0:ncp] + sub_ref[0, h, :, 1:ncp + 1])
            vct = 0.5 * (sub_ref[1, h, :, 0:ncp] + sub_ref[1, h, :, 1:ncp + 1])
            qs = q_ref[0, h] * jnp.asarray(HD ** -0.5, BF16)
            s = jnp.dot(qs, kct.astype(BF16), preferred_element_type=F32)
            p = _masked_softmax(s, cmask)
            ocmp_ref[0, h] = lax.dot_general(p.astype(BF16), vct.astype(BF16), NT, preferred_element_type=F32)
            imp = p[0:S]
            for g in range(1, G):
                imp = imp + p[g * S:(g + 1) * S]
            impb = jnp.dot(imp, e4, precision=HI, preferred_element_type=F32)
            score = jnp.where(forced, MASK_BIG, jnp.where(valid, impb, -MASK_BIG))
            sel_ref[0, h] = _lane_extract_topn(score, SEL_TOPN, -0.5 * MASK_BIG)


def _s2_kernel(pt_ref, *refs, past, S, nbp):
    pgs = refs[:PAGES_PER_STEP]
    tail_ref, q_ref, sel_ref, osel_ref, m_s, l_s, acc_s = refs[PAGES_PER_STEP:]
    P = pgs[0].shape[-1]
    j = pl.program_id(1)
    nfull = pl.num_programs(1) - 1
    KVH, G, HD = NSA_KV_HEADS, NSA_GROUP, HEAD_DIM
    R = G * S

    @pl.when(j == 0)
    def _():
        m_s[...] = jnp.full(m_s.shape, NEG, F32)
        l_s[...] = jnp.zeros(l_s.shape, F32)
        acc_s[...] = jnp.zeros(acc_s.shape, F32)

    row = lax.broadcasted_iota(jnp.int32, (R, 1), 0)
    pos = past + (row & (S - 1))

    def process(kt_of, vt_of, n, kpos0):
        kpos = kpos0 + lax.broadcasted_iota(jnp.int32, (1, n), 1)
        kblk = _div(kpos, SEL_BLOCK)
        esel = jnp.where(lax.broadcasted_iota(jnp.int32, (nbp, n), 0) == kblk, 1.0, 0.0).astype(BF16)
        causal = kpos <= pos
        for h in range(KVH):
            kt = kt_of(h).astype(BF16)
            vt = vt_of(h).astype(BF16)
            qs = q_ref[0, h] * jnp.asarray(HD ** -0.5, BF16)
            s = jnp.dot(qs, kt, preferred_element_type=F32)
            mk = jnp.dot(sel_ref[0, h].astype(BF16), esel, preferred_element_type=F32)
            mask = (jnp.concatenate([mk] * G, axis=0) > 0.5) & causal
            sm = jnp.where(mask, s, NEG)
            m_old = m_s[h]
            m_new = jnp.maximum(m_old, jnp.max(sm, -1, keepdims=True))
            alpha = jnp.exp(m_old - m_new)
            e = jnp.where(mask, jnp.exp(sm - m_new), 0.0)
            l_s[h] = alpha * l_s[h] + jnp.sum(e, -1, keepdims=True)
            acc_s[h] = alpha * acc_s[h] + lax.dot_general(e.astype(BF16), vt, NT, preferred_element_type=F32)
            m_s[h] = m_new

    @pl.when(j < nfull)
    def _():
        process(lambda h: jnp.concatenate([r[0, 0, 0, h] for r in pgs], axis=1),
                lambda h: jnp.concatenate([r[0, 0, 1, h] for r in pgs], axis=1),
                PAGES_PER_STEP * P, j * (PAGES_PER_STEP * P))

    @pl.when(j == nfull)
    def _():
        process(lambda h: tail_ref[0, 0, h], lambda h: tail_ref[0, 1, h], P, past)
        for h in range(KVH):
            osel_ref[0, h] = acc_s[h] * (1.0 / jnp.maximum(l_s[h], 1e-30))


def _page_specs(layer, half, n_pages, P):
    def spec(r):
        return pl.BlockSpec(
            (1, 1, 2, NSA_KV_HEADS, HEAD_DIM, P),
            lambda b, j, pt, r=r: (layer, pt[b, jnp.minimum(j * PAGES_PER_STEP + r, n_pages - 1)],
                                   half, 0, 0, 0))
    return [spec(r) for r in range(PAGES_PER_STEP)]


def _nsa_sample_global(cache_t, layer, page_table, tail_t, q_s, S):
    B, n_pages = page_table.shape
    P = cache_t.shape[-1]
    KVH, G, HD = NSA_KV_HEADS, NSA_GROUP, HEAD_DIM
    past = n_pages * P
    assert n_pages % PAGES_PER_STEP == 0 and S & (S - 1) == 0 and S <= SEL_BLOCK
    nsteps = n_pages // PAGES_PER_STEP + 1
    ncp = -(-(past + P) // CMP_STRIDE // LANES) * LANES
    nbp = -(-(past + P) // SEL_BLOCK // LANES) * LANES
    R = G * S
    qspec = pl.BlockSpec((1, KVH, R, HD), lambda b, j, pt: (b, 0, 0, 0))
    o_cmp, sel = pl.pallas_call(
        functools.partial(_s1_kernel, past=past, S=S, ncp=ncp, nbp=nbp),
        out_shape=[jax.ShapeDtypeStruct((B, KVH, R, HD), F32), jax.ShapeDtypeStruct((B, KVH, S, nbp), F32)],
        grid_spec=pltpu.PrefetchScalarGridSpec(
            num_scalar_prefetch=1, grid=(B, nsteps),
            in_specs=_page_specs(layer, 0, n_pages, P)
            + [pl.BlockSpec((1, 2, KVH, HD, P), lambda b, j, pt: (b, 0, 0, 0, 0)), qspec],
            out_specs=[pl.BlockSpec((1, KVH, R, HD), lambda b, j, pt: (b, 0, 0, 0)),
                       pl.BlockSpec((1, KVH, S, nbp), lambda b, j, pt: (b, 0, 0, 0))],
            scratch_shapes=[pltpu.VMEM((2, KVH, HD, ncp + LANES), F32)]),
        compiler_params=_cparams(("parallel", "arbitrary")),
        name="nsa_sample_cmp",
    )(page_table, *([cache_t] * PAGES_PER_STEP), tail_t, q_s)
    o_sel = pl.pallas_call(
        functools.partial(_s2_kernel, past=past, S=S, nbp=nbp),
        out_shape=jax.ShapeDtypeStruct((B, KVH, R, HD), F32),
        grid_spec=pltpu.PrefetchScalarGridSpec(
            num_scalar_prefetch=1, grid=(B, nsteps),
            in_specs=_page_specs(layer, 1, n_pages, P)
            + [pl.BlockSpec((1, 2, KVH, HD, P), lambda b, j, pt: (b, 1, 0, 0, 0)), qspec,
               pl.BlockSpec((1, KVH, S, nbp), lambda b, j, pt: (b, 0, 0, 0))],
            out_specs=pl.BlockSpec((1, KVH, R, HD), lambda b, j, pt: (b, 0, 0, 0)),
            scratch_shapes=[pltpu.VMEM((KVH, R, 1), F32), pltpu.VMEM((KVH, R, 1), F32),
                            pltpu.VMEM((KVH, R, HD), F32)]),
        compiler_params=_cparams(("parallel", "arbitrary")),
        name="nsa_sample_sel",
    )(page_table, *([cache_t] * PAGES_PER_STEP), tail_t, q_s, sel)
    return o_cmp, o_sel


def _s3_kernel(q_ref, kw_ref, vw_ref, ocmp_ref, osel_ref, gt_ref, y_ref, *, past, S):
    KVH, G, HD = NSA_KV_HEADS, NSA_GROUP, HEAD_DIM
    R = G * S
    nk = kw_ref.shape[1]
    row = lax.broadcasted_iota(jnp.int32, (R, 1), 0)
    pos = past + (row & (S - 1))
    kpos = past - (nk - S) + lax.broadcasted_iota(jnp.int32, (1, nk), 1)
    diff = pos - kpos
    wmask = (diff >= 0) & (diff < WINDOW)
    sg = jax.nn.sigmoid(gt_ref[...])
    kw = kw_ref[0].astype(BF16)
    vw = vw_ref[0].astype(BF16)
    outs = []
    for h in range(KVH):
        qs = q_ref[0, h] * jnp.asarray(HD ** -0.5, BF16)
        s = lax.dot_general(qs, kw[:, h * HD:(h + 1) * HD], NT, preferred_element_type=F32)
        p = _masked_softmax(s, wmask)
        o_win = jnp.dot(p.astype(BF16), vw[:, h * HD:(h + 1) * HD], preferred_element_type=F32)
        oc = ocmp_ref[0, h]
        osl = osel_ref[0, h]
        for g in range(G):
            c0 = (h * G + g) * 3
            rs = slice(g * S, (g + 1) * S)
            outs.append(sg[:, c0:c0 + 1] * oc[rs] + sg[:, c0 + 1:c0 + 2] * osl[rs]
                        + sg[:, c0 + 2:c0 + 3] * o_win[rs])
    y_ref[...] = jnp.concatenate(outs, axis=-1)


def _nsa_sample_combine(q_s, kwin, vwin, o_cmp, o_sel, z, row0, gate_cb, past, S):
    B, KVH, R, HD = q_s.shape
    nk = kwin.shape[1]
    W = KVH * HD
    ospec = pl.BlockSpec((1, KVH, R, HD), lambda b: (b, 0, 0, 0))
    return pl.pallas_call(
        functools.partial(_s3_kernel, past=past, S=S),
        out_shape=jax.ShapeDtypeStruct((B * S, KVH * NSA_GROUP * HD), F32),
        grid=(B,),
        in_specs=[ospec,
                  pl.BlockSpec((1, nk, W), lambda b: (b, 0, 0)),
                  pl.BlockSpec((1, nk, W), lambda b: (b, 0, 0)),
                  ospec, ospec,
                  pl.BlockSpec((S, LANES), lambda b: (row0 // S + b, gate_cb))],
        out_specs=pl.BlockSpec((S, KVH * NSA_GROUP * HD), lambda b: (b, 0)),
        compiler_params=_cparams(("parallel",)),
        name="nsa_sample_win",
    )(q_s, kwin, vwin, o_cmp, o_sel, z)


def _out_proj_kernel(x_ref, ya_ref, yb_ref, yc_ref, yt_ref, w_ref, o_ref, *, n_head_tiles):
    i = pl.program_id(0)
    ka, kb = ya_ref.shape[1], yb_ref.shape[1]

    @pl.when(i < n_head_tiles)
    def _():
        acc = jnp.dot(ya_ref[...].astype(BF16), w_ref[0:ka, :], preferred_element_type=F32)
        acc = acc + jnp.dot(yb_ref[...].astype(BF16), w_ref[ka:ka + kb, :], preferred_element_type=F32)
        acc = acc + jnp.dot(yc_ref[...].astype(BF16), w_ref[ka + kb:, :], preferred_element_type=F32)
        o_ref[...] = x_ref[...] + acc

    @pl.when(i >= n_head_tiles)
    def _():
        o_ref[...] = x_ref[...] + jnp.dot(yt_ref[...].astype(BF16), w_ref[...], preferred_element_type=F32)


def _out_proj(x, ya, yb, yc, ytail, w, tm):
    T, D = x.shape
    TH = ya.shape[0]
    assert TH % tm == 0 and (T - TH) % tm == 0 and ytail.shape == (T - TH, D)
    nh = TH // tm

    def head(a):
        return pl.BlockSpec((tm, a.shape[1]), lambda i: (jnp.minimum(i, nh - 1), 0))

    return pl.pallas_call(
        functools.partial(_out_proj_kernel, n_head_tiles=nh),
        out_shape=jax.ShapeDtypeStruct((T, D), F32),
        grid=(T // tm,),
        in_specs=[pl.BlockSpec((tm, D), lambda i: (i, 0)),
                  head(ya), head(yb), head(yc),
                  pl.BlockSpec((tm, D), lambda i: (jnp.maximum(i - nh, 0), 0)),
                  pl.BlockSpec((D, D), lambda i: (0, 0))],
        out_specs=pl.BlockSpec((tm, D), lambda i: (i, 0)),
        compiler_params=_cparams(("parallel",)),
        name="out_proj",
    )(x, ya, yb, yc, ytail, w)


def _structural_pairs(k):
    return [(i, j) for i in range(k) for j in range(k) if (i + 1) * (j + 1) <= k]


RANK_MARK = 2.0 ** 126


def _extract_topk(cur_ref, top_ref, c, exact):
    NK, K = PEER_NKEYS, PEER_TOPK
    for k in range(K):
        cur = cur_ref[c]
        m = jnp.max(cur, axis=0)
        if exact:
            key = lax.broadcasted_iota(jnp.int32, cur.shape, 0).astype(F32)
            idx = jnp.min(jnp.where(cur == m[None], key, float(NK)), axis=0)
            hit = key == idx[None]
        else:
            hit = cur == m[None]
        cur_ref[c] = jnp.where(hit, -RANK_MARK * (1.0 + k / 32.0), cur)
        top_ref[c, k] = m


def _router_kernel(q_ref, wb_ref, n_out, e1_out, r2_out, e2_out, s_ref, cur_ref, top_ref, tmp_ref):
    H, NK, K = PEER_HEADS, PEER_NKEYS, PEER_TOPK
    Tt = q_ref.shape[0]
    half = q_ref.shape[1] // 2
    for c in range(2):
        s = lax.dot_general(wb_ref[c], q_ref[:, c * half:(c + 1) * half], NT, preferred_element_type=F32)
        s_ref[c] = s.reshape(NK, H, Tt)
        cur_ref[c] = s_ref[c]
        _extract_topk(cur_ref, top_ref, c, exact=False)

    marked = jnp.sum(jnp.where(cur_ref[...] < -0.5 * RANK_MARK, 1.0, 0.0), axis=1)
    tied = jnp.max(jnp.where(marked != float(K), 1.0, 0.0)) > 0.0

    @pl.when(tied)
    def _():
        for c in range(2):
            cur_ref[c] = s_ref[c]
            _extract_topk(cur_ref, top_ref, c, exact=True)

    v1 = [top_ref[0, k] for k in range(K)]
    v2 = [top_ref[1, k] for k in range(K)]

    pairs = _structural_pairs(K)
    cand = [v1[i] + v2[j] for (i, j) in pairs]
    n = len(pairs)
    rank = []
    for p in range(n):
        rank.append(jnp.zeros((H, Tt), F32))
    for p in range(n):
        ip, jp = pairs[p]
        for q in range(p + 1, n):
            iq, jq = pairs[q]
            if ip <= iq and jp <= jq:
                rank[q] = rank[q] + 1.0
            else:
                b = jnp.where(cand[p] >= cand[q], 1.0, 0.0)
                rank[q] = rank[q] + b
                rank[p] = rank[p] + (1.0 - b)
    sel = [jnp.where(r < K, 1.0, 0.0) for r in rank]
    e1 = [jnp.exp(v1[i] - v1[0]) for i in range(K)]
    e2 = [jnp.exp(v2[j] - v2[0]) for j in range(K)]
    cnt = [jnp.zeros((H, Tt), F32) for _ in range(K)]
    zsum = jnp.zeros((H, Tt), F32)
    for p, (i, j) in enumerate(pairs):
        cnt[i] = cnt[i] + sel[p]
        zsum = zsum + sel[p] * (e1[i] * e2[j])
    inv_z = 1.0 / zsum

    def rank_of(c):
        cur = cur_ref[c]
        return jnp.where(cur < -0.5 * RANK_MARK, (cur * (-1.0 / RANK_MARK) - 1.0) * 32.0, float(NK))

    rk1 = rank_of(0)
    nk = jnp.zeros((NK, H, Tt), F32)
    for i in range(K):
        nk = jnp.where(rk1 == float(i), cnt[i][None], nk)
    outs = ((n_out, nk),
            (e1_out, jnp.exp(s_ref[0] - v1[0][None]) * inv_z[None]),
            (r2_out, rank_of(1)),
            (e2_out, jnp.exp(s_ref[1] - v2[0][None])))
    for ref, val in outs:
        tmp_ref[...] = val.reshape(NK * H, Tt)
        for h in range(H):
            ref[h] = tmp_ref[pl.ds(h, NK, stride=H), :].astype(ref.dtype)


def _router(q, wb, tt):
    T = q.shape[0]
    H, NK = PEER_HEADS, PEER_NKEYS
    shp = jax.ShapeDtypeStruct((H, NK, T), F32)
    shp_b = jax.ShapeDtypeStruct((H, NK, T), BF16)
    ospec = pl.BlockSpec((H, NK, tt), lambda i: (0, 0, i))
    return pl.pallas_call(
        _router_kernel,
        out_shape=[shp, shp, shp_b, shp_b],
        grid=(T // tt,),
        in_specs=[pl.BlockSpec((tt, q.shape[1]), lambda i: (i, 0)),
                  pl.BlockSpec(wb.shape, lambda i: (0, 0, 0))],
        out_specs=[ospec, ospec, ospec, ospec],
        scratch_shapes=[pltpu.VMEM((2, NK, H, tt), F32), pltpu.VMEM((2, NK, H, tt), F32),
                        pltpu.VMEM((2, PEER_TOPK, H, tt), F32), pltpu.VMEM((NK * H, tt), F32)],
        compiler_params=_cparams(("parallel",)),
        name="peer_router",
    )(q, wb)


def _gelu_tanh(x):
    return 0.5 * x * (1.0 + jnp.tanh(np.sqrt(2.0 / np.pi) * (x + 0.044715 * (x * x * x))))


PEER_ACHUNK = 2


def _peer_dense_kernel(xn_ref, u_ref, v_ref, n_ref, e1_ref, r2_ref, e2_ref, x1_ref, g_ref, o_ref, *, final_norm):
    H, NK = PEER_HEADS, PEER_NKEYS
    j = pl.program_id(1)
    na = u_ref.shape[0] // NK
    tt = xn_ref.shape[0]

    @pl.when(j == 0)
    def _():
        o_ref[...] = x1_ref[...]

    xn = xn_ref[...]
    parts = []
    for c in range(na // PEER_ACHUNK):
        rows = slice(c * PEER_ACHUNK * NK, (c + 1) * PEER_ACHUNK * NK)
        act = lax.dot_general(u_ref[rows, :], xn, NT, preferred_element_type=F32)
        gates = []
        for a in range(c * PEER_ACHUNK, (c + 1) * PEER_ACHUNK):
            row = pl.ds(j * na + a, 1)
            gate = None
            for h in range(H):
                t = jnp.where(r2_ref[h] < n_ref[h, row, :].astype(BF16),
                              e1_ref[h, row, :].astype(BF16) * e2_ref[h], jnp.zeros((), BF16))
                gate = t if gate is None else gate + t
            gates.append(gate)
        parts.append(_gelu_tanh(act).astype(BF16) * jnp.concatenate(gates, axis=0))
    wt = jnp.concatenate(parts, axis=0)
    o_ref[...] += lax.dot_general(wt, v_ref[...], TN, preferred_element_type=F32)

    if final_norm:
        @pl.when(j == pl.num_programs(1) - 1)
        def _():
            x = o_ref[...]
            o_ref[...] = x * lax.rsqrt(jnp.mean(x * x, -1, keepdims=True) + RMS_EPS) * g_ref[...]


def _peer_dense(xn, u, v, layer, nk, e1, r2, e2, x1, g, tt, te, final_norm):
    T, D = xn.shape
    E = u.shape[1]
    H, NK = PEER_HEADS, PEER_NKEYS
    once = pl.Buffered(1)
    tab = pl.BlockSpec((H, NK, tt), lambda i, j: (0, 0, i), pipeline_mode=once)
    return pl.pallas_call(
        functools.partial(_peer_dense_kernel, final_norm=final_norm),
        out_shape=jax.ShapeDtypeStruct((T, D), F32),
        grid=(T // tt, E // te),
        in_specs=[pl.BlockSpec((tt, D), lambda i, j: (i, 0), pipeline_mode=once),
                  pl.BlockSpec((None, te, D), lambda i, j: (layer, j, 0)),
                  pl.BlockSpec((None, te, D), lambda i, j: (layer, j, 0)),
                  tab, tab, tab, tab,
                  pl.BlockSpec((tt, D), lambda i, j: (i, 0), pipeline_mode=once),
                  pl.BlockSpec((1, D), lambda i, j: (0, 0))],
        out_specs=pl.BlockSpec((tt, D), lambda i, j: (i, 0)),
        compiler_params=_cparams(("parallel", "arbitrary")),
        name="peer_dense",
    )(xn, u, v, nk, e1, r2, e2, x1, g.reshape(1, D))


T_TILE = 384
ROUTER_TILE = 128
PEER_T_TILE = 384
PAD_TILE = 768
Z_COL_TILE = 1280
OUT_TILE = 256
EXPERT_TILE = 1024


def kernel(x_prompt, x_sample, cache_nsa_kv, state_win_kv, state_conv, state_mlstm_C, state_mlstm_n,
           state_mlstm_m, page_table, norm1_g, w_in, conv_w, mlstm_gate_b, mlstm_norm_g, w_out, norm2_g,
           peer_wq, peer_subkeys, peer_u, peer_v, final_norm_g):
    BP, SP, D = x_prompt.shape
    BS, SS, _ = x_sample.shape
    depth = w_in.shape[0]
    KVH, G, HD = NSA_KV_HEADS, NSA_GROUP, HEAD_DIM
    H, DH = MLSTM_HEADS, MLSTM_DH
    CW = conv_w.shape[2]
    NW = KVH * G * HD
    KVW = 6 * KVH * HD
    NG = 3 * KVH * G
    MW = H * DH
    TP, TS = BP * SP, BS * SS
    T = TP + TS
    assert PAD_TILE % T_TILE == 0 and PAD_TILE % ROUTER_TILE == 0 and PAD_TILE % PEER_T_TILE == 0
    TPAD = -(-T // PAD_TILE) * PAD_TILE
    n_pages = page_table.shape[1]
    P = cache_nsa_kv.shape[2]
    past = n_pages * P
    wb_rows = state_win_kv.shape[2]

    c_q = 3 * CW
    c_kv = c_q + NW
    c_m = c_kv + KVW
    c_g = c_m + 4 * MW
    ZW = -(-(c_g + LANES) // Z_COL_TILE) * Z_COL_TILE
    assert CW % LANES == 0 and c_m % MW == 0 and c_g % LANES == 0 and NG + 2 * H <= LANES
    gate_cb = c_g // LANES
    gi, gf = NG, NG + H
    o_gate = 3 * CW + NW + KVW

    x = jnp.concatenate([x_prompt.reshape(TP, D), x_sample.reshape(TS, D),
                         jnp.zeros((TPAD - T, D), F32)], axis=0)
    cache_t = jnp.transpose(cache_nsa_kv, (0, 1, 3, 4, 5, 2))
    eye_h = jnp.eye(PEER_HEADS, dtype=F32)
    u_bf = peer_u.astype(BF16)
    v_bf = peer_v.astype(BF16)
    dk = peer_subkeys.shape[-1]

    p_st, s_st = [], []
    for l in range(depth):
        wi = w_in[l]
        w_perm = jnp.concatenate(
            [wi[:, :o_gate], wi[:, o_gate + NG:o_gate + NG + 4 * MW], wi[:, o_gate:o_gate + NG],
             wi[:, o_gate + NG + 4 * MW:], jnp.zeros((D, ZW - wi.shape[1]), F32)], axis=1).astype(BF16)
        z = _rms_proj(x, norm1_g[l], w_perm, PAD_TILE, Z_COL_TILE, F32)

        ya_p, conv_p = _conv(z, 0, BP, SP, jnp.zeros((BP, 2, CW), F32), conv_w[l])
        ya_s, conv_s = _conv(z, TP, BS, SS, state_conv[l], conv_w[l])

        yb_p = _nsa_prompt(z, c_q, c_kv, gate_cb, BP, SP)
        zkv = z[:TP, c_kv:c_kv + KVW].reshape(BP, SP, 3, 2, KVH, HD)
        rows_p = zkv[:, :, :2].reshape(BP, SP, 4, KVH, HD)
        win_p = zkv[:, SP - min(WINDOW, SP):, 2]

        zs = z[TP:T]
        q_s = zs[:, c_q:c_q + NW].reshape(BS, SS, KVH, G, HD).transpose(0, 2, 3, 1, 4)
        q_s = q_s.reshape(BS, KVH, G * SS, HD).astype(BF16)
        zkv_s = zs[:, c_kv:c_kv + KVW].reshape(BS, SS, 3, 2, KVH, HD)
        rows_s = zkv_s[:, :, :2].reshape(BS, SS, 4, KVH, HD)
        tail_t = jnp.pad(rows_s.transpose(0, 2, 3, 4, 1), ((0, 0), (0, 0), (0, 0), (0, 0), (0, P - SS)))
        o_cmp, o_sel = _nsa_sample_global(cache_t, l, page_table, tail_t, q_s, SS)
        w_all = jnp.concatenate([state_win_kv[l], zkv_s[:, :, 2]], axis=1)
        kwin = w_all[:, :, 0].reshape(BS, wb_rows + SS, KVH * HD)
        vwin = w_all[:, :, 1].reshape(BS, wb_rows + SS, KVH * HD)
        yb_s = _nsa_sample_combine(q_s, kwin, vwin, o_cmp, o_sel, z, TP, gate_cb, past, SS)
        win_s = w_all[:, SS:]

        yc_p, C_p, n_p, m_p = _mlstm(z, 0, BP, SP, c_m // MW, gate_cb, gi, gf, mlstm_gate_b[l],
                                     mlstm_norm_g[l], jnp.zeros((BP, H, DH, DH), F32),
                                     jnp.zeros((BP, H, DH), F32), jnp.zeros((BP, H), F32))
        yc_s, C_s, n_s, m_s = _mlstm(z, TP, BS, SS, c_m // MW, gate_cb, gi, gf, mlstm_gate_b[l],
                                     mlstm_norm_g[l], state_mlstm_C[l], state_mlstm_n[l], state_mlstm_m[l])

        ytail = jnp.concatenate([jnp.concatenate([ya_s, yb_s, yc_s], axis=1),
                                 jnp.zeros((TPAD - T, D), F32)], axis=0)
        x1 = _out_proj(x, ya_p, yb_p, yc_p, ytail, w_out[l].astype(BF16), OUT_TILE)

        nh = PEER_HEADS
        wq_perm = peer_wq[l].reshape(D, nh, 2, dk).transpose(0, 2, 1, 3).reshape(D, 2 * nh * dk).astype(BF16)
        q_peer, xn2 = _rms_proj(x1, norm2_g[l], wq_perm, T_TILE, nh * dk, BF16, emit_xn=True)
        wb = jnp.einsum('hcnd,hg->cnhgd', peer_subkeys[l], eye_h).reshape(2, PEER_NKEYS * nh, nh * dk).astype(BF16)
        nk, e1, r2, e2 = _router(q_peer, wb, ROUTER_TILE)
        x = _peer_dense(xn2, u_bf, v_bf, l, nk, e1, r2, e2, x1,
                        final_norm_g, PEER_T_TILE, EXPERT_TILE, final_norm=(l == depth - 1))

        p_st.append((rows_p, win_p, conv_p, C_p, n_p, m_p))
        s_st.append((rows_s, win_s, conv_s, C_s, n_s, m_s))

    p_rows, p_win, p_conv, p_C, p_n, p_m = [jnp.stack(a) for a in zip(*p_st)]
    s_rows, s_win, s_conv, s_C, s_n, s_m = [jnp.stack(a) for a in zip(*s_st)]
    y_prompt = x[:TP].reshape(BP, SP, D)
    y_sample = x[TP:T].reshape(BS, SS, D)
    return (y_prompt, y_sample, p_rows, p_win, p_conv, p_C, p_n, p_m, s_rows, s_win, s_conv, s_C, s_n, s_m)
```

```python
import functools

import numpy as np
import jax
import jax.numpy as jnp
from jax import lax
from jax.experimental import pallas as pl
from jax.experimental.pallas import tpu as pltpu

F32 = jnp.float32
BF16 = jnp.bfloat16
HI = lax.Precision.HIGHEST

RMS_EPS = 1e-6
HEAD_DIM = 64
NSA_KV_HEADS = 4
NSA_GROUP = 4
CMP_STRIDE = 16
SEL_BLOCK = 64
SEL_TOPN = 16
WINDOW = 512
MLSTM_HEADS = 4
MLSTM_DH = 128
MLSTM_CHUNK = 64
PEER_HEADS = 8
PEER_NKEYS = 128
PEER_TOPK = 16
MASK_BIG = 1e9
NEG = -1e30

LANES = 128
VMEM_LIMIT = 56 * 1024 * 1024

NT = (((1,), (1,)), ((), ()))
TN = (((0,), (0,)), ((), ()))


def _cparams(sem):
    return pltpu.CompilerParams(dimension_semantics=sem, vmem_limit_bytes=VMEM_LIMIT)


def _div(x, d):
    assert d & (d - 1) == 0
    return lax.shift_right_arithmetic(x, jnp.int32(d.bit_length() - 1))


def _pool_dot(pool, x, x_is_lhs):
    hi = x.astype(BF16)
    lo = (x - hi.astype(F32)).astype(BF16)
    if x_is_lhs:
        return (jnp.dot(hi, pool, preferred_element_type=F32) + jnp.dot(lo, pool, preferred_element_type=F32))
    return (jnp.dot(pool, hi, preferred_element_type=F32) + jnp.dot(pool, lo, preferred_element_type=F32))


def _masked_softmax(s, mask, exp=jnp.exp):
    s = jnp.where(mask, s, NEG)
    m = jnp.max(s, -1, keepdims=True)
    e = jnp.where(mask, exp(s - m), 0.0)
    d = jnp.maximum(jnp.sum(e, -1, keepdims=True), 1e-30)
    return e * (1.0 / d)


def _rms_proj_kernel(x_ref, g_ref, w_ref, *rest, emit_xn):
    if emit_xn:
        o_ref, xo_ref, xn_ref = rest
    else:
        o_ref, xn_ref = rest

    @pl.when(pl.program_id(1) == 0)
    def _():
        x = x_ref[...]
        r = x * lax.rsqrt(jnp.mean(x * x, -1, keepdims=True) + RMS_EPS)
        xn = (r * g_ref[...]).astype(BF16)
        xn_ref[...] = xn
        if emit_xn:
            xo_ref[...] = xn

    o_ref[...] = jnp.dot(xn_ref[...], w_ref[...], preferred_element_type=F32).astype(o_ref.dtype)


def _rms_proj(x, g, w, tm, tn, out_dtype, emit_xn=False):
    T, D = x.shape
    N = w.shape[1]
    out_shape = [jax.ShapeDtypeStruct((T, N), out_dtype)]
    out_specs = [pl.BlockSpec((tm, tn), lambda i, j: (i, j))]
    if emit_xn:
        out_shape.append(jax.ShapeDtypeStruct((T, D), BF16))
        out_specs.append(pl.BlockSpec((tm, D), lambda i, j: (i, 0)))
    res = pl.pallas_call(
        functools.partial(_rms_proj_kernel, emit_xn=emit_xn),
        out_shape=out_shape,
        grid=(T // tm, N // tn),
        in_specs=[pl.BlockSpec((tm, D), lambda i, j: (i, 0)),
                  pl.BlockSpec((1, D), lambda i, j: (0, 0)),
                  pl.BlockSpec((D, tn), lambda i, j: (0, j))],
        out_specs=out_specs,
        scratch_shapes=[pltpu.VMEM((tm, D), BF16)],
        compiler_params=_cparams(("parallel", "arbitrary")),
        name="rms_proj",
    )(x, g.reshape(1, D), w)
    return res if emit_xn else res[0]


def _conv_kernel(cb_ref, cc_ref, ch_ref, buf_ref, w_ref, y_ref, new_ref, ext_ref, *, S):
    C = cb_ref.shape[-1]
    u = cc_ref[...] * ch_ref[...]
    ext_ref[0:8, :] = jnp.zeros((8, C), F32)
    ext_ref[6:8, :] = buf_ref[0]
    ext_ref[8:8 + S, :] = u
    w = w_ref[...]
    y = w[0:1] * ext_ref[6:6 + S, :] + w[1:2] * ext_ref[7:7 + S, :] + w[2:3] * u
    y_ref[...] = cb_ref[...] * y
    new_ref[0] = u[S - 2:S]


def _conv(z, row0, B, S, buf, w):
    C = w.shape[1]
    rb0 = row0 // S
    return pl.pallas_call(
        functools.partial(_conv_kernel, S=S),
        out_shape=[jax.ShapeDtypeStruct((B * S, C), F32), jax.ShapeDtypeStruct((B, 2, C), F32)],
        grid=(B,),
        in_specs=[pl.BlockSpec((S, C), lambda b: (rb0 + b, 0)),
                  pl.BlockSpec((S, C), lambda b: (rb0 + b, 1)),
                  pl.BlockSpec((S, C), lambda b: (rb0 + b, 2)),
                  pl.BlockSpec((1, 2, C), lambda b: (b, 0, 0)),
                  pl.BlockSpec((3, C), lambda b: (0, 0))],
        out_specs=[pl.BlockSpec((S, C), lambda b: (b, 0)),
                   pl.BlockSpec((1, 2, C), lambda b: (b, 0, 0))],
        scratch_shapes=[pltpu.VMEM((S + 8, C), F32)],
        compiler_params=_cparams(("parallel",)),
        name="short_conv",
    )(z, z, z, buf, w)


def _topn_rank_select(score, n_sel):
    NB = score.shape[1]
    blk = lax.broadcasted_iota(jnp.int32, score.shape, 1)
    rank = jnp.zeros(score.shape, F32)
    for i in range(NB):
        ci = score[:, i:i + 1]
        beats = jnp.where(ci > score, 1.0, jnp.where((ci == score) & (blk > i), 1.0, 0.0))
        rank = rank + beats
    return rank < n_sel


SEL_KCHUNK = 512


def _exp2_softmax_pv(pieces, hd):
    m = None
    for s, _ in pieces:
        mm = jnp.max(s, -1, keepdims=True)
        m = mm if m is None else jnp.maximum(m, mm)
    oa = None
    for s, v in pieces:
        t = jnp.dot(jnp.exp2(s - m).astype(BF16), v, preferred_element_type=F32)
        oa = t if oa is None else oa + t
    return oa[:, 0:hd] * (1.0 / oa[:, hd:hd + 1])


NSA_HPS = 2


def _nsa_prompt_kernel(q_ref, kcs_ref, vcs_ref, kss_ref, vss_ref, kws_ref, vws_ref, gt_ref,
                       y_ref, kc_ref, vc_ref, ks_ref, vs_ref, kw_ref, vw_ref, osel_s, *, S, tq):
    pair = pl.program_id(1)
    qi = pl.program_id(2)
    G, HD = NSA_GROUP, HEAD_DIM
    NC = S // CMP_STRIDE
    NB = S // SEL_BLOCK
    n_sel = min(SEL_TOPN, NB)
    span = min(WINDOW + tq, S)

    @pl.when(qi == 0)
    def _():
        j = lax.broadcasted_iota(jnp.int32, (NC, S), 0)
        r = lax.broadcasted_iota(jnp.int32, (NC, S), 1)
        lo = j * CMP_STRIDE
        pool = jnp.where((r >= lo) & (r < lo + 2 * CMP_STRIDE), 0.5 / CMP_STRIDE, 0.0).astype(BF16)
        kcp = _pool_dot(pool, kcs_ref[...], False).astype(BF16)
        vcp = _pool_dot(pool, vcs_ref[...], False).astype(BF16)
        blk1h = jnp.where(_div(lax.broadcasted_iota(jnp.int32, (S, NB), 0), SEL_BLOCK)
                          == lax.broadcasted_iota(jnp.int32, (S, NB), 1), 1.0, 0.0).astype(BF16)
        kpad = jnp.zeros((S, LANES - HD - NB), BF16)
        ones = jnp.where(lax.broadcasted_iota(jnp.int32, (S, LANES - HD), 1) == 0, 1.0, 0.0).astype(BF16)
        for hh in range(NSA_HPS):
            cols = slice(hh * HD, (hh + 1) * HD)
            kc_ref[hh] = kcp[:, cols]
            vc_ref[hh] = vcp[:, cols]
            ks_ref[hh] = jnp.concatenate([kss_ref[:, cols].astype(BF16), blk1h, kpad], axis=1)
            vs_ref[hh] = jnp.concatenate([vss_ref[:, cols].astype(BF16), ones], axis=1)
            kw_ref[hh] = kws_ref[:, cols].astype(BF16)
            vw_ref[hh] = jnp.concatenate([vws_ref[:, cols].astype(BF16), ones], axis=1)

    t0 = qi * tq
    row = lax.broadcasted_iota(jnp.int32, (G * tq, 1), 0)
    pos = t0 + (row & (tq - 1))
    posq = t0 + lax.broadcasted_iota(jnp.int32, (tq, 1), 0)
    c_end = lax.broadcasted_iota(jnp.int32, (1, NC), 1) * CMP_STRIDE + (2 * CMP_STRIDE - 1)
    cmask = c_end <= pos
    per = SEL_BLOCK // CMP_STRIDE
    e4 = jnp.where(_div(lax.broadcasted_iota(jnp.int32, (NC, NB), 0), per)
                   == lax.broadcasted_iota(jnp.int32, (NC, NB), 1), 1.0, 0.0).astype(F32)
    blk = lax.broadcasted_iota(jnp.int32, (1, NB), 1)
    qblk = _div(posq, SEL_BLOCK)
    valid = blk * SEL_BLOCK <= posq
    forced = (blk == 0) | (blk == qblk) | (blk == qblk - 1)

    qscale = (HD ** -0.5) * np.log2(np.e)
    qss, qas, o_cmps = [], [], []
    for hh in range(NSA_HPS):
        qh = q_ref[:, hh * G * HD:(hh + 1) * G * HD] * qscale
        qs = jnp.concatenate([qh[:, g * HD:(g + 1) * HD] for g in range(G)], axis=0).astype(BF16)
        s = lax.dot_general(qs, kc_ref[hh], NT, preferred_element_type=F32)
        p = _masked_softmax(s, cmask, jnp.exp2)
        o_cmps.append(jnp.dot(p.astype(BF16), vc_ref[hh], preferred_element_type=F32))
        imp = p[0:tq]
        for g in range(1, G):
            imp = imp + p[g * tq:(g + 1) * tq]
        impb = jnp.dot(imp, e4, precision=HI, preferred_element_type=F32)
        score = jnp.where(forced, MASK_BIG, jnp.where(valid, impb, -MASK_BIG))
        sel = _topn_rank_select(score, n_sel) & (score > -0.5 * MASK_BIG)
        selneg = jnp.where(sel, 0.0, NEG).astype(BF16)
        qas.append(jnp.concatenate([qs, jnp.concatenate([selneg] * G, axis=0),
                                    jnp.zeros((G * tq, LANES - HD - NB), BF16)], axis=1))
        qss.append(qs)

    def sel_branch(klen):
        tail = min(klen, SEL_KCHUNK)
        kpos = (klen - tail) + lax.broadcasted_iota(jnp.int32, (1, tail), 1)
        tail_bias = jnp.where(kpos <= posq, 0.0, NEG)
        for hh in range(NSA_HPS):
            for g in range(G):
                qa = qas[hh][g * tq:(g + 1) * tq]
                pieces = []
                if klen > tail:
                    pieces.append((lax.dot_general(qa, ks_ref[hh, 0:klen - tail, :], NT,
                                                   preferred_element_type=F32),
                                   vs_ref[hh, 0:klen - tail, :]))
                pieces.append((lax.dot_general(qa, ks_ref[hh, klen - tail:klen, :], NT,
                                               preferred_element_type=F32) + tail_bias,
                               vs_ref[hh, klen - tail:klen, :]))
                osel_s[hh, g * tq:(g + 1) * tq, :] = _exp2_softmax_pv(pieces, HD)

    nvar = -(-S // SEL_KCHUNK)
    per_var = SEL_KCHUNK // tq
    for v in range(nvar):
        @pl.when(qi // per_var == v)
        def _(v=v):
            sel_branch(min(S, (v + 1) * SEL_KCHUNK))

    start = pl.multiple_of(jnp.maximum(t0 + tq - span, 0), tq)
    diff = posq - (start + lax.broadcasted_iota(jnp.int32, (1, span), 1))
    win_bias = jnp.where((diff >= 0) & (diff < WINDOW), 0.0, NEG)

    sg = jax.nn.sigmoid(gt_ref[...])
    outs = []
    for hh in range(NSA_HPS):
        kvh = pair * NSA_HPS + hh
        pick = jnp.where(lax.broadcasted_iota(jnp.int32, (LANES, LANES), 0)
                         == lax.broadcasted_iota(jnp.int32, (LANES, LANES), 1) + kvh * (3 * G),
                         1.0, 0.0).astype(F32)
        g12 = jnp.dot(sg, pick, precision=HI, preferred_element_type=F32)
        kw = kw_ref[hh, pl.ds(start, span), :]
        vw = vw_ref[hh, pl.ds(start, span), :]
        for g in range(G):
            rows = slice(g * tq, (g + 1) * tq)
            s_win = lax.dot_general(qss[hh][rows], kw, NT, preferred_element_type=F32) + win_bias
            o_win = _exp2_softmax_pv([(s_win, vw)], HD)
            outs.append(g12[:, 3 * g:3 * g + 1] * o_cmps[hh][rows]
                        + g12[:, 3 * g + 1:3 * g + 2] * osel_s[hh, rows, :]
                        + g12[:, 3 * g + 2:3 * g + 3] * o_win)
    y_ref[...] = jnp.concatenate(outs, axis=-1)


def _nsa_prompt(z, c_q, c_kv, gate_cb, B, S, tq=256):
    nq = S // tq
    KVH, G, HD = NSA_KV_HEADS, NSA_GROUP, HEAD_DIM
    NC = S // CMP_STRIDE
    qw = NSA_HPS * G * HD
    assert NSA_HPS * HD == LANES and c_q % qw == 0 and c_kv % LANES == 0

    def kv_spec(i):
        cb = (c_kv + i * KVH * HD) // LANES
        return pl.BlockSpec((S, LANES), lambda b, p, t, cb=cb: (b, cb + p))

    head_kv = pltpu.VMEM((NSA_HPS, S, HD), BF16)
    head_aug = pltpu.VMEM((NSA_HPS, S, LANES), BF16)
    assert HD + S // SEL_BLOCK <= LANES
    return pl.pallas_call(
        functools.partial(_nsa_prompt_kernel, S=S, tq=tq),
        out_shape=jax.ShapeDtypeStruct((B * S, KVH * G * HD), F32),
        grid=(B, KVH // NSA_HPS, nq),
        in_specs=[pl.BlockSpec((tq, qw), lambda b, p, t: (b * nq + t, c_q // qw + p)),
                  kv_spec(0), kv_spec(1), kv_spec(2), kv_spec(3), kv_spec(4), kv_spec(5),
                  pl.BlockSpec((tq, LANES), lambda b, p, t: (b * nq + t, gate_cb))],
        out_specs=pl.BlockSpec((tq, qw), lambda b, p, t: (b * nq + t, p)),
        scratch_shapes=[pltpu.VMEM((NSA_HPS, NC, HD), BF16), pltpu.VMEM((NSA_HPS, NC, HD), BF16),
                        head_aug, head_aug, head_kv, head_aug,
                        pltpu.VMEM((NSA_HPS, G * tq, HD), F32)],
        compiler_params=_cparams(("parallel", "parallel", "arbitrary")),
        name="nsa_prompt",
    )(z, z, z, z, z, z, z, z)


def _log_sigmoid(x):
    return jnp.minimum(x, 0.0) - jnp.log(1.0 + jnp.exp(-jnp.abs(x)))


def _mlstm_kernel(gb_ref, q_ref, k_ref, v_ref, o_ref, gt_ref, mg_ref, c0_ref, n0_ref, m0_ref,
                  y_ref, cn_ref, nn_ref, mn_ref, c_s, n_s, m_s, *, L, Lb, gi, gf):
    H, DH = MLSTM_HEADS, MLSTM_DH
    c = pl.program_id(1)

    @pl.when(c == 0)
    def _():
        c_s[...] = c0_ref[0]
        n_s[...] = n0_ref[0]
        m_s[...] = m0_ref[0]

    def padrows(a):
        if Lb == L:
            return a
        return jnp.concatenate([a, jnp.zeros((L - Lb, a.shape[1]), a.dtype)], axis=0)

    lane = lax.broadcasted_iota(jnp.int32, (1, LANES), 1)
    bias = jnp.zeros((1, LANES), F32)
    for h in range(H):
        bias = bias + jnp.where(lane == gi + h, gb_ref[0, h], 0.0) + jnp.where(lane == gf + h, gb_ref[1, h], 0.0)
    is_f = (lane >= gf) & (lane < gf + H)
    pre = padrows(gt_ref[...]) + bias
    gate = jnp.where(is_f, _log_sigmoid(pre), pre)
    if Lb != L:
        live = lax.broadcasted_iota(jnp.int32, (L, 1), 0) < Lb
        gate = jnp.where(live, gate, jnp.where(is_f, 0.0, NEG))
    rr = lax.broadcasted_iota(jnp.int32, (L, L), 0)
    cc = lax.broadcasted_iota(jnp.int32, (L, L), 1)
    tril = rr >= cc
    bcum = jnp.dot(jnp.where(tril, 1.0, 0.0).astype(F32), gate, precision=HI, preferred_element_type=F32)
    e8 = jnp.where(lax.broadcasted_iota(jnp.int32, (8, LANES), 1)
                   == lax.broadcasted_iota(jnp.int32, (8, LANES), 0) + gi, 1.0, 0.0).astype(F32)
    rg = lax.dot_general(e8, gate, NT, precision=HI, preferred_element_type=F32)
    rb = lax.dot_general(e8, bcum, NT, precision=HI, preferred_element_type=F32)

    q = padrows(q_ref[...])
    k = padrows(k_ref[...])
    v = padrows(v_ref[...])
    og = padrows(o_ref[...])
    ys = []
    for h in range(H):
        sl = slice(h * DH, (h + 1) * DH)
        qq = q[:, sl]
        kk = k[:, sl] * (DH ** -0.5)
        vv = v[:, sl]
        qb, kb, vb = qq.astype(BF16), kk.astype(BF16), vv.astype(BF16)
        b_col = bcum[:, gf + h:gf + h + 1]
        b_row = rb[H + h:H + h + 1, :]
        i_row = rg[h:h + 1, :]
        i_col = gate[:, gi + h:gi + h + 1]
        m_prev = m_s[h][:, 0:1]
        cmat = c_s[h]
        n_row = n_s[h]
        dmat = jnp.where(tril, b_col - b_row + i_row, NEG)
        inter = b_col + m_prev
        mt = jnp.maximum(inter, jnp.max(dmat, -1, keepdims=True))
        w = jnp.exp(dmat - mt)
        a = jnp.exp(inter - mt)
        wqk = w * lax.dot_general(qb, kb, NT, preferred_element_type=F32)
        num = (a * jnp.dot(qb, cmat.astype(BF16), preferred_element_type=F32)
               + jnp.dot(wqk.astype(BF16), vb, preferred_element_type=F32))
        den = a * jnp.sum(qq * n_row, -1, keepdims=True) + jnp.sum(wqk, -1, keepdims=True)
        hh = num * (1.0 / jnp.maximum(jnp.abs(den), jnp.exp(-mt)))
        b_last = b_col[L - 1:L]
        m_new = mt[L - 1:L]
        wl = jnp.exp(b_last - b_col + i_col - m_new)
        decay = jnp.exp(b_last + m_prev - m_new)
        kw = wl * kk
        c_s[h] = decay * cmat + lax.dot_general(kw.astype(BF16), vb, TN, preferred_element_type=F32)
        n_s[h] = decay * n_row + jnp.sum(kw, 0, keepdims=True)
        m_s[h] = jnp.broadcast_to(m_new, (1, LANES))
        hn = hh * lax.rsqrt(jnp.mean(hh * hh, -1, keepdims=True) + RMS_EPS)
        ys.append(jax.nn.sigmoid(og[:, sl]) * hn * mg_ref[:, sl])
    y = jnp.concatenate(ys, axis=-1)
    y_ref[...] = y[0:Lb]

    @pl.when(c == pl.num_programs(1) - 1)
    def _():
        cn_ref[0] = c_s[...]
        nn_ref[0] = n_s[...]
        mn_ref[0] = m_s[...]


def _mlstm(z, row0, B, S, col_q, gate_cb, gi, gf, gate_b, mnorm_g, C0, n0, m0):
    H, DH = MLSTM_HEADS, MLSTM_DH
    W = H * DH
    L = MLSTM_CHUNK
    Lb = L if S % L == 0 else S
    assert Lb <= L
    nc = S // Lb
    rb0 = row0 // Lb
    n0 = n0.reshape(B, H, 1, DH)
    m0 = jnp.broadcast_to(m0.reshape(B, H, 1, 1), (B, H, 1, LANES))

    def zspec(cb, width):
        return pl.BlockSpec((Lb, width), lambda b, c, cb=cb: (rb0 + b * nc + c, cb))

    y, C, n, m = pl.pallas_call(
        functools.partial(_mlstm_kernel, L=L, Lb=Lb, gi=gi, gf=gf),
        out_shape=[jax.ShapeDtypeStruct((B * S, W), F32),
                   jax.ShapeDtypeStruct((B, H, DH, DH), F32),
                   jax.ShapeDtypeStruct((B, H, 1, DH), F32),
                   jax.ShapeDtypeStruct((B, H, 1, LANES), F32)],
        grid=(B, nc),
        in_specs=[pl.BlockSpec(memory_space=pltpu.SMEM),
                  zspec(col_q, W), zspec(col_q + 1, W), zspec(col_q + 2, W), zspec(col_q + 3, W),
                  zspec(gate_cb, LANES),
                  pl.BlockSpec((1, W), lambda b, c: (0, 0)),
                  pl.BlockSpec((1, H, DH, DH), lambda b, c: (b, 0, 0, 0)),
                  pl.BlockSpec((1, H, 1, DH), lambda b, c: (b, 0, 0, 0)),
                  pl.BlockSpec((1, H, 1, LANES), lambda b, c: (b, 0, 0, 0))],
        out_specs=[pl.BlockSpec((Lb, W), lambda b, c: (b * nc + c, 0)),
                   pl.BlockSpec((1, H, DH, DH), lambda b, c: (b, 0, 0, 0)),
                   pl.BlockSpec((1, H, 1, DH), lambda b, c: (b, 0, 0, 0)),
                   pl.BlockSpec((1, H, 1, LANES), lambda b, c: (b, 0, 0, 0))],
        scratch_shapes=[pltpu.VMEM((H, DH, DH), F32), pltpu.VMEM((H, 1, DH), F32),
                        pltpu.VMEM((H, 1, LANES), F32)],
        compiler_params=_cparams(("parallel", "arbitrary")),
        name="mlstm",
    )(gate_b, z, z, z, z, z, mnorm_g.reshape(1, W), C0, n0, m0)
    return y, C, n.reshape(B, H, DH), m[:, :, 0, 0]


PAGES_PER_STEP = 16


def _lane_extract_topn(score, n_sel, floor):
    lane = lax.broadcasted_iota(jnp.int32, score.shape, 1).astype(F32)
    sel = jnp.zeros(score.shape, F32)
    sc = score
    for _ in range(n_sel):
        m = jnp.max(sc, -1, keepdims=True)
        idx = jnp.min(jnp.where(sc == m, lane, float(score.shape[1])), -1, keepdims=True)
        hit = lane == idx
        sel = jnp.where(hit & (m > floor), 1.0, sel)
        sc = jnp.where(hit, -jnp.inf, sc)
    return sel


def _s1_kernel(pt_ref, *refs, past, S, ncp, nbp):
    pgs = refs[:PAGES_PER_STEP]
    tail_ref, q_ref, ocmp_ref, sel_ref, sub_ref = refs[PAGES_PER_STEP:]
    P = pgs[0].shape[-1]
    j = pl.program_id(1)
    nfull = pl.num_programs(1) - 1
    KVH, G, HD = NSA_KV_HEADS, NSA_GROUP, HEAD_DIM
    cols_step = PAGES_PER_STEP * P // CMP_STRIDE
    assert cols_step == LANES

    @pl.when(j == 0)
    def _():
        sub_ref[...] = jnp.zeros(sub_ref.shape, F32)

    def pool_t(n):
        return jnp.where(_div(lax.broadcasted_iota(jnp.int32, (n, LANES), 0), CMP_STRIDE)
                         == lax.broadcasted_iota(jnp.int32, (n, LANES), 1),
                         1.0 / CMP_STRIDE, 0.0).astype(BF16)

    @pl.when(j < nfull)
    def _():
        pt = pool_t(PAGES_PER_STEP * P)
        col = pl.ds(pl.multiple_of(j * cols_step, cols_step), cols_step)
        for k in range(2):
            for h in range(KVH):
                xt = jnp.concatenate([r[0, 0, k, h] for r in pgs], axis=1)
                sub_ref[k, h, :, col] = _pool_dot(pt, xt, True)

    @pl.when(j == nfull)
    def _():
        base = past // CMP_STRIDE
        pt = pool_t(P)
        for k in range(2):
            for h in range(KVH):
                sub_ref[k, h, :, base:base + LANES] = _pool_dot(pt, tail_ref[0, k, h], True)
        R = G * S
        row = lax.broadcasted_iota(jnp.int32, (R, 1), 0)
        pos = past + (row & (S - 1))
        posq = past + lax.broadcasted_iota(jnp.int32, (S, 1), 0)
        c_end = lax.broadcasted_iota(jnp.int32, (1, ncp), 1) * CMP_STRIDE + (2 * CMP_STRIDE - 1)
        cmask = c_end <= pos
        per = SEL_BLOCK // CMP_STRIDE
        e4 = jnp.where(_div(lax.broadcasted_iota(jnp.int32, (ncp, nbp), 0), per)
                       == lax.broadcasted_iota(jnp.int32, (ncp, nbp), 1), 1.0, 0.0).astype(F32)
        blk = lax.broadcasted_iota(jnp.int32, (1, nbp), 1)
        qblk = _div(posq, SEL_BLOCK)
        valid = blk * SEL_BLOCK <= posq
        forced = (blk == 0) | (blk == qblk) | (blk == qblk - 1)
        for h in range(KVH):
            kct = 0.5 * (sub_ref[0, h, :, 0:ncp] + sub_ref[0, h, :, 1:ncp + 1])
            vct = 0.5 * (sub_ref[1, h, :, 0:ncp] + sub_ref[1, h, :, 1:ncp + 1])
            qs = q_ref[0, h] * jnp.asarray(HD ** -0.5, BF16)
            s = jnp.dot(qs, kct.astype(BF16), preferred_element_type=F32)
            p = _masked_softmax(s, cmask)
            ocmp_ref[0, h] = lax.dot_general(p.astype(BF16), vct.astype(BF16), NT, preferred_element_type=F32)
            imp = p[0:S]
            for g in range(1, G):
                imp = imp + p[g * S:(g + 1) * S]
            impb = jnp.dot(imp, e4, precision=HI, preferred_element_type=F32)
            score = jnp.where(forced, MASK_BIG, jnp.where(valid, impb, -MASK_BIG))
            sel_ref[0, h] = _lane_extract_topn(score, SEL_TOPN, -0.5 * MASK_BIG)


def _s2_kernel(pt_ref, *refs, past, S, nbp):
    pgs = refs[:PAGES_PER_STEP]
    tail_ref, q_ref, sel_ref, osel_ref, m_s, l_s, acc_s = refs[PAGES_PER_STEP:]
    P = pgs[0].shape[-1]
    j = pl.program_id(1)
    nfull = pl.num_programs(1) - 1
    KVH, G, HD = NSA_KV_HEADS, NSA_GROUP, HEAD_DIM
    R = G * S

    @pl.when(j == 0)
    def _():
        m_s[...] = jnp.full(m_s.shape, NEG, F32)
        l_s[...] = jnp.zeros(l_s.shape, F32)
        acc_s[...] = jnp.zeros(acc_s.shape, F32)

    row = lax.broadcasted_iota(jnp.int32, (R, 1), 0)
    pos = past + (row & (S - 1))

    def process(kt_of, vt_of, n, kpos0):
        kpos = kpos0 + lax.broadcasted_iota(jnp.int32, (1, n), 1)
        kblk = _div(kpos, SEL_BLOCK)
        esel = jnp.where(lax.broadcasted_iota(jnp.int32, (nbp, n), 0) == kblk, 1.0, 0.0).astype(BF16)
        causal = kpos <= pos
        for h in range(KVH):
            kt = kt_of(h).astype(BF16)
            vt = vt_of(h).astype(BF16)
            qs = q_ref[0, h] * jnp.asarray(HD ** -0.5, BF16)
            s = jnp.dot(qs, kt, preferred_element_type=F32)
            mk = jnp.dot(sel_ref[0, h].astype(BF16), esel, preferred_element_type=F32)
            mask = (jnp.concatenate([mk] * G, axis=0) > 0.5) & causal
            sm = jnp.where(mask, s, NEG)
            m_old = m_s[h]
            m_new = jnp.maximum(m_old, jnp.max(sm, -1, keepdims=True))
            alpha = jnp.exp(m_old - m_new)
            e = jnp.where(mask, jnp.exp(sm - m_new), 0.0)
            l_s[h] = alpha * l_s[h] + jnp.sum(e, -1, keepdims=True)
            acc_s[h] = alpha * acc_s[h] + lax.dot_general(e.astype(BF16), vt, NT, preferred_element_type=F32)
            m_s[h] = m_new

    @pl.when(j < nfull)
    def _():
        process(lambda h: jnp.concatenate([r[0, 0, 0, h] for r in pgs], axis=1),
                lambda h: jnp.concatenate([r[0, 0, 1, h] for r in pgs], axis=1),
                PAGES_PER_STEP * P, j * (PAGES_PER_STEP * P))

    @pl.when(j == nfull)
    def _():
        process(lambda h: tail_ref[0, 0, h], lambda h: tail_ref[0, 1, h], P, past)
        for h in range(KVH):
            osel_ref[0, h] = acc_s[h] * (1.0 / jnp.maximum(l_s[h], 1e-30))


def _page_specs(layer, half, n_pages, P):
    def spec(r):
        return pl.BlockSpec(
            (1, 1, 2, NSA_KV_HEADS, HEAD_DIM, P),
            lambda b, j, pt, r=r: (layer, pt[b, jnp.minimum(j * PAGES_PER_STEP + r, n_pages - 1)],
                                   half, 0, 0, 0))
    return [spec(r) for r in range(PAGES_PER_STEP)]


def _nsa_sample_global(cache_t, layer, page_table, tail_t, q_s, S):
    B, n_pages = page_table.shape
    P = cache_t.shape[-1]
    KVH, G, HD = NSA_KV_HEADS, NSA_GROUP, HEAD_DIM
    past = n_pages * P
    assert n_pages % PAGES_PER_STEP == 0 and S & (S - 1) == 0 and S <= SEL_BLOCK
    nsteps = n_pages // PAGES_PER_STEP + 1
    ncp = -(-(past + P) // CMP_STRIDE // LANES) * LANES
    nbp = -(-(past + P) // SEL_BLOCK // LANES) * LANES
    R = G * S
    qspec = pl.BlockSpec((1, KVH, R, HD), lambda b, j, pt: (b, 0, 0, 0))
    o_cmp, sel = pl.pallas_call(
        functools.partial(_s1_kernel, past=past, S=S, ncp=ncp, nbp=nbp),
        out_shape=[jax.ShapeDtypeStruct((B, KVH, R, HD), F32), jax.ShapeDtypeStruct((B, KVH, S, nbp), F32)],
        grid_spec=pltpu.PrefetchScalarGridSpec(
            num_scalar_prefetch=1, grid=(B, nsteps),
            in_specs=_page_specs(layer, 0, n_pages, P)
            + [pl.BlockSpec((1, 2, KVH, HD, P), lambda b, j, pt: (b, 0, 0, 0, 0)), qspec],
            out_specs=[pl.BlockSpec((1, KVH, R, HD), lambda b, j, pt: (b, 0, 0, 0)),
                       pl.BlockSpec((1, KVH, S, nbp), lambda b, j, pt: (b, 0, 0, 0))],
            scratch_shapes=[pltpu.VMEM((2, KVH, HD, ncp + LANES), F32)]),
        compiler_params=_cparams(("parallel", "arbitrary")),
        name="nsa_sample_cmp",
    )(page_table, *([cache_t] * PAGES_PER_STEP), tail_t, q_s)
    o_sel = pl.pallas_call(
        functools.partial(_s2_kernel, past=past, S=S, nbp=nbp),
        out_shape=jax.ShapeDtypeStruct((B, KVH, R, HD), F32),
        grid_spec=pltpu.PrefetchScalarGridSpec(
            num_scalar_prefetch=1, grid=(B, nsteps),
            in_specs=_page_specs(layer, 1, n_pages, P)
            + [pl.BlockSpec((1, 2, KVH, HD, P), lambda b, j, pt: (b, 1, 0, 0, 0)), qspec,
               pl.BlockSpec((1, KVH, S, nbp), lambda b, j, pt: (b, 0, 0, 0))],
            out_specs=pl.BlockSpec((1, KVH, R, HD), lambda b, j, pt: (b, 0, 0, 0)),
            scratch_shapes=[pltpu.VMEM((KVH, R, 1), F32), pltpu.VMEM((KVH, R, 1), F32),
                            pltpu.VMEM((KVH, R, HD), F32)]),
        compiler_params=_cparams(("parallel", "arbitrary")),
        name="nsa_sample_sel",
    )(page_table, *([cache_t] * PAGES_PER_STEP), tail_t, q_s, sel)
    return o_cmp, o_sel


def _s3_kernel(q_ref, kw_ref, vw_ref, ocmp_ref, osel_ref, gt_ref, y_ref, *, past, S):
    KVH, G, HD = NSA_KV_HEADS, NSA_GROUP, HEAD_DIM
    R = G * S
    nk = kw_ref.shape[1]
    row = lax.broadcasted_iota(jnp.int32, (R, 1), 0)
    pos = past + (row & (S - 1))
    kpos = past - (nk - S) + lax.broadcasted_iota(jnp.int32, (1, nk), 1)
    diff = pos - kpos
    wmask = (diff >= 0) & (diff < WINDOW)
    sg = jax.nn.sigmoid(gt_ref[...])
    kw = kw_ref[0].astype(BF16)
    vw = vw_ref[0].astype(BF16)
    outs = []
    for h in range(KVH):
        qs = q_ref[0, h] * jnp.asarray(HD ** -0.5, BF16)
        s = lax.dot_general(qs, kw[:, h * HD:(h + 1) * HD], NT, preferred_element_type=F32)
        p = _masked_softmax(s, wmask)
        o_win = jnp.dot(p.astype(BF16), vw[:, h * HD:(h + 1) * HD], preferred_element_type=F32)
        oc = ocmp_ref[0, h]
        osl = osel_ref[0, h]
        for g in range(G):
            c0 = (h * G + g) * 3
            rs = slice(g * S, (g + 1) * S)
            outs.append(sg[:, c0:c0 + 1] * oc[rs] + sg[:, c0 + 1:c0 + 2] * osl[rs]
                        + sg[:, c0 + 2:c0 + 3] * o_win[rs])
    y_ref[...] = jnp.concatenate(outs, axis=-1)


def _nsa_sample_combine(q_s, kwin, vwin, o_cmp, o_sel, z, row0, gate_cb, past, S):
    B, KVH, R, HD = q_s.shape
    nk = kwin.shape[1]
    W = KVH * HD
    ospec = pl.BlockSpec((1, KVH, R, HD), lambda b: (b, 0, 0, 0))
    return pl.pallas_call(
        functools.partial(_s3_kernel, past=past, S=S),
        out_shape=jax.ShapeDtypeStruct((B * S, KVH * NSA_GROUP * HD), F32),
        grid=(B,),
        in_specs=[ospec,
                  pl.BlockSpec((1, nk, W), lambda b: (b, 0, 0)),
                  pl.BlockSpec((1, nk, W), lambda b: (b, 0, 0)),
                  ospec, ospec,
                  pl.BlockSpec((S, LANES), lambda b: (row0 // S + b, gate_cb))],
        out_specs=pl.BlockSpec((S, KVH * NSA_GROUP * HD), lambda b: (b, 0)),
        compiler_params=_cparams(("parallel",)),
        name="nsa_sample_win",
    )(q_s, kwin, vwin, o_cmp, o_sel, z)


def _out_proj_kernel(x_ref, ya_ref, yb_ref, yc_ref, yt_ref, w_ref, o_ref, *, n_head_tiles):
    i = pl.program_id(0)
    ka, kb = ya_ref.shape[1], yb_ref.shape[1]

    @pl.when(i < n_head_tiles)
    def _():
        acc = jnp.dot(ya_ref[...].astype(BF16), w_ref[0:ka, :], preferred_element_type=F32)
        acc = acc + jnp.dot(yb_ref[...].astype(BF16), w_ref[ka:ka + kb, :], preferred_element_type=F32)
        acc = acc + jnp.dot(yc_ref[...].astype(BF16), w_ref[ka + kb:, :], preferred_element_type=F32)
        o_ref[...] = x_ref[...] + acc

    @pl.when(i >= n_head_tiles)
    def _():
        o_ref[...] = x_ref[...] + jnp.dot(yt_ref[...].astype(BF16), w_ref[...], preferred_element_type=F32)


def _out_proj(x, ya, yb, yc, ytail, w, tm):
    T, D = x.shape
    TH = ya.shape[0]
    assert TH % tm == 0 and (T - TH) % tm == 0 and ytail.shape == (T - TH, D)
    nh = TH // tm

    def head(a):
        return pl.BlockSpec((tm, a.shape[1]), lambda i: (jnp.minimum(i, nh - 1), 0))

    return pl.pallas_call(
        functools.partial(_out_proj_kernel, n_head_tiles=nh),
        out_shape=jax.ShapeDtypeStruct((T, D), F32),
        grid=(T // tm,),
        in_specs=[pl.BlockSpec((tm, D), lambda i: (i, 0)),
                  head(ya), head(yb), head(yc),
                  pl.BlockSpec((tm, D), lambda i: (jnp.maximum(i - nh, 0), 0)),
                  pl.BlockSpec((D, D), lambda i: (0, 0))],
        out_specs=pl.BlockSpec((tm, D), lambda i: (i, 0)),
        compiler_params=_cparams(("parallel",)),
        name="out_proj",
    )(x, ya, yb, yc, ytail, w)


def _structural_pairs(k):
    return [(i, j) for i in range(k) for j in range(k) if (i + 1) * (j + 1) <= k]


RANK_MARK = 2.0 ** 126


def _extract_topk(cur_ref, top_ref, c, exact):
    NK, K = PEER_NKEYS, PEER_TOPK
    for k in range(K):
        cur = cur_ref[c]
        m = jnp.max(cur, axis=0)
        if exact:
            key = lax.broadcasted_iota(jnp.int32, cur.shape, 0).astype(F32)
            idx = jnp.min(jnp.where(cur == m[None], key, float(NK)), axis=0)
            hit = key == idx[None]
        else:
            hit = cur == m[None]
        cur_ref[c] = jnp.where(hit, -RANK_MARK * (1.0 + k / 32.0), cur)
        top_ref[c, k] = m


def _router_kernel(q_ref, wb_ref, n_out, e1_out, r2_out, e2_out, s_ref, cur_ref, top_ref, tmp_ref):
    H, NK, K = PEER_HEADS, PEER_NKEYS, PEER_TOPK
    Tt = q_ref.shape[0]
    half = q_ref.shape[1] // 2
    for c in range(2):
        s = lax.dot_general(wb_ref[c], q_ref[:, c * half:(c + 1) * half], NT, preferred_element_type=F32)
        s_ref[c] = s.reshape(NK, H, Tt)
        cur_ref[c] = s_ref[c]
        _extract_topk(cur_ref, top_ref, c, exact=False)

    marked = jnp.sum(jnp.where(cur_ref[...] < -0.5 * RANK_MARK, 1.0, 0.0), axis=1)
    tied = jnp.max(jnp.where(marked != float(K), 1.0, 0.0)) > 0.0

    @pl.when(tied)
    def _():
        for c in range(2):
            cur_ref[c] = s_ref[c]
            _extract_topk(cur_ref, top_ref, c, exact=True)

    v1 = [top_ref[0, k] for k in range(K)]
    v2 = [top_ref[1, k] for k in range(K)]

    pairs = _structural_pairs(K)
    cand = [v1[i] + v2[j] for (i, j) in pairs]
    n = len(pairs)
    rank = []
    for p in range(n):
        rank.append(jnp.zeros((H, Tt), F32))
    for p in range(n):
        ip, jp = pairs[p]
        for q in range(p + 1, n):
            iq, jq = pairs[q]
            if ip <= iq and jp <= jq:
                rank[q] = rank[q] + 1.0
            else:
                b = jnp.where(cand[p] >= cand[q], 1.0, 0.0)
                rank[q] = rank[q] + b
                rank[p] = rank[p] + (1.0 - b)
    sel = [jnp.where(r < K, 1.0, 0.0) for r in rank]
    e1 = [jnp.exp(v1[i] - v1[0]) for i in range(K)]
    e2 = [jnp.exp(v2[j] - v2[0]) for j in range(K)]
    cnt = [jnp.zeros((H, Tt), F32) for _ in range(K)]
    zsum = jnp.zeros((H, Tt), F32)
    for p, (i, j) in enumerate(pairs):
        cnt[i] = cnt[i] + sel[p]
        zsum = zsum + sel[p] * (e1[i] * e2[j])
    inv_z = 1.0 / zsum

    def rank_of(c):
        cur = cur_ref[c]
        return jnp.where(cur < -0.5 * RANK_MARK, (cur * (-1.0 / RANK_MARK) - 1.0) * 32.0, float(NK))

    rk1 = rank_of(0)
    nk = jnp.zeros((NK, H, Tt), F32)
    for i in range(K):
        nk = jnp.where(rk1 == float(i), cnt[i][None], nk)
    outs = ((n_out, nk),
            (e1_out, jnp.exp(s_ref[0] - v1[0][None]) * inv_z[None]),
            (r2_out, rank_of(1)),
            (e2_out, jnp.exp(s_ref[1] - v2[0][None])))
    for ref, val in outs:
        tmp_ref[...] = val.reshape(NK * H, Tt)
        for h in range(H):
            ref[h] = tmp_ref[pl.ds(h, NK, stride=H), :].astype(ref.dtype)


def _router(q, wb, tt):
    T = q.shape[0]
    H, NK = PEER_HEADS, PEER_NKEYS
    shp = jax.ShapeDtypeStruct((H, NK, T), F32)
    shp_b = jax.ShapeDtypeStruct((H, NK, T), BF16)
    ospec = pl.BlockSpec((H, NK, tt), lambda i: (0, 0, i))
    return pl.pallas_call(
        _router_kernel,
        out_shape=[shp, shp, shp_b, shp_b],
        grid=(T // tt,),
        in_specs=[pl.BlockSpec((tt, q.shape[1]), lambda i: (i, 0)),
                  pl.BlockSpec(wb.shape, lambda i: (0, 0, 0))],
        out_specs=[ospec, ospec, ospec, ospec],
        scratch_shapes=[pltpu.VMEM((2, NK, H, tt), F32), pltpu.VMEM((2, NK, H, tt), F32),
                        pltpu.VMEM((2, PEER_TOPK, H, tt), F32), pltpu.VMEM((NK * H, tt), F32)],
        compiler_params=_cparams(("parallel",)),
        name="peer_router",
    )(q, wb)


def _gelu_tanh(x):
    return 0.5 * x * (1.0 + jnp.tanh(np.sqrt(2.0 / np.pi) * (x + 0.044715 * (x * x * x))))


PEER_ACHUNK = 2


def _peer_dense_kernel(xn_ref, u_ref, v_ref, n_ref, e1_ref, r2_ref, e2_ref, x1_ref, g_ref, o_ref, *, final_norm):
    H, NK = PEER_HEADS, PEER_NKEYS
    j = pl.program_id(1)
    na = u_ref.shape[0] // NK
    tt = xn_ref.shape[0]

    @pl.when(j == 0)
    def _():
        o_ref[...] = x1_ref[...]

    xn = xn_ref[...]
    parts = []
    for c in range(na // PEER_ACHUNK):
        rows = slice(c * PEER_ACHUNK * NK, (c + 1) * PEER_ACHUNK * NK)
        act = lax.dot_general(u_ref[rows, :], xn, NT, preferred_element_type=F32)
        gates = []
        for a in range(c * PEER_ACHUNK, (c + 1) * PEER_ACHUNK):
            row = pl.ds(j * na + a, 1)
            gate = None
            for h in range(H):
                t = jnp.where(r2_ref[h] < n_ref[h, row, :].astype(BF16),
                              e1_ref[h, row, :].astype(BF16) * e2_ref[h], jnp.zeros((), BF16))
                gate = t if gate is None else gate + t
            gates.append(gate)
        parts.append(_gelu_tanh(act).astype(BF16) * jnp.concatenate(gates, axis=0))
    wt = jnp.concatenate(parts, axis=0)
    o_ref[...] += lax.dot_general(wt, v_ref[...], TN, preferred_element_type=F32)

    if final_norm:
        @pl.when(j == pl.num_programs(1) - 1)
        def _():
            x = o_ref[...]
            o_ref[...] = x * lax.rsqrt(jnp.mean(x * x, -1, keepdims=True) + RMS_EPS) * g_ref[...]


def _peer_dense(xn, u, v, layer, nk, e1, r2, e2, x1, g, tt, te, final_norm):
    T, D = xn.shape
    E = u.shape[1]
    H, NK = PEER_HEADS, PEER_NKEYS
    once = pl.Buffered(1)
    tab = pl.BlockSpec((H, NK, tt), lambda i, j: (0, 0, i), pipeline_mode=once)
    return pl.pallas_call(
        functools.partial(_peer_dense_kernel, final_norm=final_norm),
        out_shape=jax.ShapeDtypeStruct((T, D), F32),
        grid=(T // tt, E // te),
        in_specs=[pl.BlockSpec((tt, D), lambda i, j: (i, 0), pipeline_mode=once),
                  pl.BlockSpec((None, te, D), lambda i, j: (layer, j, 0)),
                  pl.BlockSpec((None, te, D), lambda i, j: (layer, j, 0)),
                  tab, tab, tab, tab,
                  pl.BlockSpec((tt, D), lambda i, j: (i, 0), pipeline_mode=once),
                  pl.BlockSpec((1, D), lambda i, j: (0, 0))],
        out_specs=pl.BlockSpec((tt, D), lambda i, j: (i, 0)),
        compiler_params=_cparams(("parallel", "arbitrary")),
        name="peer_dense",
    )(xn, u, v, nk, e1, r2, e2, x1, g.reshape(1, D))


T_TILE = 384
ROUTER_TILE = 128
PEER_T_TILE = 384
PAD_TILE = 768
Z_COL_TILE = 1280
OUT_TILE = 256
EXPERT_TILE = 1024


def kernel(x_prompt, x_sample, cache_nsa_kv, state_win_kv, state_conv, state_mlstm_C, state_mlstm_n,
           state_mlstm_m, page_table, norm1_g, w_in, conv_w, mlstm_gate_b, mlstm_norm_g, w_out, norm2_g,
           peer_wq, peer_subkeys, peer_u, peer_v, final_norm_g):
    BP, SP, D = x_prompt.shape
    BS, SS, _ = x_sample.shape
    depth = w_in.shape[0]
    KVH, G, HD = NSA_KV_HEADS, NSA_GROUP, HEAD_DIM
    H, DH = MLSTM_HEADS, MLSTM_DH
    CW = conv_w.shape[2]
    NW = KVH * G * HD
    KVW = 6 * KVH * HD
    NG = 3 * KVH * G
    MW = H * DH
    TP, TS = BP * SP, BS * SS
    T = TP + TS
    assert PAD_TILE % T_TILE == 0 and PAD_TILE % ROUTER_TILE == 0 and PAD_TILE % PEER_T_TILE == 0
    TPAD = -(-T // PAD_TILE) * PAD_TILE
    n_pages = page_table.shape[1]
    P = cache_nsa_kv.shape[2]
    past = n_pages * P
    wb_rows = state_win_kv.shape[2]

    c_q = 3 * CW
    c_kv = c_q + NW
    c_m = c_kv + KVW
    c_g = c_m + 4 * MW
    ZW = -(-(c_g + LANES) // Z_COL_TILE) * Z_COL_TILE
    assert CW % LANES == 0 and c_m % MW == 0 and c_g % LANES == 0 and NG + 2 * H <= LANES
    gate_cb = c_g // LANES
    gi, gf = NG, NG + H
    o_gate = 3 * CW + NW + KVW

    x = jnp.concatenate([x_prompt.reshape(TP, D), x_sample.reshape(TS, D),
                         jnp.zeros((TPAD - T, D), F32)], axis=0)
    cache_t = jnp.transpose(cache_nsa_kv, (0, 1, 3, 4, 5, 2))
    eye_h = jnp.eye(PEER_HEADS, dtype=F32)
    u_bf = peer_u.astype(BF16)
    v_bf = peer_v.astype(BF16)
    dk = peer_subkeys.shape[-1]

    p_st, s_st = [], []
    for l in range(depth):
        wi = w_in[l]
        w_perm = jnp.concatenate(
            [wi[:, :o_gate], wi[:, o_gate + NG:o_gate + NG + 4 * MW], wi[:, o_gate:o_gate + NG],
             wi[:, o_gate + NG + 4 * MW:], jnp.zeros((D, ZW - wi.shape[1]), F32)], axis=1).astype(BF16)
        z = _rms_proj(x, norm1_g[l], w_perm, PAD_TILE, Z_COL_TILE, F32)

        ya_p, conv_p = _conv(z, 0, BP, SP, jnp.zeros((BP, 2, CW), F32), conv_w[l])
        ya_s, conv_s = _conv(z, TP, BS, SS, state_conv[l], conv_w[l])

        yb_p = _nsa_prompt(z, c_q, c_kv, gate_cb, BP, SP)
        zkv = z[:TP, c_kv:c_kv + KVW].reshape(BP, SP, 3, 2, KVH, HD)
        rows_p = zkv[:, :, :2].reshape(BP, SP, 4, KVH, HD)
        win_p = zkv[:, SP - min(WINDOW, SP):, 2]

        zs = z[TP:T]
        q_s = zs[:, c_q:c_q + NW].reshape(BS, SS, KVH, G, HD).transpose(0, 2, 3, 1, 4)
        q_s = q_s.reshape(BS, KVH, G * SS, HD).astype(BF16)
        zkv_s = zs[:, c_kv:c_kv + KVW].reshape(BS, SS, 3, 2, KVH, HD)
        rows_s = zkv_s[:, :, :2].reshape(BS, SS, 4, KVH, HD)
        tail_t = jnp.pad(rows_s.transpose(0, 2, 3, 4, 1), ((0, 0), (0, 0), (0, 0), (0, 0), (0, P - SS)))
        o_cmp, o_sel = _nsa_sample_global(cache_t, l, page_table, tail_t, q_s, SS)
        w_all = jnp.concatenate([state_win_kv[l], zkv_s[:, :, 2]], axis=1)
        kwin = w_all[:, :, 0].reshape(BS, wb_rows + SS, KVH * HD)
        vwin = w_all[:, :, 1].reshape(BS, wb_rows + SS, KVH * HD)
        yb_s = _nsa_sample_combine(q_s, kwin, vwin, o_cmp, o_sel, z, TP, gate_cb, past, SS)
        win_s = w_all[:, SS:]

        yc_p, C_p, n_p, m_p = _mlstm(z, 0, BP, SP, c_m // MW, gate_cb, gi, gf, mlstm_gate_b[l],
                                     mlstm_norm_g[l], jnp.zeros((BP, H, DH, DH), F32),
                                     jnp.zeros((BP, H, DH), F32), jnp.zeros((BP, H), F32))
        yc_s, C_s, n_s, m_s = _mlstm(z, TP, BS, SS, c_m // MW, gate_cb, gi, gf, mlstm_gate_b[l],
                                     mlstm_norm_g[l], state_mlstm_C[l], state_mlstm_n[l], state_mlstm_m[l])

        ytail = jnp.concatenate([jnp.concatenate([ya_s, yb_s, yc_s], axis=1),
                                 jnp.zeros((TPAD - T, D), F32)], axis=0)
        x1 = _out_proj(x, ya_p, yb_p, yc_p, ytail, w_out[l].astype(BF16), OUT_TILE)

        nh = PEER_HEADS
        wq_perm = peer_wq[l].reshape(D, nh, 2, dk).transpose(0, 2, 1, 3).reshape(D, 2 * nh * dk).astype(BF16)
        q_peer, xn2 = _rms_proj(x1, norm2_g[l], wq_perm, T_TILE, nh * dk, BF16, emit_xn=True)
        wb = jnp.einsum('hcnd,hg->cnhgd', peer_subkeys[l], eye_h).reshape(2, PEER_NKEYS * nh, nh * dk).astype(BF16)
        nk, e1, r2, e2 = _router(q_peer, wb, ROUTER_TILE)
        x = _peer_dense(xn2, u_bf, v_bf, l, nk, e1, r2, e2, x1,
                        final_norm_g, PEER_T_TILE, EXPERT_TILE, final_norm=(l == depth - 1))

        p_st.append((rows_p, win_p, conv_p, C_p, n_p, m_p))
        s_st.append((rows_s, win_s, conv_s, C_s, n_s, m_s))

    p_rows, p_win, p_conv, p_C, p_n, p_m = [jnp.stack(a) for a in zip(*p_st)]
    s_rows, s_win, s_conv, s_C, s_n, s_m = [jnp.stack(a) for a in zip(*s_st)]
    y_prompt = x[:TP].reshape(BP, SP, D)
    y_sample = x[TP:T].reshape(BS, SS, D)
    return (y_prompt, y_sample, p_rows, p_win, p_conv, p_C, p_n, p_m, s_rows, s_win, s_conv, s_C, s_n, s_m)
```

```python
import functools

import numpy as np
import jax
import jax.numpy as jnp
from jax import lax
from jax.experimental import pallas as pl
from jax.experimental.pallas import tpu as pltpu

F32 = jnp.float32
BF16 = jnp.bfloat16
HI = lax.Precision.HIGHEST

RMS_EPS = 1e-6
HEAD_DIM = 64
NSA_KV_HEADS = 4
NSA_GROUP = 4
CMP_STRIDE = 16
SEL_BLOCK = 64
SEL_TOPN = 16
WINDOW = 512
MLSTM_HEADS = 4
MLSTM_DH = 128
MLSTM_CHUNK = 64
PEER_HEADS = 8
PEER_NKEYS = 128
PEER_TOPK = 16
MASK_BIG = 1e9
NEG = -1e30

LANES = 128
VMEM_LIMIT = 56 * 1024 * 1024

NT = (((1,), (1,)), ((), ()))
TN = (((0,), (0,)), ((), ()))


def _cparams(sem):
    return pltpu.CompilerParams(dimension_semantics=sem, vmem_limit_bytes=VMEM_LIMIT)


def _div(x, d):
    assert d & (d - 1) == 0
    return lax.shift_right_arithmetic(x, jnp.int32(d.bit_length() - 1))


def _pool_dot(pool, x, x_is_lhs):
    hi = x.astype(BF16)
    lo = (x - hi.astype(F32)).astype(BF16)
    if x_is_lhs:
        return (jnp.dot(hi, pool, preferred_element_type=F32) + jnp.dot(lo, pool, preferred_element_type=F32))
    return (jnp.dot(pool, hi, preferred_element_type=F32) + jnp.dot(pool, lo, preferred_element_type=F32))


def _masked_softmax(s, mask, exp=jnp.exp):
    s = jnp.where(mask, s, NEG)
    m = jnp.max(s, -1, keepdims=True)
    e = jnp.where(mask, exp(s - m), 0.0)
    d = jnp.maximum(jnp.sum(e, -1, keepdims=True), 1e-30)
    return e * (1.0 / d)


def _rms_proj_kernel(x_ref, g_ref, w_ref, *rest, emit_xn):
    if emit_xn:
        o_ref, xo_ref, xn_ref = rest
    else:
        o_ref, xn_ref = rest

    @pl.when(pl.program_id(1) == 0)
    def _():
        x = x_ref[...]
        r = x * lax.rsqrt(jnp.mean(x * x, -1, keepdims=True) + RMS_EPS)
        xn = (r * g_ref[...]).astype(BF16)
        xn_ref[...] = xn
        if emit_xn:
            xo_ref[...] = xn

    o_ref[...] = jnp.dot(xn_ref[...], w_ref[...], preferred_element_type=F32).astype(o_ref.dtype)


def _rms_proj(x, g, w, tm, tn, out_dtype, emit_xn=False):
    T, D = x.shape
    N = w.shape[1]
    out_shape = [jax.ShapeDtypeStruct((T, N), out_dtype)]
    out_specs = [pl.BlockSpec((tm, tn), lambda i, j: (i, j))]
    if emit_xn:
        out_shape.append(jax.ShapeDtypeStruct((T, D), BF16))
        out_specs.append(pl.BlockSpec((tm, D), lambda i, j: (i, 0)))
    res = pl.pallas_call(
        functools.partial(_rms_proj_kernel, emit_xn=emit_xn),
        out_shape=out_shape,
        grid=(T // tm, N // tn),
        in_specs=[pl.BlockSpec((tm, D), lambda i, j: (i, 0)),
                  pl.BlockSpec((1, D), lambda i, j: (0, 0)),
                  pl.BlockSpec((D, tn), lambda i, j: (0, j))],
        out_specs=out_specs,
        scratch_shapes=[pltpu.VMEM((tm, D), BF16)],
        compiler_params=_cparams(("parallel", "arbitrary")),
        name="rms_proj",
    )(x, g.reshape(1, D), w)
    return res if emit_xn else res[0]


def _conv_kernel(cb_ref, cc_ref, ch_ref, buf_ref, w_ref, y_ref, new_ref, ext_ref, *, S):
    C = cb_ref.shape[-1]
    u = cc_ref[...] * ch_ref[...]
    ext_ref[0:8, :] = jnp.zeros((8, C), F32)
    ext_ref[6:8, :] = buf_ref[0]
    ext_ref[8:8 + S, :] = u
    w = w_ref[...]
    y = w[0:1] * ext_ref[6:6 + S, :] + w[1:2] * ext_ref[7:7 + S, :] + w[2:3] * u
    y_ref[...] = cb_ref[...] * y
    new_ref[0] = u[S - 2:S]


def _conv(z, row0, B, S, buf, w):
    C = w.shape[1]
    rb0 = row0 // S
    return pl.pallas_call(
        functools.partial(_conv_kernel, S=S),
        out_shape=[jax.ShapeDtypeStruct((B * S, C), F32), jax.ShapeDtypeStruct((B, 2, C), F32)],
        grid=(B,),
        in_specs=[pl.BlockSpec((S, C), lambda b: (rb0 + b, 0)),
                  pl.BlockSpec((S, C), lambda b: (rb0 + b, 1)),
                  pl.BlockSpec((S, C), lambda b: (rb0 + b, 2)),
                  pl.BlockSpec((1, 2, C), lambda b: (b, 0, 0)),
                  pl.BlockSpec((3, C), lambda b: (0, 0))],
        out_specs=[pl.BlockSpec((S, C), lambda b: (b, 0)),
                   pl.BlockSpec((1, 2, C), lambda b: (b, 0, 0))],
        scratch_shapes=[pltpu.VMEM((S + 8, C), F32)],
        compiler_params=_cparams(("parallel",)),
        name="short_conv",
    )(z, z, z, buf, w)


def _topn_rank_select(score, n_sel):
    NB = score.shape[1]
    blk = lax.broadcasted_iota(jnp.int32, score.shape, 1)
    rank = jnp.zeros(score.shape, F32)
    for i in range(NB):
        ci = score[:, i:i + 1]
        beats = jnp.where(ci > score, 1.0, jnp.where((ci == score) & (blk > i), 1.0, 0.0))
        rank = rank + beats
    return rank < n_sel


SEL_KCHUNK = 512


def _exp2_softmax_pv(pieces, hd):
    m = None
    for s, _ in pieces:
        mm = jnp.max(s, -1, keepdims=True)
        m = mm if m is None else jnp.maximum(m, mm)
    oa = None
    for s, v in pieces:
        t = jnp.dot(jnp.exp2(s - m).astype(BF16), v, preferred_element_type=F32)
        oa = t if oa is None else oa + t
    return oa[:, 0:hd] * (1.0 / oa[:, hd:hd + 1])


NSA_HPS = 2


def _nsa_prompt_kernel(q_ref, kcs_ref, vcs_ref, kss_ref, vss_ref, kws_ref, vws_ref, gt_ref,
                       y_ref, rows_ref, win_ref, kc_ref, vc_ref, ks_ref, vs_ref, kw_ref, vw_ref, osel_s, *, S, tq):
    pair = pl.program_id(1)
    qi = pl.program_id(2)
    G, HD = NSA_GROUP, HEAD_DIM
    NC = S // CMP_STRIDE
    NB = S // SEL_BLOCK
    n_sel = min(SEL_TOPN, NB)
    span = min(WINDOW + tq, S)

    @pl.when(qi == 0)
    def _():
        j = lax.broadcasted_iota(jnp.int32, (NC, S), 0)
        r = lax.broadcasted_iota(jnp.int32, (NC, S), 1)
        lo = j * CMP_STRIDE
        pool = jnp.where((r >= lo) & (r < lo + 2 * CMP_STRIDE), 0.5 / CMP_STRIDE, 0.0).astype(BF16)
        kcp = _pool_dot(pool, kcs_ref[...], False).astype(BF16)
        vcp = _pool_dot(pool, vcs_ref[...], False).astype(BF16)
        blk1h = jnp.where(_div(lax.broadcasted_iota(jnp.int32, (S, NB), 0), SEL_BLOCK)
                          == lax.broadcasted_iota(jnp.int32, (S, NB), 1), 1.0, 0.0).astype(BF16)
        kpad = jnp.zeros((S, LANES - HD - NB), BF16)
        ones = jnp.where(lax.broadcasted_iota(jnp.int32, (S, LANES - HD), 1) == 0, 1.0, 0.0).astype(BF16)
        for hh in range(NSA_HPS):
            cols = slice(hh * HD, (hh + 1) * HD)
            kc_ref[hh] = kcp[:, cols]
            vc_ref[hh] = vcp[:, cols]
            ks_ref[hh] = jnp.concatenate([kss_ref[:, cols].astype(BF16), blk1h, kpad], axis=1)
            vs_ref[hh] = jnp.concatenate([vss_ref[:, cols].astype(BF16), ones], axis=1)
            kw_ref[hh] = kws_ref[:, cols].astype(BF16)
            vw_ref[hh] = jnp.concatenate([vws_ref[:, cols].astype(BF16), ones], axis=1)
        for kind, src in enumerate((kcs_ref, vcs_ref, kss_ref, vss_ref)):
            rows_ref[0, kind] = src[...].T.reshape(NSA_HPS, HD, S)
        wn = win_ref.shape[-1]
        for kind, src in enumerate((kws_ref, vws_ref)):
            win_ref[0, kind] = src[S - wn:S, :].T.reshape(NSA_HPS, HD, wn)

    t0 = qi * tq
    row = lax.broadcasted_iota(jnp.int32, (G * tq, 1), 0)
    pos = t0 + (row & (tq - 1))
    posq = t0 + lax.broadcasted_iota(jnp.int32, (tq, 1), 0)
    c_end = lax.broadcasted_iota(jnp.int32, (1, NC), 1) * CMP_STRIDE + (2 * CMP_STRIDE - 1)
    cmask = c_end <= pos
    per = SEL_BLOCK // CMP_STRIDE
    e4 = jnp.where(_div(lax.broadcasted_iota(jnp.int32, (NC, NB), 0), per)
                   == lax.broadcasted_iota(jnp.int32, (NC, NB), 1), 1.0, 0.0).astype(F32)
    blk = lax.broadcasted_iota(jnp.int32, (1, NB), 1)
    qblk = _div(posq, SEL_BLOCK)
    valid = blk * SEL_BLOCK <= posq
    forced = (blk == 0) | (blk == qblk) | (blk == qblk - 1)

    qscale = (HD ** -0.5) * np.log2(np.e)
    qss, qas, o_cmps = [], [], []
    for hh in range(NSA_HPS):
        qh = q_ref[:, hh * G * HD:(hh + 1) * G * HD] * qscale
        qs = jnp.concatenate([qh[:, g * HD:(g + 1) * HD] for g in range(G)], axis=0).astype(BF16)
        s = lax.dot_general(qs, kc_ref[hh], NT, preferred_element_type=F32)
        p = _masked_softmax(s, cmask, jnp.exp2)
        o_cmps.append(jnp.dot(p.astype(BF16), vc_ref[hh], preferred_element_type=F32))
        imp = p[0:tq]
        for g in range(1, G):
            imp = imp + p[g * tq:(g + 1) * tq]
        impb = jnp.dot(imp, e4, precision=HI, preferred_element_type=F32)
        score = jnp.where(forced, MASK_BIG, jnp.where(valid, impb, -MASK_BIG))
        sel = _topn_rank_select(score, n_sel) & (score > -0.5 * MASK_BIG)
        selneg = jnp.where(sel, 0.0, NEG).astype(BF16)
        qas.append(jnp.concatenate([qs, jnp.concatenate([selneg] * G, axis=0),
                                    jnp.zeros((G * tq, LANES - HD - NB), BF16)], axis=1))
        qss.append(qs)

    def sel_branch(klen):
        tail = min(klen, SEL_KCHUNK)
        kpos = (klen - tail) + lax.broadcasted_iota(jnp.int32, (1, tail), 1)
        tail_bias = jnp.where(kpos <= posq, 0.0, NEG)
        for hh in range(NSA_HPS):
            for g in range(G):
                qa = qas[hh][g * tq:(g + 1) * tq]
                pieces = []
                if klen > tail:
                    pieces.append((lax.dot_general(qa, ks_ref[hh, 0:klen - tail, :], NT,
                                                   preferred_element_type=F32),
                                   vs_ref[hh, 0:klen - tail, :]))
                pieces.append((lax.dot_general(qa, ks_ref[hh, klen - tail:klen, :], NT,
                                               preferred_element_type=F32) + tail_bias,
                               vs_ref[hh, klen - tail:klen, :]))
                osel_s[hh, g * tq:(g + 1) * tq, :] = _exp2_softmax_pv(pieces, HD)

    nvar = -(-S // SEL_KCHUNK)
    per_var = SEL_KCHUNK // tq
    for v in range(nvar):
        @pl.when(qi // per_var == v)
        def _(v=v):
            sel_branch(min(S, (v + 1) * SEL_KCHUNK))

    start = pl.multiple_of(jnp.maximum(t0 + tq - span, 0), tq)
    diff = posq - (start + lax.broadcasted_iota(jnp.int32, (1, span), 1))
    win_bias = jnp.where((diff >= 0) & (diff < WINDOW), 0.0, NEG)

    sg = jax.nn.sigmoid(gt_ref[...])
    outs = []
    for hh in range(NSA_HPS):
        kvh = pair * NSA_HPS + hh
        pick = jnp.where(lax.broadcasted_iota(jnp.int32, (LANES, LANES), 0)
                         == lax.broadcasted_iota(jnp.int32, (LANES, LANES), 1) + kvh * (3 * G),
                         1.0, 0.0).astype(F32)
        g12 = jnp.dot(sg, pick, precision=HI, preferred_element_type=F32)
        kw = kw_ref[hh, pl.ds(start, span), :]
        vw = vw_ref[hh, pl.ds(start, span), :]
        for g in range(G):
            rows = slice(g * tq, (g + 1) * tq)
            s_win = lax.dot_general(qss[hh][rows], kw, NT, preferred_element_type=F32) + win_bias
            o_win = _exp2_softmax_pv([(s_win, vw)], HD)
            outs.append(g12[:, 3 * g:3 * g + 1] * o_cmps[hh][rows]
                        + g12[:, 3 * g + 1:3 * g + 2] * osel_s[hh, rows, :]
                        + g12[:, 3 * g + 2:3 * g + 3] * o_win)
    y_ref[...] = jnp.concatenate(outs, axis=-1)


def _nsa_prompt(z, c_q, c_kv, gate_cb, B, S, tq=256):
    nq = S // tq
    KVH, G, HD = NSA_KV_HEADS, NSA_GROUP, HEAD_DIM
    NC = S // CMP_STRIDE
    qw = NSA_HPS * G * HD
    assert NSA_HPS * HD == LANES and c_q % qw == 0 and c_kv % LANES == 0

    def kv_spec(i):
        cb = (c_kv + i * KVH * HD) // LANES
        return pl.BlockSpec((S, LANES), lambda b, p, t, cb=cb: (b, cb + p))

    head_kv = pltpu.VMEM((NSA_HPS, S, HD), BF16)
    head_aug = pltpu.VMEM((NSA_HPS, S, LANES), BF16)
    assert HD + S // SEL_BLOCK <= LANES
    wn = min(WINDOW, S)
    return pl.pallas_call(
        functools.partial(_nsa_prompt_kernel, S=S, tq=tq),
        out_shape=[jax.ShapeDtypeStruct((B * S, KVH * G * HD), F32),
                   jax.ShapeDtypeStruct((B, 4, KVH, HD, S), F32),
                   jax.ShapeDtypeStruct((B, 2, KVH, HD, wn), F32)],
        grid=(B, KVH // NSA_HPS, nq),
        in_specs=[pl.BlockSpec((tq, qw), lambda b, p, t: (b * nq + t, c_q // qw + p)),
                  kv_spec(0), kv_spec(1), kv_spec(2), kv_spec(3), kv_spec(4), kv_spec(5),
                  pl.BlockSpec((tq, LANES), lambda b, p, t: (b * nq + t, gate_cb))],
        out_specs=[pl.BlockSpec((tq, qw), lambda b, p, t: (b * nq + t, p)),
                   pl.BlockSpec((1, 4, NSA_HPS, HD, S), lambda b, p, t: (b, 0, p, 0, 0)),
                   pl.BlockSpec((1, 2, NSA_HPS, HD, wn), lambda b, p, t: (b, 0, p, 0, 0))],
        scratch_shapes=[pltpu.VMEM((NSA_HPS, NC, HD), BF16), pltpu.VMEM((NSA_HPS, NC, HD), BF16),
                        head_aug, head_aug, head_kv, head_aug,
                        pltpu.VMEM((NSA_HPS, G * tq, HD), F32)],
        compiler_params=_cparams(("parallel", "parallel", "arbitrary")),
        name="nsa_prompt",
    )(z, z, z, z, z, z, z, z)


def _log_sigmoid(x):
    return jnp.minimum(x, 0.0) - jnp.log(1.0 + jnp.exp(-jnp.abs(x)))


def _mlstm_kernel(gb_ref, q_ref, k_ref, v_ref, o_ref, gt_ref, mg_ref, c0_ref, n0_ref, m0_ref,
                  y_ref, cn_ref, nn_ref, mn_ref, c_s, n_s, m_s, *, L, Lb, gi, gf):
    H, DH = MLSTM_HEADS, MLSTM_DH
    c = pl.program_id(1)

    @pl.when(c == 0)
    def _():
        c_s[...] = c0_ref[0]
        n_s[...] = n0_ref[0]
        m_s[...] = m0_ref[0]

    def padrows(a):
        if Lb == L:
            return a
        return jnp.concatenate([a, jnp.zeros((L - Lb, a.shape[1]), a.dtype)], axis=0)

    lane = lax.broadcasted_iota(jnp.int32, (1, LANES), 1)
    bias = jnp.zeros((1, LANES), F32)
    for h in range(H):
        bias = bias + jnp.where(lane == gi + h, gb_ref[0, h], 0.0) + jnp.where(lane == gf + h, gb_ref[1, h], 0.0)
    is_f = (lane >= gf) & (lane < gf + H)
    pre = padrows(gt_ref[...]) + bias
    gate = jnp.where(is_f, _log_sigmoid(pre), pre)
    if Lb != L:
        live = lax.broadcasted_iota(jnp.int32, (L, 1), 0) < Lb
        gate = jnp.where(live, gate, jnp.where(is_f, 0.0, NEG))
    rr = lax.broadcasted_iota(jnp.int32, (L, L), 0)
    cc = lax.broadcasted_iota(jnp.int32, (L, L), 1)
    tril = rr >= cc
    bcum = jnp.dot(jnp.where(tril, 1.0, 0.0).astype(F32), gate, precision=HI, preferred_element_type=F32)
    e8 = jnp.where(lax.broadcasted_iota(jnp.int32, (8, LANES), 1)
                   == lax.broadcasted_iota(jnp.int32, (8, LANES), 0) + gi, 1.0, 0.0).astype(F32)
    rg = lax.dot_general(e8, gate, NT, precision=HI, preferred_element_type=F32)
    rb = lax.dot_general(e8, bcum, NT, precision=HI, preferred_element_type=F32)

    q = padrows(q_ref[...])
    k = padrows(k_ref[...])
    v = padrows(v_ref[...])
    og = padrows(o_ref[...])
    ys = []
    for h in range(H):
        sl = slice(h * DH, (h + 1) * DH)
        qq = q[:, sl]
        kk = k[:, sl] * (DH ** -0.5)
        vv = v[:, sl]
        qb, kb, vb = qq.astype(BF16), kk.astype(BF16), vv.astype(BF16)
        b_col = bcum[:, gf + h:gf + h + 1]
        b_row = rb[H + h:H + h + 1, :]
        i_row = rg[h:h + 1, :]
        i_col = gate[:, gi + h:gi + h + 1]
        m_prev = m_s[h][:, 0:1]
        cmat = c_s[h]
        n_row = n_s[h]
        dmat = jnp.where(tril, b_col - b_row + i_row, NEG)
        inter = b_col + m_prev
        mt = jnp.maximum(inter, jnp.max(dmat, -1, keepdims=True))
        w = jnp.exp(dmat - mt)
        a = jnp.exp(inter - mt)
        wqk = w * lax.dot_general(qb, kb, NT, preferred_element_type=F32)
        num = (a * jnp.dot(qb, cmat.astype(BF16), preferred_element_type=F32)
               + jnp.dot(wqk.astype(BF16), vb, preferred_element_type=F32))
        den = a * jnp.sum(qq * n_row, -1, keepdims=True) + jnp.sum(wqk, -1, keepdims=True)
        hh = num * (1.0 / jnp.maximum(jnp.abs(den), jnp.exp(-mt)))
        b_last = b_col[L - 1:L]
        m_new = mt[L - 1:L]
        wl = jnp.exp(b_last - b_col + i_col - m_new)
        decay = jnp.exp(b_last + m_prev - m_new)
        kw = wl * kk
        c_s[h] = decay * cmat + lax.dot_general(kw.astype(BF16), vb, TN, preferred_element_type=F32)
        n_s[h] = decay * n_row + jnp.sum(kw, 0, keepdims=True)
        m_s[h] = jnp.broadcast_to(m_new, (1, LANES))
        hn = hh * lax.rsqrt(jnp.mean(hh * hh, -1, keepdims=True) + RMS_EPS)
        ys.append(jax.nn.sigmoid(og[:, sl]) * hn * mg_ref[:, sl])
    y = jnp.concatenate(ys, axis=-1)
    y_ref[...] = y[0:Lb]

    @pl.when(c == pl.num_programs(1) - 1)
    def _():
        cn_ref[0] = c_s[...]
        nn_ref[0] = n_s[...]
        mn_ref[0] = m_s[...]


def _mlstm(z, row0, B, S, col_q, gate_cb, gi, gf, gate_b, mnorm_g, C0, n0, m0):
    H, DH = MLSTM_HEADS, MLSTM_DH
    W = H * DH
    L = MLSTM_CHUNK
    Lb = L if S % L == 0 else S
    assert Lb <= L
    nc = S // Lb
    rb0 = row0 // Lb
    n0 = n0.reshape(B, H, 1, DH)
    m0 = jnp.broadcast_to(m0.reshape(B, H, 1, 1), (B, H, 1, LANES))

    def zspec(cb, width):
        return pl.BlockSpec((Lb, width), lambda b, c, cb=cb: (rb0 + b * nc + c, cb))

    y, C, n, m = pl.pallas_call(
        functools.partial(_mlstm_kernel, L=L, Lb=Lb, gi=gi, gf=gf),
        out_shape=[jax.ShapeDtypeStruct((B * S, W), F32),
                   jax.ShapeDtypeStruct((B, H, DH, DH), F32),
                   jax.ShapeDtypeStruct((B, H, 1, DH), F32),
                   jax.ShapeDtypeStruct((B, H, 1, LANES), F32)],
        grid=(B, nc),
        in_specs=[pl.BlockSpec(memory_space=pltpu.SMEM),
                  zspec(col_q, W), zspec(col_q + 1, W), zspec(col_q + 2, W), zspec(col_q + 3, W),
                  zspec(gate_cb, LANES),
                  pl.BlockSpec((1, W), lambda b, c: (0, 0)),
                  pl.BlockSpec((1, H, DH, DH), lambda b, c: (b, 0, 0, 0)),
                  pl.BlockSpec((1, H, 1, DH), lambda b, c: (b, 0, 0, 0)),
                  pl.BlockSpec((1, H, 1, LANES), lambda b, c: (b, 0, 0, 0))],
        out_specs=[pl.BlockSpec((Lb, W), lambda b, c: (b * nc + c, 0)),
                   pl.BlockSpec((1, H, DH, DH), lambda b, c: (b, 0, 0, 0)),
                   pl.BlockSpec((1, H, 1, DH), lambda b, c: (b, 0, 0, 0)),
                   pl.BlockSpec((1, H, 1, LANES), lambda b, c: (b, 0, 0, 0))],
        scratch_shapes=[pltpu.VMEM((H, DH, DH), F32), pltpu.VMEM((H, 1, DH), F32),
                        pltpu.VMEM((H, 1, LANES), F32)],
        compiler_params=_cparams(("parallel", "arbitrary")),
        name="mlstm",
    )(gate_b, z, z, z, z, z, mnorm_g.reshape(1, W), C0, n0, m0)
    return y, C, n.reshape(B, H, DH), m[:, :, 0, 0]


PAGES_PER_STEP = 16


def _lane_extract_topn(score, n_sel, floor):
    lane = lax.broadcasted_iota(jnp.int32, score.shape, 1).astype(F32)
    sel = jnp.zeros(score.shape, F32)
    sc = score
    for _ in range(n_sel):
        m = jnp.max(sc, -1, keepdims=True)
        idx = jnp.min(jnp.where(sc == m, lane, float(score.shape[1])), -1, keepdims=True)
        hit = lane == idx
        sel = jnp.where(hit & (m > floor), 1.0, sel)
        sc = jnp.where(hit, -jnp.inf, sc)
    return sel


def _s1_kernel(pt_ref, *refs, past, S, ncp, nbp):
    pgs = refs[:PAGES_PER_STEP]
    tail_ref, q_ref, ocmp_ref, sel_ref, sub_ref = refs[PAGES_PER_STEP:]
    P = pgs[0].shape[-1]
    j = pl.program_id(1)
    nfull = pl.num_programs(1) - 1
    KVH, G, HD = NSA_KV_HEADS, NSA_GROUP, HEAD_DIM
    cols_step = PAGES_PER_STEP * P // CMP_STRIDE
    assert cols_step == LANES

    @pl.when(j == 0)
    def _():
        sub_ref[...] = jnp.zeros(sub_ref.shape, F32)

    def pool_t(n):
        return jnp.where(_div(lax.broadcasted_iota(jnp.int32, (n, LANES), 0), CMP_STRIDE)
                         == lax.broadcasted_iota(jnp.int32, (n, LANES), 1),
                         1.0 / CMP_STRIDE, 0.0).astype(BF16)

    @pl.when(j < nfull)
    def _():
        pt = pool_t(PAGES_PER_STEP * P)
        col = pl.ds(pl.multiple_of(j * cols_step, cols_step), cols_step)
        for k in range(2):
            for h in range(KVH):
                xt = jnp.concatenate([r[0, 0, k, h] for r in pgs], axis=1)
                sub_ref[k, h, :, col] = _pool_dot(pt, xt, True)

    @pl.when(j == nfull)
    def _():
        base = past // CMP_STRIDE
        pt = pool_t(P)
        for k in range(2):
            for h in range(KVH):
                sub_ref[k, h, :, base:base + LANES] = _pool_dot(pt, tail_ref[0, k, h], True)
        R = G * S
        row = lax.broadcasted_iota(jnp.int32, (R, 1), 0)
        pos = past + (row & (S - 1))
        posq = past + lax.broadcasted_iota(jnp.int32, (S, 1), 0)
        c_end = lax.broadcasted_iota(jnp.int32, (1, ncp), 1) * CMP_STRIDE + (2 * CMP_STRIDE - 1)
        cmask = c_end <= pos
        per = SEL_BLOCK // CMP_STRIDE
        e4 = jnp.where(_div(lax.broadcasted_iota(jnp.int32, (ncp, nbp), 0), per)
                       == lax.broadcasted_iota(jnp.int32, (ncp, nbp), 1), 1.0, 0.0).astype(F32)
        blk = lax.broadcasted_iota(jnp.int32, (1, nbp), 1)
        qblk = _div(posq, SEL_BLOCK)
        valid = blk * SEL_BLOCK <= posq
        forced = (blk == 0) | (blk == qblk) | (blk == qblk - 1)
        for h in range(KVH):
            kct = 0.5 * (sub_ref[0, h, :, 0:ncp] + sub_ref[0, h, :, 1:ncp + 1])
            vct = 0.5 * (sub_ref[1, h, :, 0:ncp] + sub_ref[1, h, :, 1:ncp + 1])
            qs = q_ref[0, h] * jnp.asarray(HD ** -0.5, BF16)
            s = jnp.dot(qs, kct.astype(BF16), preferred_element_type=F32)
            p = _masked_softmax(s, cmask)
            ocmp_ref[0, h] = lax.dot_general(p.astype(BF16), vct.astype(BF16), NT, preferred_element_type=F32)
            imp = p[0:S]
            for g in range(1, G):
                imp = imp + p[g * S:(g + 1) * S]
            impb = jnp.dot(imp, e4, precision=HI, preferred_element_type=F32)
            score = jnp.where(forced, MASK_BIG, jnp.where(valid, impb, -MASK_BIG))
            sel_ref[0, h] = _lane_extract_topn(score, SEL_TOPN, -0.5 * MASK_BIG)


def _s2_kernel(pt_ref, *refs, past, S, nbp):
    pgs = refs[:PAGES_PER_STEP]
    tail_ref, q_ref, sel_ref, osel_ref, m_s, l_s, acc_s = refs[PAGES_PER_STEP:]
    P = pgs[0].shape[-1]
    j = pl.program_id(1)
    nfull = pl.num_programs(1) - 1
    KVH, G, HD = NSA_KV_HEADS, NSA_GROUP, HEAD_DIM
    R = G * S

    @pl.when(j == 0)
    def _():
        m_s[...] = jnp.full(m_s.shape, NEG, F32)
        l_s[...] = jnp.zeros(l_s.shape, F32)
        acc_s[...] = jnp.zeros(acc_s.shape, F32)

    row = lax.broadcasted_iota(jnp.int32, (R, 1), 0)
    pos = past + (row & (S - 1))

    def process(kt_of, vt_of, n, kpos0):
        kpos = kpos0 + lax.broadcasted_iota(jnp.int32, (1, n), 1)
        nblk = LANES
        assert n // SEL_BLOCK <= nblk
        blk0 = kpos0 // SEL_BLOCK
        pick = jnp.where(lax.broadcasted_iota(jnp.int32, (nbp, nblk), 0)
                         == blk0 + lax.broadcasted_iota(jnp.int32, (nbp, nblk), 1), 1.0, 0.0).astype(BF16)
        spread = jnp.where(_div(lax.broadcasted_iota(jnp.int32, (nblk, n), 1), SEL_BLOCK)
                           == lax.broadcasted_iota(jnp.int32, (nblk, n), 0), 1.0, 0.0).astype(BF16)
        causal = kpos <= pos
        for h in range(KVH):
            kt = kt_of(h).astype(BF16)
            vt = vt_of(h).astype(BF16)
            qs = q_ref[0, h] * jnp.asarray(HD ** -0.5, BF16)
            s = jnp.dot(qs, kt, preferred_element_type=F32)
            mkb = jnp.dot(sel_ref[0, h].astype(BF16), pick, preferred_element_type=F32)
            mk = jnp.dot(mkb.astype(BF16), spread, preferred_element_type=F32)
            mask = (jnp.concatenate([mk] * G, axis=0) > 0.5) & causal
            sm = jnp.where(mask, s, NEG)
            m_old = m_s[h]
            m_new = jnp.maximum(m_old, jnp.max(sm, -1, keepdims=True))
            alpha = jnp.exp(m_old - m_new)
            e = jnp.where(mask, jnp.exp(sm - m_new), 0.0)
            l_s[h] = alpha * l_s[h] + jnp.sum(e, -1, keepdims=True)
            acc_s[h] = alpha * acc_s[h] + lax.dot_general(e.astype(BF16), vt, NT, preferred_element_type=F32)
            m_s[h] = m_new

    @pl.when(j < nfull)
    def _():
        process(lambda h: jnp.concatenate([r[0, 0, 0, h] for r in pgs], axis=1),
                lambda h: jnp.concatenate([r[0, 0, 1, h] for r in pgs], axis=1),
                PAGES_PER_STEP * P, j * (PAGES_PER_STEP * P))

    @pl.when(j == nfull)
    def _():
        process(lambda h: tail_ref[0, 0, h], lambda h: tail_ref[0, 1, h], P, past)
        for h in range(KVH):
            osel_ref[0, h] = acc_s[h] * (1.0 / jnp.maximum(l_s[h], 1e-30))


def _page_specs(layer, half, n_pages, P):
    def spec(r):
        return pl.BlockSpec(
            (1, 1, 2, NSA_KV_HEADS, HEAD_DIM, P),
            lambda b, j, pt, r=r: (layer, pt[b, jnp.minimum(j * PAGES_PER_STEP + r, n_pages - 1)],
                                   half, 0, 0, 0))
    return [spec(r) for r in range(PAGES_PER_STEP)]


def _nsa_sample_global(cache_t, layer, page_table, tail_t, q_s, S):
    B, n_pages = page_table.shape
    P = cache_t.shape[-1]
    KVH, G, HD = NSA_KV_HEADS, NSA_GROUP, HEAD_DIM
    past = n_pages * P
    assert n_pages % PAGES_PER_STEP == 0 and S & (S - 1) == 0 and S <= SEL_BLOCK
    nsteps = n_pages // PAGES_PER_STEP + 1
    ncp = -(-(past + P) // CMP_STRIDE // LANES) * LANES
    nbp = -(-(past + P) // SEL_BLOCK // LANES) * LANES
    R = G * S
    qspec = pl.BlockSpec((1, KVH, R, HD), lambda b, j, pt: (b, 0, 0, 0))
    o_cmp, sel = pl.pallas_call(
        functools.partial(_s1_kernel, past=past, S=S, ncp=ncp, nbp=nbp),
        out_shape=[jax.ShapeDtypeStruct((B, KVH, R, HD), F32), jax.ShapeDtypeStruct((B, KVH, S, nbp), F32)],
        grid_spec=pltpu.PrefetchScalarGridSpec(
            num_scalar_prefetch=1, grid=(B, nsteps),
            in_specs=_page_specs(layer, 0, n_pages, P)
            + [pl.BlockSpec((1, 2, KVH, HD, P), lambda b, j, pt: (b, 0, 0, 0, 0)), qspec],
            out_specs=[pl.BlockSpec((1, KVH, R, HD), lambda b, j, pt: (b, 0, 0, 0)),
                       pl.BlockSpec((1, KVH, S, nbp), lambda b, j, pt: (b, 0, 0, 0))],
            scratch_shapes=[pltpu.VMEM((2, KVH, HD, ncp + LANES), F32)]),
        compiler_params=_cparams(("parallel", "arbitrary")),
        name="nsa_sample_cmp",
    )(page_table, *([cache_t] * PAGES_PER_STEP), tail_t, q_s)
    o_sel = pl.pallas_call(
        functools.partial(_s2_kernel, past=past, S=S, nbp=nbp),
        out_shape=jax.ShapeDtypeStruct((B, KVH, R, HD), F32),
        grid_spec=pltpu.PrefetchScalarGridSpec(
            num_scalar_prefetch=1, grid=(B, nsteps),
            in_specs=_page_specs(layer, 1, n_pages, P)
            + [pl.BlockSpec((1, 2, KVH, HD, P), lambda b, j, pt: (b, 1, 0, 0, 0)), qspec,
               pl.BlockSpec((1, KVH, S, nbp), lambda b, j, pt: (b, 0, 0, 0))],
            out_specs=pl.BlockSpec((1, KVH, R, HD), lambda b, j, pt: (b, 0, 0, 0)),
            scratch_shapes=[pltpu.VMEM((KVH, R, 1), F32), pltpu.VMEM((KVH, R, 1), F32),
                            pltpu.VMEM((KVH, R, HD), F32)]),
        compiler_params=_cparams(("parallel", "arbitrary")),
        name="nsa_sample_sel",
    )(page_table, *([cache_t] * PAGES_PER_STEP), tail_t, q_s, sel)
    return o_cmp, o_sel


def _s3_kernel(q_ref, kw_ref, vw_ref, ocmp_ref, osel_ref, gt_ref, y_ref, *, past, S):
    KVH, G, HD = NSA_KV_HEADS, NSA_GROUP, HEAD_DIM
    R = G * S
    nk = kw_ref.shape[1]
    row = lax.broadcasted_iota(jnp.int32, (R, 1), 0)
    pos = past + (row & (S - 1))
    kpos = past - (nk - S) + lax.broadcasted_iota(jnp.int32, (1, nk), 1)
    diff = pos - kpos
    wmask = (diff >= 0) & (diff < WINDOW)
    sg = jax.nn.sigmoid(gt_ref[...])
    kw = kw_ref[0].astype(BF16)
    vw = vw_ref[0].astype(BF16)
    outs = []
    for h in range(KVH):
        qs = q_ref[0, h] * jnp.asarray(HD ** -0.5, BF16)
        s = lax.dot_general(qs, kw[:, h * HD:(h + 1) * HD], NT, preferred_element_type=F32)
        p = _masked_softmax(s, wmask)
        o_win = jnp.dot(p.astype(BF16), vw[:, h * HD:(h + 1) * HD], preferred_element_type=F32)
        oc = ocmp_ref[0, h]
        osl = osel_ref[0, h]
        for g in range(G):
            c0 = (h * G + g) * 3
            rs = slice(g * S, (g + 1) * S)
            outs.append(sg[:, c0:c0 + 1] * oc[rs] + sg[:, c0 + 1:c0 + 2] * osl[rs]
                        + sg[:, c0 + 2:c0 + 3] * o_win[rs])
    y_ref[...] = jnp.concatenate(outs, axis=-1)


def _nsa_sample_combine(q_s, kwin, vwin, o_cmp, o_sel, z, row0, gate_cb, past, S):
    B, KVH, R, HD = q_s.shape
    nk = kwin.shape[1]
    W = KVH * HD
    ospec = pl.BlockSpec((1, KVH, R, HD), lambda b: (b, 0, 0, 0))
    return pl.pallas_call(
        functools.partial(_s3_kernel, past=past, S=S),
        out_shape=jax.ShapeDtypeStruct((B * S, KVH * NSA_GROUP * HD), F32),
        grid=(B,),
        in_specs=[ospec,
                  pl.BlockSpec((1, nk, W), lambda b: (b, 0, 0)),
                  pl.BlockSpec((1, nk, W), lambda b: (b, 0, 0)),
                  ospec, ospec,
                  pl.BlockSpec((S, LANES), lambda b: (row0 // S + b, gate_cb))],
        out_specs=pl.BlockSpec((S, KVH * NSA_GROUP * HD), lambda b: (b, 0)),
        compiler_params=_cparams(("parallel",)),
        name="nsa_sample_win",
    )(q_s, kwin, vwin, o_cmp, o_sel, z)


def _out_proj_kernel(x_ref, ya_ref, yb_ref, yc_ref, yt_ref, w_ref, o_ref, *, n_head_tiles):
    i = pl.program_id(0)
    ka, kb = ya_ref.shape[1], yb_ref.shape[1]

    @pl.when(i < n_head_tiles)
    def _():
        acc = jnp.dot(ya_ref[...].astype(BF16), w_ref[0:ka, :], preferred_element_type=F32)
        acc = acc + jnp.dot(yb_ref[...].astype(BF16), w_ref[ka:ka + kb, :], preferred_element_type=F32)
        acc = acc + jnp.dot(yc_ref[...].astype(BF16), w_ref[ka + kb:, :], preferred_element_type=F32)
        o_ref[...] = x_ref[...] + acc

    @pl.when(i >= n_head_tiles)
    def _():
        o_ref[...] = x_ref[...] + jnp.dot(yt_ref[...].astype(BF16), w_ref[...], preferred_element_type=F32)


def _out_proj(x, ya, yb, yc, ytail, w, tm):
    T, D = x.shape
    TH = ya.shape[0]
    assert TH % tm == 0 and (T - TH) % tm == 0 and ytail.shape == (T - TH, D)
    nh = TH // tm

    def head(a):
        return pl.BlockSpec((tm, a.shape[1]), lambda i: (jnp.minimum(i, nh - 1), 0))

    return pl.pallas_call(
        functools.partial(_out_proj_kernel, n_head_tiles=nh),
        out_shape=jax.ShapeDtypeStruct((T, D), F32),
        grid=(T // tm,),
        in_specs=[pl.BlockSpec((tm, D), lambda i: (i, 0)),
                  head(ya), head(yb), head(yc),
                  pl.BlockSpec((tm, D), lambda i: (jnp.maximum(i - nh, 0), 0)),
                  pl.BlockSpec((D, D), lambda i: (0, 0))],
        out_specs=pl.BlockSpec((tm, D), lambda i: (i, 0)),
        compiler_params=_cparams(("parallel",)),
        name="out_proj",
    )(x, ya, yb, yc, ytail, w)


def _structural_pairs(k):
    return [(i, j) for i in range(k) for j in range(k) if (i + 1) * (j + 1) <= k]


RANK_MARK = 2.0 ** 126


def _extract_topk(cur_ref, top_ref, c, exact):
    NK, K = PEER_NKEYS, PEER_TOPK
    for k in range(K):
        cur = cur_ref[c]
        m = jnp.max(cur, axis=0)
        if exact:
            key = lax.broadcasted_iota(jnp.int32, cur.shape, 0).astype(F32)
            idx = jnp.min(jnp.where(cur == m[None], key, float(NK)), axis=0)
            hit = key == idx[None]
        else:
            hit = cur == m[None]
        cur_ref[c] = jnp.where(hit, -RANK_MARK * (1.0 + k / 32.0), cur)
        top_ref[c, k] = m


def _router_kernel(q_ref, wb_ref, n_out, e1_out, r2_out, e2_out, s_ref, cur_ref, top_ref, tmp_ref):
    H, NK, K = PEER_HEADS, PEER_NKEYS, PEER_TOPK
    Tt = q_ref.shape[0]
    half = q_ref.shape[1] // 2
    for c in range(2):
        s = lax.dot_general(wb_ref[c], q_ref[:, c * half:(c + 1) * half], NT, preferred_element_type=F32)
        s_ref[c] = s.reshape(NK, H, Tt)
        cur_ref[c] = s_ref[c]
        _extract_topk(cur_ref, top_ref, c, exact=False)

    marked = jnp.sum(jnp.where(cur_ref[...] < -0.5 * RANK_MARK, 1.0, 0.0), axis=1)
    tied = jnp.max(jnp.where(marked != float(K), 1.0, 0.0)) > 0.0

    @pl.when(tied)
    def _():
        for c in range(2):
            cur_ref[c] = s_ref[c]
            _extract_topk(cur_ref, top_ref, c, exact=True)

    v1 = [top_ref[0, k] for k in range(K)]
    v2 = [top_ref[1, k] for k in range(K)]

    pairs = _structural_pairs(K)
    cand = [v1[i] + v2[j] for (i, j) in pairs]
    n = len(pairs)
    rank = []
    for p in range(n):
        rank.append(jnp.zeros((H, Tt), F32))
    for p in range(n):
        ip, jp = pairs[p]
        for q in range(p + 1, n):
            iq, jq = pairs[q]
            if ip <= iq and jp <= jq:
                rank[q] = rank[q] + 1.0
            else:
                b = jnp.where(cand[p] >= cand[q], 1.0, 0.0)
                rank[q] = rank[q] + b
                rank[p] = rank[p] + (1.0 - b)
    sel = [jnp.where(r < K, 1.0, 0.0) for r in rank]
    e1 = [jnp.exp(v1[i] - v1[0]) for i in range(K)]
    e2 = [jnp.exp(v2[j] - v2[0]) for j in range(K)]
    cnt = [jnp.zeros((H, Tt), F32) for _ in range(K)]
    zsum = jnp.zeros((H, Tt), F32)
    for p, (i, j) in enumerate(pairs):
        cnt[i] = cnt[i] + sel[p]
        zsum = zsum + sel[p] * (e1[i] * e2[j])
    inv_z = 1.0 / zsum

    def rank_of(c):
        cur = cur_ref[c]
        return jnp.where(cur < -0.5 * RANK_MARK, (cur * (-1.0 / RANK_MARK) - 1.0) * 32.0, float(NK))

    rk1 = rank_of(0)
    nk = jnp.zeros((NK, H, Tt), F32)
    for i in range(K):
        nk = jnp.where(rk1 == float(i), cnt[i][None], nk)
    outs = ((n_out, nk),
            (e1_out, jnp.exp(s_ref[0] - v1[0][None]) * inv_z[None]),
            (r2_out, rank_of(1)),
            (e2_out, jnp.exp(s_ref[1] - v2[0][None])))
    for ref, val in outs:
        tmp_ref[...] = val.reshape(NK * H, Tt)
        for h in range(H):
            ref[h] = tmp_ref[pl.ds(h, NK, stride=H), :].astype(ref.dtype)


def _router(q, wb, tt):
    T = q.shape[0]
    H, NK = PEER_HEADS, PEER_NKEYS
    shp = jax.ShapeDtypeStruct((H, NK, T), F32)
    shp_b = jax.ShapeDtypeStruct((H, NK, T), BF16)
    ospec = pl.BlockSpec((H, NK, tt), lambda i: (0, 0, i))
    return pl.pallas_call(
        _router_kernel,
        out_shape=[shp, shp, shp_b, shp_b],
        grid=(T // tt,),
        in_specs=[pl.BlockSpec((tt, q.shape[1]), lambda i: (i, 0)),
                  pl.BlockSpec(wb.shape, lambda i: (0, 0, 0))],
        out_specs=[ospec, ospec, ospec, ospec],
        scratch_shapes=[pltpu.VMEM((2, NK, H, tt), F32), pltpu.VMEM((2, NK, H, tt), F32),
                        pltpu.VMEM((2, PEER_TOPK, H, tt), F32), pltpu.VMEM((NK * H, tt), F32)],
        compiler_params=_cparams(("parallel",)),
        name="peer_router",
    )(q, wb)


def _gelu_tanh(x):
    return 0.5 * x * (1.0 + jnp.tanh(np.sqrt(2.0 / np.pi) * (x + 0.044715 * (x * x * x))))


PEER_ACHUNK = 2


def _peer_dense_kernel(xn_ref, u_ref, v_ref, n_ref, e1_ref, r2_ref, e2_ref, x1_ref, g_ref, o_ref, *, final_norm):
    H, NK = PEER_HEADS, PEER_NKEYS
    j = pl.program_id(1)
    na = u_ref.shape[0] // NK
    tt = xn_ref.shape[0]

    @pl.when(j == 0)
    def _():
        o_ref[...] = x1_ref[...]

    xn = xn_ref[...]
    parts = []
    for c in range(na // PEER_ACHUNK):
        rows = slice(c * PEER_ACHUNK * NK, (c + 1) * PEER_ACHUNK * NK)
        act = lax.dot_general(u_ref[rows, :], xn, NT, preferred_element_type=F32)
        gates = []
        for a in range(c * PEER_ACHUNK, (c + 1) * PEER_ACHUNK):
            row = pl.ds(j * na + a, 1)
            gate = None
            for h in range(H):
                t = jnp.where(r2_ref[h] < n_ref[h, row, :].astype(BF16),
                              e1_ref[h, row, :].astype(BF16) * e2_ref[h], jnp.zeros((), BF16))
                gate = t if gate is None else gate + t
            gates.append(gate)
        parts.append(_gelu_tanh(act).astype(BF16) * jnp.concatenate(gates, axis=0))
    wt = jnp.concatenate(parts, axis=0)
    o_ref[...] += lax.dot_general(wt, v_ref[...], TN, preferred_element_type=F32)

    if final_norm:
        @pl.when(j == pl.num_programs(1) - 1)
        def _():
            x = o_ref[...]
            o_ref[...] = x * lax.rsqrt(jnp.mean(x * x, -1, keepdims=True) + RMS_EPS) * g_ref[...]


def _peer_dense(xn, u, v, layer, nk, e1, r2, e2, x1, g, tt, te, final_norm):
    T, D = xn.shape
    E = u.shape[1]
    H, NK = PEER_HEADS, PEER_NKEYS
    once = pl.Buffered(1)
    tab = pl.BlockSpec((H, NK, tt), lambda i, j: (0, 0, i), pipeline_mode=once)
    return pl.pallas_call(
        functools.partial(_peer_dense_kernel, final_norm=final_norm),
        out_shape=jax.ShapeDtypeStruct((T, D), F32),
        grid=(T // tt, E // te),
        in_specs=[pl.BlockSpec((tt, D), lambda i, j: (i, 0), pipeline_mode=once),
                  pl.BlockSpec((None, te, D), lambda i, j: (layer, j, 0)),
                  pl.BlockSpec((None, te, D), lambda i, j: (layer, j, 0)),
                  tab, tab, tab, tab,
                  pl.BlockSpec((tt, D), lambda i, j: (i, 0), pipeline_mode=once),
                  pl.BlockSpec((1, D), lambda i, j: (0, 0))],
        out_specs=pl.BlockSpec((tt, D), lambda i, j: (i, 0)),
        compiler_params=_cparams(("parallel", "arbitrary")),
        name="peer_dense",
    )(xn, u, v, nk, e1, r2, e2, x1, g.reshape(1, D))


T_TILE = 384
ROUTER_TILE = 128
PEER_T_TILE = 384
PAD_TILE = 768
Z_COL_TILE = 1280
OUT_TILE = 256
EXPERT_TILE = 1024


def kernel(x_prompt, x_sample, cache_nsa_kv, state_win_kv, state_conv, state_mlstm_C, state_mlstm_n,
           state_mlstm_m, page_table, norm1_g, w_in, conv_w, mlstm_gate_b, mlstm_norm_g, w_out, norm2_g,
           peer_wq, peer_subkeys, peer_u, peer_v, final_norm_g):
    BP, SP, D = x_prompt.shape
    BS, SS, _ = x_sample.shape
    depth = w_in.shape[0]
    KVH, G, HD = NSA_KV_HEADS, NSA_GROUP, HEAD_DIM
    H, DH = MLSTM_HEADS, MLSTM_DH
    CW = conv_w.shape[2]
    NW = KVH * G * HD
    KVW = 6 * KVH * HD
    NG = 3 * KVH * G
    MW = H * DH
    TP, TS = BP * SP, BS * SS
    T = TP + TS
    assert PAD_TILE % T_TILE == 0 and PAD_TILE % ROUTER_TILE == 0 and PAD_TILE % PEER_T_TILE == 0
    TPAD = -(-T // PAD_TILE) * PAD_TILE
    n_pages = page_table.shape[1]
    P = cache_nsa_kv.shape[2]
    past = n_pages * P
    wb_rows = state_win_kv.shape[2]

    c_q = 3 * CW
    c_kv = c_q + NW
    c_m = c_kv + KVW
    c_g = c_m + 4 * MW
    ZW = -(-(c_g + LANES) // Z_COL_TILE) * Z_COL_TILE
    assert CW % LANES == 0 and c_m % MW == 0 and c_g % LANES == 0 and NG + 2 * H <= LANES
    gate_cb = c_g // LANES
    gi, gf = NG, NG + H
    o_gate = 3 * CW + NW + KVW

    x = jnp.concatenate([x_prompt.reshape(TP, D), x_sample.reshape(TS, D),
                         jnp.zeros((TPAD - T, D), F32)], axis=0)
    cache_t = jnp.transpose(cache_nsa_kv, (0, 1, 3, 4, 5, 2))
    eye_h = jnp.eye(PEER_HEADS, dtype=F32)
    u_bf = peer_u.astype(BF16)
    v_bf = peer_v.astype(BF16)
    dk = peer_subkeys.shape[-1]

    p_st, s_st = [], []
    for l in range(depth):
        wi = w_in[l]
        w_perm = jnp.concatenate(
            [wi[:, :o_gate], wi[:, o_gate + NG:o_gate + NG + 4 * MW], wi[:, o_gate:o_gate + NG],
             wi[:, o_gate + NG + 4 * MW:], jnp.zeros((D, ZW - wi.shape[1]), F32)], axis=1).astype(BF16)
        z = _rms_proj(x, norm1_g[l], w_perm, PAD_TILE, Z_COL_TILE, F32)

        ya_p, conv_p = _conv(z, 0, BP, SP, jnp.zeros((BP, 2, CW), F32), conv_w[l])
        ya_s, conv_s = _conv(z, TP, BS, SS, state_conv[l], conv_w[l])

        yb_p, rows_t, win_t = _nsa_prompt(z, c_q, c_kv, gate_cb, BP, SP)
        rows_p = rows_t.transpose(0, 4, 1, 2, 3)
        win_p = win_t.transpose(0, 4, 1, 2, 3)

        zs = z[TP:T]
        q_s = zs[:, c_q:c_q + NW].reshape(BS, SS, KVH, G, HD).transpose(0, 2, 3, 1, 4)
        q_s = q_s.reshape(BS, KVH, G * SS, HD).astype(BF16)
        zkv_s = zs[:, c_kv:c_kv + KVW].reshape(BS, SS, 3, 2, KVH, HD)
        rows_s = zkv_s[:, :, :2].reshape(BS, SS, 4, KVH, HD)
        tail_t = jnp.pad(rows_s.transpose(0, 2, 3, 4, 1), ((0, 0), (0, 0), (0, 0), (0, 0), (0, P - SS)))
        o_cmp, o_sel = _nsa_sample_global(cache_t, l, page_table, tail_t, q_s, SS)
        w_all = jnp.concatenate([state_win_kv[l], zkv_s[:, :, 2]], axis=1)
        kwin = w_all[:, :, 0].reshape(BS, wb_rows + SS, KVH * HD)
        vwin = w_all[:, :, 1].reshape(BS, wb_rows + SS, KVH * HD)
        yb_s = _nsa_sample_combine(q_s, kwin, vwin, o_cmp, o_sel, z, TP, gate_cb, past, SS)
        win_s = w_all[:, SS:]

        yc_p, C_p, n_p, m_p = _mlstm(z, 0, BP, SP, c_m // MW, gate_cb, gi, gf, mlstm_gate_b[l],
                                     mlstm_norm_g[l], jnp.zeros((BP, H, DH, DH), F32),
                                     jnp.zeros((BP, H, DH), F32), jnp.zeros((BP, H), F32))
        yc_s, C_s, n_s, m_s = _mlstm(z, TP, BS, SS, c_m // MW, gate_cb, gi, gf, mlstm_gate_b[l],
                                     mlstm_norm_g[l], state_mlstm_C[l], state_mlstm_n[l], state_mlstm_m[l])

        ytail = jnp.concatenate([jnp.concatenate([ya_s, yb_s, yc_s], axis=1),
                                 jnp.zeros((TPAD - T, D), F32)], axis=0)
        x1 = _out_proj(x, ya_p, yb_p, yc_p, ytail, w_out[l].astype(BF16), OUT_TILE)

        nh = PEER_HEADS
        wq_perm = peer_wq[l].reshape(D, nh, 2, dk).transpose(0, 2, 1, 3).reshape(D, 2 * nh * dk).astype(BF16)
        q_peer, xn2 = _rms_proj(x1, norm2_g[l], wq_perm, PAD_TILE, nh * dk, BF16, emit_xn=True)
        wb = jnp.einsum('hcnd,hg->cnhgd', peer_subkeys[l], eye_h).reshape(2, PEER_NKEYS * nh, nh * dk).astype(BF16)
        nk, e1, r2, e2 = _router(q_peer, wb, ROUTER_TILE)
        x = _peer_dense(xn2, u_bf, v_bf, l, nk, e1, r2, e2, x1,
                        final_norm_g, PEER_T_TILE, EXPERT_TILE, final_norm=(l == depth - 1))

        p_st.append((rows_p, win_p, conv_p, C_p, n_p, m_p))
        s_st.append((rows_s, win_s, conv_s, C_s, n_s, m_s))

    p_rows, p_win, p_conv, p_C, p_n, p_m = [jnp.stack(a) for a in zip(*p_st)]
    s_rows, s_win, s_conv, s_C, s_n, s_m = [jnp.stack(a) for a in zip(*s_st)]
    y_prompt = x[:TP].reshape(BP, SP, D)
    y_sample = x[TP:T].reshape(BS, SS, D)
    return (y_prompt, y_sample, p_rows, p_win, p_conv, p_C, p_n, p_m, s_rows, s_win, s_conv, s_C, s_n, s_m)
```

```python
import functools

import numpy as np
import jax
import jax.numpy as jnp
from jax import lax
from jax.experimental import pallas as pl
from jax.experimental.pallas import tpu as pltpu

F32 = jnp.float32
BF16 = jnp.bfloat16
HI = lax.Precision.HIGHEST

RMS_EPS = 1e-6
HEAD_DIM = 64
NSA_KV_HEADS = 4
NSA_GROUP = 4
CMP_STRIDE = 16
SEL_BLOCK = 64
SEL_TOPN = 16
WINDOW = 512
MLSTM_HEADS = 4
MLSTM_DH = 128
MLSTM_CHUNK = 64
PEER_HEADS = 8
PEER_NKEYS = 128
PEER_TOPK = 16
MASK_BIG = 1e9
NEG = -1e30

LANES = 128
VMEM_LIMIT = 56 * 1024 * 1024

NT = (((1,), (1,)), ((), ()))
TN = (((0,), (0,)), ((), ()))


def _cparams(sem):
    return pltpu.CompilerParams(dimension_semantics=sem, vmem_limit_bytes=VMEM_LIMIT)


def _div(x, d):
    assert d & (d - 1) == 0
    return lax.shift_right_arithmetic(x, jnp.int32(d.bit_length() - 1))


def _pool_dot(pool, x, x_is_lhs):
    hi = x.astype(BF16)
    lo = (x - hi.astype(F32)).astype(BF16)
    if x_is_lhs:
        return (jnp.dot(hi, pool, preferred_element_type=F32) + jnp.dot(lo, pool, preferred_element_type=F32))
    return (jnp.dot(pool, hi, preferred_element_type=F32) + jnp.dot(pool, lo, preferred_element_type=F32))


def _masked_softmax(s, mask, exp=jnp.exp):
    s = jnp.where(mask, s, NEG)
    m = jnp.max(s, -1, keepdims=True)
    e = jnp.where(mask, exp(s - m), 0.0)
    d = jnp.maximum(jnp.sum(e, -1, keepdims=True), 1e-30)
    return e * (1.0 / d)


def _rms_proj_kernel(x_ref, g_ref, w_ref, *rest, emit_xn):
    if emit_xn:
        o_ref, xo_ref, xn_ref = rest
    else:
        o_ref, xn_ref = rest

    @pl.when(pl.program_id(1) == 0)
    def _():
        x = x_ref[...]
        r = x * lax.rsqrt(jnp.mean(x * x, -1, keepdims=True) + RMS_EPS)
        xn = (r * g_ref[...]).astype(BF16)
        xn_ref[...] = xn
        if emit_xn:
            xo_ref[...] = xn

    o_ref[...] = jnp.dot(xn_ref[...], w_ref[...], preferred_element_type=F32).astype(o_ref.dtype)


def _rms_proj(x, g, w, tm, tn, out_dtype, emit_xn=False):
    T, D = x.shape
    N = w.shape[1]
    out_shape = [jax.ShapeDtypeStruct((T, N), out_dtype)]
    out_specs = [pl.BlockSpec((tm, tn), lambda i, j: (i, j))]
    if emit_xn:
        out_shape.append(jax.ShapeDtypeStruct((T, D), BF16))
        out_specs.append(pl.BlockSpec((tm, D), lambda i, j: (i, 0)))
    res = pl.pallas_call(
        functools.partial(_rms_proj_kernel, emit_xn=emit_xn),
        out_shape=out_shape,
        grid=(T // tm, N // tn),
        in_specs=[pl.BlockSpec((tm, D), lambda i, j: (i, 0)),
                  pl.BlockSpec((1, D), lambda i, j: (0, 0)),
                  pl.BlockSpec((D, tn), lambda i, j: (0, j))],
        out_specs=out_specs,
        scratch_shapes=[pltpu.VMEM((tm, D), BF16)],
        compiler_params=_cparams(("parallel", "arbitrary")),
        name="rms_proj",
    )(x, g.reshape(1, D), w)
    return res if emit_xn else res[0]


def _conv_kernel(cb_ref, cc_ref, ch_ref, buf_ref, w_ref, y_ref, new_ref, ext_ref, *, S):
    C = cb_ref.shape[-1]
    u = cc_ref[...] * ch_ref[...]
    ext_ref[0:8, :] = jnp.zeros((8, C), F32)
    ext_ref[6:8, :] = buf_ref[0]
    ext_ref[8:8 + S, :] = u
    w = w_ref[...]
    y = w[0:1] * ext_ref[6:6 + S, :] + w[1:2] * ext_ref[7:7 + S, :] + w[2:3] * u
    y_ref[...] = cb_ref[...] * y
    new_ref[0] = u[S - 2:S]


def _conv(z, row0, B, S, buf, w):
    C = w.shape[1]
    rb0 = row0 // S
    return pl.pallas_call(
        functools.partial(_conv_kernel, S=S),
        out_shape=[jax.ShapeDtypeStruct((B * S, C), F32), jax.ShapeDtypeStruct((B, 2, C), F32)],
        grid=(B,),
        in_specs=[pl.BlockSpec((S, C), lambda b: (rb0 + b, 0)),
                  pl.BlockSpec((S, C), lambda b: (rb0 + b, 1)),
                  pl.BlockSpec((S, C), lambda b: (rb0 + b, 2)),
                  pl.BlockSpec((1, 2, C), lambda b: (b, 0, 0)),
                  pl.BlockSpec((3, C), lambda b: (0, 0))],
        out_specs=[pl.BlockSpec((S, C), lambda b: (b, 0)),
                   pl.BlockSpec((1, 2, C), lambda b: (b, 0, 0))],
        scratch_shapes=[pltpu.VMEM((S + 8, C), F32)],
        compiler_params=_cparams(("parallel",)),
        name="short_conv",
    )(z, z, z, buf, w)


def _topn_rank_select_t(score_t, n_sel):
    NB = score_t.shape[0]
    blk = lax.broadcasted_iota(jnp.int32, score_t.shape, 0)
    rank = jnp.zeros(score_t.shape, F32)
    for i in range(NB):
        ci = score_t[i:i + 1, :]
        beats = jnp.where(ci > score_t, 1.0, jnp.where((ci == score_t) & (blk > i), 1.0, 0.0))
        rank = rank + beats
    return rank < n_sel


SEL_KCHUNK = 512


def _exp2_softmax_pv(pieces, hd):
    m = None
    for s, _ in pieces:
        mm = jnp.max(s, -1, keepdims=True)
        m = mm if m is None else jnp.maximum(m, mm)
    oa = None
    for s, v in pieces:
        t = jnp.dot(jnp.exp2(s - m).astype(BF16), v, preferred_element_type=F32)
        oa = t if oa is None else oa + t
    return oa[:, 0:hd] * (1.0 / oa[:, hd:hd + 1])


NSA_HPS = 2
QK_WIDTH = 128


def _nsa_prompt_kernel(q_ref, kcs_ref, vcs_ref, kss_ref, vss_ref, kws_ref, vws_ref, gt_ref,
                       y_ref, rows_ref, win_ref, kc_ref, vc_ref, ks_ref, vs_ref, kw_ref, vw_ref, osel_s, *, S, tq):
    pair = pl.program_id(1)
    qi = pl.program_id(2)
    G, HD = NSA_GROUP, HEAD_DIM
    NC = S // CMP_STRIDE
    NB = S // SEL_BLOCK
    n_sel = min(SEL_TOPN, NB)
    span = min(WINDOW + tq, S)

    @pl.when(qi == 0)
    def _():
        j = lax.broadcasted_iota(jnp.int32, (NC, S), 0)
        r = lax.broadcasted_iota(jnp.int32, (NC, S), 1)
        lo = j * CMP_STRIDE
        pool = jnp.where((r >= lo) & (r < lo + 2 * CMP_STRIDE), 0.5 / CMP_STRIDE, 0.0).astype(BF16)
        kcp = _pool_dot(pool, kcs_ref[...], False).astype(BF16)
        vcp = _pool_dot(pool, vcs_ref[...], False).astype(BF16)
        blk1h = jnp.where(_div(lax.broadcasted_iota(jnp.int32, (S, NB), 0), SEL_BLOCK)
                          == lax.broadcasted_iota(jnp.int32, (S, NB), 1), 1.0, 0.0).astype(BF16)
        kpad = jnp.zeros((S, QK_WIDTH - HD - NB), BF16)
        ones = jnp.where(lax.broadcasted_iota(jnp.int32, (S, LANES - HD), 1) == 0, 1.0, 0.0).astype(BF16)
        for hh in range(NSA_HPS):
            cols = slice(hh * HD, (hh + 1) * HD)
            kc_ref[hh] = kcp[:, cols]
            vc_ref[hh] = vcp[:, cols]
            ks_ref[hh] = jnp.concatenate([kss_ref[:, cols].astype(BF16), blk1h, kpad], axis=1)
            vs_ref[hh] = jnp.concatenate([vss_ref[:, cols].astype(BF16), ones], axis=1)
            kw_ref[hh] = kws_ref[:, cols].astype(BF16)
            vw_ref[hh] = jnp.concatenate([vws_ref[:, cols].astype(BF16), ones], axis=1)
        for kind, src in enumerate((kcs_ref, vcs_ref, kss_ref, vss_ref)):
            rows_ref[0, kind] = src[...].T.reshape(NSA_HPS, HD, S)
        wn = win_ref.shape[-1]
        for kind, src in enumerate((kws_ref, vws_ref)):
            win_ref[0, kind] = src[S - wn:S, :].T.reshape(NSA_HPS, HD, wn)

    t0 = qi * tq
    row = lax.broadcasted_iota(jnp.int32, (G * tq, 1), 0)
    pos = t0 + (row & (tq - 1))
    posq = t0 + lax.broadcasted_iota(jnp.int32, (tq, 1), 0)
    c_end = lax.broadcasted_iota(jnp.int32, (1, NC), 1) * CMP_STRIDE + (2 * CMP_STRIDE - 1)
    cmask = c_end <= pos
    per = SEL_BLOCK // CMP_STRIDE
    e4t = jnp.where(_div(lax.broadcasted_iota(jnp.int32, (NB, NC), 1), per)
                    == lax.broadcasted_iota(jnp.int32, (NB, NC), 0), 1.0, 0.0).astype(F32)
    eye_nb = jnp.where(lax.broadcasted_iota(jnp.int32, (NB, NB), 0)
                       == lax.broadcasted_iota(jnp.int32, (NB, NB), 1), 1.0, 0.0).astype(BF16)
    blk_t = lax.broadcasted_iota(jnp.int32, (NB, 1), 0)
    posq_t = t0 + lax.broadcasted_iota(jnp.int32, (1, tq), 1)
    qblk_t = _div(posq_t, SEL_BLOCK)
    valid_t = blk_t * SEL_BLOCK <= posq_t
    forced_t = (blk_t == 0) | (blk_t == qblk_t) | (blk_t == qblk_t - 1)

    qscale = (HD ** -0.5) * np.log2(np.e)
    qss, qas, o_cmps = [], [], []
    for hh in range(NSA_HPS):
        qh = q_ref[:, hh * G * HD:(hh + 1) * G * HD] * qscale
        qs = jnp.concatenate([qh[:, g * HD:(g + 1) * HD] for g in range(G)], axis=0).astype(BF16)
        s = lax.dot_general(qs, kc_ref[hh], NT, preferred_element_type=F32)
        p = _masked_softmax(s, cmask, jnp.exp2)
        o_cmps.append(jnp.dot(p.astype(BF16), vc_ref[hh], preferred_element_type=F32))
        imp = p[0:tq]
        for g in range(1, G):
            imp = imp + p[g * tq:(g + 1) * tq]
        impb_t = lax.dot_general(e4t, imp, NT, precision=HI, preferred_element_type=F32)
        score_t = jnp.where(forced_t, MASK_BIG, jnp.where(valid_t, impb_t, -MASK_BIG))
        sel_t = _topn_rank_select_t(score_t, n_sel) & (score_t > -0.5 * MASK_BIG)
        selneg_t = jnp.where(sel_t, 0.0, NEG).astype(BF16)
        selneg = lax.dot_general(selneg_t, eye_nb, TN, preferred_element_type=F32).astype(BF16)
        qas.append(jnp.concatenate([qs, jnp.concatenate([selneg] * G, axis=0),
                                    jnp.zeros((G * tq, QK_WIDTH - HD - NB), BF16)], axis=1))
        qss.append(qs)

    def sel_branch(klen):
        tail = min(klen, SEL_KCHUNK)
        kpos = (klen - tail) + lax.broadcasted_iota(jnp.int32, (1, tail), 1)
        tail_bias = jnp.where(kpos <= posq, 0.0, NEG)
        for hh in range(NSA_HPS):
            for g in range(G):
                qa = qas[hh][g * tq:(g + 1) * tq]
                pieces = []
                if klen > tail:
                    pieces.append((lax.dot_general(qa, ks_ref[hh, 0:klen - tail, :], NT,
                                                   preferred_element_type=F32),
                                   vs_ref[hh, 0:klen - tail, :]))
                pieces.append((lax.dot_general(qa, ks_ref[hh, klen - tail:klen, :], NT,
                                               preferred_element_type=F32) + tail_bias,
                               vs_ref[hh, klen - tail:klen, :]))
                osel_s[hh, g * tq:(g + 1) * tq, :] = _exp2_softmax_pv(pieces, HD)

    nvar = -(-S // SEL_KCHUNK)
    per_var = SEL_KCHUNK // tq
    for v in range(nvar):
        @pl.when(qi // per_var == v)
        def _(v=v):
            sel_branch(min(S, (v + 1) * SEL_KCHUNK))

    start = pl.multiple_of(jnp.maximum(t0 + tq - span, 0), tq)
    diff = posq - (start + lax.broadcasted_iota(jnp.int32, (1, span), 1))
    win_bias = jnp.where((diff >= 0) & (diff < WINDOW), 0.0, NEG)

    sg = jax.nn.sigmoid(gt_ref[...])
    outs = []
    for hh in range(NSA_HPS):
        kvh = pair * NSA_HPS + hh
        pick = jnp.where(lax.broadcasted_iota(jnp.int32, (LANES, LANES), 0)
                         == lax.broadcasted_iota(jnp.int32, (LANES, LANES), 1) + kvh * (3 * G),
                         1.0, 0.0).astype(F32)
        g12 = jnp.dot(sg, pick, precision=HI, preferred_element_type=F32)
        kw = kw_ref[hh, pl.ds(start, span), :]
        vw = vw_ref[hh, pl.ds(start, span), :]
        for g in range(G):
            rows = slice(g * tq, (g + 1) * tq)
            s_win = lax.dot_general(qss[hh][rows], kw, NT, preferred_element_type=F32) + win_bias
            o_win = _exp2_softmax_pv([(s_win, vw)], HD)
            outs.append(g12[:, 3 * g:3 * g + 1] * o_cmps[hh][rows]
                        + g12[:, 3 * g + 1:3 * g + 2] * osel_s[hh, rows, :]
                        + g12[:, 3 * g + 2:3 * g + 3] * o_win)
    y_ref[...] = jnp.concatenate(outs, axis=-1)


def _nsa_prompt(z, c_q, c_kv, gate_cb, B, S, tq=256):
    nq = S // tq
    KVH, G, HD = NSA_KV_HEADS, NSA_GROUP, HEAD_DIM
    NC = S // CMP_STRIDE
    qw = NSA_HPS * G * HD
    assert NSA_HPS * HD == LANES and c_q % qw == 0 and c_kv % LANES == 0

    def kv_spec(i):
        cb = (c_kv + i * KVH * HD) // LANES
        return pl.BlockSpec((S, LANES), lambda b, p, t, cb=cb: (b, cb + p))

    head_kv = pltpu.VMEM((NSA_HPS, S, HD), BF16)
    head_aug = pltpu.VMEM((NSA_HPS, S, LANES), BF16)
    assert HD + S // SEL_BLOCK <= LANES
    wn = min(WINDOW, S)
    return pl.pallas_call(
        functools.partial(_nsa_prompt_kernel, S=S, tq=tq),
        out_shape=[jax.ShapeDtypeStruct((B * S, KVH * G * HD), F32),
                   jax.ShapeDtypeStruct((B, 4, KVH, HD, S), F32),
                   jax.ShapeDtypeStruct((B, 2, KVH, HD, wn), F32)],
        grid=(B, KVH // NSA_HPS, nq),
        in_specs=[pl.BlockSpec((tq, qw), lambda b, p, t: (b * nq + t, c_q // qw + p)),
                  kv_spec(0), kv_spec(1), kv_spec(2), kv_spec(3), kv_spec(4), kv_spec(5),
                  pl.BlockSpec((tq, LANES), lambda b, p, t: (b * nq + t, gate_cb))],
        out_specs=[pl.BlockSpec((tq, qw), lambda b, p, t: (b * nq + t, p)),
                   pl.BlockSpec((1, 4, NSA_HPS, HD, S), lambda b, p, t: (b, 0, p, 0, 0)),
                   pl.BlockSpec((1, 2, NSA_HPS, HD, wn), lambda b, p, t: (b, 0, p, 0, 0))],
        scratch_shapes=[pltpu.VMEM((NSA_HPS, NC, HD), BF16), pltpu.VMEM((NSA_HPS, NC, HD), BF16),
                        pltpu.VMEM((NSA_HPS, S, QK_WIDTH), BF16), head_aug, head_kv, head_aug,
                        pltpu.VMEM((NSA_HPS, G * tq, HD), F32)],
        compiler_params=_cparams(("parallel", "parallel", "arbitrary")),
        name="nsa_prompt",
    )(z, z, z, z, z, z, z, z)


def _log_sigmoid(x):
    return jnp.minimum(x, 0.0) - jnp.log(1.0 + jnp.exp(-jnp.abs(x)))


def _mlstm_kernel(gb_ref, q_ref, k_ref, v_ref, o_ref, gt_ref, mg_ref, c0_ref, n0_ref, m0_ref,
                  y_ref, cn_ref, nn_ref, mn_ref, c_s, n_s, m_s, *, L, Lb, gi, gf):
    H, DH = MLSTM_HEADS, MLSTM_DH
    c = pl.program_id(1)

    @pl.when(c == 0)
    def _():
        c_s[...] = c0_ref[0]
        n_s[...] = n0_ref[0]
        m_s[...] = m0_ref[0]

    def padrows(a):
        if Lb == L:
            return a
        return jnp.concatenate([a, jnp.zeros((L - Lb, a.shape[1]), a.dtype)], axis=0)

    lane = lax.broadcasted_iota(jnp.int32, (1, LANES), 1)
    bias = jnp.zeros((1, LANES), F32)
    for h in range(H):
        bias = bias + jnp.where(lane == gi + h, gb_ref[0, h], 0.0) + jnp.where(lane == gf + h, gb_ref[1, h], 0.0)
    is_f = (lane >= gf) & (lane < gf + H)
    pre = padrows(gt_ref[...]) + bias
    gate = jnp.where(is_f, _log_sigmoid(pre), pre)
    if Lb != L:
        live = lax.broadcasted_iota(jnp.int32, (L, 1), 0) < Lb
        gate = jnp.where(live, gate, jnp.where(is_f, 0.0, NEG))
    rr = lax.broadcasted_iota(jnp.int32, (L, L), 0)
    cc = lax.broadcasted_iota(jnp.int32, (L, L), 1)
    tril = rr >= cc
    bcum = jnp.dot(jnp.where(tril, 1.0, 0.0).astype(F32), gate, precision=HI, preferred_element_type=F32)
    e8 = jnp.where(lax.broadcasted_iota(jnp.int32, (8, LANES), 1)
                   == lax.broadcasted_iota(jnp.int32, (8, LANES), 0) + gi, 1.0, 0.0).astype(F32)
    rg = lax.dot_general(e8, gate, NT, precision=HI, preferred_element_type=F32)
    rb = lax.dot_general(e8, bcum, NT, precision=HI, preferred_element_type=F32)

    q = padrows(q_ref[...])
    k = padrows(k_ref[...])
    v = padrows(v_ref[...])
    og = padrows(o_ref[...])
    ys = []
    for h in range(H):
        sl = slice(h * DH, (h + 1) * DH)
        qq = q[:, sl]
        kk = k[:, sl] * (DH ** -0.5)
        vv = v[:, sl]
        qb, kb, vb = qq.astype(BF16), kk.astype(BF16), vv.astype(BF16)
        b_col = bcum[:, gf + h:gf + h + 1]
        b_row = rb[H + h:H + h + 1, :]
        i_row = rg[h:h + 1, :]
        i_col = gate[:, gi + h:gi + h + 1]
        m_prev = m_s[h][:, 0:1]
        cmat = c_s[h]
        n_row = n_s[h]
        dmat = jnp.where(tril, b_col - b_row + i_row, NEG)
        inter = b_col + m_prev
        mt = jnp.maximum(inter, jnp.max(dmat, -1, keepdims=True))
        w = jnp.exp(dmat - mt)
        a = jnp.exp(inter - mt)
        wqk = w * lax.dot_general(qb, kb, NT, preferred_element_type=F32)
        num = (a * jnp.dot(qb, cmat.astype(BF16), preferred_element_type=F32)
               + jnp.dot(wqk.astype(BF16), vb, preferred_element_type=F32))
        den = a * jnp.sum(qq * n_row, -1, keepdims=True) + jnp.sum(wqk, -1, keepdims=True)
        hh = num * (1.0 / jnp.maximum(jnp.abs(den), jnp.exp(-mt)))
        b_last = b_col[L - 1:L]
        m_new = mt[L - 1:L]
        wl = jnp.exp(b_last - b_col + i_col - m_new)
        decay = jnp.exp(b_last + m_prev - m_new)
        kw = wl * kk
        c_s[h] = decay * cmat + lax.dot_general(kw.astype(BF16), vb, TN, preferred_element_type=F32)
        n_s[h] = decay * n_row + jnp.sum(kw, 0, keepdims=True)
        m_s[h] = jnp.broadcast_to(m_new, (1, LANES))
        hn = hh * lax.rsqrt(jnp.mean(hh * hh, -1, keepdims=True) + RMS_EPS)
        ys.append(jax.nn.sigmoid(og[:, sl]) * hn * mg_ref[:, sl])
    y = jnp.concatenate(ys, axis=-1)
    y_ref[...] = y[0:Lb]

    @pl.when(c == pl.num_programs(1) - 1)
    def _():
        cn_ref[0] = c_s[...]
        nn_ref[0] = n_s[...]
        mn_ref[0] = m_s[...]


def _mlstm(z, row0, B, S, col_q, gate_cb, gi, gf, gate_b, mnorm_g, C0, n0, m0):
    H, DH = MLSTM_HEADS, MLSTM_DH
    W = H * DH
    L = MLSTM_CHUNK
    Lb = L if S % L == 0 else S
    assert Lb <= L
    nc = S // Lb
    rb0 = row0 // Lb
    n0 = n0.reshape(B, H, 1, DH)
    m0 = jnp.broadcast_to(m0.reshape(B, H, 1, 1), (B, H, 1, LANES))

    def zspec(cb, width):
        return pl.BlockSpec((Lb, width), lambda b, c, cb=cb: (rb0 + b * nc + c, cb))

    y, C, n, m = pl.pallas_call(
        functools.partial(_mlstm_kernel, L=L, Lb=Lb, gi=gi, gf=gf),
        out_shape=[jax.ShapeDtypeStruct((B * S, W), F32),
                   jax.ShapeDtypeStruct((B, H, DH, DH), F32),
                   jax.ShapeDtypeStruct((B, H, 1, DH), F32),
                   jax.ShapeDtypeStruct((B, H, 1, LANES), F32)],
        grid=(B, nc),
        in_specs=[pl.BlockSpec(memory_space=pltpu.SMEM),
                  zspec(col_q, W), zspec(col_q + 1, W), zspec(col_q + 2, W), zspec(col_q + 3, W),
                  zspec(gate_cb, LANES),
                  pl.BlockSpec((1, W), lambda b, c: (0, 0)),
                  pl.BlockSpec((1, H, DH, DH), lambda b, c: (b, 0, 0, 0)),
                  pl.BlockSpec((1, H, 1, DH), lambda b, c: (b, 0, 0, 0)),
                  pl.BlockSpec((1, H, 1, LANES), lambda b, c: (b, 0, 0, 0))],
        out_specs=[pl.BlockSpec((Lb, W), lambda b, c: (b * nc + c, 0)),
                   pl.BlockSpec((1, H, DH, DH), lambda b, c: (b, 0, 0, 0)),
                   pl.BlockSpec((1, H, 1, DH), lambda b, c: (b, 0, 0, 0)),
                   pl.BlockSpec((1, H, 1, LANES), lambda b, c: (b, 0, 0, 0))],
        scratch_shapes=[pltpu.VMEM((H, DH, DH), F32), pltpu.VMEM((H, 1, DH), F32),
                        pltpu.VMEM((H, 1, LANES), F32)],
        compiler_params=_cparams(("parallel", "arbitrary")),
        name="mlstm",
    )(gate_b, z, z, z, z, z, mnorm_g.reshape(1, W), C0, n0, m0)
    return y, C, n.reshape(B, H, DH), m[:, :, 0, 0]


PAGES_PER_STEP = 16


def _lane_extract_topn(score, n_sel, floor):
    lane = lax.broadcasted_iota(jnp.int32, score.shape, 1).astype(F32)
    sel = jnp.zeros(score.shape, F32)
    sc = score
    for _ in range(n_sel):
        m = jnp.max(sc, -1, keepdims=True)
        idx = jnp.min(jnp.where(sc == m, lane, float(score.shape[1])), -1, keepdims=True)
        hit = lane == idx
        sel = jnp.where(hit & (m > floor), 1.0, sel)
        sc = jnp.where(hit, -jnp.inf, sc)
    return sel


def _s1_kernel(pt_ref, *refs, past, S, ncp, nbp):
    pgs = refs[:PAGES_PER_STEP]
    tail_ref, q_ref, ocmp_ref, sel_ref, sub_ref = refs[PAGES_PER_STEP:]
    P = pgs[0].shape[-1]
    j = pl.program_id(1)
    nfull = pl.num_programs(1) - 1
    KVH, G, HD = NSA_KV_HEADS, NSA_GROUP, HEAD_DIM
    cols_step = PAGES_PER_STEP * P // CMP_STRIDE
    assert cols_step == LANES

    @pl.when(j == 0)
    def _():
        sub_ref[...] = jnp.zeros(sub_ref.shape, F32)

    def pool_t(n):
        return jnp.where(_div(lax.broadcasted_iota(jnp.int32, (n, LANES), 0), CMP_STRIDE)
                         == lax.broadcasted_iota(jnp.int32, (n, LANES), 1),
                         1.0 / CMP_STRIDE, 0.0).astype(BF16)

    @pl.when(j < nfull)
    def _():
        pt = pool_t(PAGES_PER_STEP * P)
        col = pl.ds(pl.multiple_of(j * cols_step, cols_step), cols_step)
        for k in range(2):
            for h in range(KVH):
                xt = jnp.concatenate([r[0, 0, k, h] for r in pgs], axis=1)
                sub_ref[k, h, :, col] = _pool_dot(pt, xt, True)

    @pl.when(j == nfull)
    def _():
        base = past // CMP_STRIDE
        pt = pool_t(P)
        for k in range(2):
            for h in range(KVH):
                sub_ref[k, h, :, base:base + LANES] = _pool_dot(pt, tail_ref[0, k, h], True)
        R = G * S
        row = lax.broadcasted_iota(jnp.int32, (R, 1), 0)
        pos = past + (row & (S - 1))
        posq = past + lax.broadcasted_iota(jnp.int32, (S, 1), 0)
        c_end = lax.broadcasted_iota(jnp.int32, (1, ncp), 1) * CMP_STRIDE + (2 * CMP_STRIDE - 1)
        cmask = c_end <= pos
        per = SEL_BLOCK // CMP_STRIDE
        e4 = jnp.where(_div(lax.broadcasted_iota(jnp.int32, (ncp, nbp), 0), per)
                       == lax.broadcasted_iota(jnp.int32, (ncp, nbp), 1), 1.0, 0.0).astype(F32)
        blk = lax.broadcasted_iota(jnp.int32, (1, nbp), 1)
        qblk = _div(posq, SEL_BLOCK)
        valid = blk * SEL_BLOCK <= posq
        forced = (blk == 0) | (blk == qblk) | (blk == qblk - 1)
        for h in range(KVH):
            kct = 0.5 * (sub_ref[0, h, :, 0:ncp] + sub_ref[0, h, :, 1:ncp + 1])
            vct = 0.5 * (sub_ref[1, h, :, 0:ncp] + sub_ref[1, h, :, 1:ncp + 1])
            qs = q_ref[0, h] * jnp.asarray(HD ** -0.5, BF16)
            s = jnp.dot(qs, kct.astype(BF16), preferred_element_type=F32)
            p = _masked_softmax(s, cmask)
            ocmp_ref[0, h] = lax.dot_general(p.astype(BF16), vct.astype(BF16), NT, preferred_element_type=F32)
            imp = p[0:S]
            for g in range(1, G):
                imp = imp + p[g * S:(g + 1) * S]
            impb = jnp.dot(imp, e4, precision=HI, preferred_element_type=F32)
            score = jnp.where(forced, MASK_BIG, jnp.where(valid, impb, -MASK_BIG))
            sel_ref[0, h] = _lane_extract_topn(score, SEL_TOPN, -0.5 * MASK_BIG)


def _s2_kernel(pt_ref, *refs, past, S, nbp):
    pgs = refs[:PAGES_PER_STEP]
    tail_ref, q_ref, sel_ref, osel_ref, m_s, l_s, acc_s = refs[PAGES_PER_STEP:]
    P = pgs[0].shape[-1]
    j = pl.program_id(1)
    nfull = pl.num_programs(1) - 1
    KVH, G, HD = NSA_KV_HEADS, NSA_GROUP, HEAD_DIM
    R = G * S

    @pl.when(j == 0)
    def _():
        m_s[...] = jnp.full(m_s.shape, NEG, F32)
        l_s[...] = jnp.zeros(l_s.shape, F32)
        acc_s[...] = jnp.zeros(acc_s.shape, F32)

    row = lax.broadcasted_iota(jnp.int32, (R, 1), 0)
    pos = past + (row & (S - 1))

    def process(kt_of, vt_of, n, kpos0):
        kpos = kpos0 + lax.broadcasted_iota(jnp.int32, (1, n), 1)
        nblk = LANES
        assert n // SEL_BLOCK <= nblk
        blk0 = kpos0 // SEL_BLOCK
        pick = jnp.where(lax.broadcasted_iota(jnp.int32, (nbp, nblk), 0)
                         == blk0 + lax.broadcasted_iota(jnp.int32, (nbp, nblk), 1), 1.0, 0.0).astype(BF16)
        spread = jnp.where(_div(lax.broadcasted_iota(jnp.int32, (nblk, n), 1), SEL_BLOCK)
                           == lax.broadcasted_iota(jnp.int32, (nblk, n), 0), 1.0, 0.0).astype(BF16)
        causal = kpos <= pos
        for h in range(KVH):
            kt = kt_of(h).astype(BF16)
            vt = vt_of(h).astype(BF16)
            qs = q_ref[0, h] * jnp.asarray(HD ** -0.5, BF16)
            s = jnp.dot(qs, kt, preferred_element_type=F32)
            mkb = jnp.dot(sel_ref[0, h].astype(BF16), pick, preferred_element_type=F32)
            mk = jnp.dot(mkb.astype(BF16), spread, preferred_element_type=F32)
            mask = (jnp.concatenate([mk] * G, axis=0) > 0.5) & causal
            sm = jnp.where(mask, s, NEG)
            m_old = m_s[h]
            m_new = jnp.maximum(m_old, jnp.max(sm, -1, keepdims=True))
            alpha = jnp.exp(m_old - m_new)
            e = jnp.where(mask, jnp.exp(sm - m_new), 0.0)
            l_s[h] = alpha * l_s[h] + jnp.sum(e, -1, keepdims=True)
            acc_s[h] = alpha * acc_s[h] + lax.dot_general(e.astype(BF16), vt, NT, preferred_element_type=F32)
            m_s[h] = m_new

    @pl.when(j < nfull)
    def _():
        process(lambda h: jnp.concatenate([r[0, 0, 0, h] for r in pgs], axis=1),
                lambda h: jnp.concatenate([r[0, 0, 1, h] for r in pgs], axis=1),
                PAGES_PER_STEP * P, j * (PAGES_PER_STEP * P))

    @pl.when(j == nfull)
    def _():
        process(lambda h: tail_ref[0, 0, h], lambda h: tail_ref[0, 1, h], P, past)
        for h in range(KVH):
            osel_ref[0, h] = acc_s[h] * (1.0 / jnp.maximum(l_s[h], 1e-30))


def _page_specs(layer, half, n_pages, P):
    def spec(r):
        return pl.BlockSpec(
            (1, 1, 2, NSA_KV_HEADS, HEAD_DIM, P),
            lambda b, j, pt, r=r: (layer, pt[b, jnp.minimum(j * PAGES_PER_STEP + r, n_pages - 1)],
                                   half, 0, 0, 0))
    return [spec(r) for r in range(PAGES_PER_STEP)]


def _nsa_sample_global(cache_t, layer, page_table, tail_t, q_s, S):
    B, n_pages = page_table.shape
    P = cache_t.shape[-1]
    KVH, G, HD = NSA_KV_HEADS, NSA_GROUP, HEAD_DIM
    past = n_pages * P
    assert n_pages % PAGES_PER_STEP == 0 and S & (S - 1) == 0 and S <= SEL_BLOCK
    nsteps = n_pages // PAGES_PER_STEP + 1
    ncp = -(-(past + P) // CMP_STRIDE // LANES) * LANES
    nbp = -(-(past + P) // SEL_BLOCK // LANES) * LANES
    R = G * S
    qspec = pl.BlockSpec((1, KVH, R, HD), lambda b, j, pt: (b, 0, 0, 0))
    o_cmp, sel = pl.pallas_call(
        functools.partial(_s1_kernel, past=past, S=S, ncp=ncp, nbp=nbp),
        out_shape=[jax.ShapeDtypeStruct((B, KVH, R, HD), F32), jax.ShapeDtypeStruct((B, KVH, S, nbp), F32)],
        grid_spec=pltpu.PrefetchScalarGridSpec(
            num_scalar_prefetch=1, grid=(B, nsteps),
            in_specs=_page_specs(layer, 0, n_pages, P)
            + [pl.BlockSpec((1, 2, KVH, HD, P), lambda b, j, pt: (b, 0, 0, 0, 0)), qspec],
            out_specs=[pl.BlockSpec((1, KVH, R, HD), lambda b, j, pt: (b, 0, 0, 0)),
                       pl.BlockSpec((1, KVH, S, nbp), lambda b, j, pt: (b, 0, 0, 0))],
            scratch_shapes=[pltpu.VMEM((2, KVH, HD, ncp + LANES), F32)]),
        compiler_params=_cparams(("parallel", "arbitrary")),
        name="nsa_sample_cmp",
    )(page_table, *([cache_t] * PAGES_PER_STEP), tail_t, q_s)
    o_sel = pl.pallas_call(
        functools.partial(_s2_kernel, past=past, S=S, nbp=nbp),
        out_shape=jax.ShapeDtypeStruct((B, KVH, R, HD), F32),
        grid_spec=pltpu.PrefetchScalarGridSpec(
            num_scalar_prefetch=1, grid=(B, nsteps),
            in_specs=_page_specs(layer, 1, n_pages, P)
            + [pl.BlockSpec((1, 2, KVH, HD, P), lambda b, j, pt: (b, 1, 0, 0, 0)), qspec,
               pl.BlockSpec((1, KVH, S, nbp), lambda b, j, pt: (b, 0, 0, 0))],
            out_specs=pl.BlockSpec((1, KVH, R, HD), lambda b, j, pt: (b, 0, 0, 0)),
            scratch_shapes=[pltpu.VMEM((KVH, R, 1), F32), pltpu.VMEM((KVH, R, 1), F32),
                            pltpu.VMEM((KVH, R, HD), F32)]),
        compiler_params=_cparams(("parallel", "arbitrary")),
        name="nsa_sample_sel",
    )(page_table, *([cache_t] * PAGES_PER_STEP), tail_t, q_s, sel)
    return o_cmp, o_sel


def _s3_kernel(q_ref, kw_ref, vw_ref, ocmp_ref, osel_ref, gt_ref, y_ref, *, past, S):
    KVH, G, HD = NSA_KV_HEADS, NSA_GROUP, HEAD_DIM
    R = G * S
    nk = kw_ref.shape[1]
    row = lax.broadcasted_iota(jnp.int32, (R, 1), 0)
    pos = past + (row & (S - 1))
    kpos = past - (nk - S) + lax.broadcasted_iota(jnp.int32, (1, nk), 1)
    diff = pos - kpos
    wmask = (diff >= 0) & (diff < WINDOW)
    sg = jax.nn.sigmoid(gt_ref[...])
    kw = kw_ref[0].astype(BF16)
    vw = vw_ref[0].astype(BF16)
    outs = []
    for h in range(KVH):
        qs = q_ref[0, h] * jnp.asarray(HD ** -0.5, BF16)
        s = lax.dot_general(qs, kw[:, h * HD:(h + 1) * HD], NT, preferred_element_type=F32)
        p = _masked_softmax(s, wmask)
        o_win = jnp.dot(p.astype(BF16), vw[:, h * HD:(h + 1) * HD], preferred_element_type=F32)
        oc = ocmp_ref[0, h]
        osl = osel_ref[0, h]
        for g in range(G):
            c0 = (h * G + g) * 3
            rs = slice(g * S, (g + 1) * S)
            outs.append(sg[:, c0:c0 + 1] * oc[rs] + sg[:, c0 + 1:c0 + 2] * osl[rs]
                        + sg[:, c0 + 2:c0 + 3] * o_win[rs])
    y_ref[...] = jnp.concatenate(outs, axis=-1)


def _nsa_sample_combine(q_s, kwin, vwin, o_cmp, o_sel, z, row0, gate_cb, past, S):
    B, KVH, R, HD = q_s.shape
    nk = kwin.shape[1]
    W = KVH * HD
    ospec = pl.BlockSpec((1, KVH, R, HD), lambda b: (b, 0, 0, 0))
    return pl.pallas_call(
        functools.partial(_s3_kernel, past=past, S=S),
        out_shape=jax.ShapeDtypeStruct((B * S, KVH * NSA_GROUP * HD), F32),
        grid=(B,),
        in_specs=[ospec,
                  pl.BlockSpec((1, nk, W), lambda b: (b, 0, 0)),
                  pl.BlockSpec((1, nk, W), lambda b: (b, 0, 0)),
                  ospec, ospec,
                  pl.BlockSpec((S, LANES), lambda b: (row0 // S + b, gate_cb))],
        out_specs=pl.BlockSpec((S, KVH * NSA_GROUP * HD), lambda b: (b, 0)),
        compiler_params=_cparams(("parallel",)),
        name="nsa_sample_win",
    )(q_s, kwin, vwin, o_cmp, o_sel, z)


def _out_proj_kernel(x_ref, ya_ref, yb_ref, yc_ref, yt_ref, w_ref, o_ref, *, n_head_tiles):
    i = pl.program_id(0)
    ka, kb = ya_ref.shape[1], yb_ref.shape[1]

    @pl.when(i < n_head_tiles)
    def _():
        acc = jnp.dot(ya_ref[...].astype(BF16), w_ref[0:ka, :], preferred_element_type=F32)
        acc = acc + jnp.dot(yb_ref[...].astype(BF16), w_ref[ka:ka + kb, :], preferred_element_type=F32)
        acc = acc + jnp.dot(yc_ref[...].astype(BF16), w_ref[ka + kb:, :], preferred_element_type=F32)
        o_ref[...] = x_ref[...] + acc

    @pl.when(i >= n_head_tiles)
    def _():
        o_ref[...] = x_ref[...] + jnp.dot(yt_ref[...].astype(BF16), w_ref[...], preferred_element_type=F32)


def _out_proj(x, ya, yb, yc, ytail, w, tm):
    T, D = x.shape
    TH = ya.shape[0]
    assert TH % tm == 0 and (T - TH) % tm == 0 and ytail.shape == (T - TH, D)
    nh = TH // tm

    def head(a):
        return pl.BlockSpec((tm, a.shape[1]), lambda i: (jnp.minimum(i, nh - 1), 0))

    return pl.pallas_call(
        functools.partial(_out_proj_kernel, n_head_tiles=nh),
        out_shape=jax.ShapeDtypeStruct((T, D), F32),
        grid=(T // tm,),
        in_specs=[pl.BlockSpec((tm, D), lambda i: (i, 0)),
                  head(ya), head(yb), head(yc),
                  pl.BlockSpec((tm, D), lambda i: (jnp.maximum(i - nh, 0), 0)),
                  pl.BlockSpec((D, D), lambda i: (0, 0))],
        out_specs=pl.BlockSpec((tm, D), lambda i: (i, 0)),
        compiler_params=_cparams(("parallel",)),
        name="out_proj",
    )(x, ya, yb, yc, ytail, w)


def _structural_pairs(k):
    return [(i, j) for i in range(k) for j in range(k) if (i + 1) * (j + 1) <= k]


RANK_MARK = 2.0 ** 126


def _extract_topk(cur_ref, top_ref, c, exact):
    NK, K = PEER_NKEYS, PEER_TOPK
    for k in range(K):
        cur = cur_ref[c]
        m = jnp.max(cur, axis=0)
        if exact:
            key = lax.broadcasted_iota(jnp.int32, cur.shape, 0).astype(F32)
            idx = jnp.min(jnp.where(cur == m[None], key, float(NK)), axis=0)
            hit = key == idx[None]
        else:
            hit = cur == m[None]
        cur_ref[c] = jnp.where(hit, -RANK_MARK * (1.0 + k / 32.0), cur)
        top_ref[c, k] = m


def _router_kernel(q_ref, wb_ref, n_out, e1_out, r2_out, e2_out, s_ref, cur_ref, top_ref, tmp_ref):
    H, NK, K = PEER_HEADS, PEER_NKEYS, PEER_TOPK
    Tt = q_ref.shape[0]
    half = q_ref.shape[1] // 2
    for c in range(2):
        s = lax.dot_general(wb_ref[c], q_ref[:, c * half:(c + 1) * half], NT, preferred_element_type=F32)
        s_ref[c] = s.reshape(NK, H, Tt)
        cur_ref[c] = s_ref[c]
        _extract_topk(cur_ref, top_ref, c, exact=False)

    marked = jnp.sum(jnp.where(cur_ref[...] < -0.5 * RANK_MARK, 1.0, 0.0), axis=1)
    tied = jnp.max(jnp.where(marked != float(K), 1.0, 0.0)) > 0.0

    @pl.when(tied)
    def _():
        for c in range(2):
            cur_ref[c] = s_ref[c]
            _extract_topk(cur_ref, top_ref, c, exact=True)

    v1 = [top_ref[0, k] for k in range(K)]
    v2 = [top_ref[1, k] for k in range(K)]

    pairs = _structural_pairs(K)
    cand = [v1[i] + v2[j] for (i, j) in pairs]
    n = len(pairs)
    rank = []
    for p in range(n):
        rank.append(jnp.zeros((H, Tt), F32))
    for p in range(n):
        ip, jp = pairs[p]
        for q in range(p + 1, n):
            iq, jq = pairs[q]
            if ip <= iq and jp <= jq:
                rank[q] = rank[q] + 1.0
            else:
                b = jnp.where(cand[p] >= cand[q], 1.0, 0.0)
                rank[q] = rank[q] + b
                rank[p] = rank[p] + (1.0 - b)
    sel = [jnp.where(r < K, 1.0, 0.0) for r in rank]
    e1 = [jnp.exp(v1[i] - v1[0]) for i in range(K)]
    e2 = [jnp.exp(v2[j] - v2[0]) for j in range(K)]
    cnt = [jnp.zeros((H, Tt), F32) for _ in range(K)]
    zsum = jnp.zeros((H, Tt), F32)
    for p, (i, j) in enumerate(pairs):
        cnt[i] = cnt[i] + sel[p]
        zsum = zsum + sel[p] * (e1[i] * e2[j])
    inv_z = 1.0 / zsum

    def rank_of(c):
        cur = cur_ref[c]
        return jnp.where(cur < -0.5 * RANK_MARK, (cur * (-1.0 / RANK_MARK) - 1.0) * 32.0, float(NK))

    rk1 = rank_of(0)
    nk = jnp.zeros((NK, H, Tt), F32)
    for i in range(K):
        nk = jnp.where(rk1 == float(i), cnt[i][None], nk)
    outs = ((n_out, nk),
            (e1_out, jnp.exp(s_ref[0] - v1[0][None]) * inv_z[None]),
            (r2_out, rank_of(1)),
            (e2_out, jnp.exp(s_ref[1] - v2[0][None])))
    for ref, val in outs:
        tmp_ref[...] = val.reshape(NK * H, Tt)
        for h in range(H):
            ref[h] = tmp_ref[pl.ds(h, NK, stride=H), :].astype(ref.dtype)


def _router(q, wb, tt):
    T = q.shape[0]
    H, NK = PEER_HEADS, PEER_NKEYS
    shp = jax.ShapeDtypeStruct((H, NK, T), F32)
    shp_b = jax.ShapeDtypeStruct((H, NK, T), BF16)
    ospec = pl.BlockSpec((H, NK, tt), lambda i: (0, 0, i))
    return pl.pallas_call(
        _router_kernel,
        out_shape=[shp, shp, shp_b, shp_b],
        grid=(T // tt,),
        in_specs=[pl.BlockSpec((tt, q.shape[1]), lambda i: (i, 0)),
                  pl.BlockSpec(wb.shape, lambda i: (0, 0, 0))],
        out_specs=[ospec, ospec, ospec, ospec],
        scratch_shapes=[pltpu.VMEM((2, NK, H, tt), F32), pltpu.VMEM((2, NK, H, tt), F32),
                        pltpu.VMEM((2, PEER_TOPK, H, tt), F32), pltpu.VMEM((NK * H, tt), F32)],
        compiler_params=_cparams(("parallel",)),
        name="peer_router",
    )(q, wb)


def _gelu_tanh(x):
    return 0.5 * x * (1.0 + jnp.tanh(np.sqrt(2.0 / np.pi) * (x + 0.044715 * (x * x * x))))


PEER_ACHUNK = 2


def _peer_dense_kernel(xn_ref, u_ref, v_ref, n_ref, e1_ref, r2_ref, e2_ref, x1_ref, g_ref, o_ref, *, final_norm):
    H, NK = PEER_HEADS, PEER_NKEYS
    j = pl.program_id(1)
    na = u_ref.shape[0] // NK
    tt = xn_ref.shape[0]

    @pl.when(j == 0)
    def _():
        o_ref[...] = x1_ref[...]

    xn = xn_ref[...]
    parts = []
    for c in range(na // PEER_ACHUNK):
        rows = slice(c * PEER_ACHUNK * NK, (c + 1) * PEER_ACHUNK * NK)
        act = lax.dot_general(u_ref[rows, :], xn, NT, preferred_element_type=F32)
        gates = []
        for a in range(c * PEER_ACHUNK, (c + 1) * PEER_ACHUNK):
            row = pl.ds(j * na + a, 1)
            gate = None
            for h in range(H):
                t = jnp.where(r2_ref[h] < n_ref[h, row, :].astype(BF16),
                              e1_ref[h, row, :].astype(BF16) * e2_ref[h], jnp.zeros((), BF16))
                gate = t if gate is None else gate + t
            gates.append(gate)
        parts.append(_gelu_tanh(act).astype(BF16) * jnp.concatenate(gates, axis=0))
    wt = jnp.concatenate(parts, axis=0)
    o_ref[...] += lax.dot_general(wt, v_ref[...], TN, preferred_element_type=F32)

    if final_norm:
        @pl.when(j == pl.num_programs(1) - 1)
        def _():
            x = o_ref[...]
            o_ref[...] = x * lax.rsqrt(jnp.mean(x * x, -1, keepdims=True) + RMS_EPS) * g_ref[...]


def _peer_dense(xn, u, v, layer, nk, e1, r2, e2, x1, g, tt, te, final_norm):
    T, D = xn.shape
    E = u.shape[1]
    H, NK = PEER_HEADS, PEER_NKEYS
    once = pl.Buffered(1)
    tab = pl.BlockSpec((H, NK, tt), lambda i, j: (0, 0, i), pipeline_mode=once)
    return pl.pallas_call(
        functools.partial(_peer_dense_kernel, final_norm=final_norm),
        out_shape=jax.ShapeDtypeStruct((T, D), F32),
        grid=(T // tt, E // te),
        in_specs=[pl.BlockSpec((tt, D), lambda i, j: (i, 0), pipeline_mode=once),
                  pl.BlockSpec((None, te, D), lambda i, j: (layer, j, 0)),
                  pl.BlockSpec((None, te, D), lambda i, j: (layer, j, 0)),
                  tab, tab, tab, tab,
                  pl.BlockSpec((tt, D), lambda i, j: (i, 0), pipeline_mode=once),
                  pl.BlockSpec((1, D), lambda i, j: (0, 0))],
        out_specs=pl.BlockSpec((tt, D), lambda i, j: (i, 0)),
        compiler_params=_cparams(("parallel", "arbitrary")),
        name="peer_dense",
    )(xn, u, v, nk, e1, r2, e2, x1, g.reshape(1, D))


T_TILE = 384
ROUTER_TILE = 128
PEER_T_TILE = 384
PAD_TILE = 768
Z_COL_TILE = 1280
OUT_TILE = 256
EXPERT_TILE = 1024


def kernel(x_prompt, x_sample, cache_nsa_kv, state_win_kv, state_conv, state_mlstm_C, state_mlstm_n,
           state_mlstm_m, page_table, norm1_g, w_in, conv_w, mlstm_gate_b, mlstm_norm_g, w_out, norm2_g,
           peer_wq, peer_subkeys, peer_u, peer_v, final_norm_g):
    BP, SP, D = x_prompt.shape
    BS, SS, _ = x_sample.shape
    depth = w_in.shape[0]
    KVH, G, HD = NSA_KV_HEADS, NSA_GROUP, HEAD_DIM
    H, DH = MLSTM_HEADS, MLSTM_DH
    CW = conv_w.shape[2]
    NW = KVH * G * HD
    KVW = 6 * KVH * HD
    NG = 3 * KVH * G
    MW = H * DH
    TP, TS = BP * SP, BS * SS
    T = TP + TS
    assert PAD_TILE % T_TILE == 0 and PAD_TILE % ROUTER_TILE == 0 and PAD_TILE % PEER_T_TILE == 0
    TPAD = -(-T // PAD_TILE) * PAD_TILE
    n_pages = page_table.shape[1]
    P = cache_nsa_kv.shape[2]
    past = n_pages * P
    wb_rows = state_win_kv.shape[2]

    c_q = 3 * CW
    c_kv = c_q + NW
    c_m = c_kv + KVW
    c_g = c_m + 4 * MW
    ZW = -(-(c_g + LANES) // Z_COL_TILE) * Z_COL_TILE
    assert CW % LANES == 0 and c_m % MW == 0 and c_g % LANES == 0 and NG + 2 * H <= LANES
    gate_cb = c_g // LANES
    gi, gf = NG, NG + H
    o_gate = 3 * CW + NW + KVW

    x = jnp.concatenate([x_prompt.reshape(TP, D), x_sample.reshape(TS, D),
                         jnp.zeros((TPAD - T, D), F32)], axis=0)
    cache_t = jnp.transpose(cache_nsa_kv, (0, 1, 3, 4, 5, 2))
    eye_h = jnp.eye(PEER_HEADS, dtype=F32)
    u_bf = peer_u.astype(BF16)
    v_bf = peer_v.astype(BF16)
    dk = peer_subkeys.shape[-1]

    p_st, s_st = [], []
    for l in range(depth):
        wi = w_in[l]
        w_perm = jnp.concatenate(
            [wi[:, :o_gate], wi[:, o_gate + NG:o_gate + NG + 4 * MW], wi[:, o_gate:o_gate + NG],
             wi[:, o_gate + NG + 4 * MW:], jnp.zeros((D, ZW - wi.shape[1]), F32)], axis=1).astype(BF16)
        z = _rms_proj(x, norm1_g[l], w_perm, PAD_TILE, Z_COL_TILE, F32)

        ya_p, conv_p = _conv(z, 0, BP, SP, jnp.zeros((BP, 2, CW), F32), conv_w[l])
        ya_s, conv_s = _conv(z, TP, BS, SS, state_conv[l], conv_w[l])

        yb_p, rows_t, win_t = _nsa_prompt(z, c_q, c_kv, gate_cb, BP, SP)
        rows_p = rows_t.transpose(0, 4, 1, 2, 3)
        win_p = win_t.transpose(0, 4, 1, 2, 3)

        zs = z[TP:T]
        q_s = zs[:, c_q:c_q + NW].reshape(BS, SS, KVH, G, HD).transpose(0, 2, 3, 1, 4)
        q_s = q_s.reshape(BS, KVH, G * SS, HD).astype(BF16)
        zkv_s = zs[:, c_kv:c_kv + KVW].reshape(BS, SS, 3, 2, KVH, HD)
        rows_s = zkv_s[:, :, :2].reshape(BS, SS, 4, KVH, HD)
        tail_t = jnp.pad(rows_s.transpose(0, 2, 3, 4, 1), ((0, 0), (0, 0), (0, 0), (0, 0), (0, P - SS)))
        o_cmp, o_sel = _nsa_sample_global(cache_t, l, page_table, tail_t, q_s, SS)
        w_all = jnp.concatenate([state_win_kv[l], zkv_s[:, :, 2]], axis=1)
        kwin = w_all[:, :, 0].reshape(BS, wb_rows + SS, KVH * HD)
        vwin = w_all[:, :, 1].reshape(BS, wb_rows + SS, KVH * HD)
        yb_s = _nsa_sample_combine(q_s, kwin, vwin, o_cmp, o_sel, z, TP, gate_cb, past, SS)
        win_s = w_all[:, SS:]

        yc_p, C_p, n_p, m_p = _mlstm(z, 0, BP, SP, c_m // MW, gate_cb, gi, gf, mlstm_gate_b[l],
                                     mlstm_norm_g[l], jnp.zeros((BP, H, DH, DH), F32),
                                     jnp.zeros((BP, H, DH), F32), jnp.zeros((BP, H), F32))
        yc_s, C_s, n_s, m_s = _mlstm(z, TP, BS, SS, c_m // MW, gate_cb, gi, gf, mlstm_gate_b[l],
                                     mlstm_norm_g[l], state_mlstm_C[l], state_mlstm_n[l], state_mlstm_m[l])

        ytail = jnp.concatenate([jnp.concatenate([ya_s, yb_s, yc_s], axis=1),
                                 jnp.zeros((TPAD - T, D), F32)], axis=0)
        x1 = _out_proj(x, ya_p, yb_p, yc_p, ytail, w_out[l].astype(BF16), OUT_TILE)

        nh = PEER_HEADS
        wq_perm = peer_wq[l].reshape(D, nh, 2, dk).transpose(0, 2, 1, 3).reshape(D, 2 * nh * dk).astype(BF16)
        q_peer, xn2 = _rms_proj(x1, norm2_g[l], wq_perm, PAD_TILE, nh * dk, BF16, emit_xn=True)
        wb = jnp.einsum('hcnd,hg->cnhgd', peer_subkeys[l], eye_h).reshape(2, PEER_NKEYS * nh, nh * dk).astype(BF16)
        nk, e1, r2, e2 = _router(q_peer, wb, ROUTER_TILE)
        x = _peer_dense(xn2, u_bf, v_bf, l, nk, e1, r2, e2, x1,
                        final_norm_g, PEER_T_TILE, EXPERT_TILE, final_norm=(l == depth - 1))

        p_st.append((rows_p, win_p, conv_p, C_p, n_p, m_p))
        s_st.append((rows_s, win_s, conv_s, C_s, n_s, m_s))

    p_rows, p_win, p_conv, p_C, p_n, p_m = [jnp.stack(a) for a in zip(*p_st)]
    s_rows, s_win, s_conv, s_C, s_n, s_m = [jnp.stack(a) for a in zip(*s_st)]
    y_prompt = x[:TP].reshape(BP, SP, D)
    y_sample = x[TP:T].reshape(BS, SS, D)
    return (y_prompt, y_sample, p_rows, p_win, p_conv, p_C, p_n, p_m, s_rows, s_win, s_conv, s_C, s_n, s_m)
```

```python
import functools

import numpy as np
import jax
import jax.numpy as jnp
from jax import lax
from jax.experimental import pallas as pl
from jax.experimental.pallas import tpu as pltpu

F32 = jnp.float32
BF16 = jnp.bfloat16
HI = lax.Precision.HIGHEST

RMS_EPS = 1e-6
HEAD_DIM = 64
NSA_KV_HEADS = 4
NSA_GROUP = 4
CMP_STRIDE = 16
SEL_BLOCK = 64
SEL_TOPN = 16
WINDOW = 512
MLSTM_HEADS = 4
MLSTM_DH = 128
MLSTM_CHUNK = 64
PEER_HEADS = 8
PEER_NKEYS = 128
PEER_TOPK = 16
MASK_BIG = 1e9
NEG = -1e30

LANES = 128
VMEM_LIMIT = 56 * 1024 * 1024

NT = (((1,), (1,)), ((), ()))
TN = (((0,), (0,)), ((), ()))


def _cparams(sem):
    return pltpu.CompilerParams(dimension_semantics=sem, vmem_limit_bytes=VMEM_LIMIT)


def _div(x, d):
    assert d & (d - 1) == 0
    return lax.shift_right_arithmetic(x, jnp.int32(d.bit_length() - 1))


def _pool_dot(pool, x, x_is_lhs):
    hi = x.astype(BF16)
    lo = (x - hi.astype(F32)).astype(BF16)
    if x_is_lhs:
        return (jnp.dot(hi, pool, preferred_element_type=F32) + jnp.dot(lo, pool, preferred_element_type=F32))
    return (jnp.dot(pool, hi, preferred_element_type=F32) + jnp.dot(pool, lo, preferred_element_type=F32))


def _masked_softmax(s, mask, exp=jnp.exp):
    s = jnp.where(mask, s, NEG)
    m = jnp.max(s, -1, keepdims=True)
    e = jnp.where(mask, exp(s - m), 0.0)
    d = jnp.maximum(jnp.sum(e, -1, keepdims=True), 1e-30)
    return e * (1.0 / d)


def _rms_proj_kernel(x_ref, g_ref, w_ref, *rest, emit_xn):
    if emit_xn:
        o_ref, xo_ref, xn_ref = rest
    else:
        o_ref, xn_ref = rest

    @pl.when(pl.program_id(1) == 0)
    def _():
        x = x_ref[...]
        r = x * lax.rsqrt(jnp.mean(x * x, -1, keepdims=True) + RMS_EPS)
        xn = (r * g_ref[...]).astype(BF16)
        xn_ref[...] = xn
        if emit_xn:
            xo_ref[...] = xn

    o_ref[...] = jnp.dot(xn_ref[...], w_ref[...], preferred_element_type=F32).astype(o_ref.dtype)


def _rms_proj(x, g, w, tm, tn, out_dtype, emit_xn=False):
    T, D = x.shape
    N = w.shape[1]
    out_shape = [jax.ShapeDtypeStruct((T, N), out_dtype)]
    out_specs = [pl.BlockSpec((tm, tn), lambda i, j: (i, j))]
    if emit_xn:
        out_shape.append(jax.ShapeDtypeStruct((T, D), BF16))
        out_specs.append(pl.BlockSpec((tm, D), lambda i, j: (i, 0)))
    res = pl.pallas_call(
        functools.partial(_rms_proj_kernel, emit_xn=emit_xn),
        out_shape=out_shape,
        grid=(T // tm, N // tn),
        in_specs=[pl.BlockSpec((tm, D), lambda i, j: (i, 0)),
                  pl.BlockSpec((1, D), lambda i, j: (0, 0)),
                  pl.BlockSpec((D, tn), lambda i, j: (0, j))],
        out_specs=out_specs,
        scratch_shapes=[pltpu.VMEM((tm, D), BF16)],
        compiler_params=_cparams(("parallel", "arbitrary")),
        name="rms_proj",
    )(x, g.reshape(1, D), w)
    return res if emit_xn else res[0]


def _conv_kernel(cb_ref, cc_ref, ch_ref, buf_ref, w_ref, y_ref, new_ref, ext_ref, *, S):
    C = cb_ref.shape[-1]
    u = cc_ref[...] * ch_ref[...]
    ext_ref[0:8, :] = jnp.zeros((8, C), F32)
    ext_ref[6:8, :] = buf_ref[0]
    ext_ref[8:8 + S, :] = u
    w = w_ref[...]
    y = w[0:1] * ext_ref[6:6 + S, :] + w[1:2] * ext_ref[7:7 + S, :] + w[2:3] * u
    y_ref[...] = cb_ref[...] * y
    new_ref[0] = u[S - 2:S]


def _conv(z, row0, B, S, buf, w):
    C = w.shape[1]
    rb0 = row0 // S
    return pl.pallas_call(
        functools.partial(_conv_kernel, S=S),
        out_shape=[jax.ShapeDtypeStruct((B * S, C), F32), jax.ShapeDtypeStruct((B, 2, C), F32)],
        grid=(B,),
        in_specs=[pl.BlockSpec((S, C), lambda b: (rb0 + b, 0)),
                  pl.BlockSpec((S, C), lambda b: (rb0 + b, 1)),
                  pl.BlockSpec((S, C), lambda b: (rb0 + b, 2)),
                  pl.BlockSpec((1, 2, C), lambda b: (b, 0, 0)),
                  pl.BlockSpec((3, C), lambda b: (0, 0))],
        out_specs=[pl.BlockSpec((S, C), lambda b: (b, 0)),
                   pl.BlockSpec((1, 2, C), lambda b: (b, 0, 0))],
        scratch_shapes=[pltpu.VMEM((S + 8, C), F32)],
        compiler_params=_cparams(("parallel",)),
        name="short_conv",
    )(z, z, z, buf, w)


def _topn_rank_select_t(score_t, n_sel):
    NB = score_t.shape[0]
    blk = lax.broadcasted_iota(jnp.int32, score_t.shape, 0)
    rank = jnp.zeros(score_t.shape, F32)
    for i in range(NB):
        ci = score_t[i:i + 1, :]
        beats = jnp.where(ci > score_t, 1.0, jnp.where((ci == score_t) & (blk > i), 1.0, 0.0))
        rank = rank + beats
    return rank < n_sel


SEL_KCHUNK = 512


def _exp2_softmax_pv(pieces, hd):
    m = None
    for s, _ in pieces:
        mm = jnp.max(s, -1, keepdims=True)
        m = mm if m is None else jnp.maximum(m, mm)
    oa = None
    for s, v in pieces:
        t = jnp.dot(jnp.exp2(s - m).astype(BF16), v, preferred_element_type=F32)
        oa = t if oa is None else oa + t
    return oa[:, 0:hd] * (1.0 / oa[:, hd:hd + 1])


NSA_HPS = 2
QK_WIDTH = 128


def _nsa_prompt_kernel(q_ref, kcs_ref, vcs_ref, kss_ref, vss_ref, kws_ref, vws_ref, gt_ref,
                       y_ref, rows_ref, win_ref, kc_ref, vc_ref, ks_ref, vs_ref, kw_ref, vw_ref, osel_s, *, S, tq):
    pair = pl.program_id(1)
    qi = pl.program_id(2)
    G, HD = NSA_GROUP, HEAD_DIM
    NC = S // CMP_STRIDE
    NB = S // SEL_BLOCK
    n_sel = min(SEL_TOPN, NB)
    span = min(WINDOW + tq, S)

    @pl.when(qi == 0)
    def _():
        j = lax.broadcasted_iota(jnp.int32, (NC, S), 0)
        r = lax.broadcasted_iota(jnp.int32, (NC, S), 1)
        lo = j * CMP_STRIDE
        pool = jnp.where((r >= lo) & (r < lo + 2 * CMP_STRIDE), 0.5 / CMP_STRIDE, 0.0).astype(BF16)
        kcp = _pool_dot(pool, kcs_ref[...], False).astype(BF16)
        vcp = _pool_dot(pool, vcs_ref[...], False).astype(BF16)
        blk1h = jnp.where(_div(lax.broadcasted_iota(jnp.int32, (S, NB), 0), SEL_BLOCK)
                          == lax.broadcasted_iota(jnp.int32, (S, NB), 1), 1.0, 0.0).astype(BF16)
        kpad = jnp.zeros((S, QK_WIDTH - HD - NB), BF16)
        ones = jnp.where(lax.broadcasted_iota(jnp.int32, (S, LANES - HD), 1) == 0, 1.0, 0.0).astype(BF16)
        for hh in range(NSA_HPS):
            cols = slice(hh * HD, (hh + 1) * HD)
            kc_ref[hh] = kcp[:, cols]
            vc_ref[hh] = vcp[:, cols]
            ks_ref[hh] = jnp.concatenate([kss_ref[:, cols].astype(BF16), blk1h, kpad], axis=1)
            vs_ref[hh] = jnp.concatenate([vss_ref[:, cols].astype(BF16), ones], axis=1)
            kw_ref[hh] = kws_ref[:, cols].astype(BF16)
            vw_ref[hh] = jnp.concatenate([vws_ref[:, cols].astype(BF16), ones], axis=1)
        for kind, src in enumerate((kcs_ref, vcs_ref, kss_ref, vss_ref)):
            rows_ref[0, kind] = src[...].T.reshape(NSA_HPS, HD, S)
        wn = win_ref.shape[-1]
        for kind, src in enumerate((kws_ref, vws_ref)):
            win_ref[0, kind] = src[S - wn:S, :].T.reshape(NSA_HPS, HD, wn)

    t0 = qi * tq
    row = lax.broadcasted_iota(jnp.int32, (G * tq, 1), 0)
    pos = t0 + (row & (tq - 1))
    posq = t0 + lax.broadcasted_iota(jnp.int32, (tq, 1), 0)
    c_end = lax.broadcasted_iota(jnp.int32, (1, NC), 1) * CMP_STRIDE + (2 * CMP_STRIDE - 1)
    cmask = c_end <= pos
    per = SEL_BLOCK // CMP_STRIDE
    e4t = jnp.where(_div(lax.broadcasted_iota(jnp.int32, (NB, NC), 1), per)
                    == lax.broadcasted_iota(jnp.int32, (NB, NC), 0), 1.0, 0.0).astype(F32)
    eye_nb = jnp.where(lax.broadcasted_iota(jnp.int32, (NB, NB), 0)
                       == lax.broadcasted_iota(jnp.int32, (NB, NB), 1), 1.0, 0.0).astype(BF16)
    blk_t = lax.broadcasted_iota(jnp.int32, (NB, 1), 0)
    posq_t = t0 + lax.broadcasted_iota(jnp.int32, (1, tq), 1)
    qblk_t = _div(posq_t, SEL_BLOCK)
    valid_t = blk_t * SEL_BLOCK <= posq_t
    forced_t = (blk_t == 0) | (blk_t == qblk_t) | (blk_t == qblk_t - 1)

    qscale = (HD ** -0.5) * np.log2(np.e)
    qss, qas, o_cmps = [], [], []
    for hh in range(NSA_HPS):
        qh = q_ref[:, hh * G * HD:(hh + 1) * G * HD] * qscale
        qs = jnp.concatenate([qh[:, g * HD:(g + 1) * HD] for g in range(G)], axis=0).astype(BF16)
        s = lax.dot_general(qs, kc_ref[hh], NT, preferred_element_type=F32)
        p = _masked_softmax(s, cmask, jnp.exp2)
        o_cmps.append(jnp.dot(p.astype(BF16), vc_ref[hh], preferred_element_type=F32))
        imp = p[0:tq]
        for g in range(1, G):
            imp = imp + p[g * tq:(g + 1) * tq]
        impb_t = lax.dot_general(e4t, imp, NT, precision=HI, preferred_element_type=F32)
        score_t = jnp.where(forced_t, MASK_BIG, jnp.where(valid_t, impb_t, -MASK_BIG))
        sel_t = _topn_rank_select_t(score_t, n_sel) & (score_t > -0.5 * MASK_BIG)
        selneg_t = jnp.where(sel_t, 0.0, NEG).astype(BF16)
        selneg = lax.dot_general(selneg_t, eye_nb, TN, preferred_element_type=F32).astype(BF16)
        qas.append(jnp.concatenate([qs, jnp.concatenate([selneg] * G, axis=0),
                                    jnp.zeros((G * tq, QK_WIDTH - HD - NB), BF16)], axis=1))
        qss.append(qs)

    def sel_branch(klen):
        tail = min(klen, SEL_KCHUNK)
        kpos = (klen - tail) + lax.broadcasted_iota(jnp.int32, (1, tail), 1)
        tail_bias = jnp.where(kpos <= posq, 0.0, NEG)
        for hh in range(NSA_HPS):
            for g in range(G):
                qa = qas[hh][g * tq:(g + 1) * tq]
                pieces = []
                if klen > tail:
                    pieces.append((lax.dot_general(qa, ks_ref[hh, 0:klen - tail, :], NT,
                                                   preferred_element_type=F32),
                                   vs_ref[hh, 0:klen - tail, :]))
                pieces.append((lax.dot_general(qa, ks_ref[hh, klen - tail:klen, :], NT,
                                               preferred_element_type=F32) + tail_bias,
                               vs_ref[hh, klen - tail:klen, :]))
                osel_s[hh, g * tq:(g + 1) * tq, :] = _exp2_softmax_pv(pieces, HD)

    nvar = -(-S // SEL_KCHUNK)
    per_var = SEL_KCHUNK // tq
    for v in range(nvar):
        @pl.when(qi // per_var == v)
        def _(v=v):
            sel_branch(min(S, (v + 1) * SEL_KCHUNK))

    start = pl.multiple_of(jnp.maximum(t0 + tq - span, 0), tq)
    diff = posq - (start + lax.broadcasted_iota(jnp.int32, (1, span), 1))
    win_bias = jnp.where((diff >= 0) & (diff < WINDOW), 0.0, NEG)

    sg = jax.nn.sigmoid(gt_ref[...])
    outs = []
    for hh in range(NSA_HPS):
        kvh = pair * NSA_HPS + hh
        pick = jnp.where(lax.broadcasted_iota(jnp.int32, (LANES, LANES), 0)
                         == lax.broadcasted_iota(jnp.int32, (LANES, LANES), 1) + kvh * (3 * G),
                         1.0, 0.0).astype(F32)
        g12 = jnp.dot(sg, pick, precision=HI, preferred_element_type=F32)
        kw = kw_ref[hh, pl.ds(start, span), :]
        vw = vw_ref[hh, pl.ds(start, span), :]
        for g in range(G):
            rows = slice(g * tq, (g + 1) * tq)
            s_win = lax.dot_general(qss[hh][rows], kw, NT, preferred_element_type=F32) + win_bias
            o_win = _exp2_softmax_pv([(s_win, vw)], HD)
            outs.append(g12[:, 3 * g:3 * g + 1] * o_cmps[hh][rows]
                        + g12[:, 3 * g + 1:3 * g + 2] * osel_s[hh, rows, :]
                        + g12[:, 3 * g + 2:3 * g + 3] * o_win)
    y_ref[...] = jnp.concatenate(outs, axis=-1)


def _nsa_prompt(z, c_q, c_kv, gate_cb, B, S, tq=512):
    nq = S // tq
    KVH, G, HD = NSA_KV_HEADS, NSA_GROUP, HEAD_DIM
    NC = S // CMP_STRIDE
    qw = NSA_HPS * G * HD
    assert NSA_HPS * HD == LANES and c_q % qw == 0 and c_kv % LANES == 0

    def kv_spec(i):
        cb = (c_kv + i * KVH * HD) // LANES
        return pl.BlockSpec((S, LANES), lambda b, p, t, cb=cb: (b, cb + p))

    head_kv = pltpu.VMEM((NSA_HPS, S, HD), BF16)
    head_aug = pltpu.VMEM((NSA_HPS, S, LANES), BF16)
    assert HD + S // SEL_BLOCK <= LANES
    wn = min(WINDOW, S)
    return pl.pallas_call(
        functools.partial(_nsa_prompt_kernel, S=S, tq=tq),
        out_shape=[jax.ShapeDtypeStruct((B * S, KVH * G * HD), F32),
                   jax.ShapeDtypeStruct((B, 4, KVH, HD, S), F32),
                   jax.ShapeDtypeStruct((B, 2, KVH, HD, wn), F32)],
        grid=(B, KVH // NSA_HPS, nq),
        in_specs=[pl.BlockSpec((tq, qw), lambda b, p, t: (b * nq + t, c_q // qw + p)),
                  kv_spec(0), kv_spec(1), kv_spec(2), kv_spec(3), kv_spec(4), kv_spec(5),
                  pl.BlockSpec((tq, LANES), lambda b, p, t: (b * nq + t, gate_cb))],
        out_specs=[pl.BlockSpec((tq, qw), lambda b, p, t: (b * nq + t, p)),
                   pl.BlockSpec((1, 4, NSA_HPS, HD, S), lambda b, p, t: (b, 0, p, 0, 0)),
                   pl.BlockSpec((1, 2, NSA_HPS, HD, wn), lambda b, p, t: (b, 0, p, 0, 0))],
        scratch_shapes=[pltpu.VMEM((NSA_HPS, NC, HD), BF16), pltpu.VMEM((NSA_HPS, NC, HD), BF16),
                        pltpu.VMEM((NSA_HPS, S, QK_WIDTH), BF16), head_aug, head_kv, head_aug,
                        pltpu.VMEM((NSA_HPS, G * tq, HD), F32)],
        compiler_params=_cparams(("parallel", "parallel", "arbitrary")),
        name="nsa_prompt",
    )(z, z, z, z, z, z, z, z)


def _log_sigmoid(x):
    return jnp.minimum(x, 0.0) - jnp.log(1.0 + jnp.exp(-jnp.abs(x)))


def _mlstm_kernel(gb_ref, q_ref, k_ref, v_ref, o_ref, gt_ref, mg_ref, c0_ref, n0_ref, m0_ref,
                  y_ref, cn_ref, nn_ref, mn_ref, c_s, n_s, m_s, *, L, Lb, gi, gf):
    H, DH = MLSTM_HEADS, MLSTM_DH
    c = pl.program_id(1)

    @pl.when(c == 0)
    def _():
        c_s[...] = c0_ref[0]
        n_s[...] = n0_ref[0]
        m_s[...] = m0_ref[0]

    def padrows(a):
        if Lb == L:
            return a
        return jnp.concatenate([a, jnp.zeros((L - Lb, a.shape[1]), a.dtype)], axis=0)

    lane = lax.broadcasted_iota(jnp.int32, (1, LANES), 1)
    bias = jnp.zeros((1, LANES), F32)
    for h in range(H):
        bias = bias + jnp.where(lane == gi + h, gb_ref[0, h], 0.0) + jnp.where(lane == gf + h, gb_ref[1, h], 0.0)
    is_f = (lane >= gf) & (lane < gf + H)
    pre = padrows(gt_ref[...]) + bias
    gate = jnp.where(is_f, _log_sigmoid(pre), pre)
    if Lb != L:
        live = lax.broadcasted_iota(jnp.int32, (L, 1), 0) < Lb
        gate = jnp.where(live, gate, jnp.where(is_f, 0.0, NEG))
    rr = lax.broadcasted_iota(jnp.int32, (L, L), 0)
    cc = lax.broadcasted_iota(jnp.int32, (L, L), 1)
    tril = rr >= cc
    bcum = jnp.dot(jnp.where(tril, 1.0, 0.0).astype(F32), gate, precision=HI, preferred_element_type=F32)
    e8 = jnp.where(lax.broadcasted_iota(jnp.int32, (8, LANES), 1)
                   == lax.broadcasted_iota(jnp.int32, (8, LANES), 0) + gi, 1.0, 0.0).astype(F32)
    rg = lax.dot_general(e8, gate, NT, precision=HI, preferred_element_type=F32)
    rb = lax.dot_general(e8, bcum, NT, precision=HI, preferred_element_type=F32)

    q = padrows(q_ref[...])
    k = padrows(k_ref[...])
    v = padrows(v_ref[...])
    og = padrows(o_ref[...])
    ys = []
    for h in range(H):
        sl = slice(h * DH, (h + 1) * DH)
        qq = q[:, sl]
        kk = k[:, sl] * (DH ** -0.5)
        vv = v[:, sl]
        qb, kb, vb = qq.astype(BF16), kk.astype(BF16), vv.astype(BF16)
        b_col = bcum[:, gf + h:gf + h + 1]
        b_row = rb[H + h:H + h + 1, :]
        i_row = rg[h:h + 1, :]
        i_col = gate[:, gi + h:gi + h + 1]
        m_prev = m_s[h][:, 0:1]
        cmat = c_s[h]
        n_row = n_s[h]
        dmat = jnp.where(tril, b_col - b_row + i_row, NEG)
        inter = b_col + m_prev
        mt = jnp.maximum(inter, jnp.max(dmat, -1, keepdims=True))
        w = jnp.exp(dmat - mt)
        a = jnp.exp(inter - mt)
        wqk = w * lax.dot_general(qb, kb, NT, preferred_element_type=F32)
        num = (a * jnp.dot(qb, cmat.astype(BF16), preferred_element_type=F32)
               + jnp.dot(wqk.astype(BF16), vb, preferred_element_type=F32))
        den = a * jnp.sum(qq * n_row, -1, keepdims=True) + jnp.sum(wqk, -1, keepdims=True)
        hh = num * (1.0 / jnp.maximum(jnp.abs(den), jnp.exp(-mt)))
        b_last = b_col[L - 1:L]
        m_new = mt[L - 1:L]
        wl = jnp.exp(b_last - b_col + i_col - m_new)
        decay = jnp.exp(b_last + m_prev - m_new)
        kw = wl * kk
        c_s[h] = decay * cmat + lax.dot_general(kw.astype(BF16), vb, TN, preferred_element_type=F32)
        n_s[h] = decay * n_row + jnp.sum(kw, 0, keepdims=True)
        m_s[h] = jnp.broadcast_to(m_new, (1, LANES))
        hn = hh * lax.rsqrt(jnp.mean(hh * hh, -1, keepdims=True) + RMS_EPS)
        ys.append(jax.nn.sigmoid(og[:, sl]) * hn * mg_ref[:, sl])
    y = jnp.concatenate(ys, axis=-1)
    y_ref[...] = y[0:Lb]

    @pl.when(c == pl.num_programs(1) - 1)
    def _():
        cn_ref[0] = c_s[...]
        nn_ref[0] = n_s[...]
        mn_ref[0] = m_s[...]


def _mlstm(z, row0, B, S, col_q, gate_cb, gi, gf, gate_b, mnorm_g, C0, n0, m0):
    H, DH = MLSTM_HEADS, MLSTM_DH
    W = H * DH
    L = MLSTM_CHUNK
    Lb = L if S % L == 0 else S
    assert Lb <= L
    nc = S // Lb
    rb0 = row0 // Lb
    n0 = n0.reshape(B, H, 1, DH)
    m0 = jnp.broadcast_to(m0.reshape(B, H, 1, 1), (B, H, 1, LANES))

    def zspec(cb, width):
        return pl.BlockSpec((Lb, width), lambda b, c, cb=cb: (rb0 + b * nc + c, cb))

    y, C, n, m = pl.pallas_call(
        functools.partial(_mlstm_kernel, L=L, Lb=Lb, gi=gi, gf=gf),
        out_shape=[jax.ShapeDtypeStruct((B * S, W), F32),
                   jax.ShapeDtypeStruct((B, H, DH, DH), F32),
                   jax.ShapeDtypeStruct((B, H, 1, DH), F32),
                   jax.ShapeDtypeStruct((B, H, 1, LANES), F32)],
        grid=(B, nc),
        in_specs=[pl.BlockSpec(memory_space=pltpu.SMEM),
                  zspec(col_q, W), zspec(col_q + 1, W), zspec(col_q + 2, W), zspec(col_q + 3, W),
                  zspec(gate_cb, LANES),
                  pl.BlockSpec((1, W), lambda b, c: (0, 0)),
                  pl.BlockSpec((1, H, DH, DH), lambda b, c: (b, 0, 0, 0)),
                  pl.BlockSpec((1, H, 1, DH), lambda b, c: (b, 0, 0, 0)),
                  pl.BlockSpec((1, H, 1, LANES), lambda b, c: (b, 0, 0, 0))],
        out_specs=[pl.BlockSpec((Lb, W), lambda b, c: (b * nc + c, 0)),
                   pl.BlockSpec((1, H, DH, DH), lambda b, c: (b, 0, 0, 0)),
                   pl.BlockSpec((1, H, 1, DH), lambda b, c: (b, 0, 0, 0)),
                   pl.BlockSpec((1, H, 1, LANES), lambda b, c: (b, 0, 0, 0))],
        scratch_shapes=[pltpu.VMEM((H, DH, DH), F32), pltpu.VMEM((H, 1, DH), F32),
                        pltpu.VMEM((H, 1, LANES), F32)],
        compiler_params=_cparams(("parallel", "arbitrary")),
        name="mlstm",
    )(gate_b, z, z, z, z, z, mnorm_g.reshape(1, W), C0, n0, m0)
    return y, C, n.reshape(B, H, DH), m[:, :, 0, 0]


PAGES_PER_STEP = 16


def _lane_extract_topn(score, n_sel, floor):
    lane = lax.broadcasted_iota(jnp.int32, score.shape, 1).astype(F32)
    sel = jnp.zeros(score.shape, F32)
    sc = score
    for _ in range(n_sel):
        m = jnp.max(sc, -1, keepdims=True)
        idx = jnp.min(jnp.where(sc == m, lane, float(score.shape[1])), -1, keepdims=True)
        hit = lane == idx
        sel = jnp.where(hit & (m > floor), 1.0, sel)
        sc = jnp.where(hit, -jnp.inf, sc)
    return sel


def _s1_kernel(pt_ref, *refs, past, S, ncp, nbp):
    pgs = refs[:PAGES_PER_STEP]
    tail_ref, q_ref, ocmp_ref, sel_ref, sub_ref = refs[PAGES_PER_STEP:]
    P = pgs[0].shape[-1]
    j = pl.program_id(1)
    nfull = pl.num_programs(1) - 1
    KVH, G, HD = NSA_KV_HEADS, NSA_GROUP, HEAD_DIM
    cols_step = PAGES_PER_STEP * P // CMP_STRIDE
    assert cols_step == LANES

    @pl.when(j == 0)
    def _():
        sub_ref[...] = jnp.zeros(sub_ref.shape, F32)

    def pool_t(n):
        return jnp.where(_div(lax.broadcasted_iota(jnp.int32, (n, LANES), 0), CMP_STRIDE)
                         == lax.broadcasted_iota(jnp.int32, (n, LANES), 1),
                         1.0 / CMP_STRIDE, 0.0).astype(BF16)

    @pl.when(j < nfull)
    def _():
        pt = pool_t(PAGES_PER_STEP * P)
        col = pl.ds(pl.multiple_of(j * cols_step, cols_step), cols_step)
        for k in range(2):
            for h in range(KVH):
                xt = jnp.concatenate([r[0, 0, k, h] for r in pgs], axis=1)
                sub_ref[k, h, :, col] = _pool_dot(pt, xt, True)

    @pl.when(j == nfull)
    def _():
        base = past // CMP_STRIDE
        pt = pool_t(P)
        for k in range(2):
            for h in range(KVH):
                sub_ref[k, h, :, base:base + LANES] = _pool_dot(pt, tail_ref[0, k, h], True)
        R = G * S
        row = lax.broadcasted_iota(jnp.int32, (R, 1), 0)
        pos = past + (row & (S - 1))
        posq = past + lax.broadcasted_iota(jnp.int32, (S, 1), 0)
        c_end = lax.broadcasted_iota(jnp.int32, (1, ncp), 1) * CMP_STRIDE + (2 * CMP_STRIDE - 1)
        cmask = c_end <= pos
        per = SEL_BLOCK // CMP_STRIDE
        e4 = jnp.where(_div(lax.broadcasted_iota(jnp.int32, (ncp, nbp), 0), per)
                       == lax.broadcasted_iota(jnp.int32, (ncp, nbp), 1), 1.0, 0.0).astype(F32)
        blk = lax.broadcasted_iota(jnp.int32, (1, nbp), 1)
        qblk = _div(posq, SEL_BLOCK)
        valid = blk * SEL_BLOCK <= posq
        forced = (blk == 0) | (blk == qblk) | (blk == qblk - 1)
        for h in range(KVH):
            kct = 0.5 * (sub_ref[0, h, :, 0:ncp] + sub_ref[0, h, :, 1:ncp + 1])
            vct = 0.5 * (sub_ref[1, h, :, 0:ncp] + sub_ref[1, h, :, 1:ncp + 1])
            qs = q_ref[0, h] * jnp.asarray(HD ** -0.5, BF16)
            s = jnp.dot(qs, kct.astype(BF16), preferred_element_type=F32)
            p = _masked_softmax(s, cmask)
            ocmp_ref[0, h] = lax.dot_general(p.astype(BF16), vct.astype(BF16), NT, preferred_element_type=F32)
            imp = p[0:S]
            for g in range(1, G):
                imp = imp + p[g * S:(g + 1) * S]
            impb = jnp.dot(imp, e4, precision=HI, preferred_element_type=F32)
            score = jnp.where(forced, MASK_BIG, jnp.where(valid, impb, -MASK_BIG))
            sel_ref[0, h] = _lane_extract_topn(score, SEL_TOPN, -0.5 * MASK_BIG)


def _s2_kernel(pt_ref, *refs, past, S, nbp):
    pgs = refs[:PAGES_PER_STEP]
    tail_ref, q_ref, sel_ref, osel_ref, m_s, l_s, acc_s = refs[PAGES_PER_STEP:]
    P = pgs[0].shape[-1]
    j = pl.program_id(1)
    nfull = pl.num_programs(1) - 1
    KVH, G, HD = NSA_KV_HEADS, NSA_GROUP, HEAD_DIM
    R = G * S

    @pl.when(j == 0)
    def _():
        m_s[...] = jnp.full(m_s.shape, NEG, F32)
        l_s[...] = jnp.zeros(l_s.shape, F32)
        acc_s[...] = jnp.zeros(acc_s.shape, F32)

    row = lax.broadcasted_iota(jnp.int32, (R, 1), 0)
    pos = past + (row & (S - 1))

    def process(kt_of, vt_of, n, kpos0):
        kpos = kpos0 + lax.broadcasted_iota(jnp.int32, (1, n), 1)
        nblk = LANES
        assert n // SEL_BLOCK <= nblk
        blk0 = kpos0 // SEL_BLOCK
        pick = jnp.where(lax.broadcasted_iota(jnp.int32, (nbp, nblk), 0)
                         == blk0 + lax.broadcasted_iota(jnp.int32, (nbp, nblk), 1), 1.0, 0.0).astype(BF16)
        spread = jnp.where(_div(lax.broadcasted_iota(jnp.int32, (nblk, n), 1), SEL_BLOCK)
                           == lax.broadcasted_iota(jnp.int32, (nblk, n), 0), 1.0, 0.0).astype(BF16)
        causal = kpos <= pos
        for h in range(KVH):
            kt = kt_of(h).astype(BF16)
            vt = vt_of(h).astype(BF16)
            qs = q_ref[0, h] * jnp.asarray(HD ** -0.5, BF16)
            s = jnp.dot(qs, kt, preferred_element_type=F32)
            mkb = jnp.dot(sel_ref[0, h].astype(BF16), pick, preferred_element_type=F32)
            mk = jnp.dot(mkb.astype(BF16), spread, preferred_element_type=F32)
            mask = (jnp.concatenate([mk] * G, axis=0) > 0.5) & causal
            sm = jnp.where(mask, s, NEG)
            m_old = m_s[h]
            m_new = jnp.maximum(m_old, jnp.max(sm, -1, keepdims=True))
            alpha = jnp.exp(m_old - m_new)
            e = jnp.where(mask, jnp.exp(sm - m_new), 0.0)
            l_s[h] = alpha * l_s[h] + jnp.sum(e, -1, keepdims=True)
            acc_s[h] = alpha * acc_s[h] + lax.dot_general(e.astype(BF16), vt, NT, preferred_element_type=F32)
            m_s[h] = m_new

    @pl.when(j < nfull)
    def _():
        process(lambda h: jnp.concatenate([r[0, 0, 0, h] for r in pgs], axis=1),
                lambda h: jnp.concatenate([r[0, 0, 1, h] for r in pgs], axis=1),
                PAGES_PER_STEP * P, j * (PAGES_PER_STEP * P))

    @pl.when(j == nfull)
    def _():
        process(lambda h: tail_ref[0, 0, h], lambda h: tail_ref[0, 1, h], P, past)
        for h in range(KVH):
            osel_ref[0, h] = acc_s[h] * (1.0 / jnp.maximum(l_s[h], 1e-30))


def _page_specs(layer, half, n_pages, P):
    def spec(r):
        return pl.BlockSpec(
            (1, 1, 2, NSA_KV_HEADS, HEAD_DIM, P),
            lambda b, j, pt, r=r: (layer, pt[b, jnp.minimum(j * PAGES_PER_STEP + r, n_pages - 1)],
                                   half, 0, 0, 0))
    return [spec(r) for r in range(PAGES_PER_STEP)]


def _nsa_sample_global(cache_t, layer, page_table, tail_t, q_s, S):
    B, n_pages = page_table.shape
    P = cache_t.shape[-1]
    KVH, G, HD = NSA_KV_HEADS, NSA_GROUP, HEAD_DIM
    past = n_pages * P
    assert n_pages % PAGES_PER_STEP == 0 and S & (S - 1) == 0 and S <= SEL_BLOCK
    nsteps = n_pages // PAGES_PER_STEP + 1
    ncp = -(-(past + P) // CMP_STRIDE // LANES) * LANES
    nbp = -(-(past + P) // SEL_BLOCK // LANES) * LANES
    R = G * S
    qspec = pl.BlockSpec((1, KVH, R, HD), lambda b, j, pt: (b, 0, 0, 0))
    o_cmp, sel = pl.pallas_call(
        functools.partial(_s1_kernel, past=past, S=S, ncp=ncp, nbp=nbp),
        out_shape=[jax.ShapeDtypeStruct((B, KVH, R, HD), F32), jax.ShapeDtypeStruct((B, KVH, S, nbp), F32)],
        grid_spec=pltpu.PrefetchScalarGridSpec(
            num_scalar_prefetch=1, grid=(B, nsteps),
            in_specs=_page_specs(layer, 0, n_pages, P)
            + [pl.BlockSpec((1, 2, KVH, HD, P), lambda b, j, pt: (b, 0, 0, 0, 0)), qspec],
            out_specs=[pl.BlockSpec((1, KVH, R, HD), lambda b, j, pt: (b, 0, 0, 0)),
                       pl.BlockSpec((1, KVH, S, nbp), lambda b, j, pt: (b, 0, 0, 0))],
            scratch_shapes=[pltpu.VMEM((2, KVH, HD, ncp + LANES), F32)]),
        compiler_params=_cparams(("parallel", "arbitrary")),
        name="nsa_sample_cmp",
    )(page_table, *([cache_t] * PAGES_PER_STEP), tail_t, q_s)
    o_sel = pl.pallas_call(
        functools.partial(_s2_kernel, past=past, S=S, nbp=nbp),
        out_shape=jax.ShapeDtypeStruct((B, KVH, R, HD), F32),
        grid_spec=pltpu.PrefetchScalarGridSpec(
            num_scalar_prefetch=1, grid=(B, nsteps),
            in_specs=_page_specs(layer, 1, n_pages, P)
            + [pl.BlockSpec((1, 2, KVH, HD, P), lambda b, j, pt: (b, 1, 0, 0, 0)), qspec,
               pl.BlockSpec((1, KVH, S, nbp), lambda b, j, pt: (b, 0, 0, 0))],
            out_specs=pl.BlockSpec((1, KVH, R, HD), lambda b, j, pt: (b, 0, 0, 0)),
            scratch_shapes=[pltpu.VMEM((KVH, R, 1), F32), pltpu.VMEM((KVH, R, 1), F32),
                            pltpu.VMEM((KVH, R, HD), F32)]),
        compiler_params=_cparams(("parallel", "arbitrary")),
        name="nsa_sample_sel",
    )(page_table, *([cache_t] * PAGES_PER_STEP), tail_t, q_s, sel)
    return o_cmp, o_sel


def _s3_kernel(q_ref, kw_ref, vw_ref, ocmp_ref, osel_ref, gt_ref, y_ref, *, past, S):
    KVH, G, HD = NSA_KV_HEADS, NSA_GROUP, HEAD_DIM
    R = G * S
    nk = kw_ref.shape[1]
    row = lax.broadcasted_iota(jnp.int32, (R, 1), 0)
    pos = past + (row & (S - 1))
    kpos = past - (nk - S) + lax.broadcasted_iota(jnp.int32, (1, nk), 1)
    diff = pos - kpos
    wmask = (diff >= 0) & (diff < WINDOW)
    sg = jax.nn.sigmoid(gt_ref[...])
    kw = kw_ref[0].astype(BF16)
    vw = vw_ref[0].astype(BF16)
    outs = []
    for h in range(KVH):
        qs = q_ref[0, h] * jnp.asarray(HD ** -0.5, BF16)
        s = lax.dot_general(qs, kw[:, h * HD:(h + 1) * HD], NT, preferred_element_type=F32)
        p = _masked_softmax(s, wmask)
        o_win = jnp.dot(p.astype(BF16), vw[:, h * HD:(h + 1) * HD], preferred_element_type=F32)
        oc = ocmp_ref[0, h]
        osl = osel_ref[0, h]
        for g in range(G):
            c0 = (h * G + g) * 3
            rs = slice(g * S, (g + 1) * S)
            outs.append(sg[:, c0:c0 + 1] * oc[rs] + sg[:, c0 + 1:c0 + 2] * osl[rs]
                        + sg[:, c0 + 2:c0 + 3] * o_win[rs])
    y_ref[...] = jnp.concatenate(outs, axis=-1)


def _nsa_sample_combine(q_s, kwin, vwin, o_cmp, o_sel, z, row0, gate_cb, past, S):
    B, KVH, R, HD = q_s.shape
    nk = kwin.shape[1]
    W = KVH * HD
    ospec = pl.BlockSpec((1, KVH, R, HD), lambda b: (b, 0, 0, 0))
    return pl.pallas_call(
        functools.partial(_s3_kernel, past=past, S=S),
        out_shape=jax.ShapeDtypeStruct((B * S, KVH * NSA_GROUP * HD), F32),
        grid=(B,),
        in_specs=[ospec,
                  pl.BlockSpec((1, nk, W), lambda b: (b, 0, 0)),
                  pl.BlockSpec((1, nk, W), lambda b: (b, 0, 0)),
                  ospec, ospec,
                  pl.BlockSpec((S, LANES), lambda b: (row0 // S + b, gate_cb))],
        out_specs=pl.BlockSpec((S, KVH * NSA_GROUP * HD), lambda b: (b, 0)),
        compiler_params=_cparams(("parallel",)),
        name="nsa_sample_win",
    )(q_s, kwin, vwin, o_cmp, o_sel, z)


def _out_proj_kernel(x_ref, ya_ref, yb_ref, yc_ref, yt_ref, w_ref, o_ref, *, n_head_tiles):
    i = pl.program_id(0)
    ka, kb = ya_ref.shape[1], yb_ref.shape[1]

    @pl.when(i < n_head_tiles)
    def _():
        acc = jnp.dot(ya_ref[...].astype(BF16), w_ref[0:ka, :], preferred_element_type=F32)
        acc = acc + jnp.dot(yb_ref[...].astype(BF16), w_ref[ka:ka + kb, :], preferred_element_type=F32)
        acc = acc + jnp.dot(yc_ref[...].astype(BF16), w_ref[ka + kb:, :], preferred_element_type=F32)
        o_ref[...] = x_ref[...] + acc

    @pl.when(i >= n_head_tiles)
    def _():
        o_ref[...] = x_ref[...] + jnp.dot(yt_ref[...].astype(BF16), w_ref[...], preferred_element_type=F32)


def _out_proj(x, ya, yb, yc, ytail, w, tm):
    T, D = x.shape
    TH = ya.shape[0]
    assert TH % tm == 0 and (T - TH) % tm == 0 and ytail.shape == (T - TH, D)
    nh = TH // tm

    def head(a):
        return pl.BlockSpec((tm, a.shape[1]), lambda i: (jnp.minimum(i, nh - 1), 0))

    return pl.pallas_call(
        functools.partial(_out_proj_kernel, n_head_tiles=nh),
        out_shape=jax.ShapeDtypeStruct((T, D), F32),
        grid=(T // tm,),
        in_specs=[pl.BlockSpec((tm, D), lambda i: (i, 0)),
                  head(ya), head(yb), head(yc),
                  pl.BlockSpec((tm, D), lambda i: (jnp.maximum(i - nh, 0), 0)),
                  pl.BlockSpec((D, D), lambda i: (0, 0))],
        out_specs=pl.BlockSpec((tm, D), lambda i: (i, 0)),
        compiler_params=_cparams(("parallel",)),
        name="out_proj",
    )(x, ya, yb, yc, ytail, w)


def _structural_pairs(k):
    return [(i, j) for i in range(k) for j in range(k) if (i + 1) * (j + 1) <= k]


RANK_MARK = 2.0 ** 126


def _extract_topk(cur_ref, top_ref, c, exact):
    NK, K = PEER_NKEYS, PEER_TOPK
    for k in range(K):
        cur = cur_ref[c]
        m = jnp.max(cur, axis=0)
        if exact:
            key = lax.broadcasted_iota(jnp.int32, cur.shape, 0).astype(F32)
            idx = jnp.min(jnp.where(cur == m[None], key, float(NK)), axis=0)
            hit = key == idx[None]
        else:
            hit = cur == m[None]
        cur_ref[c] = jnp.where(hit, -RANK_MARK * (1.0 + k / 32.0), cur)
        top_ref[c, k] = m


def _router_kernel(q_ref, wb_ref, n_out, e1_out, r2_out, e2_out, s_ref, cur_ref, top_ref, tmp_ref):
    H, NK, K = PEER_HEADS, PEER_NKEYS, PEER_TOPK
    Tt = q_ref.shape[0]
    half = q_ref.shape[1] // 2
    for c in range(2):
        s = lax.dot_general(wb_ref[c], q_ref[:, c * half:(c + 1) * half], NT, preferred_element_type=F32)
        s_ref[c] = s.reshape(NK, H, Tt)
        cur_ref[c] = s_ref[c]
        _extract_topk(cur_ref, top_ref, c, exact=False)

    marked = jnp.sum(jnp.where(cur_ref[...] < -0.5 * RANK_MARK, 1.0, 0.0), axis=1)
    tied = jnp.max(jnp.where(marked != float(K), 1.0, 0.0)) > 0.0

    @pl.when(tied)
    def _():
        for c in range(2):
            cur_ref[c] = s_ref[c]
            _extract_topk(cur_ref, top_ref, c, exact=True)

    v1 = [top_ref[0, k] for k in range(K)]
    v2 = [top_ref[1, k] for k in range(K)]

    pairs = _structural_pairs(K)
    cand = [v1[i] + v2[j] for (i, j) in pairs]
    n = len(pairs)
    rank = []
    for p in range(n):
        rank.append(jnp.zeros((H, Tt), F32))
    for p in range(n):
        ip, jp = pairs[p]
        for q in range(p + 1, n):
            iq, jq = pairs[q]
            if ip <= iq and jp <= jq:
                rank[q] = rank[q] + 1.0
            else:
                b = jnp.where(cand[p] >= cand[q], 1.0, 0.0)
                rank[q] = rank[q] + b
                rank[p] = rank[p] + (1.0 - b)
    sel = [jnp.where(r < K, 1.0, 0.0) for r in rank]
    e1 = [jnp.exp(v1[i] - v1[0]) for i in range(K)]
    e2 = [jnp.exp(v2[j] - v2[0]) for j in range(K)]
    cnt = [jnp.zeros((H, Tt), F32) for _ in range(K)]
    zsum = jnp.zeros((H, Tt), F32)
    for p, (i, j) in enumerate(pairs):
        cnt[i] = cnt[i] + sel[p]
        zsum = zsum + sel[p] * (e1[i] * e2[j])
    inv_z = 1.0 / zsum

    def rank_of(c):
        cur = cur_ref[c]
        return jnp.where(cur < -0.5 * RANK_MARK, (cur * (-1.0 / RANK_MARK) - 1.0) * 32.0, float(NK))

    rk1 = rank_of(0)
    nk = jnp.zeros((NK, H, Tt), F32)
    for i in range(K):
        nk = jnp.where(rk1 == float(i), cnt[i][None], nk)
    outs = ((n_out, nk),
            (e1_out, jnp.exp(s_ref[0] - v1[0][None]) * inv_z[None]),
            (r2_out, rank_of(1)),
            (e2_out, jnp.exp(s_ref[1] - v2[0][None])))
    for ref, val in outs:
        tmp_ref[...] = val.reshape(NK * H, Tt)
        for h in range(H):
            ref[h] = tmp_ref[pl.ds(h, NK, stride=H), :].astype(ref.dtype)


def _router(q, wb, tt):
    T = q.shape[0]
    H, NK = PEER_HEADS, PEER_NKEYS
    shp = jax.ShapeDtypeStruct((H, NK, T), F32)
    shp_b = jax.ShapeDtypeStruct((H, NK, T), BF16)
    ospec = pl.BlockSpec((H, NK, tt), lambda i: (0, 0, i))
    return pl.pallas_call(
        _router_kernel,
        out_shape=[shp, shp, shp_b, shp_b],
        grid=(T // tt,),
        in_specs=[pl.BlockSpec((tt, q.shape[1]), lambda i: (i, 0)),
                  pl.BlockSpec(wb.shape, lambda i: (0, 0, 0))],
        out_specs=[ospec, ospec, ospec, ospec],
        scratch_shapes=[pltpu.VMEM((2, NK, H, tt), F32), pltpu.VMEM((2, NK, H, tt), F32),
                        pltpu.VMEM((2, PEER_TOPK, H, tt), F32), pltpu.VMEM((NK * H, tt), F32)],
        compiler_params=_cparams(("parallel",)),
        name="peer_router",
    )(q, wb)


def _gelu_tanh(x):
    return 0.5 * x * (1.0 + jnp.tanh(np.sqrt(2.0 / np.pi) * (x + 0.044715 * (x * x * x))))


PEER_ACHUNK = 2


def _peer_dense_kernel(xn_ref, u_ref, v_ref, n_ref, e1_ref, r2_ref, e2_ref, x1_ref, g_ref, o_ref, *, final_norm):
    H, NK = PEER_HEADS, PEER_NKEYS
    j = pl.program_id(1)
    na = u_ref.shape[0] // NK
    tt = xn_ref.shape[0]

    @pl.when(j == 0)
    def _():
        o_ref[...] = x1_ref[...]

    xn = xn_ref[...]
    parts = []
    for c in range(na // PEER_ACHUNK):
        rows = slice(c * PEER_ACHUNK * NK, (c + 1) * PEER_ACHUNK * NK)
        act = lax.dot_general(u_ref[rows, :], xn, NT, preferred_element_type=F32)
        gates = []
        for a in range(c * PEER_ACHUNK, (c + 1) * PEER_ACHUNK):
            row = pl.ds(j * na + a, 1)
            gate = None
            for h in range(H):
                t = jnp.where(r2_ref[h] < n_ref[h, row, :].astype(BF16),
                              e1_ref[h, row, :].astype(BF16) * e2_ref[h], jnp.zeros((), BF16))
                gate = t if gate is None else gate + t
            gates.append(gate)
        parts.append(_gelu_tanh(act).astype(BF16) * jnp.concatenate(gates, axis=0))
    wt = jnp.concatenate(parts, axis=0)
    o_ref[...] += lax.dot_general(wt, v_ref[...], TN, preferred_element_type=F32)

    if final_norm:
        @pl.when(j == pl.num_programs(1) - 1)
        def _():
            x = o_ref[...]
            o_ref[...] = x * lax.rsqrt(jnp.mean(x * x, -1, keepdims=True) + RMS_EPS) * g_ref[...]


def _peer_dense(xn, u, v, layer, nk, e1, r2, e2, x1, g, tt, te, final_norm):
    T, D = xn.shape
    E = u.shape[1]
    H, NK = PEER_HEADS, PEER_NKEYS
    once = pl.Buffered(1)
    tab = pl.BlockSpec((H, NK, tt), lambda i, j: (0, 0, i), pipeline_mode=once)
    return pl.pallas_call(
        functools.partial(_peer_dense_kernel, final_norm=final_norm),
        out_shape=jax.ShapeDtypeStruct((T, D), F32),
        grid=(T // tt, E // te),
        in_specs=[pl.BlockSpec((tt, D), lambda i, j: (i, 0), pipeline_mode=once),
                  pl.BlockSpec((None, te, D), lambda i, j: (layer, j, 0)),
                  pl.BlockSpec((None, te, D), lambda i, j: (layer, j, 0)),
                  tab, tab, tab, tab,
                  pl.BlockSpec((tt, D), lambda i, j: (i, 0), pipeline_mode=once),
                  pl.BlockSpec((1, D), lambda i, j: (0, 0))],
        out_specs=pl.BlockSpec((tt, D), lambda i, j: (i, 0)),
        compiler_params=_cparams(("parallel", "arbitrary")),
        name="peer_dense",
    )(xn, u, v, nk, e1, r2, e2, x1, g.reshape(1, D))


T_TILE = 384
ROUTER_TILE = 128
PEER_T_TILE = 384
PAD_TILE = 768
Z_COL_TILE = 1280
OUT_TILE = 256
EXPERT_TILE = 2048


def kernel(x_prompt, x_sample, cache_nsa_kv, state_win_kv, state_conv, state_mlstm_C, state_mlstm_n,
           state_mlstm_m, page_table, norm1_g, w_in, conv_w, mlstm_gate_b, mlstm_norm_g, w_out, norm2_g,
           peer_wq, peer_subkeys, peer_u, peer_v, final_norm_g):
    BP, SP, D = x_prompt.shape
    BS, SS, _ = x_sample.shape
    depth = w_in.shape[0]
    KVH, G, HD = NSA_KV_HEADS, NSA_GROUP, HEAD_DIM
    H, DH = MLSTM_HEADS, MLSTM_DH
    CW = conv_w.shape[2]
    NW = KVH * G * HD
    KVW = 6 * KVH * HD
    NG = 3 * KVH * G
    MW = H * DH
    TP, TS = BP * SP, BS * SS
    T = TP + TS
    assert PAD_TILE % T_TILE == 0 and PAD_TILE % ROUTER_TILE == 0 and PAD_TILE % PEER_T_TILE == 0
    TPAD = -(-T // PAD_TILE) * PAD_TILE
    n_pages = page_table.shape[1]
    P = cache_nsa_kv.shape[2]
    past = n_pages * P
    wb_rows = state_win_kv.shape[2]

    c_q = 3 * CW
    c_kv = c_q + NW
    c_m = c_kv + KVW
    c_g = c_m + 4 * MW
    ZW = -(-(c_g + LANES) // Z_COL_TILE) * Z_COL_TILE
    assert CW % LANES == 0 and c_m % MW == 0 and c_g % LANES == 0 and NG + 2 * H <= LANES
    gate_cb = c_g // LANES
    gi, gf = NG, NG + H
    o_gate = 3 * CW + NW + KVW

    x = jnp.concatenate([x_prompt.reshape(TP, D), x_sample.reshape(TS, D),
                         jnp.zeros((TPAD - T, D), F32)], axis=0)
    cache_t = jnp.transpose(cache_nsa_kv, (0, 1, 3, 4, 5, 2))
    eye_h = jnp.eye(PEER_HEADS, dtype=F32)
    u_bf = peer_u.astype(BF16)
    v_bf = peer_v.astype(BF16)
    dk = peer_subkeys.shape[-1]

    p_st, s_st = [], []
    for l in range(depth):
        wi = w_in[l]
        w_perm = jnp.concatenate(
            [wi[:, :o_gate], wi[:, o_gate + NG:o_gate + NG + 4 * MW], wi[:, o_gate:o_gate + NG],
             wi[:, o_gate + NG + 4 * MW:], jnp.zeros((D, ZW - wi.shape[1]), F32)], axis=1).astype(BF16)
        z = _rms_proj(x, norm1_g[l], w_perm, PAD_TILE, Z_COL_TILE, F32)

        ya_p, conv_p = _conv(z, 0, BP, SP, jnp.zeros((BP, 2, CW), F32), conv_w[l])
        ya_s, conv_s = _conv(z, TP, BS, SS, state_conv[l], conv_w[l])

        yb_p, rows_t, win_t = _nsa_prompt(z, c_q, c_kv, gate_cb, BP, SP)
        rows_p = rows_t.transpose(0, 4, 1, 2, 3)
        win_p = win_t.transpose(0, 4, 1, 2, 3)

        zs = z[TP:T]
        q_s = zs[:, c_q:c_q + NW].reshape(BS, SS, KVH, G, HD).transpose(0, 2, 3, 1, 4)
        q_s = q_s.reshape(BS, KVH, G * SS, HD).astype(BF16)
        zkv_s = zs[:, c_kv:c_kv + KVW].reshape(BS, SS, 3, 2, KVH, HD)
        rows_s = zkv_s[:, :, :2].reshape(BS, SS, 4, KVH, HD)
        tail_t = jnp.pad(rows_s.transpose(0, 2, 3, 4, 1), ((0, 0), (0, 0), (0, 0), (0, 0), (0, P - SS)))
        o_cmp, o_sel = _nsa_sample_global(cache_t, l, page_table, tail_t, q_s, SS)
        w_all = jnp.concatenate([state_win_kv[l], zkv_s[:, :, 2]], axis=1)
        kwin = w_all[:, :, 0].reshape(BS, wb_rows + SS, KVH * HD)
        vwin = w_all[:, :, 1].reshape(BS, wb_rows + SS, KVH * HD)
        yb_s = _nsa_sample_combine(q_s, kwin, vwin, o_cmp, o_sel, z, TP, gate_cb, past, SS)
        win_s = w_all[:, SS:]

        yc_p, C_p, n_p, m_p = _mlstm(z, 0, BP, SP, c_m // MW, gate_cb, gi, gf, mlstm_gate_b[l],
                                     mlstm_norm_g[l], jnp.zeros((BP, H, DH, DH), F32),
                                     jnp.zeros((BP, H, DH), F32), jnp.zeros((BP, H), F32))
        yc_s, C_s, n_s, m_s = _mlstm(z, TP, BS, SS, c_m // MW, gate_cb, gi, gf, mlstm_gate_b[l],
                                     mlstm_norm_g[l], state_mlstm_C[l], state_mlstm_n[l], state_mlstm_m[l])

        ytail = jnp.concatenate([jnp.concatenate([ya_s, yb_s, yc_s], axis=1),
                                 jnp.zeros((TPAD - T, D), F32)], axis=0)
        x1 = _out_proj(x, ya_p, yb_p, yc_p, ytail, w_out[l].astype(BF16), OUT_TILE)

        nh = PEER_HEADS
        wq_perm = peer_wq[l].reshape(D, nh, 2, dk).transpose(0, 2, 1, 3).reshape(D, 2 * nh * dk).astype(BF16)
        q_peer, xn2 = _rms_proj(x1, norm2_g[l], wq_perm, PAD_TILE, nh * dk, BF16, emit_xn=True)
        wb = jnp.einsum('hcnd,hg->cnhgd', peer_subkeys[l], eye_h).reshape(2, PEER_NKEYS * nh, nh * dk).astype(BF16)
        nk, e1, r2, e2 = _router(q_peer, wb, ROUTER_TILE)
        x = _peer_dense(xn2, u_bf, v_bf, l, nk, e1, r2, e2, x1,
                        final_norm_g, PEER_T_TILE, EXPERT_TILE, final_norm=(l == depth - 1))

        p_st.append((rows_p, win_p, conv_p, C_p, n_p, m_p))
        s_st.append((rows_s, win_s, conv_s, C_s, n_s, m_s))

    p_rows, p_win, p_conv, p_C, p_n, p_m = [jnp.stack(a) for a in zip(*p_st)]
    s_rows, s_win, s_conv, s_C, s_n, s_m = [jnp.stack(a) for a in zip(*s_st)]
    y_prompt = x[:TP].reshape(BP, SP, D)
    y_sample = x[TP:T].reshape(BS, SS, D)
    return (y_prompt, y_sample, p_rows, p_win, p_conv, p_C, p_n, p_m, s_rows, s_win, s_conv, s_C, s_n, s_m)
```

```python
import functools

import numpy as np
import jax
import jax.numpy as jnp
from jax import lax
from jax.experimental import pallas as pl
from jax.experimental.pallas import tpu as pltpu

F32 = jnp.float32
BF16 = jnp.bfloat16
HI = lax.Precision.HIGHEST

RMS_EPS = 1e-6
HEAD_DIM = 64
NSA_KV_HEADS = 4
NSA_GROUP = 4
CMP_STRIDE = 16
SEL_BLOCK = 64
SEL_TOPN = 16
WINDOW = 512
MLSTM_HEADS = 4
MLSTM_DH = 128
MLSTM_CHUNK = 64
PEER_HEADS = 8
PEER_NKEYS = 128
PEER_TOPK = 16
MASK_BIG = 1e9
NEG = -1e30

LANES = 128
VMEM_LIMIT = 56 * 1024 * 1024

NT = (((1,), (1,)), ((), ()))
TN = (((0,), (0,)), ((), ()))


def _cparams(sem):
    return pltpu.CompilerParams(dimension_semantics=sem, vmem_limit_bytes=VMEM_LIMIT)


def _div(x, d):
    assert d & (d - 1) == 0
    return lax.shift_right_arithmetic(x, jnp.int32(d.bit_length() - 1))


def _pool_dot(pool, x, x_is_lhs):
    hi = x.astype(BF16)
    lo = (x - hi.astype(F32)).astype(BF16)
    if x_is_lhs:
        return (jnp.dot(hi, pool, preferred_element_type=F32) + jnp.dot(lo, pool, preferred_element_type=F32))
    return (jnp.dot(pool, hi, preferred_element_type=F32) + jnp.dot(pool, lo, preferred_element_type=F32))


def _masked_softmax(s, mask, exp=jnp.exp):
    s = jnp.where(mask, s, NEG)
    m = jnp.max(s, -1, keepdims=True)
    e = jnp.where(mask, exp(s - m), 0.0)
    d = jnp.maximum(jnp.sum(e, -1, keepdims=True), 1e-30)
    return e * (1.0 / d)


def _rms_proj_kernel(x_ref, g_ref, w_ref, *rest, emit_xn):
    if emit_xn:
        o_ref, xo_ref, xn_ref = rest
    else:
        o_ref, xn_ref = rest

    @pl.when(pl.program_id(1) == 0)
    def _():
        x = x_ref[...]
        r = x * lax.rsqrt(jnp.mean(x * x, -1, keepdims=True) + RMS_EPS)
        xn = (r * g_ref[...]).astype(BF16)
        xn_ref[...] = xn
        if emit_xn:
            xo_ref[...] = xn

    o_ref[...] = jnp.dot(xn_ref[...], w_ref[...], preferred_element_type=F32).astype(o_ref.dtype)


def _rms_proj(x, g, w, tm, tn, out_dtype, emit_xn=False):
    T, D = x.shape
    N = w.shape[1]
    out_shape = [jax.ShapeDtypeStruct((T, N), out_dtype)]
    out_specs = [pl.BlockSpec((tm, tn), lambda i, j: (i, j))]
    if emit_xn:
        out_shape.append(jax.ShapeDtypeStruct((T, D), BF16))
        out_specs.append(pl.BlockSpec((tm, D), lambda i, j: (i, 0)))
    res = pl.pallas_call(
        functools.partial(_rms_proj_kernel, emit_xn=emit_xn),
        out_shape=out_shape,
        grid=(T // tm, N // tn),
        in_specs=[pl.BlockSpec((tm, D), lambda i, j: (i, 0)),
                  pl.BlockSpec((1, D), lambda i, j: (0, 0)),
                  pl.BlockSpec((D, tn), lambda i, j: (0, j))],
        out_specs=out_specs,
        scratch_shapes=[pltpu.VMEM((tm, D), BF16)],
        compiler_params=_cparams(("parallel", "arbitrary")),
        name="rms_proj",
    )(x, g.reshape(1, D), w)
    return res if emit_xn else res[0]


def _conv_kernel(cb_ref, cc_ref, ch_ref, buf_ref, w_ref, y_ref, new_ref, ext_ref, *, S):
    C = cb_ref.shape[-1]
    u = cc_ref[...] * ch_ref[...]
    ext_ref[0:8, :] = jnp.zeros((8, C), F32)
    ext_ref[6:8, :] = buf_ref[0]
    ext_ref[8:8 + S, :] = u
    w = w_ref[...]
    y = w[0:1] * ext_ref[6:6 + S, :] + w[1:2] * ext_ref[7:7 + S, :] + w[2:3] * u
    y_ref[...] = cb_ref[...] * y
    new_ref[0] = u[S - 2:S]


def _conv(z, row0, B, S, buf, w):
    C = w.shape[1]
    rb0 = row0 // S
    return pl.pallas_call(
        functools.partial(_conv_kernel, S=S),
        out_shape=[jax.ShapeDtypeStruct((B * S, C), F32), jax.ShapeDtypeStruct((B, 2, C), F32)],
        grid=(B,),
        in_specs=[pl.BlockSpec((S, C), lambda b: (rb0 + b, 0)),
                  pl.BlockSpec((S, C), lambda b: (rb0 + b, 1)),
                  pl.BlockSpec((S, C), lambda b: (rb0 + b, 2)),
                  pl.BlockSpec((1, 2, C), lambda b: (b, 0, 0)),
                  pl.BlockSpec((3, C), lambda b: (0, 0))],
        out_specs=[pl.BlockSpec((S, C), lambda b: (b, 0)),
                   pl.BlockSpec((1, 2, C), lambda b: (b, 0, 0))],
        scratch_shapes=[pltpu.VMEM((S + 8, C), F32)],
        compiler_params=_cparams(("parallel",)),
        name="short_conv",
    )(z, z, z, buf, w)


def _topn_rank_select_t(score_t, n_sel):
    NB = score_t.shape[0]
    blk = lax.broadcasted_iota(jnp.int32, score_t.shape, 0)
    rank = jnp.zeros(score_t.shape, F32)
    for i in range(NB):
        ci = score_t[i:i + 1, :]
        beats = jnp.where(ci > score_t, 1.0, jnp.where((ci == score_t) & (blk > i), 1.0, 0.0))
        rank = rank + beats
    return rank < n_sel


SEL_KCHUNK = 512


def _exp2_softmax_pv(pieces, hd):
    m = None
    for s, _ in pieces:
        mm = jnp.max(s, -1, keepdims=True)
        m = mm if m is None else jnp.maximum(m, mm)
    oa = None
    for s, v in pieces:
        t = jnp.dot(jnp.exp2(s - m).astype(BF16), v, preferred_element_type=F32)
        oa = t if oa is None else oa + t
    return oa[:, 0:hd] * (1.0 / oa[:, hd:hd + 1])


NSA_HPS = 2
QK_WIDTH = 128


def _nsa_prompt_kernel(q_ref, kcs_ref, vcs_ref, kss_ref, vss_ref, kws_ref, vws_ref, gt_ref,
                       y_ref, rows_ref, win_ref, kc_ref, vc_ref, ks_ref, vs_ref, kw_ref, vw_ref, osel_s, *, S, tq):
    pair = pl.program_id(1)
    qi = pl.program_id(2)
    G, HD = NSA_GROUP, HEAD_DIM
    NC = S // CMP_STRIDE
    NB = S // SEL_BLOCK
    n_sel = min(SEL_TOPN, NB)
    span = min(WINDOW + tq, S)

    @pl.when(qi == 0)
    def _():
        j = lax.broadcasted_iota(jnp.int32, (NC, S), 0)
        r = lax.broadcasted_iota(jnp.int32, (NC, S), 1)
        lo = j * CMP_STRIDE
        pool = jnp.where((r >= lo) & (r < lo + 2 * CMP_STRIDE), 0.5 / CMP_STRIDE, 0.0).astype(BF16)
        kcp = _pool_dot(pool, kcs_ref[...], False).astype(BF16)
        vcp = _pool_dot(pool, vcs_ref[...], False).astype(BF16)
        blk1h = jnp.where(_div(lax.broadcasted_iota(jnp.int32, (S, NB), 0), SEL_BLOCK)
                          == lax.broadcasted_iota(jnp.int32, (S, NB), 1), 1.0, 0.0).astype(BF16)
        kpad = jnp.zeros((S, QK_WIDTH - HD - NB), BF16)
        ones = jnp.where(lax.broadcasted_iota(jnp.int32, (S, LANES - HD), 1) == 0, 1.0, 0.0).astype(BF16)
        for hh in range(NSA_HPS):
            cols = slice(hh * HD, (hh + 1) * HD)
            kc_ref[hh] = kcp[:, cols]
            vc_ref[hh] = vcp[:, cols]
            ks_ref[hh] = jnp.concatenate([kss_ref[:, cols].astype(BF16), blk1h, kpad], axis=1)
            vs_ref[hh] = jnp.concatenate([vss_ref[:, cols].astype(BF16), ones], axis=1)
            kw_ref[hh] = kws_ref[:, cols].astype(BF16)
            vw_ref[hh] = jnp.concatenate([vws_ref[:, cols].astype(BF16), ones], axis=1)
        for kind, src in enumerate((kcs_ref, vcs_ref, kss_ref, vss_ref)):
            rows_ref[0, kind] = src[...].T.reshape(NSA_HPS, HD, S)
        wn = win_ref.shape[-1]
        for kind, src in enumerate((kws_ref, vws_ref)):
            win_ref[0, kind] = src[S - wn:S, :].T.reshape(NSA_HPS, HD, wn)

    t0 = qi * tq
    row = lax.broadcasted_iota(jnp.int32, (G * tq, 1), 0)
    pos = t0 + (row & (tq - 1))
    posq = t0 + lax.broadcasted_iota(jnp.int32, (tq, 1), 0)
    c_end = lax.broadcasted_iota(jnp.int32, (1, NC), 1) * CMP_STRIDE + (2 * CMP_STRIDE - 1)
    cmask = c_end <= pos
    per = SEL_BLOCK // CMP_STRIDE
    e4t = jnp.where(_div(lax.broadcasted_iota(jnp.int32, (NB, NC), 1), per)
                    == lax.broadcasted_iota(jnp.int32, (NB, NC), 0), 1.0, 0.0).astype(F32)
    eye_nb = jnp.where(lax.broadcasted_iota(jnp.int32, (NB, NB), 0)
                       == lax.broadcasted_iota(jnp.int32, (NB, NB), 1), 1.0, 0.0).astype(BF16)
    blk_t = lax.broadcasted_iota(jnp.int32, (NB, 1), 0)
    posq_t = t0 + lax.broadcasted_iota(jnp.int32, (1, tq), 1)
    qblk_t = _div(posq_t, SEL_BLOCK)
    valid_t = blk_t * SEL_BLOCK <= posq_t
    forced_t = (blk_t == 0) | (blk_t == qblk_t) | (blk_t == qblk_t - 1)

    qscale = (HD ** -0.5) * np.log2(np.e)
    qss, qas, o_cmps = [], [], []
    for hh in range(NSA_HPS):
        qh = q_ref[:, hh * G * HD:(hh + 1) * G * HD] * qscale
        qs = jnp.concatenate([qh[:, g * HD:(g + 1) * HD] for g in range(G)], axis=0).astype(BF16)
        s = lax.dot_general(qs, kc_ref[hh], NT, preferred_element_type=F32)
        p = _masked_softmax(s, cmask, jnp.exp2)
        o_cmps.append(jnp.dot(p.astype(BF16), vc_ref[hh], preferred_element_type=F32))
        imp = p[0:tq]
        for g in range(1, G):
            imp = imp + p[g * tq:(g + 1) * tq]
        impb_t = lax.dot_general(e4t, imp, NT, precision=HI, preferred_element_type=F32)
        score_t = jnp.where(forced_t, MASK_BIG, jnp.where(valid_t, impb_t, -MASK_BIG))
        sel_t = _topn_rank_select_t(score_t, n_sel) & (score_t > -0.5 * MASK_BIG)
        selneg_t = jnp.where(sel_t, 0.0, NEG).astype(BF16)
        selneg = lax.dot_general(selneg_t, eye_nb, TN, preferred_element_type=F32).astype(BF16)
        qas.append(jnp.concatenate([qs, jnp.concatenate([selneg] * G, axis=0),
                                    jnp.zeros((G * tq, QK_WIDTH - HD - NB), BF16)], axis=1))
        qss.append(qs)

    def sel_branch(klen):
        tail = min(klen, SEL_KCHUNK)
        kpos = (klen - tail) + lax.broadcasted_iota(jnp.int32, (1, tail), 1)
        tail_bias = jnp.where(kpos <= posq, 0.0, NEG)
        def scores(hh, g):
            qa = qas[hh][g * tq:(g + 1) * tq]
            pieces = []
            if klen > tail:
                pieces.append((lax.dot_general(qa, ks_ref[hh, 0:klen - tail, :], NT,
                                               preferred_element_type=F32),
                               vs_ref[hh, 0:klen - tail, :]))
            pieces.append((lax.dot_general(qa, ks_ref[hh, klen - tail:klen, :], NT,
                                           preferred_element_type=F32) + tail_bias,
                           vs_ref[hh, klen - tail:klen, :]))
            return pieces

        items = [(hh, g) for hh in range(NSA_HPS) for g in range(G)]
        nxt = scores(*items[0])
        for i, (hh, g) in enumerate(items):
            cur = nxt
            if i + 1 < len(items):
                nxt = scores(*items[i + 1])
            osel_s[hh, g * tq:(g + 1) * tq, :] = _exp2_softmax_pv(cur, HD)

    nvar = -(-S // SEL_KCHUNK)
    per_var = SEL_KCHUNK // tq
    for v in range(nvar):
        @pl.when(qi // per_var == v)
        def _(v=v):
            sel_branch(min(S, (v + 1) * SEL_KCHUNK))

    start = pl.multiple_of(jnp.maximum(t0 + tq - span, 0), tq)
    diff = posq - (start + lax.broadcasted_iota(jnp.int32, (1, span), 1))
    win_bias = jnp.where((diff >= 0) & (diff < WINDOW), 0.0, NEG)

    sg = jax.nn.sigmoid(gt_ref[...])
    outs = []
    for hh in range(NSA_HPS):
        kvh = pair * NSA_HPS + hh
        pick = jnp.where(lax.broadcasted_iota(jnp.int32, (LANES, LANES), 0)
                         == lax.broadcasted_iota(jnp.int32, (LANES, LANES), 1) + kvh * (3 * G),
                         1.0, 0.0).astype(F32)
        g12 = jnp.dot(sg, pick, precision=HI, preferred_element_type=F32)
        kw = kw_ref[hh, pl.ds(start, span), :]
        vw = vw_ref[hh, pl.ds(start, span), :]
        for g in range(G):
            rows = slice(g * tq, (g + 1) * tq)
            s_win = lax.dot_general(qss[hh][rows], kw, NT, preferred_element_type=F32) + win_bias
            o_win = _exp2_softmax_pv([(s_win, vw)], HD)
            outs.append(g12[:, 3 * g:3 * g + 1] * o_cmps[hh][rows]
                        + g12[:, 3 * g + 1:3 * g + 2] * osel_s[hh, rows, :]
                        + g12[:, 3 * g + 2:3 * g + 3] * o_win)
    y_ref[...] = jnp.concatenate(outs, axis=-1)


def _nsa_prompt(z, c_q, c_kv, gate_cb, B, S, tq=256):
    nq = S // tq
    KVH, G, HD = NSA_KV_HEADS, NSA_GROUP, HEAD_DIM
    NC = S // CMP_STRIDE
    qw = NSA_HPS * G * HD
    assert NSA_HPS * HD == LANES and c_q % qw == 0 and c_kv % LANES == 0

    def kv_spec(i):
        cb = (c_kv + i * KVH * HD) // LANES
        return pl.BlockSpec((S, LANES), lambda b, p, t, cb=cb: (b, cb + p))

    head_kv = pltpu.VMEM((NSA_HPS, S, HD), BF16)
    head_aug = pltpu.VMEM((NSA_HPS, S, LANES), BF16)
    assert HD + S // SEL_BLOCK <= LANES
    wn = min(WINDOW, S)
    return pl.pallas_call(
        functools.partial(_nsa_prompt_kernel, S=S, tq=tq),
        out_shape=[jax.ShapeDtypeStruct((B * S, KVH * G * HD), F32),
                   jax.ShapeDtypeStruct((B, 4, KVH, HD, S), F32),
                   jax.ShapeDtypeStruct((B, 2, KVH, HD, wn), F32)],
        grid=(B, KVH // NSA_HPS, nq),
        in_specs=[pl.BlockSpec((tq, qw), lambda b, p, t: (b * nq + t, c_q // qw + p)),
                  kv_spec(0), kv_spec(1), kv_spec(2), kv_spec(3), kv_spec(4), kv_spec(5),
                  pl.BlockSpec((tq, LANES), lambda b, p, t: (b * nq + t, gate_cb))],
        out_specs=[pl.BlockSpec((tq, qw), lambda b, p, t: (b * nq + t, p)),
                   pl.BlockSpec((1, 4, NSA_HPS, HD, S), lambda b, p, t: (b, 0, p, 0, 0)),
                   pl.BlockSpec((1, 2, NSA_HPS, HD, wn), lambda b, p, t: (b, 0, p, 0, 0))],
        scratch_shapes=[pltpu.VMEM((NSA_HPS, NC, HD), BF16), pltpu.VMEM((NSA_HPS, NC, HD), BF16),
                        pltpu.VMEM((NSA_HPS, S, QK_WIDTH), BF16), head_aug, head_kv, head_aug,
                        pltpu.VMEM((NSA_HPS, G * tq, HD), F32)],
        compiler_params=_cparams(("parallel", "parallel", "arbitrary")),
        name="nsa_prompt",
    )(z, z, z, z, z, z, z, z)


def _log_sigmoid(x):
    return jnp.minimum(x, 0.0) - jnp.log(1.0 + jnp.exp(-jnp.abs(x)))


def _mlstm_kernel(gb_ref, q_ref, k_ref, v_ref, o_ref, gt_ref, mg_ref, c0_ref, n0_ref, m0_ref,
                  y_ref, cn_ref, nn_ref, mn_ref, c_s, n_s, m_s, *, L, Lb, gi, gf):
    H, DH = MLSTM_HEADS, MLSTM_DH
    c = pl.program_id(1)

    @pl.when(c == 0)
    def _():
        c_s[...] = c0_ref[0]
        n_s[...] = n0_ref[0]
        m_s[...] = m0_ref[0]

    def padrows(a):
        if Lb == L:
            return a
        return jnp.concatenate([a, jnp.zeros((L - Lb, a.shape[1]), a.dtype)], axis=0)

    lane = lax.broadcasted_iota(jnp.int32, (1, LANES), 1)
    bias = jnp.zeros((1, LANES), F32)
    for h in range(H):
        bias = bias + jnp.where(lane == gi + h, gb_ref[0, h], 0.0) + jnp.where(lane == gf + h, gb_ref[1, h], 0.0)
    is_f = (lane >= gf) & (lane < gf + H)
    pre = padrows(gt_ref[...]) + bias
    gate = jnp.where(is_f, _log_sigmoid(pre), pre)
    if Lb != L:
        live = lax.broadcasted_iota(jnp.int32, (L, 1), 0) < Lb
        gate = jnp.where(live, gate, jnp.where(is_f, 0.0, NEG))
    rr = lax.broadcasted_iota(jnp.int32, (L, L), 0)
    cc = lax.broadcasted_iota(jnp.int32, (L, L), 1)
    tril = rr >= cc
    bcum = jnp.dot(jnp.where(tril, 1.0, 0.0).astype(F32), gate, precision=HI, preferred_element_type=F32)
    e8 = jnp.where(lax.broadcasted_iota(jnp.int32, (8, LANES), 1)
                   == lax.broadcasted_iota(jnp.int32, (8, LANES), 0) + gi, 1.0, 0.0).astype(F32)
    rg = lax.dot_general(e8, gate, NT, precision=HI, preferred_element_type=F32)
    rb = lax.dot_general(e8, bcum, NT, precision=HI, preferred_element_type=F32)

    q = padrows(q_ref[...])
    k = padrows(k_ref[...])
    v = padrows(v_ref[...])
    og = padrows(o_ref[...])
    hs = range(H)
    sls = [slice(h * DH, (h + 1) * DH) for h in hs]
    qq = [q[:, sl] for sl in sls]
    kk = [k[:, sl] * (DH ** -0.5) for sl in sls]
    qb = [x.astype(BF16) for x in qq]
    kb = [x.astype(BF16) for x in kk]
    vb = [v[:, sl].astype(BF16) for sl in sls]
    b_col = [bcum[:, gf + h:gf + h + 1] for h in hs]
    i_col = [gate[:, gi + h:gi + h + 1] for h in hs]
    m_prev = [m_s[h][:, 0:1] for h in hs]
    cmat = [c_s[h] for h in hs]
    n_row = [n_s[h] for h in hs]
    qk = [lax.dot_general(qb[h], kb[h], NT, preferred_element_type=F32) for h in hs]
    qc = [jnp.dot(qb[h], cmat[h].astype(BF16), preferred_element_type=F32) for h in hs]
    dmat = [jnp.where(tril, b_col[h] - rb[H + h:H + h + 1, :] + rg[h:h + 1, :], NEG) for h in hs]
    inter = [b_col[h] + m_prev[h] for h in hs]
    mt = [jnp.maximum(inter[h], jnp.max(dmat[h], -1, keepdims=True)) for h in hs]
    wqk = [jnp.exp(dmat[h] - mt[h]) * qk[h] for h in hs]
    a = [jnp.exp(inter[h] - mt[h]) for h in hs]
    wv = [jnp.dot(wqk[h].astype(BF16), vb[h], preferred_element_type=F32) for h in hs]
    m_new = [mt[h][L - 1:L] for h in hs]
    wl = [jnp.exp(b_col[h][L - 1:L] - b_col[h] + i_col[h] - m_new[h]) for h in hs]
    decay = [jnp.exp(b_col[h][L - 1:L] + m_prev[h] - m_new[h]) for h in hs]
    kw = [wl[h] * kk[h] for h in hs]
    kv = [lax.dot_general(kw[h].astype(BF16), vb[h], TN, preferred_element_type=F32) for h in hs]
    ys = []
    for h in hs:
        num = a[h] * qc[h] + wv[h]
        den = a[h] * jnp.sum(qq[h] * n_row[h], -1, keepdims=True) + jnp.sum(wqk[h], -1, keepdims=True)
        hh = num * (1.0 / jnp.maximum(jnp.abs(den), jnp.exp(-mt[h])))
        c_s[h] = decay[h] * cmat[h] + kv[h]
        n_s[h] = decay[h] * n_row[h] + jnp.sum(kw[h], 0, keepdims=True)
        m_s[h] = jnp.broadcast_to(m_new[h], (1, LANES))
        hn = hh * lax.rsqrt(jnp.mean(hh * hh, -1, keepdims=True) + RMS_EPS)
        ys.append(jax.nn.sigmoid(og[:, sls[h]]) * hn * mg_ref[:, sls[h]])
    y = jnp.concatenate(ys, axis=-1)
    y_ref[...] = y[0:Lb]

    @pl.when(c == pl.num_programs(1) - 1)
    def _():
        cn_ref[0] = c_s[...]
        nn_ref[0] = n_s[...]
        mn_ref[0] = m_s[...]


def _mlstm(z, row0, B, S, col_q, gate_cb, gi, gf, gate_b, mnorm_g, C0, n0, m0):
    H, DH = MLSTM_HEADS, MLSTM_DH
    W = H * DH
    L = MLSTM_CHUNK
    Lb = L if S % L == 0 else S
    assert Lb <= L
    nc = S // Lb
    rb0 = row0 // Lb
    n0 = n0.reshape(B, H, 1, DH)
    m0 = jnp.broadcast_to(m0.reshape(B, H, 1, 1), (B, H, 1, LANES))

    def zspec(cb, width):
        return pl.BlockSpec((Lb, width), lambda b, c, cb=cb: (rb0 + b * nc + c, cb))

    y, C, n, m = pl.pallas_call(
        functools.partial(_mlstm_kernel, L=L, Lb=Lb, gi=gi, gf=gf),
        out_shape=[jax.ShapeDtypeStruct((B * S, W), F32),
                   jax.ShapeDtypeStruct((B, H, DH, DH), F32),
                   jax.ShapeDtypeStruct((B, H, 1, DH), F32),
                   jax.ShapeDtypeStruct((B, H, 1, LANES), F32)],
        grid=(B, nc),
        in_specs=[pl.BlockSpec(memory_space=pltpu.SMEM),
                  zspec(col_q, W), zspec(col_q + 1, W), zspec(col_q + 2, W), zspec(col_q + 3, W),
                  zspec(gate_cb, LANES),
                  pl.BlockSpec((1, W), lambda b, c: (0, 0)),
                  pl.BlockSpec((1, H, DH, DH), lambda b, c: (b, 0, 0, 0)),
                  pl.BlockSpec((1, H, 1, DH), lambda b, c: (b, 0, 0, 0)),
                  pl.BlockSpec((1, H, 1, LANES), lambda b, c: (b, 0, 0, 0))],
        out_specs=[pl.BlockSpec((Lb, W), lambda b, c: (b * nc + c, 0)),
                   pl.BlockSpec((1, H, DH, DH), lambda b, c: (b, 0, 0, 0)),
                   pl.BlockSpec((1, H, 1, DH), lambda b, c: (b, 0, 0, 0)),
                   pl.BlockSpec((1, H, 1, LANES), lambda b, c: (b, 0, 0, 0))],
        scratch_shapes=[pltpu.VMEM((H, DH, DH), F32), pltpu.VMEM((H, 1, DH), F32),
                        pltpu.VMEM((H, 1, LANES), F32)],
        compiler_params=_cparams(("parallel", "arbitrary")),
        name="mlstm",
    )(gate_b, z, z, z, z, z, mnorm_g.reshape(1, W), C0, n0, m0)
    return y, C, n.reshape(B, H, DH), m[:, :, 0, 0]


PAGES_PER_STEP = 16


def _lane_extract_topn(score, n_sel, floor):
    lane = lax.broadcasted_iota(jnp.int32, score.shape, 1).astype(F32)
    sel = jnp.zeros(score.shape, F32)
    sc = score
    for _ in range(n_sel):
        m = jnp.max(sc, -1, keepdims=True)
        idx = jnp.min(jnp.where(sc == m, lane, float(score.shape[1])), -1, keepdims=True)
        hit = lane == idx
        sel = jnp.where(hit & (m > floor), 1.0, sel)
        sc = jnp.where(hit, -jnp.inf, sc)
    return sel


def _s1_kernel(pt_ref, *refs, past, S, ncp, nbp):
    pgs = refs[:PAGES_PER_STEP]
    tail_ref, q_ref, ocmp_ref, sel_ref, sub_ref = refs[PAGES_PER_STEP:]
    P = pgs[0].shape[-1]
    j = pl.program_id(1)
    nfull = pl.num_programs(1) - 1
    KVH, G, HD = NSA_KV_HEADS, NSA_GROUP, HEAD_DIM
    cols_step = PAGES_PER_STEP * P // CMP_STRIDE
    assert cols_step == LANES

    @pl.when(j == 0)
    def _():
        sub_ref[...] = jnp.zeros(sub_ref.shape, F32)

    def pool_t(n):
        return jnp.where(_div(lax.broadcasted_iota(jnp.int32, (n, LANES), 0), CMP_STRIDE)
                         == lax.broadcasted_iota(jnp.int32, (n, LANES), 1),
                         1.0 / CMP_STRIDE, 0.0).astype(BF16)

    @pl.when(j < nfull)
    def _():
        pt = pool_t(PAGES_PER_STEP * P)
        col = pl.ds(pl.multiple_of(j * cols_step, cols_step), cols_step)
        xt = jnp.concatenate([r[0, 0].reshape(2 * KVH * HD, P) for r in pgs], axis=1)
        sub_ref[:, :, :, col] = _pool_dot(pt, xt, True).reshape(2, KVH, HD, cols_step)

    @pl.when(j == nfull)
    def _():
        base = past // CMP_STRIDE
        pt = pool_t(P)
        for k in range(2):
            for h in range(KVH):
                sub_ref[k, h, :, base:base + LANES] = _pool_dot(pt, tail_ref[0, k, h], True)
        R = G * S
        row = lax.broadcasted_iota(jnp.int32, (R, 1), 0)
        pos = past + (row & (S - 1))
        posq = past + lax.broadcasted_iota(jnp.int32, (S, 1), 0)
        c_end = lax.broadcasted_iota(jnp.int32, (1, ncp), 1) * CMP_STRIDE + (2 * CMP_STRIDE - 1)
        cmask = c_end <= pos
        per = SEL_BLOCK // CMP_STRIDE
        e4 = jnp.where(_div(lax.broadcasted_iota(jnp.int32, (ncp, nbp), 0), per)
                       == lax.broadcasted_iota(jnp.int32, (ncp, nbp), 1), 1.0, 0.0).astype(F32)
        blk = lax.broadcasted_iota(jnp.int32, (1, nbp), 1)
        qblk = _div(posq, SEL_BLOCK)
        valid = blk * SEL_BLOCK <= posq
        forced = (blk == 0) | (blk == qblk) | (blk == qblk - 1)
        hs = range(KVH)
        kct = [(0.5 * (sub_ref[0, h, :, 0:ncp] + sub_ref[0, h, :, 1:ncp + 1])).astype(BF16) for h in hs]
        vct = [(0.5 * (sub_ref[1, h, :, 0:ncp] + sub_ref[1, h, :, 1:ncp + 1])).astype(BF16) for h in hs]
        scale = jnp.asarray(HD ** -0.5, BF16)
        s = [jnp.dot(q_ref[0, h] * scale, kct[h], preferred_element_type=F32) for h in hs]
        p = [_masked_softmax(s[h], cmask) for h in hs]
        oc = [lax.dot_general(p[h].astype(BF16), vct[h], NT, preferred_element_type=F32) for h in hs]
        imp = []
        for h in hs:
            t = p[h][0:S]
            for g in range(1, G):
                t = t + p[h][g * S:(g + 1) * S]
            imp.append(t)
        impb = jnp.dot(jnp.concatenate(imp, axis=0), e4, precision=HI, preferred_element_type=F32)
        scores = []
        for h in hs:
            ocmp_ref[0, h] = oc[h]
            scores.append(jnp.where(forced, MASK_BIG, jnp.where(valid, impb[h * S:(h + 1) * S], -MASK_BIG)))
        sel = _lane_extract_topn(jnp.concatenate(scores, axis=0), SEL_TOPN, -0.5 * MASK_BIG)
        for h in range(KVH):
            sel_ref[0, h] = sel[h * S:(h + 1) * S]


def _s2_kernel(pt_ref, *refs, past, S, nbp):
    pgs = refs[:PAGES_PER_STEP]
    tail_ref, q_ref, sel_ref, osel_ref, m_s, l_s, acc_s = refs[PAGES_PER_STEP:]
    P = pgs[0].shape[-1]
    j = pl.program_id(1)
    nfull = pl.num_programs(1) - 1
    KVH, G, HD = NSA_KV_HEADS, NSA_GROUP, HEAD_DIM
    R = G * S

    @pl.when(j == 0)
    def _():
        m_s[...] = jnp.full(m_s.shape, NEG, F32)
        l_s[...] = jnp.zeros(l_s.shape, F32)
        acc_s[...] = jnp.zeros(acc_s.shape, F32)

    row = lax.broadcasted_iota(jnp.int32, (R, 1), 0)
    pos = past + (row & (S - 1))

    def process(kt_of, vt_of, n, kpos0):
        kpos = kpos0 + lax.broadcasted_iota(jnp.int32, (1, n), 1)
        nblk = LANES
        assert n // SEL_BLOCK <= nblk
        blk0 = kpos0 // SEL_BLOCK
        pick = jnp.where(lax.broadcasted_iota(jnp.int32, (nbp, nblk), 0)
                         == blk0 + lax.broadcasted_iota(jnp.int32, (nbp, nblk), 1), 1.0, 0.0).astype(BF16)
        spread = jnp.where(_div(lax.broadcasted_iota(jnp.int32, (nblk, n), 1), SEL_BLOCK)
                           == lax.broadcasted_iota(jnp.int32, (nblk, n), 0), 1.0, 0.0).astype(BF16)
        causal = kpos <= pos
        hs = range(KVH)
        scale = jnp.asarray(HD ** -0.5, BF16)
        s = [jnp.dot(q_ref[0, h] * scale, kt_of(h).astype(BF16), preferred_element_type=F32) for h in hs]
        mkb = [jnp.dot(sel_ref[0, h].astype(BF16), pick, preferred_element_type=F32) for h in hs]
        mk = [jnp.dot(mkb[h].astype(BF16), spread, preferred_element_type=F32) for h in hs]
        mask = [(jnp.concatenate([mk[h]] * G, axis=0) > 0.5) & causal for h in hs]
        sm = [jnp.where(mask[h], s[h], NEG) for h in hs]
        m_old = [m_s[h] for h in hs]
        m_new = [jnp.maximum(m_old[h], jnp.max(sm[h], -1, keepdims=True)) for h in hs]
        e = [jnp.where(mask[h], jnp.exp(sm[h] - m_new[h]), 0.0) for h in hs]
        pv = [lax.dot_general(e[h].astype(BF16), vt_of(h).astype(BF16), NT, preferred_element_type=F32) for h in hs]
        for h in hs:
            alpha = jnp.exp(m_old[h] - m_new[h])
            l_s[h] = alpha * l_s[h] + jnp.sum(e[h], -1, keepdims=True)
            acc_s[h] = alpha * acc_s[h] + pv[h]
            m_s[h] = m_new[h]

    @pl.when(j < nfull)
    def _():
        process(lambda h: jnp.concatenate([r[0, 0, 0, h] for r in pgs], axis=1),
                lambda h: jnp.concatenate([r[0, 0, 1, h] for r in pgs], axis=1),
                PAGES_PER_STEP * P, j * (PAGES_PER_STEP * P))

    @pl.when(j == nfull)
    def _():
        process(lambda h: tail_ref[0, 0, h], lambda h: tail_ref[0, 1, h], P, past)
        for h in range(KVH):
            osel_ref[0, h] = acc_s[h] * (1.0 / jnp.maximum(l_s[h], 1e-30))


def _page_specs(layer, half, n_pages, P):
    def spec(r):
        return pl.BlockSpec(
            (1, 1, 2, NSA_KV_HEADS, HEAD_DIM, P),
            lambda b, j, pt, r=r: (layer, pt[b, jnp.minimum(j * PAGES_PER_STEP + r, n_pages - 1)],
                                   half, 0, 0, 0))
    return [spec(r) for r in range(PAGES_PER_STEP)]


def _nsa_sample_global(cache_t, layer, page_table, tail_t, q_s, S):
    B, n_pages = page_table.shape
    P = cache_t.shape[-1]
    KVH, G, HD = NSA_KV_HEADS, NSA_GROUP, HEAD_DIM
    past = n_pages * P
    assert n_pages % PAGES_PER_STEP == 0 and S & (S - 1) == 0 and S <= SEL_BLOCK
    nsteps = n_pages // PAGES_PER_STEP + 1
    ncp = -(-(past + P) // CMP_STRIDE // LANES) * LANES
    nbp = -(-(past + P) // SEL_BLOCK // LANES) * LANES
    R = G * S
    qspec = pl.BlockSpec((1, KVH, R, HD), lambda b, j, pt: (b, 0, 0, 0))
    o_cmp, sel = pl.pallas_call(
        functools.partial(_s1_kernel, past=past, S=S, ncp=ncp, nbp=nbp),
        out_shape=[jax.ShapeDtypeStruct((B, KVH, R, HD), F32), jax.ShapeDtypeStruct((B, KVH, S, nbp), F32)],
        grid_spec=pltpu.PrefetchScalarGridSpec(
            num_scalar_prefetch=1, grid=(B, nsteps),
            in_specs=_page_specs(layer, 0, n_pages, P)
            + [pl.BlockSpec((1, 2, KVH, HD, P), lambda b, j, pt: (b, 0, 0, 0, 0)), qspec],
            out_specs=[pl.BlockSpec((1, KVH, R, HD), lambda b, j, pt: (b, 0, 0, 0)),
                       pl.BlockSpec((1, KVH, S, nbp), lambda b, j, pt: (b, 0, 0, 0))],
            scratch_shapes=[pltpu.VMEM((2, KVH, HD, ncp + LANES), F32)]),
        compiler_params=_cparams(("parallel", "arbitrary")),
        name="nsa_sample_cmp",
    )(page_table, *([cache_t] * PAGES_PER_STEP), tail_t, q_s)
    o_sel = pl.pallas_call(
        functools.partial(_s2_kernel, past=past, S=S, nbp=nbp),
        out_shape=jax.ShapeDtypeStruct((B, KVH, R, HD), F32),
        grid_spec=pltpu.PrefetchScalarGridSpec(
            num_scalar_prefetch=1, grid=(B, nsteps),
            in_specs=_page_specs(layer, 1, n_pages, P)
            + [pl.BlockSpec((1, 2, KVH, HD, P), lambda b, j, pt: (b, 1, 0, 0, 0)), qspec,
               pl.BlockSpec((1, KVH, S, nbp), lambda b, j, pt: (b, 0, 0, 0))],
            out_specs=pl.BlockSpec((1, KVH, R, HD), lambda b, j, pt: (b, 0, 0, 0)),
            scratch_shapes=[pltpu.VMEM((KVH, R, 1), F32), pltpu.VMEM((KVH, R, 1), F32),
                            pltpu.VMEM((KVH, R, HD), F32)]),
        compiler_params=_cparams(("parallel", "arbitrary")),
        name="nsa_sample_sel",
    )(page_table, *([cache_t] * PAGES_PER_STEP), tail_t, q_s, sel)
    return o_cmp, o_sel


def _s3_kernel(q_ref, kw_ref, vw_ref, ocmp_ref, osel_ref, gt_ref, y_ref, *, past, S):
    KVH, G, HD = NSA_KV_HEADS, NSA_GROUP, HEAD_DIM
    R = G * S
    nk = kw_ref.shape[1]
    row = lax.broadcasted_iota(jnp.int32, (R, 1), 0)
    pos = past + (row & (S - 1))
    kpos = past - (nk - S) + lax.broadcasted_iota(jnp.int32, (1, nk), 1)
    diff = pos - kpos
    wmask = (diff >= 0) & (diff < WINDOW)
    sg = jax.nn.sigmoid(gt_ref[...])
    kw = kw_ref[0].astype(BF16)
    vw = vw_ref[0].astype(BF16)
    outs = []
    for h in range(KVH):
        qs = q_ref[0, h] * jnp.asarray(HD ** -0.5, BF16)
        s = lax.dot_general(qs, kw[:, h * HD:(h + 1) * HD], NT, preferred_element_type=F32)
        p = _masked_softmax(s, wmask)
        o_win = jnp.dot(p.astype(BF16), vw[:, h * HD:(h + 1) * HD], preferred_element_type=F32)
        oc = ocmp_ref[0, h]
        osl = osel_ref[0, h]
        for g in range(G):
            c0 = (h * G + g) * 3
            rs = slice(g * S, (g + 1) * S)
            outs.append(sg[:, c0:c0 + 1] * oc[rs] + sg[:, c0 + 1:c0 + 2] * osl[rs]
                        + sg[:, c0 + 2:c0 + 3] * o_win[rs])
    y_ref[...] = jnp.concatenate(outs, axis=-1)


def _nsa_sample_combine(q_s, kwin, vwin, o_cmp, o_sel, z, row0, gate_cb, past, S):
    B, KVH, R, HD = q_s.shape
    nk = kwin.shape[1]
    W = KVH * HD
    ospec = pl.BlockSpec((1, KVH, R, HD), lambda b: (b, 0, 0, 0))
    return pl.pallas_call(
        functools.partial(_s3_kernel, past=past, S=S),
        out_shape=jax.ShapeDtypeStruct((B * S, KVH * NSA_GROUP * HD), F32),
        grid=(B,),
        in_specs=[ospec,
                  pl.BlockSpec((1, nk, W), lambda b: (b, 0, 0)),
                  pl.BlockSpec((1, nk, W), lambda b: (b, 0, 0)),
                  ospec, ospec,
                  pl.BlockSpec((S, LANES), lambda b: (row0 // S + b, gate_cb))],
        out_specs=pl.BlockSpec((S, KVH * NSA_GROUP * HD), lambda b: (b, 0)),
        compiler_params=_cparams(("parallel",)),
        name="nsa_sample_win",
    )(q_s, kwin, vwin, o_cmp, o_sel, z)


def _out_proj_kernel(x_ref, ya_ref, yb_ref, yc_ref, yt_ref, w_ref, o_ref, *, n_head_tiles):
    i = pl.program_id(0)
    ka, kb = ya_ref.shape[1], yb_ref.shape[1]

    @pl.when(i < n_head_tiles)
    def _():
        acc = jnp.dot(ya_ref[...].astype(BF16), w_ref[0:ka, :], preferred_element_type=F32)
        acc = acc + jnp.dot(yb_ref[...].astype(BF16), w_ref[ka:ka + kb, :], preferred_element_type=F32)
        acc = acc + jnp.dot(yc_ref[...].astype(BF16), w_ref[ka + kb:, :], preferred_element_type=F32)
        o_ref[...] = x_ref[...] + acc

    @pl.when(i >= n_head_tiles)
    def _():
        o_ref[...] = x_ref[...] + jnp.dot(yt_ref[...].astype(BF16), w_ref[...], preferred_element_type=F32)


def _out_proj(x, ya, yb, yc, ytail, w, tm):
    T, D = x.shape
    TH = ya.shape[0]
    assert TH % tm == 0 and (T - TH) % tm == 0 and ytail.shape == (T - TH, D)
    nh = TH // tm

    def head(a):
        return pl.BlockSpec((tm, a.shape[1]), lambda i: (jnp.minimum(i, nh - 1), 0))

    return pl.pallas_call(
        functools.partial(_out_proj_kernel, n_head_tiles=nh),
        out_shape=jax.ShapeDtypeStruct((T, D), F32),
        grid=(T // tm,),
        in_specs=[pl.BlockSpec((tm, D), lambda i: (i, 0)),
                  head(ya), head(yb), head(yc),
                  pl.BlockSpec((tm, D), lambda i: (jnp.maximum(i - nh, 0), 0)),
                  pl.BlockSpec((D, D), lambda i: (0, 0))],
        out_specs=pl.BlockSpec((tm, D), lambda i: (i, 0)),
        compiler_params=_cparams(("parallel",)),
        name="out_proj",
    )(x, ya, yb, yc, ytail, w)


def _structural_pairs(k):
    return [(i, j) for i in range(k) for j in range(k) if (i + 1) * (j + 1) <= k]


RANK_MARK = 2.0 ** 126


def _extract_topk(cur_ref, top_ref, c, exact):
    NK, K = PEER_NKEYS, PEER_TOPK
    for k in range(K):
        cur = cur_ref[c]
        m = jnp.max(cur, axis=0)
        if exact:
            key = lax.broadcasted_iota(jnp.int32, cur.shape, 0).astype(F32)
            idx = jnp.min(jnp.where(cur == m[None], key, float(NK)), axis=0)
            hit = key == idx[None]
        else:
            hit = cur == m[None]
        cur_ref[c] = jnp.where(hit, -RANK_MARK * (1.0 + k / 32.0), cur)
        top_ref[c, k] = m


def _router_kernel(q_ref, wb_ref, n_out, e1_out, r2_out, e2_out, s_ref, cur_ref, top_ref, tmp_ref):
    H, NK, K = PEER_HEADS, PEER_NKEYS, PEER_TOPK
    Tt = q_ref.shape[0]
    half = q_ref.shape[1] // 2
    for c in range(2):
        s = lax.dot_general(wb_ref[c], q_ref[:, c * half:(c + 1) * half], NT, preferred_element_type=F32)
        s_ref[c] = s.reshape(NK, H, Tt)
        cur_ref[c] = s_ref[c]
        _extract_topk(cur_ref, top_ref, c, exact=False)

    marked = jnp.sum(jnp.where(cur_ref[...] < -0.5 * RANK_MARK, 1.0, 0.0), axis=1)
    tied = jnp.max(jnp.where(marked != float(K), 1.0, 0.0)) > 0.0

    @pl.when(tied)
    def _():
        for c in range(2):
            cur_ref[c] = s_ref[c]
            _extract_topk(cur_ref, top_ref, c, exact=True)

    v1 = [top_ref[0, k] for k in range(K)]
    v2 = [top_ref[1, k] for k in range(K)]

    pairs = _structural_pairs(K)
    cand = [v1[i] + v2[j] for (i, j) in pairs]
    n = len(pairs)
    rank = []
    for p in range(n):
        rank.append(jnp.zeros((H, Tt), F32))
    for p in range(n):
        ip, jp = pairs[p]
        for q in range(p + 1, n):
            iq, jq = pairs[q]
            if ip <= iq and jp <= jq:
                rank[q] = rank[q] + 1.0
            else:
                b = jnp.where(cand[p] >= cand[q], 1.0, 0.0)
                rank[q] = rank[q] + b
                rank[p] = rank[p] + (1.0 - b)
    sel = [jnp.where(r < K, 1.0, 0.0) for r in rank]
    e1 = [jnp.exp(v1[i] - v1[0]) for i in range(K)]
    e2 = [jnp.exp(v2[j] - v2[0]) for j in range(K)]
    cnt = [jnp.zeros((H, Tt), F32) for _ in range(K)]
    zsum = jnp.zeros((H, Tt), F32)
    for p, (i, j) in enumerate(pairs):
        cnt[i] = cnt[i] + sel[p]
        zsum = zsum + sel[p] * (e1[i] * e2[j])
    inv_z = 1.0 / zsum

    def rank_of(c):
        cur = cur_ref[c]
        return jnp.where(cur < -0.5 * RANK_MARK, (cur * (-1.0 / RANK_MARK) - 1.0) * 32.0, float(NK))

    rk1 = rank_of(0)
    nk = jnp.zeros((NK, H, Tt), F32)
    for i in range(K):
        nk = jnp.where(rk1 == float(i), cnt[i][None], nk)
    outs = ((n_out, nk),
            (e1_out, jnp.exp(s_ref[0] - v1[0][None]) * inv_z[None]),
            (r2_out, rank_of(1)),
            (e2_out, jnp.exp(s_ref[1] - v2[0][None])))
    for ref, val in outs:
        tmp_ref[...] = val.reshape(NK * H, Tt)
        for h in range(H):
            ref[h] = tmp_ref[pl.ds(h, NK, stride=H), :].astype(ref.dtype)


def _router(q, wb, tt):
    T = q.shape[0]
    H, NK = PEER_HEADS, PEER_NKEYS
    shp = jax.ShapeDtypeStruct((H, NK, T), F32)
    shp_b = jax.ShapeDtypeStruct((H, NK, T), BF16)
    ospec = pl.BlockSpec((H, NK, tt), lambda i: (0, 0, i))
    return pl.pallas_call(
        _router_kernel,
        out_shape=[shp, shp, shp_b, shp_b],
        grid=(T // tt,),
        in_specs=[pl.BlockSpec((tt, q.shape[1]), lambda i: (i, 0)),
                  pl.BlockSpec(wb.shape, lambda i: (0, 0, 0))],
        out_specs=[ospec, ospec, ospec, ospec],
        scratch_shapes=[pltpu.VMEM((2, NK, H, tt), F32), pltpu.VMEM((2, NK, H, tt), F32),
                        pltpu.VMEM((2, PEER_TOPK, H, tt), F32), pltpu.VMEM((NK * H, tt), F32)],
        compiler_params=_cparams(("parallel",)),
        name="peer_router",
    )(q, wb)


def _gelu_tanh(x):
    return 0.5 * x * (1.0 + jnp.tanh(np.sqrt(2.0 / np.pi) * (x + 0.044715 * (x * x * x))))


PEER_ACHUNK = 4


def _peer_dense_kernel(xn_ref, u_ref, v_ref, n_ref, e1_ref, r2_ref, e2_ref, x1_ref, g_ref, o_ref, *, final_norm):
    H, NK = PEER_HEADS, PEER_NKEYS
    j = pl.program_id(1)
    na = u_ref.shape[0] // NK
    tt = xn_ref.shape[0]

    @pl.when(j == 0)
    def _():
        o_ref[...] = x1_ref[...]

    xn = xn_ref[...]
    parts = []
    for c in range(na // PEER_ACHUNK):
        rows = slice(c * PEER_ACHUNK * NK, (c + 1) * PEER_ACHUNK * NK)
        act = lax.dot_general(u_ref[rows, :], xn, NT, preferred_element_type=F32)
        gates = []
        for a in range(c * PEER_ACHUNK, (c + 1) * PEER_ACHUNK):
            row = pl.ds(j * na + a, 1)
            gate = None
            for h in range(H):
                t = jnp.where(r2_ref[h] < n_ref[h, row, :].astype(BF16),
                              e1_ref[h, row, :].astype(BF16) * e2_ref[h], jnp.zeros((), BF16))
                gate = t if gate is None else gate + t
            gates.append(gate)
        parts.append(_gelu_tanh(act).astype(BF16) * jnp.concatenate(gates, axis=0))
    wt = jnp.concatenate(parts, axis=0)
    o_ref[...] += lax.dot_general(wt, v_ref[...], TN, preferred_element_type=F32)

    if final_norm:
        @pl.when(j == pl.num_programs(1) - 1)
        def _():
            x = o_ref[...]
            o_ref[...] = x * lax.rsqrt(jnp.mean(x * x, -1, keepdims=True) + RMS_EPS) * g_ref[...]


def _peer_dense(xn, u, v, layer, nk, e1, r2, e2, x1, g, tt, te, final_norm):
    T, D = xn.shape
    E = u.shape[1]
    H, NK = PEER_HEADS, PEER_NKEYS
    once = pl.Buffered(1)
    tab = pl.BlockSpec((H, NK, tt), lambda i, j: (0, 0, i), pipeline_mode=once)
    return pl.pallas_call(
        functools.partial(_peer_dense_kernel, final_norm=final_norm),
        out_shape=jax.ShapeDtypeStruct((T, D), F32),
        grid=(T // tt, E // te),
        in_specs=[pl.BlockSpec((tt, D), lambda i, j: (i, 0), pipeline_mode=once),
                  pl.BlockSpec((None, te, D), lambda i, j: (layer, j, 0)),
                  pl.BlockSpec((None, te, D), lambda i, j: (layer, j, 0)),
                  tab, tab, tab, tab,
                  pl.BlockSpec((tt, D), lambda i, j: (i, 0), pipeline_mode=once),
                  pl.BlockSpec((1, D), lambda i, j: (0, 0))],
        out_specs=pl.BlockSpec((tt, D), lambda i, j: (i, 0)),
        compiler_params=_cparams(("parallel", "arbitrary")),
        name="peer_dense",
    )(xn, u, v, nk, e1, r2, e2, x1, g.reshape(1, D))


T_TILE = 384
ROUTER_TILE = 128
PEER_T_TILE = 768
PAD_TILE = 768
Z_COL_TILE = 1280
OUT_TILE = 256
EXPERT_TILE = 1024


def kernel(x_prompt, x_sample, cache_nsa_kv, state_win_kv, state_conv, state_mlstm_C, state_mlstm_n,
           state_mlstm_m, page_table, norm1_g, w_in, conv_w, mlstm_gate_b, mlstm_norm_g, w_out, norm2_g,
           peer_wq, peer_subkeys, peer_u, peer_v, final_norm_g):
    BP, SP, D = x_prompt.shape
    BS, SS, _ = x_sample.shape
    depth = w_in.shape[0]
    KVH, G, HD = NSA_KV_HEADS, NSA_GROUP, HEAD_DIM
    H, DH = MLSTM_HEADS, MLSTM_DH
    CW = conv_w.shape[2]
    NW = KVH * G * HD
    KVW = 6 * KVH * HD
    NG = 3 * KVH * G
    MW = H * DH
    TP, TS = BP * SP, BS * SS
    T = TP + TS
    assert PAD_TILE % T_TILE == 0 and PAD_TILE % ROUTER_TILE == 0 and PAD_TILE % PEER_T_TILE == 0
    TPAD = -(-T // PAD_TILE) * PAD_TILE
    n_pages = page_table.shape[1]
    P = cache_nsa_kv.shape[2]
    past = n_pages * P
    wb_rows = state_win_kv.shape[2]

    c_q = 3 * CW
    c_kv = c_q + NW
    c_m = c_kv + KVW
    c_g = c_m + 4 * MW
    ZW = -(-(c_g + LANES) // Z_COL_TILE) * Z_COL_TILE
    assert CW % LANES == 0 and c_m % MW == 0 and c_g % LANES == 0 and NG + 2 * H <= LANES
    gate_cb = c_g // LANES
    gi, gf = NG, NG + H
    o_gate = 3 * CW + NW + KVW

    x = jnp.concatenate([x_prompt.reshape(TP, D), x_sample.reshape(TS, D),
                         jnp.zeros((TPAD - T, D), F32)], axis=0)
    cache_t = jnp.transpose(cache_nsa_kv, (0, 1, 3, 4, 5, 2))
    eye_h = jnp.eye(PEER_HEADS, dtype=F32)
    u_bf = peer_u.astype(BF16)
    v_bf = peer_v.astype(BF16)
    dk = peer_subkeys.shape[-1]

    p_st, s_st = [], []
    for l in range(depth):
        wi = w_in[l]
        w_perm = jnp.concatenate(
            [wi[:, :o_gate], wi[:, o_gate + NG:o_gate + NG + 4 * MW], wi[:, o_gate:o_gate + NG],
             wi[:, o_gate + NG + 4 * MW:], jnp.zeros((D, ZW - wi.shape[1]), F32)], axis=1).astype(BF16)
        z = _rms_proj(x, norm1_g[l], w_perm, PAD_TILE, Z_COL_TILE, F32)

        ya_p, conv_p = _conv(z, 0, BP, SP, jnp.zeros((BP, 2, CW), F32), conv_w[l])
        ya_s, conv_s = _conv(z, TP, BS, SS, state_conv[l], conv_w[l])

        yb_p, rows_t, win_t = _nsa_prompt(z, c_q, c_kv, gate_cb, BP, SP)
        rows_p = rows_t.transpose(0, 4, 1, 2, 3)
        win_p = win_t.transpose(0, 4, 1, 2, 3)

        zs = z[TP:T]
        q_s = zs[:, c_q:c_q + NW].reshape(BS, SS, KVH, G, HD).transpose(0, 2, 3, 1, 4)
        q_s = q_s.reshape(BS, KVH, G * SS, HD).astype(BF16)
        zkv_s = zs[:, c_kv:c_kv + KVW].reshape(BS, SS, 3, 2, KVH, HD)
        rows_s = zkv_s[:, :, :2].reshape(BS, SS, 4, KVH, HD)
        tail_t = jnp.pad(rows_s.transpose(0, 2, 3, 4, 1), ((0, 0), (0, 0), (0, 0), (0, 0), (0, P - SS)))
        o_cmp, o_sel = _nsa_sample_global(cache_t, l, page_table, tail_t, q_s, SS)
        w_all = jnp.concatenate([state_win_kv[l], zkv_s[:, :, 2]], axis=1)
        kwin = w_all[:, :, 0].reshape(BS, wb_rows + SS, KVH * HD)
        vwin = w_all[:, :, 1].reshape(BS, wb_rows + SS, KVH * HD)
        yb_s = _nsa_sample_combine(q_s, kwin, vwin, o_cmp, o_sel, z, TP, gate_cb, past, SS)
        win_s = w_all[:, SS:]

        yc_p, C_p, n_p, m_p = _mlstm(z, 0, BP, SP, c_m // MW, gate_cb, gi, gf, mlstm_gate_b[l],
                                     mlstm_norm_g[l], jnp.zeros((BP, H, DH, DH), F32),
                                     jnp.zeros((BP, H, DH), F32), jnp.zeros((BP, H), F32))
        yc_s, C_s, n_s, m_s = _mlstm(z, TP, BS, SS, c_m // MW, gate_cb, gi, gf, mlstm_gate_b[l],
                                     mlstm_norm_g[l], state_mlstm_C[l], state_mlstm_n[l], state_mlstm_m[l])

        ytail = jnp.concatenate([jnp.concatenate([ya_s, yb_s, yc_s], axis=1),
                                 jnp.zeros((TPAD - T, D), F32)], axis=0)
        x1 = _out_proj(x, ya_p, yb_p, yc_p, ytail, w_out[l].astype(BF16), OUT_TILE)

        nh = PEER_HEADS
        wq_perm = peer_wq[l].reshape(D, nh, 2, dk).transpose(0, 2, 1, 3).reshape(D, 2 * nh * dk).astype(BF16)
        q_peer, xn2 = _rms_proj(x1, norm2_g[l], wq_perm, PAD_TILE, nh * dk, BF16, emit_xn=True)
        wb = jnp.einsum('hcnd,hg->cnhgd', peer_subkeys[l], eye_h).reshape(2, PEER_NKEYS * nh, nh * dk).astype(BF16)
        nk, e1, r2, e2 = _router(q_peer, wb, ROUTER_TILE)
        x = _peer_dense(xn2, u_bf, v_bf, l, nk, e1, r2, e2, x1,
                        final_norm_g, PEER_T_TILE, EXPERT_TILE, final_norm=(l == depth - 1))

        p_st.append((rows_p, win_p, conv_p, C_p, n_p, m_p))
        s_st.append((rows_s, win_s, conv_s, C_s, n_s, m_s))

    p_rows, p_win, p_conv, p_C, p_n, p_m = [jnp.stack(a) for a in zip(*p_st)]
    s_rows, s_win, s_conv, s_C, s_n, s_m = [jnp.stack(a) for a in zip(*s_st)]
    y_prompt = x[:TP].reshape(BP, SP, D)
    y_sample = x[TP:T].reshape(BS, SS, D)
    return (y_prompt, y_sample, p_rows, p_win, p_conv, p_C, p_n, p_m, s_rows, s_win, s_conv, s_C, s_n, s_m)
```

```python
import functools

import numpy as np
import jax
import jax.numpy as jnp
from jax import lax
from jax.experimental import pallas as pl
from jax.experimental.pallas import tpu as pltpu

F32 = jnp.float32
BF16 = jnp.bfloat16
HI = lax.Precision.HIGHEST

RMS_EPS = 1e-6
HEAD_DIM = 64
NSA_KV_HEADS = 4
NSA_GROUP = 4
CMP_STRIDE = 16
SEL_BLOCK = 64
SEL_TOPN = 16
WINDOW = 512
MLSTM_HEADS = 4
MLSTM_DH = 128
MLSTM_CHUNK = 64
PEER_HEADS = 8
PEER_NKEYS = 128
PEER_TOPK = 16
MASK_BIG = 1e9
NEG = -1e30

LANES = 128
VMEM_LIMIT = 56 * 1024 * 1024

NT = (((1,), (1,)), ((), ()))
TN = (((0,), (0,)), ((), ()))


def _cparams(sem):
    return pltpu.CompilerParams(dimension_semantics=sem, vmem_limit_bytes=VMEM_LIMIT)


def _div(x, d):
    assert d & (d - 1) == 0
    return lax.shift_right_arithmetic(x, jnp.int32(d.bit_length() - 1))


def _pool_dot(pool, x, x_is_lhs):
    hi = x.astype(BF16)
    lo = (x - hi.astype(F32)).astype(BF16)
    if x_is_lhs:
        return (jnp.dot(hi, pool, preferred_element_type=F32) + jnp.dot(lo, pool, preferred_element_type=F32))
    return (jnp.dot(pool, hi, preferred_element_type=F32) + jnp.dot(pool, lo, preferred_element_type=F32))


def _masked_softmax(s, mask, exp=jnp.exp):
    s = jnp.where(mask, s, NEG)
    m = jnp.max(s, -1, keepdims=True)
    e = jnp.where(mask, exp(s - m), 0.0)
    d = jnp.maximum(jnp.sum(e, -1, keepdims=True), 1e-30)
    return e * (1.0 / d)


def _rms_proj_kernel(x_ref, g_ref, w_ref, *rest, emit_xn):
    if emit_xn:
        o_ref, xo_ref, xn_ref = rest
    else:
        o_ref, xn_ref = rest

    @pl.when(pl.program_id(1) == 0)
    def _():
        x = x_ref[...]
        r = x * lax.rsqrt(jnp.mean(x * x, -1, keepdims=True) + RMS_EPS)
        xn = (r * g_ref[...]).astype(BF16)
        xn_ref[...] = xn
        if emit_xn:
            xo_ref[...] = xn

    o_ref[...] = jnp.dot(xn_ref[...], w_ref[...], preferred_element_type=F32).astype(o_ref.dtype)


def _rms_proj(x, g, w, tm, tn, out_dtype, emit_xn=False):
    T, D = x.shape
    N = w.shape[1]
    out_shape = [jax.ShapeDtypeStruct((T, N), out_dtype)]
    out_specs = [pl.BlockSpec((tm, tn), lambda i, j: (i, j))]
    if emit_xn:
        out_shape.append(jax.ShapeDtypeStruct((T, D), BF16))
        out_specs.append(pl.BlockSpec((tm, D), lambda i, j: (i, 0)))
    res = pl.pallas_call(
        functools.partial(_rms_proj_kernel, emit_xn=emit_xn),
        out_shape=out_shape,
        grid=(T // tm, N // tn),
        in_specs=[pl.BlockSpec((tm, D), lambda i, j: (i, 0)),
                  pl.BlockSpec((1, D), lambda i, j: (0, 0)),
                  pl.BlockSpec((D, tn), lambda i, j: (0, j))],
        out_specs=out_specs,
        scratch_shapes=[pltpu.VMEM((tm, D), BF16)],
        compiler_params=_cparams(("parallel", "arbitrary")),
        name="rms_proj",
    )(x, g.reshape(1, D), w)
    return res if emit_xn else res[0]


def _conv_kernel(cb_ref, cc_ref, ch_ref, buf_ref, w_ref, y_ref, new_ref, ext_ref, *, S):
    C = cb_ref.shape[-1]
    u = cc_ref[...] * ch_ref[...]
    ext_ref[0:8, :] = jnp.zeros((8, C), F32)
    ext_ref[6:8, :] = buf_ref[0]
    ext_ref[8:8 + S, :] = u
    w = w_ref[...]
    y = w[0:1] * ext_ref[6:6 + S, :] + w[1:2] * ext_ref[7:7 + S, :] + w[2:3] * u
    y_ref[...] = cb_ref[...] * y
    new_ref[0] = u[S - 2:S]


def _conv(z, row0, B, S, buf, w):
    C = w.shape[1]
    rb0 = row0 // S
    return pl.pallas_call(
        functools.partial(_conv_kernel, S=S),
        out_shape=[jax.ShapeDtypeStruct((B * S, C), F32), jax.ShapeDtypeStruct((B, 2, C), F32)],
        grid=(B,),
        in_specs=[pl.BlockSpec((S, C), lambda b: (rb0 + b, 0)),
                  pl.BlockSpec((S, C), lambda b: (rb0 + b, 1)),
                  pl.BlockSpec((S, C), lambda b: (rb0 + b, 2)),
                  pl.BlockSpec((1, 2, C), lambda b: (b, 0, 0)),
                  pl.BlockSpec((3, C), lambda b: (0, 0))],
        out_specs=[pl.BlockSpec((S, C), lambda b: (b, 0)),
                   pl.BlockSpec((1, 2, C), lambda b: (b, 0, 0))],
        scratch_shapes=[pltpu.VMEM((S + 8, C), F32)],
        compiler_params=_cparams(("parallel",)),
        name="short_conv",
    )(z, z, z, buf, w)


def _topn_rank_select_t(score_t, n_sel):
    NB = score_t.shape[0]
    blk = lax.broadcasted_iota(jnp.int32, score_t.shape, 0)
    rank = jnp.zeros(score_t.shape, F32)
    for i in range(NB):
        ci = score_t[i:i + 1, :]
        beats = jnp.where(ci > score_t, 1.0, jnp.where((ci == score_t) & (blk > i), 1.0, 0.0))
        rank = rank + beats
    return rank < n_sel


SEL_KCHUNK = 512


def _exp2_softmax_pv(pieces, hd):
    m = None
    for s, _ in pieces:
        mm = jnp.max(s, -1, keepdims=True)
        m = mm if m is None else jnp.maximum(m, mm)
    oa = None
    for s, v in pieces:
        t = jnp.dot(jnp.exp2(s - m).astype(BF16), v, preferred_element_type=F32)
        oa = t if oa is None else oa + t
    return oa[:, 0:hd] * (1.0 / oa[:, hd:hd + 1])


NSA_HPS = 2
QK_WIDTH = 128


def _nsa_prompt_kernel(q_ref, kcs_ref, vcs_ref, kss_ref, vss_ref, kws_ref, vws_ref, gt_ref,
                       y_ref, rows_ref, win_ref, kc_ref, vc_ref, ks_ref, vs_ref, kw_ref, vw_ref, osel_s, *, S, tq):
    pair = pl.program_id(1)
    qi = pl.program_id(2)
    G, HD = NSA_GROUP, HEAD_DIM
    NC = S // CMP_STRIDE
    NB = S // SEL_BLOCK
    n_sel = min(SEL_TOPN, NB)
    span = min(WINDOW + tq, S)

    @pl.when(qi == 0)
    def _():
        j = lax.broadcasted_iota(jnp.int32, (NC, S), 0)
        r = lax.broadcasted_iota(jnp.int32, (NC, S), 1)
        lo = j * CMP_STRIDE
        pool = jnp.where((r >= lo) & (r < lo + 2 * CMP_STRIDE), 0.5 / CMP_STRIDE, 0.0).astype(BF16)
        kcp = _pool_dot(pool, kcs_ref[...], False).astype(BF16)
        vcp = _pool_dot(pool, vcs_ref[...], False).astype(BF16)
        blk1h = jnp.where(_div(lax.broadcasted_iota(jnp.int32, (S, NB), 0), SEL_BLOCK)
                          == lax.broadcasted_iota(jnp.int32, (S, NB), 1), 1.0, 0.0).astype(BF16)
        kpad = jnp.zeros((S, QK_WIDTH - HD - NB), BF16)
        ones = jnp.where(lax.broadcasted_iota(jnp.int32, (S, LANES - HD), 1) == 0, 1.0, 0.0).astype(BF16)
        for hh in range(NSA_HPS):
            cols = slice(hh * HD, (hh + 1) * HD)
            kc_ref[hh] = kcp[:, cols]
            vc_ref[hh] = vcp[:, cols]
            ks_ref[hh] = jnp.concatenate([kss_ref[:, cols].astype(BF16), blk1h, kpad], axis=1)
            vs_ref[hh] = jnp.concatenate([vss_ref[:, cols].astype(BF16), ones], axis=1)
            kw_ref[hh] = kws_ref[:, cols].astype(BF16)
            vw_ref[hh] = jnp.concatenate([vws_ref[:, cols].astype(BF16), ones], axis=1)
        for kind, src in enumerate((kcs_ref, vcs_ref, kss_ref, vss_ref)):
            rows_ref[0, kind] = src[...].T.reshape(NSA_HPS, HD, S)
        wn = win_ref.shape[-1]
        for kind, src in enumerate((kws_ref, vws_ref)):
            win_ref[0, kind] = src[S - wn:S, :].T.reshape(NSA_HPS, HD, wn)

    t0 = qi * tq
    row = lax.broadcasted_iota(jnp.int32, (G * tq, 1), 0)
    pos = t0 + (row & (tq - 1))
    posq = t0 + lax.broadcasted_iota(jnp.int32, (tq, 1), 0)
    c_end = lax.broadcasted_iota(jnp.int32, (1, NC), 1) * CMP_STRIDE + (2 * CMP_STRIDE - 1)
    cmask = c_end <= pos
    per = SEL_BLOCK // CMP_STRIDE
    e4t = jnp.where(_div(lax.broadcasted_iota(jnp.int32, (NB, NC), 1), per)
                    == lax.broadcasted_iota(jnp.int32, (NB, NC), 0), 1.0, 0.0).astype(F32)
    eye_nb = jnp.where(lax.broadcasted_iota(jnp.int32, (NB, NB), 0)
                       == lax.broadcasted_iota(jnp.int32, (NB, NB), 1), 1.0, 0.0).astype(BF16)
    blk_t = lax.broadcasted_iota(jnp.int32, (NB, 1), 0)
    posq_t = t0 + lax.broadcasted_iota(jnp.int32, (1, tq), 1)
    qblk_t = _div(posq_t, SEL_BLOCK)
    valid_t = blk_t * SEL_BLOCK <= posq_t
    forced_t = (blk_t == 0) | (blk_t == qblk_t) | (blk_t == qblk_t - 1)

    qscale = (HD ** -0.5) * np.log2(np.e)
    qss, qas, o_cmps = [], [], []
    for hh in range(NSA_HPS):
        qh = q_ref[:, hh * G * HD:(hh + 1) * G * HD] * qscale
        qs = jnp.concatenate([qh[:, g * HD:(g + 1) * HD] for g in range(G)], axis=0).astype(BF16)
        s = lax.dot_general(qs, kc_ref[hh], NT, preferred_element_type=F32)
        p = _masked_softmax(s, cmask, jnp.exp2)
        o_cmps.append(jnp.dot(p.astype(BF16), vc_ref[hh], preferred_element_type=F32))
        imp = p[0:tq]
        for g in range(1, G):
            imp = imp + p[g * tq:(g + 1) * tq]
        impb_t = lax.dot_general(e4t, imp, NT, precision=HI, preferred_element_type=F32)
        score_t = jnp.where(forced_t, MASK_BIG, jnp.where(valid_t, impb_t, -MASK_BIG))
        sel_t = _topn_rank_select_t(score_t, n_sel) & (score_t > -0.5 * MASK_BIG)
        selneg_t = jnp.where(sel_t, 0.0, NEG).astype(BF16)
        selneg = lax.dot_general(selneg_t, eye_nb, TN, preferred_element_type=F32).astype(BF16)
        qas.append(jnp.concatenate([qs, jnp.concatenate([selneg] * G, axis=0),
                                    jnp.zeros((G * tq, QK_WIDTH - HD - NB), BF16)], axis=1))
        qss.append(qs)

    def sel_branch(klen):
        tail = min(klen, SEL_KCHUNK)
        kpos = (klen - tail) + lax.broadcasted_iota(jnp.int32, (1, tail), 1)
        tail_bias = jnp.where(kpos <= posq, 0.0, NEG)
        def scores(hh, g):
            qa = qas[hh][g * tq:(g + 1) * tq]
            pieces = []
            if klen > tail:
                pieces.append((lax.dot_general(qa, ks_ref[hh, 0:klen - tail, :], NT,
                                               preferred_element_type=F32),
                               vs_ref[hh, 0:klen - tail, :]))
            pieces.append((lax.dot_general(qa, ks_ref[hh, klen - tail:klen, :], NT,
                                           preferred_element_type=F32) + tail_bias,
                           vs_ref[hh, klen - tail:klen, :]))
            return pieces

        items = [(hh, g) for hh in range(NSA_HPS) for g in range(G)]
        nxt = scores(*items[0])
        for i, (hh, g) in enumerate(items):
            cur = nxt
            if i + 1 < len(items):
                nxt = scores(*items[i + 1])
            osel_s[hh, g * tq:(g + 1) * tq, :] = _exp2_softmax_pv(cur, HD)

    nvar = -(-S // SEL_KCHUNK)
    per_var = SEL_KCHUNK // tq
    for v in range(nvar):
        @pl.when(qi // per_var == v)
        def _(v=v):
            sel_branch(min(S, (v + 1) * SEL_KCHUNK))

    start = pl.multiple_of(jnp.maximum(t0 + tq - span, 0), tq)
    diff = posq - (start + lax.broadcasted_iota(jnp.int32, (1, span), 1))
    win_bias = jnp.where((diff >= 0) & (diff < WINDOW), 0.0, NEG)

    sg = jax.nn.sigmoid(gt_ref[...])
    g12s, kws, vws = [], [], []
    for hh in range(NSA_HPS):
        kvh = pair * NSA_HPS + hh
        pick = jnp.where(lax.broadcasted_iota(jnp.int32, (LANES, LANES), 0)
                         == lax.broadcasted_iota(jnp.int32, (LANES, LANES), 1) + kvh * (3 * G),
                         1.0, 0.0).astype(F32)
        g12s.append(jnp.dot(sg, pick, precision=HI, preferred_element_type=F32))
        kws.append(kw_ref[hh, pl.ds(start, span), :])
        vws.append(vw_ref[hh, pl.ds(start, span), :])

    def win_scores(hh, g):
        return lax.dot_general(qss[hh][g * tq:(g + 1) * tq], kws[hh], NT, preferred_element_type=F32) + win_bias

    items = [(hh, g) for hh in range(NSA_HPS) for g in range(G)]
    outs = []
    nxt = win_scores(*items[0])
    for i, (hh, g) in enumerate(items):
        cur = nxt
        if i + 1 < len(items):
            nxt = win_scores(*items[i + 1])
        rows = slice(g * tq, (g + 1) * tq)
        o_win = _exp2_softmax_pv([(cur, vws[hh])], HD)
        g12 = g12s[hh]
        outs.append(g12[:, 3 * g:3 * g + 1] * o_cmps[hh][rows]
                    + g12[:, 3 * g + 1:3 * g + 2] * osel_s[hh, rows, :]
                    + g12[:, 3 * g + 2:3 * g + 3] * o_win)
    y_ref[...] = jnp.concatenate(outs, axis=-1)


def _nsa_prompt(z, c_q, c_kv, gate_cb, B, S, tq=256):
    nq = S // tq
    KVH, G, HD = NSA_KV_HEADS, NSA_GROUP, HEAD_DIM
    NC = S // CMP_STRIDE
    qw = NSA_HPS * G * HD
    assert NSA_HPS * HD == LANES and c_q % qw == 0 and c_kv % LANES == 0

    def kv_spec(i):
        cb = (c_kv + i * KVH * HD) // LANES
        return pl.BlockSpec((S, LANES), lambda b, p, t, cb=cb: (b, cb + p))

    head_kv = pltpu.VMEM((NSA_HPS, S, HD), BF16)
    head_aug = pltpu.VMEM((NSA_HPS, S, LANES), BF16)
    assert HD + S // SEL_BLOCK <= LANES
    wn = min(WINDOW, S)
    return pl.pallas_call(
        functools.partial(_nsa_prompt_kernel, S=S, tq=tq),
        out_shape=[jax.ShapeDtypeStruct((B * S, KVH * G * HD), F32),
                   jax.ShapeDtypeStruct((B, 4, KVH, HD, S), F32),
                   jax.ShapeDtypeStruct((B, 2, KVH, HD, wn), F32)],
        grid=(B, KVH // NSA_HPS, nq),
        in_specs=[pl.BlockSpec((tq, qw), lambda b, p, t: (b * nq + t, c_q // qw + p)),
                  kv_spec(0), kv_spec(1), kv_spec(2), kv_spec(3), kv_spec(4), kv_spec(5),
                  pl.BlockSpec((tq, LANES), lambda b, p, t: (b * nq + t, gate_cb))],
        out_specs=[pl.BlockSpec((tq, qw), lambda b, p, t: (b * nq + t, p)),
                   pl.BlockSpec((1, 4, NSA_HPS, HD, S), lambda b, p, t: (b, 0, p, 0, 0)),
                   pl.BlockSpec((1, 2, NSA_HPS, HD, wn), lambda b, p, t: (b, 0, p, 0, 0))],
        scratch_shapes=[pltpu.VMEM((NSA_HPS, NC, HD), BF16), pltpu.VMEM((NSA_HPS, NC, HD), BF16),
                        pltpu.VMEM((NSA_HPS, S, QK_WIDTH), BF16), head_aug, head_kv, head_aug,
                        pltpu.VMEM((NSA_HPS, G * tq, HD), F32)],
        compiler_params=_cparams(("parallel", "parallel", "arbitrary")),
        name="nsa_prompt",
    )(z, z, z, z, z, z, z, z)


def _log_sigmoid(x):
    return jnp.minimum(x, 0.0) - jnp.log(1.0 + jnp.exp(-jnp.abs(x)))


def _mlstm_kernel(gb_ref, q_ref, k_ref, v_ref, o_ref, gt_ref, mg_ref, c0_ref, n0_ref, m0_ref,
                  y_ref, cn_ref, nn_ref, mn_ref, c_s, n_s, m_s, *, L, Lb, gi, gf):
    H, DH = MLSTM_HEADS, MLSTM_DH
    c = pl.program_id(1)

    @pl.when(c == 0)
    def _():
        c_s[...] = c0_ref[0]
        n_s[...] = n0_ref[0]
        m_s[...] = m0_ref[0]

    def padrows(a):
        if Lb == L:
            return a
        return jnp.concatenate([a, jnp.zeros((L - Lb, a.shape[1]), a.dtype)], axis=0)

    lane = lax.broadcasted_iota(jnp.int32, (1, LANES), 1)
    bias = jnp.zeros((1, LANES), F32)
    for h in range(H):
        bias = bias + jnp.where(lane == gi + h, gb_ref[0, h], 0.0) + jnp.where(lane == gf + h, gb_ref[1, h], 0.0)
    is_f = (lane >= gf) & (lane < gf + H)
    pre = padrows(gt_ref[...]) + bias
    gate = jnp.where(is_f, _log_sigmoid(pre), pre)
    if Lb != L:
        live = lax.broadcasted_iota(jnp.int32, (L, 1), 0) < Lb
        gate = jnp.where(live, gate, jnp.where(is_f, 0.0, NEG))
    rr = lax.broadcasted_iota(jnp.int32, (L, L), 0)
    cc = lax.broadcasted_iota(jnp.int32, (L, L), 1)
    tril = rr >= cc
    bcum = jnp.dot(jnp.where(tril, 1.0, 0.0).astype(F32), gate, precision=HI, preferred_element_type=F32)
    e8 = jnp.where(lax.broadcasted_iota(jnp.int32, (8, LANES), 1)
                   == lax.broadcasted_iota(jnp.int32, (8, LANES), 0) + gi, 1.0, 0.0).astype(F32)
    rg = lax.dot_general(e8, gate, NT, precision=HI, preferred_element_type=F32)
    rb = lax.dot_general(e8, bcum, NT, precision=HI, preferred_element_type=F32)

    q = padrows(q_ref[...])
    k = padrows(k_ref[...])
    v = padrows(v_ref[...])
    og = padrows(o_ref[...])
    hs = range(H)
    sls = [slice(h * DH, (h + 1) * DH) for h in hs]
    qq = [q[:, sl] for sl in sls]
    kk = [k[:, sl] * (DH ** -0.5) for sl in sls]
    qb = [x.astype(BF16) for x in qq]
    kb = [x.astype(BF16) for x in kk]
    vb = [v[:, sl].astype(BF16) for sl in sls]
    b_col = [bcum[:, gf + h:gf + h + 1] for h in hs]
    i_col = [gate[:, gi + h:gi + h + 1] for h in hs]
    m_prev = [m_s[h][:, 0:1] for h in hs]
    cmat = [c_s[h] for h in hs]
    n_row = [n_s[h] for h in hs]
    qk = [lax.dot_general(qb[h], kb[h], NT, preferred_element_type=F32) for h in hs]
    qc = [jnp.dot(qb[h], cmat[h].astype(BF16), preferred_element_type=F32) for h in hs]
    dmat = [jnp.where(tril, b_col[h] - rb[H + h:H + h + 1, :] + rg[h:h + 1, :], NEG) for h in hs]
    inter = [b_col[h] + m_prev[h] for h in hs]
    mt = [jnp.maximum(inter[h], jnp.max(dmat[h], -1, keepdims=True)) for h in hs]
    wqk = [jnp.exp(dmat[h] - mt[h]) * qk[h] for h in hs]
    a = [jnp.exp(inter[h] - mt[h]) for h in hs]
    wv = [jnp.dot(wqk[h].astype(BF16), vb[h], preferred_element_type=F32) for h in hs]
    m_new = [mt[h][L - 1:L] for h in hs]
    wl = [jnp.exp(b_col[h][L - 1:L] - b_col[h] + i_col[h] - m_new[h]) for h in hs]
    decay = [jnp.exp(b_col[h][L - 1:L] + m_prev[h] - m_new[h]) for h in hs]
    kw = [wl[h] * kk[h] for h in hs]
    kv = [lax.dot_general(kw[h].astype(BF16), vb[h], TN, preferred_element_type=F32) for h in hs]
    ys = []
    for h in hs:
        num = a[h] * qc[h] + wv[h]
        den = a[h] * jnp.sum(qq[h] * n_row[h], -1, keepdims=True) + jnp.sum(wqk[h], -1, keepdims=True)
        hh = num * (1.0 / jnp.maximum(jnp.abs(den), jnp.exp(-mt[h])))
        c_s[h] = decay[h] * cmat[h] + kv[h]
        n_s[h] = decay[h] * n_row[h] + jnp.sum(kw[h], 0, keepdims=True)
        m_s[h] = jnp.broadcast_to(m_new[h], (1, LANES))
        hn = hh * lax.rsqrt(jnp.mean(hh * hh, -1, keepdims=True) + RMS_EPS)
        ys.append(jax.nn.sigmoid(og[:, sls[h]]) * hn * mg_ref[:, sls[h]])
    y = jnp.concatenate(ys, axis=-1)
    y_ref[...] = y[0:Lb]

    @pl.when(c == pl.num_programs(1) - 1)
    def _():
        cn_ref[0] = c_s[...]
        nn_ref[0] = n_s[...]
        mn_ref[0] = m_s[...]


def _mlstm(z, row0, B, S, col_q, gate_cb, gi, gf, gate_b, mnorm_g, C0, n0, m0):
    H, DH = MLSTM_HEADS, MLSTM_DH
    W = H * DH
    L = MLSTM_CHUNK
    Lb = L if S % L == 0 else S
    assert Lb <= L
    nc = S // Lb
    rb0 = row0 // Lb
    n0 = n0.reshape(B, H, 1, DH)
    m0 = jnp.broadcast_to(m0.reshape(B, H, 1, 1), (B, H, 1, LANES))

    def zspec(cb, width):
        return pl.BlockSpec((Lb, width), lambda b, c, cb=cb: (rb0 + b * nc + c, cb))

    y, C, n, m = pl.pallas_call(
        functools.partial(_mlstm_kernel, L=L, Lb=Lb, gi=gi, gf=gf),
        out_shape=[jax.ShapeDtypeStruct((B * S, W), F32),
                   jax.ShapeDtypeStruct((B, H, DH, DH), F32),
                   jax.ShapeDtypeStruct((B, H, 1, DH), F32),
                   jax.ShapeDtypeStruct((B, H, 1, LANES), F32)],
        grid=(B, nc),
        in_specs=[pl.BlockSpec(memory_space=pltpu.SMEM),
                  zspec(col_q, W), zspec(col_q + 1, W), zspec(col_q + 2, W), zspec(col_q + 3, W),
                  zspec(gate_cb, LANES),
                  pl.BlockSpec((1, W), lambda b, c: (0, 0)),
                  pl.BlockSpec((1, H, DH, DH), lambda b, c: (b, 0, 0, 0)),
                  pl.BlockSpec((1, H, 1, DH), lambda b, c: (b, 0, 0, 0)),
                  pl.BlockSpec((1, H, 1, LANES), lambda b, c: (b, 0, 0, 0))],
        out_specs=[pl.BlockSpec((Lb, W), lambda b, c: (b * nc + c, 0)),
                   pl.BlockSpec((1, H, DH, DH), lambda b, c: (b, 0, 0, 0)),
                   pl.BlockSpec((1, H, 1, DH), lambda b, c: (b, 0, 0, 0)),
                   pl.BlockSpec((1, H, 1, LANES), lambda b, c: (b, 0, 0, 0))],
        scratch_shapes=[pltpu.VMEM((H, DH, DH), F32), pltpu.VMEM((H, 1, DH), F32),
                        pltpu.VMEM((H, 1, LANES), F32)],
        compiler_params=_cparams(("parallel", "arbitrary")),
        name="mlstm",
    )(gate_b, z, z, z, z, z, mnorm_g.reshape(1, W), C0, n0, m0)
    return y, C, n.reshape(B, H, DH), m[:, :, 0, 0]


PAGES_PER_STEP = 16


def _lane_extract_topn(score, n_sel, floor):
    lane = lax.broadcasted_iota(jnp.int32, score.shape, 1).astype(F32)
    sel = jnp.zeros(score.shape, F32)
    sc = score
    for _ in range(n_sel):
        m = jnp.max(sc, -1, keepdims=True)
        idx = jnp.min(jnp.where(sc == m, lane, float(score.shape[1])), -1, keepdims=True)
        hit = lane == idx
        sel = jnp.where(hit & (m > floor), 1.0, sel)
        sc = jnp.where(hit, -jnp.inf, sc)
    return sel


def _s1_kernel(pt_ref, *refs, past, S, ncp, nbp):
    pgs = refs[:PAGES_PER_STEP]
    tail_ref, q_ref, ocmp_ref, sel_ref, sub_ref = refs[PAGES_PER_STEP:]
    P = pgs[0].shape[-1]
    j = pl.program_id(1)
    nfull = pl.num_programs(1) - 1
    KVH, G, HD = NSA_KV_HEADS, NSA_GROUP, HEAD_DIM
    cols_step = PAGES_PER_STEP * P // CMP_STRIDE
    assert cols_step == LANES

    @pl.when(j == 0)
    def _():
        sub_ref[...] = jnp.zeros(sub_ref.shape, F32)

    def pool_t(n):
        return jnp.where(_div(lax.broadcasted_iota(jnp.int32, (n, LANES), 0), CMP_STRIDE)
                         == lax.broadcasted_iota(jnp.int32, (n, LANES), 1),
                         1.0 / CMP_STRIDE, 0.0).astype(BF16)

    @pl.when(j < nfull)
    def _():
        pt = pool_t(PAGES_PER_STEP * P)
        col = pl.ds(pl.multiple_of(j * cols_step, cols_step), cols_step)
        xt = jnp.concatenate([r[0, 0].reshape(2 * KVH * HD, P) for r in pgs], axis=1)
        sub_ref[:, :, :, col] = _pool_dot(pt, xt, True).reshape(2, KVH, HD, cols_step)

    @pl.when(j == nfull)
    def _():
        base = past // CMP_STRIDE
        pt = pool_t(P)
        for k in range(2):
            for h in range(KVH):
                sub_ref[k, h, :, base:base + LANES] = _pool_dot(pt, tail_ref[0, k, h], True)
        R = G * S
        row = lax.broadcasted_iota(jnp.int32, (R, 1), 0)
        pos = past + (row & (S - 1))
        posq = past + lax.broadcasted_iota(jnp.int32, (S, 1), 0)
        c_end = lax.broadcasted_iota(jnp.int32, (1, ncp), 1) * CMP_STRIDE + (2 * CMP_STRIDE - 1)
        cmask = c_end <= pos
        per = SEL_BLOCK // CMP_STRIDE
        e4 = jnp.where(_div(lax.broadcasted_iota(jnp.int32, (ncp, nbp), 0), per)
                       == lax.broadcasted_iota(jnp.int32, (ncp, nbp), 1), 1.0, 0.0).astype(F32)
        blk = lax.broadcasted_iota(jnp.int32, (1, nbp), 1)
        qblk = _div(posq, SEL_BLOCK)
        valid = blk * SEL_BLOCK <= posq
        forced = (blk == 0) | (blk == qblk) | (blk == qblk - 1)
        hs = range(KVH)
        kct = [(0.5 * (sub_ref[0, h, :, 0:ncp] + sub_ref[0, h, :, 1:ncp + 1])).astype(BF16) for h in hs]
        vct = [(0.5 * (sub_ref[1, h, :, 0:ncp] + sub_ref[1, h, :, 1:ncp + 1])).astype(BF16) for h in hs]
        scale = jnp.asarray(HD ** -0.5, BF16)
        s = [jnp.dot(q_ref[0, h] * scale, kct[h], preferred_element_type=F32) for h in hs]
        p = [_masked_softmax(s[h], cmask) for h in hs]
        oc = [lax.dot_general(p[h].astype(BF16), vct[h], NT, preferred_element_type=F32) for h in hs]
        imp = []
        for h in hs:
            t = p[h][0:S]
            for g in range(1, G):
                t = t + p[h][g * S:(g + 1) * S]
            imp.append(t)
        impb = jnp.dot(jnp.concatenate(imp, axis=0), e4, precision=HI, preferred_element_type=F32)
        scores = []
        for h in hs:
            ocmp_ref[0, h] = oc[h]
            scores.append(jnp.where(forced, MASK_BIG, jnp.where(valid, impb[h * S:(h + 1) * S], -MASK_BIG)))
        sel = _lane_extract_topn(jnp.concatenate(scores, axis=0), SEL_TOPN, -0.5 * MASK_BIG)
        for h in range(KVH):
            sel_ref[0, h] = sel[h * S:(h + 1) * S]


def _s2_kernel(pt_ref, *refs, past, S, nbp):
    pgs = refs[:PAGES_PER_STEP]
    tail_ref, q_ref, sel_ref, osel_ref, m_s, l_s, acc_s = refs[PAGES_PER_STEP:]
    P = pgs[0].shape[-1]
    j = pl.program_id(1)
    nfull = pl.num_programs(1) - 1
    KVH, G, HD = NSA_KV_HEADS, NSA_GROUP, HEAD_DIM
    R = G * S

    @pl.when(j == 0)
    def _():
        m_s[...] = jnp.full(m_s.shape, NEG, F32)
        l_s[...] = jnp.zeros(l_s.shape, F32)
        acc_s[...] = jnp.zeros(acc_s.shape, F32)

    row = lax.broadcasted_iota(jnp.int32, (R, 1), 0)
    pos = past + (row & (S - 1))

    def process(kt_of, vt_of, n, kpos0):
        kpos = kpos0 + lax.broadcasted_iota(jnp.int32, (1, n), 1)
        nblk = LANES
        assert n // SEL_BLOCK <= nblk
        blk0 = kpos0 // SEL_BLOCK
        pick = jnp.where(lax.broadcasted_iota(jnp.int32, (nbp, nblk), 0)
                         == blk0 + lax.broadcasted_iota(jnp.int32, (nbp, nblk), 1), 1.0, 0.0).astype(BF16)
        spread = jnp.where(_div(lax.broadcasted_iota(jnp.int32, (nblk, n), 1), SEL_BLOCK)
                           == lax.broadcasted_iota(jnp.int32, (nblk, n), 0), 1.0, 0.0).astype(BF16)
        causal = kpos <= pos
        hs = range(KVH)
        scale = jnp.asarray(HD ** -0.5, BF16)
        s = [jnp.dot(q_ref[0, h] * scale, kt_of(h).astype(BF16), preferred_element_type=F32) for h in hs]
        mkb = [jnp.dot(sel_ref[0, h].astype(BF16), pick, preferred_element_type=F32) for h in hs]
        mk = [jnp.dot(mkb[h].astype(BF16), spread, preferred_element_type=F32) for h in hs]
        mask = [(jnp.concatenate([mk[h]] * G, axis=0) > 0.5) & causal for h in hs]
        sm = [jnp.where(mask[h], s[h], NEG) for h in hs]
        m_old = [m_s[h] for h in hs]
        m_new = [jnp.maximum(m_old[h], jnp.max(sm[h], -1, keepdims=True)) for h in hs]
        e = [jnp.where(mask[h], jnp.exp(sm[h] - m_new[h]), 0.0) for h in hs]
        pv = [lax.dot_general(e[h].astype(BF16), vt_of(h).astype(BF16), NT, preferred_element_type=F32) for h in hs]
        for h in hs:
            alpha = jnp.exp(m_old[h] - m_new[h])
            l_s[h] = alpha * l_s[h] + jnp.sum(e[h], -1, keepdims=True)
            acc_s[h] = alpha * acc_s[h] + pv[h]
            m_s[h] = m_new[h]

    @pl.when(j < nfull)
    def _():
        process(lambda h: jnp.concatenate([r[0, 0, 0, h] for r in pgs], axis=1),
                lambda h: jnp.concatenate([r[0, 0, 1, h] for r in pgs], axis=1),
                PAGES_PER_STEP * P, j * (PAGES_PER_STEP * P))

    @pl.when(j == nfull)
    def _():
        process(lambda h: tail_ref[0, 0, h], lambda h: tail_ref[0, 1, h], P, past)
        for h in range(KVH):
            osel_ref[0, h] = acc_s[h] * (1.0 / jnp.maximum(l_s[h], 1e-30))


def _page_specs(layer, half, n_pages, P):
    def spec(r):
        return pl.BlockSpec(
            (1, 1, 2, NSA_KV_HEADS, HEAD_DIM, P),
            lambda b, j, pt, r=r: (layer, pt[b, jnp.minimum(j * PAGES_PER_STEP + r, n_pages - 1)],
                                   half, 0, 0, 0))
    return [spec(r) for r in range(PAGES_PER_STEP)]


def _nsa_sample_global(cache_t, layer, page_table, tail_t, q_s, S):
    B, n_pages = page_table.shape
    P = cache_t.shape[-1]
    KVH, G, HD = NSA_KV_HEADS, NSA_GROUP, HEAD_DIM
    past = n_pages * P
    assert n_pages % PAGES_PER_STEP == 0 and S & (S - 1) == 0 and S <= SEL_BLOCK
    nsteps = n_pages // PAGES_PER_STEP + 1
    ncp = -(-(past + P) // CMP_STRIDE // LANES) * LANES
    nbp = -(-(past + P) // SEL_BLOCK // LANES) * LANES
    R = G * S
    qspec = pl.BlockSpec((1, KVH, R, HD), lambda b, j, pt: (b, 0, 0, 0))
    o_cmp, sel = pl.pallas_call(
        functools.partial(_s1_kernel, past=past, S=S, ncp=ncp, nbp=nbp),
        out_shape=[jax.ShapeDtypeStruct((B, KVH, R, HD), F32), jax.ShapeDtypeStruct((B, KVH, S, nbp), F32)],
        grid_spec=pltpu.PrefetchScalarGridSpec(
            num_scalar_prefetch=1, grid=(B, nsteps),
            in_specs=_page_specs(layer, 0, n_pages, P)
            + [pl.BlockSpec((1, 2, KVH, HD, P), lambda b, j, pt: (b, 0, 0, 0, 0)), qspec],
            out_specs=[pl.BlockSpec((1, KVH, R, HD), lambda b, j, pt: (b, 0, 0, 0)),
                       pl.BlockSpec((1, KVH, S, nbp), lambda b, j, pt: (b, 0, 0, 0))],
            scratch_shapes=[pltpu.VMEM((2, KVH, HD, ncp + LANES), F32)]),
        compiler_params=_cparams(("parallel", "arbitrary")),
        name="nsa_sample_cmp",
    )(page_table, *([cache_t] * PAGES_PER_STEP), tail_t, q_s)
    o_sel = pl.pallas_call(
        functools.partial(_s2_kernel, past=past, S=S, nbp=nbp),
        out_shape=jax.ShapeDtypeStruct((B, KVH, R, HD), F32),
        grid_spec=pltpu.PrefetchScalarGridSpec(
            num_scalar_prefetch=1, grid=(B, nsteps),
            in_specs=_page_specs(layer, 1, n_pages, P)
            + [pl.BlockSpec((1, 2, KVH, HD, P), lambda b, j, pt: (b, 1, 0, 0, 0)), qspec,
               pl.BlockSpec((1, KVH, S, nbp), lambda b, j, pt: (b, 0, 0, 0))],
            out_specs=pl.BlockSpec((1, KVH, R, HD), lambda b, j, pt: (b, 0, 0, 0)),
            scratch_shapes=[pltpu.VMEM((KVH, R, 1), F32), pltpu.VMEM((KVH, R, 1), F32),
                            pltpu.VMEM((KVH, R, HD), F32)]),
        compiler_params=_cparams(("parallel", "arbitrary")),
        name="nsa_sample_sel",
    )(page_table, *([cache_t] * PAGES_PER_STEP), tail_t, q_s, sel)
    return o_cmp, o_sel


def _s3_kernel(q_ref, kw_ref, vw_ref, ocmp_ref, osel_ref, gt_ref, y_ref, *, past, S):
    KVH, G, HD = NSA_KV_HEADS, NSA_GROUP, HEAD_DIM
    R = G * S
    nk = kw_ref.shape[1]
    row = lax.broadcasted_iota(jnp.int32, (R, 1), 0)
    pos = past + (row & (S - 1))
    kpos = past - (nk - S) + lax.broadcasted_iota(jnp.int32, (1, nk), 1)
    diff = pos - kpos
    wmask = (diff >= 0) & (diff < WINDOW)
    sg = jax.nn.sigmoid(gt_ref[...])
    kw = kw_ref[0].astype(BF16)
    vw = vw_ref[0].astype(BF16)
    outs = []
    for h in range(KVH):
        qs = q_ref[0, h] * jnp.asarray(HD ** -0.5, BF16)
        s = lax.dot_general(qs, kw[:, h * HD:(h + 1) * HD], NT, preferred_element_type=F32)
        p = _masked_softmax(s, wmask)
        o_win = jnp.dot(p.astype(BF16), vw[:, h * HD:(h + 1) * HD], preferred_element_type=F32)
        oc = ocmp_ref[0, h]
        osl = osel_ref[0, h]
        for g in range(G):
            c0 = (h * G + g) * 3
            rs = slice(g * S, (g + 1) * S)
            outs.append(sg[:, c0:c0 + 1] * oc[rs] + sg[:, c0 + 1:c0 + 2] * osl[rs]
                        + sg[:, c0 + 2:c0 + 3] * o_win[rs])
    y_ref[...] = jnp.concatenate(outs, axis=-1)


def _nsa_sample_combine(q_s, kwin, vwin, o_cmp, o_sel, z, row0, gate_cb, past, S):
    B, KVH, R, HD = q_s.shape
    nk = kwin.shape[1]
    W = KVH * HD
    ospec = pl.BlockSpec((1, KVH, R, HD), lambda b: (b, 0, 0, 0))
    return pl.pallas_call(
        functools.partial(_s3_kernel, past=past, S=S),
        out_shape=jax.ShapeDtypeStruct((B * S, KVH * NSA_GROUP * HD), F32),
        grid=(B,),
        in_specs=[ospec,
                  pl.BlockSpec((1, nk, W), lambda b: (b, 0, 0)),
                  pl.BlockSpec((1, nk, W), lambda b: (b, 0, 0)),
                  ospec, ospec,
                  pl.BlockSpec((S, LANES), lambda b: (row0 // S + b, gate_cb))],
        out_specs=pl.BlockSpec((S, KVH * NSA_GROUP * HD), lambda b: (b, 0)),
        compiler_params=_cparams(("parallel",)),
        name="nsa_sample_win",
    )(q_s, kwin, vwin, o_cmp, o_sel, z)


def _out_proj_kernel(x_ref, ya_ref, yb_ref, yc_ref, yt_ref, w_ref, o_ref, *, n_head_tiles):
    i = pl.program_id(0)
    ka, kb = ya_ref.shape[1], yb_ref.shape[1]

    @pl.when(i < n_head_tiles)
    def _():
        acc = jnp.dot(ya_ref[...].astype(BF16), w_ref[0:ka, :], preferred_element_type=F32)
        acc = acc + jnp.dot(yb_ref[...].astype(BF16), w_ref[ka:ka + kb, :], preferred_element_type=F32)
        acc = acc + jnp.dot(yc_ref[...].astype(BF16), w_ref[ka + kb:, :], preferred_element_type=F32)
        o_ref[...] = x_ref[...] + acc

    @pl.when(i >= n_head_tiles)
    def _():
        o_ref[...] = x_ref[...] + jnp.dot(yt_ref[...].astype(BF16), w_ref[...], preferred_element_type=F32)


def _out_proj(x, ya, yb, yc, ytail, w, tm):
    T, D = x.shape
    TH = ya.shape[0]
    assert TH % tm == 0 and (T - TH) % tm == 0 and ytail.shape == (T - TH, D)
    nh = TH // tm

    def head(a):
        return pl.BlockSpec((tm, a.shape[1]), lambda i: (jnp.minimum(i, nh - 1), 0))

    return pl.pallas_call(
        functools.partial(_out_proj_kernel, n_head_tiles=nh),
        out_shape=jax.ShapeDtypeStruct((T, D), F32),
        grid=(T // tm,),
        in_specs=[pl.BlockSpec((tm, D), lambda i: (i, 0)),
                  head(ya), head(yb), head(yc),
                  pl.BlockSpec((tm, D), lambda i: (jnp.maximum(i - nh, 0), 0)),
                  pl.BlockSpec((D, D), lambda i: (0, 0))],
        out_specs=pl.BlockSpec((tm, D), lambda i: (i, 0)),
        compiler_params=_cparams(("parallel",)),
        name="out_proj",
    )(x, ya, yb, yc, ytail, w)


def _structural_pairs(k):
    return [(i, j) for i in range(k) for j in range(k) if (i + 1) * (j + 1) <= k]


RANK_MARK = 2.0 ** 126


def _extract_topk(cur_ref, top_ref, c, exact):
    NK, K = PEER_NKEYS, PEER_TOPK
    for k in range(K):
        cur = cur_ref[c]
        m = jnp.max(cur, axis=0)
        if exact:
            key = lax.broadcasted_iota(jnp.int32, cur.shape, 0).astype(F32)
            idx = jnp.min(jnp.where(cur == m[None], key, float(NK)), axis=0)
            hit = key == idx[None]
        else:
            hit = cur == m[None]
        cur_ref[c] = jnp.where(hit, -RANK_MARK * (1.0 + k / 32.0), cur)
        top_ref[c, k] = m


def _router_kernel(q_ref, wb_ref, n_out, e1_out, r2_out, e2_out, s_ref, cur_ref, top_ref, tmp_ref):
    H, NK, K = PEER_HEADS, PEER_NKEYS, PEER_TOPK
    Tt = q_ref.shape[0]
    half = q_ref.shape[1] // 2
    for c in range(2):
        s = lax.dot_general(wb_ref[c], q_ref[:, c * half:(c + 1) * half], NT, preferred_element_type=F32)
        s_ref[c] = s.reshape(NK, H, Tt)
        cur_ref[c] = s_ref[c]
        _extract_topk(cur_ref, top_ref, c, exact=False)

    marked = jnp.sum(jnp.where(cur_ref[...] < -0.5 * RANK_MARK, 1.0, 0.0), axis=1)
    tied = jnp.max(jnp.where(marked != float(K), 1.0, 0.0)) > 0.0

    @pl.when(tied)
    def _():
        for c in range(2):
            cur_ref[c] = s_ref[c]
            _extract_topk(cur_ref, top_ref, c, exact=True)

    v1 = [top_ref[0, k] for k in range(K)]
    v2 = [top_ref[1, k] for k in range(K)]

    pairs = _structural_pairs(K)
    cand = [v1[i] + v2[j] for (i, j) in pairs]
    n = len(pairs)
    rank = []
    for p in range(n):
        rank.append(jnp.zeros((H, Tt), F32))
    for p in range(n):
        ip, jp = pairs[p]
        for q in range(p + 1, n):
            iq, jq = pairs[q]
            if ip <= iq and jp <= jq:
                rank[q] = rank[q] + 1.0
            else:
                b = jnp.where(cand[p] >= cand[q], 1.0, 0.0)
                rank[q] = rank[q] + b
                rank[p] = rank[p] + (1.0 - b)
    sel = [jnp.where(r < K, 1.0, 0.0) for r in rank]
    e1 = [jnp.exp(v1[i] - v1[0]) for i in range(K)]
    e2 = [jnp.exp(v2[j] - v2[0]) for j in range(K)]
    cnt = [jnp.zeros((H, Tt), F32) for _ in range(K)]
    zsum = jnp.zeros((H, Tt), F32)
    for p, (i, j) in enumerate(pairs):
        cnt[i] = cnt[i] + sel[p]
        zsum = zsum + sel[p] * (e1[i] * e2[j])
    inv_z = 1.0 / zsum

    def rank_of(c):
        cur = cur_ref[c]
        return jnp.where(cur < -0.5 * RANK_MARK, (cur * (-1.0 / RANK_MARK) - 1.0) * 32.0, float(NK))

    rk1 = rank_of(0)
    nk = jnp.zeros((NK, H, Tt), F32)
    for i in range(K):
        nk = jnp.where(rk1 == float(i), cnt[i][None], nk)
    outs = ((n_out, nk),
            (e1_out, jnp.exp(s_ref[0] - v1[0][None]) * inv_z[None]),
            (r2_out, rank_of(1)),
            (e2_out, jnp.exp(s_ref[1] - v2[0][None])))
    for ref, val in outs:
        tmp_ref[...] = val.reshape(NK * H, Tt)
        for h in range(H):
            ref[h] = tmp_ref[pl.ds(h, NK, stride=H), :].astype(ref.dtype)


def _router(q, wb, tt):
    T = q.shape[0]
    H, NK = PEER_HEADS, PEER_NKEYS
    shp = jax.ShapeDtypeStruct((H, NK, T), F32)
    shp_b = jax.ShapeDtypeStruct((H, NK, T), BF16)
    ospec = pl.BlockSpec((H, NK, tt), lambda i: (0, 0, i))
    return pl.pallas_call(
        _router_kernel,
        out_shape=[shp, shp, shp_b, shp_b],
        grid=(T // tt,),
        in_specs=[pl.BlockSpec((tt, q.shape[1]), lambda i: (i, 0)),
                  pl.BlockSpec(wb.shape, lambda i: (0, 0, 0))],
        out_specs=[ospec, ospec, ospec, ospec],
        scratch_shapes=[pltpu.VMEM((2, NK, H, tt), F32), pltpu.VMEM((2, NK, H, tt), F32),
                        pltpu.VMEM((2, PEER_TOPK, H, tt), F32), pltpu.VMEM((NK * H, tt), F32)],
        compiler_params=_cparams(("parallel",)),
        name="peer_router",
    )(q, wb)


def _gelu_tanh(x):
    return 0.5 * x * (1.0 + jnp.tanh(np.sqrt(2.0 / np.pi) * (x + 0.044715 * (x * x * x))))


PEER_ACHUNK = 4


def _peer_dense_kernel(xn_ref, u_ref, v_ref, n_ref, e1_ref, r2_ref, e2_ref, x1_ref, g_ref, o_ref, *, final_norm):
    H, NK = PEER_HEADS, PEER_NKEYS
    j = pl.program_id(1)
    na = u_ref.shape[0] // NK
    tt = xn_ref.shape[0]

    @pl.when(j == 0)
    def _():
        o_ref[...] = x1_ref[...]

    xn = xn_ref[...]
    parts = []
    for c in range(na // PEER_ACHUNK):
        rows = slice(c * PEER_ACHUNK * NK, (c + 1) * PEER_ACHUNK * NK)
        act = lax.dot_general(u_ref[rows, :], xn, NT, preferred_element_type=F32)
        gates = []
        for a in range(c * PEER_ACHUNK, (c + 1) * PEER_ACHUNK):
            row = pl.ds(j * na + a, 1)
            gate = None
            for h in range(H):
                t = jnp.where(r2_ref[h] < n_ref[h, row, :].astype(BF16),
                              e1_ref[h, row, :].astype(BF16) * e2_ref[h], jnp.zeros((), BF16))
                gate = t if gate is None else gate + t
            gates.append(gate)
        parts.append(_gelu_tanh(act).astype(BF16) * jnp.concatenate(gates, axis=0))
    wt = jnp.concatenate(parts, axis=0)
    o_ref[...] += lax.dot_general(wt, v_ref[...], TN, preferred_element_type=F32)

    if final_norm:
        @pl.when(j == pl.num_programs(1) - 1)
        def _():
            x = o_ref[...]
            o_ref[...] = x * lax.rsqrt(jnp.mean(x * x, -1, keepdims=True) + RMS_EPS) * g_ref[...]


def _peer_dense(xn, u, v, layer, nk, e1, r2, e2, x1, g, tt, te, final_norm):
    T, D = xn.shape
    E = u.shape[1]
    H, NK = PEER_HEADS, PEER_NKEYS
    once = pl.Buffered(1)
    tab = pl.BlockSpec((H, NK, tt), lambda i, j: (0, 0, i), pipeline_mode=once)
    return pl.pallas_call(
        functools.partial(_peer_dense_kernel, final_norm=final_norm),
        out_shape=jax.ShapeDtypeStruct((T, D), F32),
        grid=(T // tt, E // te),
        in_specs=[pl.BlockSpec((tt, D), lambda i, j: (i, 0), pipeline_mode=once),
                  pl.BlockSpec((None, te, D), lambda i, j: (layer, j, 0)),
                  pl.BlockSpec((None, te, D), lambda i, j: (layer, j, 0)),
                  tab, tab, tab, tab,
                  pl.BlockSpec((tt, D), lambda i, j: (i, 0), pipeline_mode=once),
                  pl.BlockSpec((1, D), lambda i, j: (0, 0))],
        out_specs=pl.BlockSpec((tt, D), lambda i, j: (i, 0)),
        compiler_params=_cparams(("parallel", "arbitrary")),
        name="peer_dense",
    )(xn, u, v, nk, e1, r2, e2, x1, g.reshape(1, D))


T_TILE = 384
ROUTER_TILE = 128
PEER_T_TILE = 768
PAD_TILE = 768
Z_COL_TILE = 1280
OUT_TILE = 256
EXPERT_TILE = 1024


def kernel(x_prompt, x_sample, cache_nsa_kv, state_win_kv, state_conv, state_mlstm_C, state_mlstm_n,
           state_mlstm_m, page_table, norm1_g, w_in, conv_w, mlstm_gate_b, mlstm_norm_g, w_out, norm2_g,
           peer_wq, peer_subkeys, peer_u, peer_v, final_norm_g):
    BP, SP, D = x_prompt.shape
    BS, SS, _ = x_sample.shape
    depth = w_in.shape[0]
    KVH, G, HD = NSA_KV_HEADS, NSA_GROUP, HEAD_DIM
    H, DH = MLSTM_HEADS, MLSTM_DH
    CW = conv_w.shape[2]
    NW = KVH * G * HD
    KVW = 6 * KVH * HD
    NG = 3 * KVH * G
    MW = H * DH
    TP, TS = BP * SP, BS * SS
    T = TP + TS
    assert PAD_TILE % T_TILE == 0 and PAD_TILE % ROUTER_TILE == 0 and PAD_TILE % PEER_T_TILE == 0
    TPAD = -(-T // PAD_TILE) * PAD_TILE
    n_pages = page_table.shape[1]
    P = cache_nsa_kv.shape[2]
    past = n_pages * P
    wb_rows = state_win_kv.shape[2]

    c_q = 3 * CW
    c_kv = c_q + NW
    c_m = c_kv + KVW
    c_g = c_m + 4 * MW
    ZW = -(-(c_g + LANES) // Z_COL_TILE) * Z_COL_TILE
    assert CW % LANES == 0 and c_m % MW == 0 and c_g % LANES == 0 and NG + 2 * H <= LANES
    gate_cb = c_g // LANES
    gi, gf = NG, NG + H
    o_gate = 3 * CW + NW + KVW

    x = jnp.concatenate([x_prompt.reshape(TP, D), x_sample.reshape(TS, D),
                         jnp.zeros((TPAD - T, D), F32)], axis=0)
    cache_t = jnp.transpose(cache_nsa_kv, (0, 1, 3, 4, 5, 2))
    eye_h = jnp.eye(PEER_HEADS, dtype=F32)
    u_bf = peer_u.astype(BF16)
    v_bf = peer_v.astype(BF16)
    dk = peer_subkeys.shape[-1]

    p_st, s_st = [], []
    for l in range(depth):
        wi = w_in[l]
        w_perm = jnp.concatenate(
            [wi[:, :o_gate], wi[:, o_gate + NG:o_gate + NG + 4 * MW], wi[:, o_gate:o_gate + NG],
             wi[:, o_gate + NG + 4 * MW:], jnp.zeros((D, ZW - wi.shape[1]), F32)], axis=1).astype(BF16)
        z = _rms_proj(x, norm1_g[l], w_perm, PAD_TILE, Z_COL_TILE, F32)

        ya_p, conv_p = _conv(z, 0, BP, SP, jnp.zeros((BP, 2, CW), F32), conv_w[l])
        ya_s, conv_s = _conv(z, TP, BS, SS, state_conv[l], conv_w[l])

        yb_p, rows_t, win_t = _nsa_prompt(z, c_q, c_kv, gate_cb, BP, SP)
        rows_p = rows_t.transpose(0, 4, 1, 2, 3)
        win_p = win_t.transpose(0, 4, 1, 2, 3)

        zs = z[TP:T]
        q_s = zs[:, c_q:c_q + NW].reshape(BS, SS, KVH, G, HD).transpose(0, 2, 3, 1, 4)
        q_s = q_s.reshape(BS, KVH, G * SS, HD).astype(BF16)
        zkv_s = zs[:, c_kv:c_kv + KVW].reshape(BS, SS, 3, 2, KVH, HD)
        rows_s = zkv_s[:, :, :2].reshape(BS, SS, 4, KVH, HD)
        tail_t = jnp.pad(rows_s.transpose(0, 2, 3, 4, 1), ((0, 0), (0, 0), (0, 0), (0, 0), (0, P - SS)))
        o_cmp, o_sel = _nsa_sample_global(cache_t, l, page_table, tail_t, q_s, SS)
        w_all = jnp.concatenate([state_win_kv[l], zkv_s[:, :, 2]], axis=1)
        kwin = w_all[:, :, 0].reshape(BS, wb_rows + SS, KVH * HD)
        vwin = w_all[:, :, 1].reshape(BS, wb_rows + SS, KVH * HD)
        yb_s = _nsa_sample_combine(q_s, kwin, vwin, o_cmp, o_sel, z, TP, gate_cb, past, SS)
        win_s = w_all[:, SS:]

        yc_p, C_p, n_p, m_p = _mlstm(z, 0, BP, SP, c_m // MW, gate_cb, gi, gf, mlstm_gate_b[l],
                                     mlstm_norm_g[l], jnp.zeros((BP, H, DH, DH), F32),
                                     jnp.zeros((BP, H, DH), F32), jnp.zeros((BP, H), F32))
        yc_s, C_s, n_s, m_s = _mlstm(z, TP, BS, SS, c_m // MW, gate_cb, gi, gf, mlstm_gate_b[l],
                                     mlstm_norm_g[l], state_mlstm_C[l], state_mlstm_n[l], state_mlstm_m[l])

        ytail = jnp.concatenate([jnp.concatenate([ya_s, yb_s, yc_s], axis=1),
                                 jnp.zeros((TPAD - T, D), F32)], axis=0)
        x1 = _out_proj(x, ya_p, yb_p, yc_p, ytail, w_out[l].astype(BF16), OUT_TILE)

        nh = PEER_HEADS
        wq_perm = peer_wq[l].reshape(D, nh, 2, dk).transpose(0, 2, 1, 3).reshape(D, 2 * nh * dk).astype(BF16)
        q_peer, xn2 = _rms_proj(x1, norm2_g[l], wq_perm, PAD_TILE, nh * dk, BF16, emit_xn=True)
        wb = jnp.einsum('hcnd,hg->cnhgd', peer_subkeys[l], eye_h).reshape(2, PEER_NKEYS * nh, nh * dk).astype(BF16)
        nk, e1, r2, e2 = _router(q_peer, wb, ROUTER_TILE)
        x = _peer_dense(xn2, u_bf, v_bf, l, nk, e1, r2, e2, x1,
                        final_norm_g, PEER_T_TILE, EXPERT_TILE, final_norm=(l == depth - 1))

        p_st.append((rows_p, win_p, conv_p, C_p, n_p, m_p))
        s_st.append((rows_s, win_s, conv_s, C_s, n_s, m_s))

    p_rows, p_win, p_conv, p_C, p_n, p_m = [jnp.stack(a) for a in zip(*p_st)]
    s_rows, s_win, s_conv, s_C, s_n, s_m = [jnp.stack(a) for a in zip(*s_st)]
    y_prompt = x[:TP].reshape(BP, SP, D)
    y_sample = x[TP:T].reshape(BS, SS, D)
    return (y_prompt, y_sample, p_rows, p_win, p_conv, p_C, p_n, p_m, s_rows, s_win, s_conv, s_C, s_n, s_m)
```

```python
import functools

import numpy as np
import jax
import jax.numpy as jnp
from jax import lax
from jax.experimental import pallas as pl
from jax.experimental.pallas import tpu as pltpu

F32 = jnp.float32
BF16 = jnp.bfloat16
HI = lax.Precision.HIGHEST

RMS_EPS = 1e-6
HEAD_DIM = 64
NSA_KV_HEADS = 4
NSA_GROUP = 4
CMP_STRIDE = 16
SEL_BLOCK = 64
SEL_TOPN = 16
WINDOW = 512
MLSTM_HEADS = 4
MLSTM_DH = 128
MLSTM_CHUNK = 64
PEER_HEADS = 8
PEER_NKEYS = 128
PEER_TOPK = 16
MASK_BIG = 1e9
NEG = -1e30

LANES = 128
VMEM_LIMIT = 56 * 1024 * 1024

NT = (((1,), (1,)), ((), ()))
TN = (((0,), (0,)), ((), ()))


def _cparams(sem):
    return pltpu.CompilerParams(dimension_semantics=sem, vmem_limit_bytes=VMEM_LIMIT)


def _div(x, d):
    assert d & (d - 1) == 0
    return lax.shift_right_arithmetic(x, jnp.int32(d.bit_length() - 1))


def _pool_dot(pool, x, x_is_lhs):
    hi = x.astype(BF16)
    lo = (x - hi.astype(F32)).astype(BF16)
    if x_is_lhs:
        return (jnp.dot(hi, pool, preferred_element_type=F32) + jnp.dot(lo, pool, preferred_element_type=F32))
    return (jnp.dot(pool, hi, preferred_element_type=F32) + jnp.dot(pool, lo, preferred_element_type=F32))


def _masked_softmax(s, mask, exp=jnp.exp):
    s = jnp.where(mask, s, NEG)
    m = jnp.max(s, -1, keepdims=True)
    e = jnp.where(mask, exp(s - m), 0.0)
    d = jnp.maximum(jnp.sum(e, -1, keepdims=True), 1e-30)
    return e * (1.0 / d)


def _rms_proj_kernel(x_ref, g_ref, w_ref, *rest, emit_xn):
    if emit_xn:
        o_ref, xo_ref, xn_ref = rest
    else:
        o_ref, xn_ref = rest

    @pl.when(pl.program_id(1) == 0)
    def _():
        x = x_ref[...]
        r = x * lax.rsqrt(jnp.mean(x * x, -1, keepdims=True) + RMS_EPS)
        xn = (r * g_ref[...]).astype(BF16)
        xn_ref[...] = xn
        if emit_xn:
            xo_ref[...] = xn

    o_ref[...] = jnp.dot(xn_ref[...], w_ref[...], preferred_element_type=F32).astype(o_ref.dtype)


def _rms_proj(x, g, w, tm, tn, out_dtype, emit_xn=False):
    T, D = x.shape
    N = w.shape[1]
    out_shape = [jax.ShapeDtypeStruct((T, N), out_dtype)]
    out_specs = [pl.BlockSpec((tm, tn), lambda i, j: (i, j))]
    if emit_xn:
        out_shape.append(jax.ShapeDtypeStruct((T, D), BF16))
        out_specs.append(pl.BlockSpec((tm, D), lambda i, j: (i, 0)))
    res = pl.pallas_call(
        functools.partial(_rms_proj_kernel, emit_xn=emit_xn),
        out_shape=out_shape,
        grid=(T // tm, N // tn),
        in_specs=[pl.BlockSpec((tm, D), lambda i, j: (i, 0)),
                  pl.BlockSpec((1, D), lambda i, j: (0, 0)),
                  pl.BlockSpec((D, tn), lambda i, j: (0, j))],
        out_specs=out_specs,
        scratch_shapes=[pltpu.VMEM((tm, D), BF16)],
        compiler_params=_cparams(("parallel", "arbitrary")),
        name="rms_proj",
    )(x, g.reshape(1, D), w)
    return res if emit_xn else res[0]


def _conv_kernel(cb_ref, cc_ref, ch_ref, buf_ref, w_ref, y_ref, new_ref, ext_ref, *, S):
    C = cb_ref.shape[-1]
    u = cc_ref[...] * ch_ref[...]
    ext_ref[0:8, :] = jnp.zeros((8, C), F32)
    ext_ref[6:8, :] = buf_ref[0]
    ext_ref[8:8 + S, :] = u
    w = w_ref[...]
    y = w[0:1] * ext_ref[6:6 + S, :] + w[1:2] * ext_ref[7:7 + S, :] + w[2:3] * u
    y_ref[...] = cb_ref[...] * y
    new_ref[0] = u[S - 2:S]


def _conv(z, row0, B, S, buf, w):
    C = w.shape[1]
    rb0 = row0 // S
    return pl.pallas_call(
        functools.partial(_conv_kernel, S=S),
        out_shape=[jax.ShapeDtypeStruct((B * S, C), F32), jax.ShapeDtypeStruct((B, 2, C), F32)],
        grid=(B,),
        in_specs=[pl.BlockSpec((S, C), lambda b: (rb0 + b, 0)),
                  pl.BlockSpec((S, C), lambda b: (rb0 + b, 1)),
                  pl.BlockSpec((S, C), lambda b: (rb0 + b, 2)),
                  pl.BlockSpec((1, 2, C), lambda b: (b, 0, 0)),
                  pl.BlockSpec((3, C), lambda b: (0, 0))],
        out_specs=[pl.BlockSpec((S, C), lambda b: (b, 0)),
                   pl.BlockSpec((1, 2, C), lambda b: (b, 0, 0))],
        scratch_shapes=[pltpu.VMEM((S + 8, C), F32)],
        compiler_params=_cparams(("parallel",)),
        name="short_conv",
    )(z, z, z, buf, w)


def _topn_rank_select_t(score_t, n_sel):
    NB = score_t.shape[0]
    blk = lax.broadcasted_iota(jnp.int32, score_t.shape, 0)
    rank = jnp.zeros(score_t.shape, F32)
    for i in range(NB):
        ci = score_t[i:i + 1, :]
        beats = jnp.where(ci > score_t, 1.0, jnp.where((ci == score_t) & (blk > i), 1.0, 0.0))
        rank = rank + beats
    return rank < n_sel


SEL_KCHUNK = 512


def _exp2_softmax_pv(pieces, hd):
    m = None
    for s, _ in pieces:
        mm = jnp.max(s, -1, keepdims=True)
        m = mm if m is None else jnp.maximum(m, mm)
    oa = None
    for s, v in pieces:
        t = jnp.dot(jnp.exp2(s - m).astype(BF16), v, preferred_element_type=F32)
        oa = t if oa is None else oa + t
    return oa[:, 0:hd] * (1.0 / oa[:, hd:hd + 1])


NSA_HPS = 2
QK_WIDTH = 128


def _nsa_prompt_kernel(q_ref, kcs_ref, vcs_ref, kss_ref, vss_ref, kws_ref, vws_ref, gt_ref,
                       y_ref, rows_ref, win_ref, kc_ref, vc_ref, ks_ref, vs_ref, kw_ref, vw_ref, osel_s, *, S, tq):
    pair = pl.program_id(1)
    qi = pl.program_id(2)
    G, HD = NSA_GROUP, HEAD_DIM
    NC = S // CMP_STRIDE
    NB = S // SEL_BLOCK
    n_sel = min(SEL_TOPN, NB)
    span = min(WINDOW + tq, S)

    @pl.when(qi == 0)
    def _():
        j = lax.broadcasted_iota(jnp.int32, (NC, S), 0)
        r = lax.broadcasted_iota(jnp.int32, (NC, S), 1)
        lo = j * CMP_STRIDE
        pool = jnp.where((r >= lo) & (r < lo + 2 * CMP_STRIDE), 0.5 / CMP_STRIDE, 0.0).astype(BF16)
        kcp = _pool_dot(pool, kcs_ref[...], False).astype(BF16)
        vcp = _pool_dot(pool, vcs_ref[...], False).astype(BF16)
        blk1h = jnp.where(_div(lax.broadcasted_iota(jnp.int32, (S, NB), 0), SEL_BLOCK)
                          == lax.broadcasted_iota(jnp.int32, (S, NB), 1), 1.0, 0.0).astype(BF16)
        kpad = jnp.zeros((S, QK_WIDTH - HD - NB), BF16)
        ones = jnp.where(lax.broadcasted_iota(jnp.int32, (S, LANES - HD), 1) == 0, 1.0, 0.0).astype(BF16)
        for hh in range(NSA_HPS):
            cols = slice(hh * HD, (hh + 1) * HD)
            kc_ref[hh] = kcp[:, cols]
            vc_ref[hh] = vcp[:, cols]
            ks_ref[hh] = jnp.concatenate([kss_ref[:, cols].astype(BF16), blk1h, kpad], axis=1)
            vs_ref[hh] = jnp.concatenate([vss_ref[:, cols].astype(BF16), ones], axis=1)
            kw_ref[hh] = kws_ref[:, cols].astype(BF16)
            vw_ref[hh] = jnp.concatenate([vws_ref[:, cols].astype(BF16), ones], axis=1)
        for kind, src in enumerate((kcs_ref, vcs_ref, kss_ref, vss_ref)):
            rows_ref[0, kind] = src[...].T.reshape(NSA_HPS, HD, S)
        wn = win_ref.shape[-1]
        for kind, src in enumerate((kws_ref, vws_ref)):
            win_ref[0, kind] = src[S - wn:S, :].T.reshape(NSA_HPS, HD, wn)

    t0 = qi * tq
    row = lax.broadcasted_iota(jnp.int32, (G * tq, 1), 0)
    pos = t0 + (row & (tq - 1))
    posq = t0 + lax.broadcasted_iota(jnp.int32, (tq, 1), 0)
    c_end = lax.broadcasted_iota(jnp.int32, (1, NC), 1) * CMP_STRIDE + (2 * CMP_STRIDE - 1)
    cmask = c_end <= pos
    per = SEL_BLOCK // CMP_STRIDE
    e4t = jnp.where(_div(lax.broadcasted_iota(jnp.int32, (NB, NC), 1), per)
                    == lax.broadcasted_iota(jnp.int32, (NB, NC), 0), 1.0, 0.0).astype(F32)
    eye_nb = jnp.where(lax.broadcasted_iota(jnp.int32, (NB, NB), 0)
                       == lax.broadcasted_iota(jnp.int32, (NB, NB), 1), 1.0, 0.0).astype(BF16)
    blk_t = lax.broadcasted_iota(jnp.int32, (NB, 1), 0)
    posq_t = t0 + lax.broadcasted_iota(jnp.int32, (1, tq), 1)
    qblk_t = _div(posq_t, SEL_BLOCK)
    valid_t = blk_t * SEL_BLOCK <= posq_t
    forced_t = (blk_t == 0) | (blk_t == qblk_t) | (blk_t == qblk_t - 1)

    qscale = (HD ** -0.5) * np.log2(np.e)
    hps = range(NSA_HPS)
    qss = []
    for hh in hps:
        qh = q_ref[:, hh * G * HD:(hh + 1) * G * HD] * qscale
        qss.append(jnp.concatenate([qh[:, g * HD:(g + 1) * HD] for g in range(G)], axis=0).astype(BF16))
    s_c = [lax.dot_general(qss[hh], kc_ref[hh], NT, preferred_element_type=F32) for hh in hps]
    p_c = [_masked_softmax(s_c[hh], cmask, jnp.exp2) for hh in hps]
    o_cmps = [jnp.dot(p_c[hh].astype(BF16), vc_ref[hh], preferred_element_type=F32) for hh in hps]
    imps = []
    for hh in hps:
        imp = p_c[hh][0:tq]
        for g in range(1, G):
            imp = imp + p_c[hh][g * tq:(g + 1) * tq]
        imps.append(imp)
    impb_t = [lax.dot_general(e4t, imps[hh], NT, precision=HI, preferred_element_type=F32) for hh in hps]
    qas = []
    for hh in hps:
        score_t = jnp.where(forced_t, MASK_BIG, jnp.where(valid_t, impb_t[hh], -MASK_BIG))
        sel_t = _topn_rank_select_t(score_t, n_sel) & (score_t > -0.5 * MASK_BIG)
        selneg_t = jnp.where(sel_t, 0.0, NEG).astype(BF16)
        selneg = lax.dot_general(selneg_t, eye_nb, TN, preferred_element_type=F32).astype(BF16)
        qas.append(jnp.concatenate([qss[hh], jnp.concatenate([selneg] * G, axis=0),
                                    jnp.zeros((G * tq, QK_WIDTH - HD - NB), BF16)], axis=1))

    def sel_branch(klen):
        tail = min(klen, SEL_KCHUNK)
        kpos = (klen - tail) + lax.broadcasted_iota(jnp.int32, (1, tail), 1)
        tail_bias = jnp.where(kpos <= posq, 0.0, NEG)
        chunks = [(lo, lo + SEL_KCHUNK, False) for lo in range(0, klen - tail, SEL_KCHUNK)] + [(klen - tail, klen, True)]

        def chunk_scores(hh, g, c):
            lo, hi, is_tail = c
            s = lax.dot_general(qas[hh][g * tq:(g + 1) * tq], ks_ref[hh, lo:hi, :], NT, preferred_element_type=F32)
            return s + tail_bias if is_tail else s

        for hh in range(NSA_HPS):
            for g in range(G):
                m = oa = None
                nxt = chunk_scores(hh, g, chunks[0])
                for i, c in enumerate(chunks):
                    s = nxt
                    if i + 1 < len(chunks):
                        nxt = chunk_scores(hh, g, chunks[i + 1])
                    mm = jnp.max(s, -1, keepdims=True)
                    m_new = mm if m is None else jnp.maximum(m, mm)
                    pv = jnp.dot(jnp.exp2(s - m_new).astype(BF16), vs_ref[hh, c[0]:c[1], :],
                                 preferred_element_type=F32)
                    oa = pv if oa is None else jnp.exp2(m - m_new) * oa + pv
                    m = m_new
                osel_s[hh, g * tq:(g + 1) * tq, :] = oa[:, 0:HD] * (1.0 / oa[:, HD:HD + 1])

    nvar = -(-S // SEL_KCHUNK)
    per_var = SEL_KCHUNK // tq
    for v in range(nvar):
        @pl.when(qi // per_var == v)
        def _(v=v):
            sel_branch(min(S, (v + 1) * SEL_KCHUNK))

    start = pl.multiple_of(jnp.maximum(t0 + tq - span, 0), tq)
    diff = posq - (start + lax.broadcasted_iota(jnp.int32, (1, span), 1))
    win_bias = jnp.where((diff >= 0) & (diff < WINDOW), 0.0, NEG)

    sg = jax.nn.sigmoid(gt_ref[...])
    g12s, kws, vws = [], [], []
    for hh in range(NSA_HPS):
        kvh = pair * NSA_HPS + hh
        pick = jnp.where(lax.broadcasted_iota(jnp.int32, (LANES, LANES), 0)
                         == lax.broadcasted_iota(jnp.int32, (LANES, LANES), 1) + kvh * (3 * G),
                         1.0, 0.0).astype(F32)
        g12s.append(jnp.dot(sg, pick, precision=HI, preferred_element_type=F32))
        kws.append(kw_ref[hh, pl.ds(start, span), :])
        vws.append(vw_ref[hh, pl.ds(start, span), :])

    def win_scores(hh, g):
        return lax.dot_general(qss[hh][g * tq:(g + 1) * tq], kws[hh], NT, preferred_element_type=F32) + win_bias

    items = [(hh, g) for hh in range(NSA_HPS) for g in range(G)]
    outs = []
    nxt = win_scores(*items[0])
    for i, (hh, g) in enumerate(items):
        cur = nxt
        if i + 1 < len(items):
            nxt = win_scores(*items[i + 1])
        rows = slice(g * tq, (g + 1) * tq)
        o_win = _exp2_softmax_pv([(cur, vws[hh])], HD)
        g12 = g12s[hh]
        outs.append(g12[:, 3 * g:3 * g + 1] * o_cmps[hh][rows]
                    + g12[:, 3 * g + 1:3 * g + 2] * osel_s[hh, rows, :]
                    + g12[:, 3 * g + 2:3 * g + 3] * o_win)
    y_ref[...] = jnp.concatenate(outs, axis=-1)


def _nsa_prompt(z, c_q, c_kv, gate_cb, B, S, tq=256):
    nq = S // tq
    KVH, G, HD = NSA_KV_HEADS, NSA_GROUP, HEAD_DIM
    NC = S // CMP_STRIDE
    qw = NSA_HPS * G * HD
    assert NSA_HPS * HD == LANES and c_q % qw == 0 and c_kv % LANES == 0

    def kv_spec(i):
        cb = (c_kv + i * KVH * HD) // LANES
        return pl.BlockSpec((S, LANES), lambda b, p, t, cb=cb: (b, cb + p))

    head_kv = pltpu.VMEM((NSA_HPS, S, HD), BF16)
    head_aug = pltpu.VMEM((NSA_HPS, S, LANES), BF16)
    assert HD + S // SEL_BLOCK <= LANES
    wn = min(WINDOW, S)
    return pl.pallas_call(
        functools.partial(_nsa_prompt_kernel, S=S, tq=tq),
        out_shape=[jax.ShapeDtypeStruct((B * S, KVH * G * HD), F32),
                   jax.ShapeDtypeStruct((B, 4, KVH, HD, S), F32),
                   jax.ShapeDtypeStruct((B, 2, KVH, HD, wn), F32)],
        grid=(B, KVH // NSA_HPS, nq),
        in_specs=[pl.BlockSpec((tq, qw), lambda b, p, t: (b * nq + t, c_q // qw + p)),
                  kv_spec(0), kv_spec(1), kv_spec(2), kv_spec(3), kv_spec(4), kv_spec(5),
                  pl.BlockSpec((tq, LANES), lambda b, p, t: (b * nq + t, gate_cb))],
        out_specs=[pl.BlockSpec((tq, qw), lambda b, p, t: (b * nq + t, p)),
                   pl.BlockSpec((1, 4, NSA_HPS, HD, S), lambda b, p, t: (b, 0, p, 0, 0)),
                   pl.BlockSpec((1, 2, NSA_HPS, HD, wn), lambda b, p, t: (b, 0, p, 0, 0))],
        scratch_shapes=[pltpu.VMEM((NSA_HPS, NC, HD), BF16), pltpu.VMEM((NSA_HPS, NC, HD), BF16),
                        pltpu.VMEM((NSA_HPS, S, QK_WIDTH), BF16), head_aug, head_kv, head_aug,
                        pltpu.VMEM((NSA_HPS, G * tq, HD), F32)],
        compiler_params=_cparams(("parallel", "parallel", "arbitrary")),
        name="nsa_prompt",
    )(z, z, z, z, z, z, z, z)


def _log_sigmoid(x):
    return jnp.minimum(x, 0.0) - jnp.log(1.0 + jnp.exp(-jnp.abs(x)))


def _mlstm_kernel(gb_ref, q_ref, k_ref, v_ref, o_ref, gt_ref, mg_ref, c0_ref, n0_ref, m0_ref,
                  y_ref, cn_ref, nn_ref, mn_ref, c_s, n_s, m_s, *, L, Lb, gi, gf):
    H, DH = MLSTM_HEADS, MLSTM_DH
    c = pl.program_id(1)

    @pl.when(c == 0)
    def _():
        c_s[...] = c0_ref[0]
        n_s[...] = n0_ref[0]
        m_s[...] = m0_ref[0]

    def padrows(a):
        if Lb == L:
            return a
        return jnp.concatenate([a, jnp.zeros((L - Lb, a.shape[1]), a.dtype)], axis=0)

    lane = lax.broadcasted_iota(jnp.int32, (1, LANES), 1)
    bias = jnp.zeros((1, LANES), F32)
    for h in range(H):
        bias = bias + jnp.where(lane == gi + h, gb_ref[0, h], 0.0) + jnp.where(lane == gf + h, gb_ref[1, h], 0.0)
    is_f = (lane >= gf) & (lane < gf + H)
    pre = padrows(gt_ref[...]) + bias
    gate = jnp.where(is_f, _log_sigmoid(pre), pre)
    if Lb != L:
        live = lax.broadcasted_iota(jnp.int32, (L, 1), 0) < Lb
        gate = jnp.where(live, gate, jnp.where(is_f, 0.0, NEG))
    rr = lax.broadcasted_iota(jnp.int32, (L, L), 0)
    cc = lax.broadcasted_iota(jnp.int32, (L, L), 1)
    tril = rr >= cc
    bcum = jnp.dot(jnp.where(tril, 1.0, 0.0).astype(F32), gate, precision=HI, preferred_element_type=F32)
    e8 = jnp.where(lax.broadcasted_iota(jnp.int32, (8, LANES), 1)
                   == lax.broadcasted_iota(jnp.int32, (8, LANES), 0) + gi, 1.0, 0.0).astype(F32)
    rg = lax.dot_general(e8, gate, NT, precision=HI, preferred_element_type=F32)
    rb = lax.dot_general(e8, bcum, NT, precision=HI, preferred_element_type=F32)

    q = padrows(q_ref[...])
    k = padrows(k_ref[...])
    v = padrows(v_ref[...])
    og = padrows(o_ref[...])
    hs = range(H)
    sls = [slice(h * DH, (h + 1) * DH) for h in hs]
    qq = [q[:, sl] for sl in sls]
    kk = [k[:, sl] * (DH ** -0.5) for sl in sls]
    qb = [x.astype(BF16) for x in qq]
    kb = [x.astype(BF16) for x in kk]
    vb = [v[:, sl].astype(BF16) for sl in sls]
    b_col = [bcum[:, gf + h:gf + h + 1] for h in hs]
    i_col = [gate[:, gi + h:gi + h + 1] for h in hs]
    m_prev = [m_s[h][:, 0:1] for h in hs]
    cmat = [c_s[h] for h in hs]
    n_row = [n_s[h] for h in hs]
    qk = [lax.dot_general(qb[h], kb[h], NT, preferred_element_type=F32) for h in hs]
    qc = [jnp.dot(qb[h], cmat[h].astype(BF16), preferred_element_type=F32) for h in hs]
    dmat = [jnp.where(tril, b_col[h] - rb[H + h:H + h + 1, :] + rg[h:h + 1, :], NEG) for h in hs]
    inter = [b_col[h] + m_prev[h] for h in hs]
    mt = [jnp.maximum(inter[h], jnp.max(dmat[h], -1, keepdims=True)) for h in hs]
    wqk = [jnp.exp(dmat[h] - mt[h]) * qk[h] for h in hs]
    a = [jnp.exp(inter[h] - mt[h]) for h in hs]
    wv = [jnp.dot(wqk[h].astype(BF16), vb[h], preferred_element_type=F32) for h in hs]
    m_new = [mt[h][L - 1:L] for h in hs]
    wl = [jnp.exp(b_col[h][L - 1:L] - b_col[h] + i_col[h] - m_new[h]) for h in hs]
    decay = [jnp.exp(b_col[h][L - 1:L] + m_prev[h] - m_new[h]) for h in hs]
    kw = [wl[h] * kk[h] for h in hs]
    kv = [lax.dot_general(kw[h].astype(BF16), vb[h], TN, preferred_element_type=F32) for h in hs]
    ys = []
    for h in hs:
        num = a[h] * qc[h] + wv[h]
        den = a[h] * jnp.sum(qq[h] * n_row[h], -1, keepdims=True) + jnp.sum(wqk[h], -1, keepdims=True)
        hh = num * (1.0 / jnp.maximum(jnp.abs(den), jnp.exp(-mt[h])))
        c_s[h] = decay[h] * cmat[h] + kv[h]
        n_s[h] = decay[h] * n_row[h] + jnp.sum(kw[h], 0, keepdims=True)
        m_s[h] = jnp.broadcast_to(m_new[h], (1, LANES))
        hn = hh * lax.rsqrt(jnp.mean(hh * hh, -1, keepdims=True) + RMS_EPS)
        ys.append(jax.nn.sigmoid(og[:, sls[h]]) * hn * mg_ref[:, sls[h]])
    y = jnp.concatenate(ys, axis=-1)
    y_ref[...] = y[0:Lb]

    @pl.when(c == pl.num_programs(1) - 1)
    def _():
        cn_ref[0] = c_s[...]
        nn_ref[0] = n_s[...]
        mn_ref[0] = m_s[...]


def _mlstm(z, row0, B, S, col_q, gate_cb, gi, gf, gate_b, mnorm_g, C0, n0, m0):
    H, DH = MLSTM_HEADS, MLSTM_DH
    W = H * DH
    L = MLSTM_CHUNK
    Lb = L if S % L == 0 else S
    assert Lb <= L
    nc = S // Lb
    rb0 = row0 // Lb
    n0 = n0.reshape(B, H, 1, DH)
    m0 = jnp.broadcast_to(m0.reshape(B, H, 1, 1), (B, H, 1, LANES))

    def zspec(cb, width):
        return pl.BlockSpec((Lb, width), lambda b, c, cb=cb: (rb0 + b * nc + c, cb))

    y, C, n, m = pl.pallas_call(
        functools.partial(_mlstm_kernel, L=L, Lb=Lb, gi=gi, gf=gf),
        out_shape=[jax.ShapeDtypeStruct((B * S, W), F32),
                   jax.ShapeDtypeStruct((B, H, DH, DH), F32),
                   jax.ShapeDtypeStruct((B, H, 1, DH), F32),
                   jax.ShapeDtypeStruct((B, H, 1, LANES), F32)],
        grid=(B, nc),
        in_specs=[pl.BlockSpec(memory_space=pltpu.SMEM),
                  zspec(col_q, W), zspec(col_q + 1, W), zspec(col_q + 2, W), zspec(col_q + 3, W),
                  zspec(gate_cb, LANES),
                  pl.BlockSpec((1, W), lambda b, c: (0, 0)),
                  pl.BlockSpec((1, H, DH, DH), lambda b, c: (b, 0, 0, 0)),
                  pl.BlockSpec((1, H, 1, DH), lambda b, c: (b, 0, 0, 0)),
                  pl.BlockSpec((1, H, 1, LANES), lambda b, c: (b, 0, 0, 0))],
        out_specs=[pl.BlockSpec((Lb, W), lambda b, c: (b * nc + c, 0)),
                   pl.BlockSpec((1, H, DH, DH), lambda b, c: (b, 0, 0, 0)),
                   pl.BlockSpec((1, H, 1, DH), lambda b, c: (b, 0, 0, 0)),
                   pl.BlockSpec((1, H, 1, LANES), lambda b, c: (b, 0, 0, 0))],
        scratch_shapes=[pltpu.VMEM((H, DH, DH), F32), pltpu.VMEM((H, 1, DH), F32),
                        pltpu.VMEM((H, 1, LANES), F32)],
        compiler_params=_cparams(("parallel", "arbitrary")),
        name="mlstm",
    )(gate_b, z, z, z, z, z, mnorm_g.reshape(1, W), C0, n0, m0)
    return y, C, n.reshape(B, H, DH), m[:, :, 0, 0]


PAGES_PER_STEP = 16


def _lane_extract_topn(score, n_sel, floor):
    lane = lax.broadcasted_iota(jnp.int32, score.shape, 1).astype(F32)
    sel = jnp.zeros(score.shape, F32)
    sc = score
    for _ in range(n_sel):
        m = jnp.max(sc, -1, keepdims=True)
        idx = jnp.min(jnp.where(sc == m, lane, float(score.shape[1])), -1, keepdims=True)
        hit = lane == idx
        sel = jnp.where(hit & (m > floor), 1.0, sel)
        sc = jnp.where(hit, -jnp.inf, sc)
    return sel


def _s1_kernel(pt_ref, *refs, past, S, ncp, nbp):
    pgs = refs[:PAGES_PER_STEP]
    tail_ref, q_ref, ocmp_ref, sel_ref, sub_ref = refs[PAGES_PER_STEP:]
    P = pgs[0].shape[-1]
    j = pl.program_id(1)
    nfull = pl.num_programs(1) - 1
    KVH, G, HD = NSA_KV_HEADS, NSA_GROUP, HEAD_DIM
    cols_step = PAGES_PER_STEP * P // CMP_STRIDE
    assert cols_step == LANES

    @pl.when(j == 0)
    def _():
        sub_ref[...] = jnp.zeros(sub_ref.shape, F32)

    def pool_t(n):
        return jnp.where(_div(lax.broadcasted_iota(jnp.int32, (n, LANES), 0), CMP_STRIDE)
                         == lax.broadcasted_iota(jnp.int32, (n, LANES), 1),
                         1.0 / CMP_STRIDE, 0.0).astype(BF16)

    @pl.when(j < nfull)
    def _():
        pt = pool_t(PAGES_PER_STEP * P)
        col = pl.ds(pl.multiple_of(j * cols_step, cols_step), cols_step)
        xt = jnp.concatenate([r[0, 0].reshape(2 * KVH * HD, P) for r in pgs], axis=1)
        sub_ref[:, :, :, col] = _pool_dot(pt, xt, True).reshape(2, KVH, HD, cols_step)

    @pl.when(j == nfull)
    def _():
        base = past // CMP_STRIDE
        pt = pool_t(P)
        for k in range(2):
            for h in range(KVH):
                sub_ref[k, h, :, base:base + LANES] = _pool_dot(pt, tail_ref[0, k, h], True)
        R = G * S
        row = lax.broadcasted_iota(jnp.int32, (R, 1), 0)
        pos = past + (row & (S - 1))
        posq = past + lax.broadcasted_iota(jnp.int32, (S, 1), 0)
        c_end = lax.broadcasted_iota(jnp.int32, (1, ncp), 1) * CMP_STRIDE + (2 * CMP_STRIDE - 1)
        cmask = c_end <= pos
        per = SEL_BLOCK // CMP_STRIDE
        e4 = jnp.where(_div(lax.broadcasted_iota(jnp.int32, (ncp, nbp), 0), per)
                       == lax.broadcasted_iota(jnp.int32, (ncp, nbp), 1), 1.0, 0.0).astype(F32)
        blk = lax.broadcasted_iota(jnp.int32, (1, nbp), 1)
        qblk = _div(posq, SEL_BLOCK)
        valid = blk * SEL_BLOCK <= posq
        forced = (blk == 0) | (blk == qblk) | (blk == qblk - 1)
        hs = range(KVH)
        kct = [(0.5 * (sub_ref[0, h, :, 0:ncp] + sub_ref[0, h, :, 1:ncp + 1])).astype(BF16) for h in hs]
        vct = [(0.5 * (sub_ref[1, h, :, 0:ncp] + sub_ref[1, h, :, 1:ncp + 1])).astype(BF16) for h in hs]
        scale = jnp.asarray(HD ** -0.5, BF16)
        s = [jnp.dot(q_ref[0, h] * scale, kct[h], preferred_element_type=F32) for h in hs]
        p = [_masked_softmax(s[h], cmask) for h in hs]
        oc = [lax.dot_general(p[h].astype(BF16), vct[h], NT, preferred_element_type=F32) for h in hs]
        imp = []
        for h in hs:
            t = p[h][0:S]
            for g in range(1, G):
                t = t + p[h][g * S:(g + 1) * S]
            imp.append(t)
        impb = jnp.dot(jnp.concatenate(imp, axis=0), e4, precision=HI, preferred_element_type=F32)
        scores = []
        for h in hs:
            ocmp_ref[0, h] = oc[h]
            scores.append(jnp.where(forced, MASK_BIG, jnp.where(valid, impb[h * S:(h + 1) * S], -MASK_BIG)))
        sel = _lane_extract_topn(jnp.concatenate(scores, axis=0), SEL_TOPN, -0.5 * MASK_BIG)
        for h in range(KVH):
            sel_ref[0, h] = sel[h * S:(h + 1) * S]


def _s2_kernel(pt_ref, *refs, past, S, nbp):
    pgs = refs[:PAGES_PER_STEP]
    tail_ref, q_ref, sel_ref, osel_ref, m_s, l_s, acc_s = refs[PAGES_PER_STEP:]
    P = pgs[0].shape[-1]
    j = pl.program_id(1)
    nfull = pl.num_programs(1) - 1
    KVH, G, HD = NSA_KV_HEADS, NSA_GROUP, HEAD_DIM
    R = G * S

    @pl.when(j == 0)
    def _():
        m_s[...] = jnp.full(m_s.shape, NEG, F32)
        l_s[...] = jnp.zeros(l_s.shape, F32)
        acc_s[...] = jnp.zeros(acc_s.shape, F32)

    row = lax.broadcasted_iota(jnp.int32, (R, 1), 0)
    pos = past + (row & (S - 1))

    def process(kt_of, vt_of, n, kpos0):
        kpos = kpos0 + lax.broadcasted_iota(jnp.int32, (1, n), 1)
        nblk = LANES
        assert n // SEL_BLOCK <= nblk
        blk0 = kpos0 // SEL_BLOCK
        pick = jnp.where(lax.broadcasted_iota(jnp.int32, (nbp, nblk), 0)
                         == blk0 + lax.broadcasted_iota(jnp.int32, (nbp, nblk), 1), 1.0, 0.0).astype(BF16)
        spread = jnp.where(_div(lax.broadcasted_iota(jnp.int32, (nblk, n), 1), SEL_BLOCK)
                           == lax.broadcasted_iota(jnp.int32, (nblk, n), 0), 1.0, 0.0).astype(BF16)
        causal = kpos <= pos
        hs = range(KVH)
        scale = jnp.asarray(HD ** -0.5, BF16)
        s = [jnp.dot(q_ref[0, h] * scale, kt_of(h).astype(BF16), preferred_element_type=F32) for h in hs]
        mkb = [jnp.dot(sel_ref[0, h].astype(BF16), pick, preferred_element_type=F32) for h in hs]
        mk = [jnp.dot(mkb[h].astype(BF16), spread, preferred_element_type=F32) for h in hs]
        mask = [(jnp.concatenate([mk[h]] * G, axis=0) > 0.5) & causal for h in hs]
        sm = [jnp.where(mask[h], s[h], NEG) for h in hs]
        m_old = [m_s[h] for h in hs]
        m_new = [jnp.maximum(m_old[h], jnp.max(sm[h], -1, keepdims=True)) for h in hs]
        e = [jnp.where(mask[h], jnp.exp(sm[h] - m_new[h]), 0.0) for h in hs]
        pv = [lax.dot_general(e[h].astype(BF16), vt_of(h).astype(BF16), NT, preferred_element_type=F32) for h in hs]
        for h in hs:
            alpha = jnp.exp(m_old[h] - m_new[h])
            l_s[h] = alpha * l_s[h] + jnp.sum(e[h], -1, keepdims=True)
            acc_s[h] = alpha * acc_s[h] + pv[h]
            m_s[h] = m_new[h]

    @pl.when(j < nfull)
    def _():
        process(lambda h: jnp.concatenate([r[0, 0, 0, h] for r in pgs], axis=1),
                lambda h: jnp.concatenate([r[0, 0, 1, h] for r in pgs], axis=1),
                PAGES_PER_STEP * P, j * (PAGES_PER_STEP * P))

    @pl.when(j == nfull)
    def _():
        process(lambda h: tail_ref[0, 0, h], lambda h: tail_ref[0, 1, h], P, past)
        for h in range(KVH):
            osel_ref[0, h] = acc_s[h] * (1.0 / jnp.maximum(l_s[h], 1e-30))


def _page_specs(layer, half, n_pages, P):
    def spec(r):
        return pl.BlockSpec(
            (1, 1, 2, NSA_KV_HEADS, HEAD_DIM, P),
            lambda b, j, pt, r=r: (layer, pt[b, jnp.minimum(j * PAGES_PER_STEP + r, n_pages - 1)],
                                   half, 0, 0, 0))
    return [spec(r) for r in range(PAGES_PER_STEP)]


def _nsa_sample_global(cache_t, layer, page_table, tail_t, q_s, S):
    B, n_pages = page_table.shape
    P = cache_t.shape[-1]
    KVH, G, HD = NSA_KV_HEADS, NSA_GROUP, HEAD_DIM
    past = n_pages * P
    assert n_pages % PAGES_PER_STEP == 0 and S & (S - 1) == 0 and S <= SEL_BLOCK
    nsteps = n_pages // PAGES_PER_STEP + 1
    ncp = -(-(past + P) // CMP_STRIDE // LANES) * LANES
    nbp = -(-(past + P) // SEL_BLOCK // LANES) * LANES
    R = G * S
    qspec = pl.BlockSpec((1, KVH, R, HD), lambda b, j, pt: (b, 0, 0, 0))
    o_cmp, sel = pl.pallas_call(
        functools.partial(_s1_kernel, past=past, S=S, ncp=ncp, nbp=nbp),
        out_shape=[jax.ShapeDtypeStruct((B, KVH, R, HD), F32), jax.ShapeDtypeStruct((B, KVH, S, nbp), F32)],
        grid_spec=pltpu.PrefetchScalarGridSpec(
            num_scalar_prefetch=1, grid=(B, nsteps),
            in_specs=_page_specs(layer, 0, n_pages, P)
            + [pl.BlockSpec((1, 2, KVH, HD, P), lambda b, j, pt: (b, 0, 0, 0, 0)), qspec],
            out_specs=[pl.BlockSpec((1, KVH, R, HD), lambda b, j, pt: (b, 0, 0, 0)),
                       pl.BlockSpec((1, KVH, S, nbp), lambda b, j, pt: (b, 0, 0, 0))],
            scratch_shapes=[pltpu.VMEM((2, KVH, HD, ncp + LANES), F32)]),
        compiler_params=_cparams(("parallel", "arbitrary")),
        name="nsa_sample_cmp",
    )(page_table, *([cache_t] * PAGES_PER_STEP), tail_t, q_s)
    o_sel = pl.pallas_call(
        functools.partial(_s2_kernel, past=past, S=S, nbp=nbp),
        out_shape=jax.ShapeDtypeStruct((B, KVH, R, HD), F32),
        grid_spec=pltpu.PrefetchScalarGridSpec(
            num_scalar_prefetch=1, grid=(B, nsteps),
            in_specs=_page_specs(layer, 1, n_pages, P)
            + [pl.BlockSpec((1, 2, KVH, HD, P), lambda b, j, pt: (b, 1, 0, 0, 0)), qspec,
               pl.BlockSpec((1, KVH, S, nbp), lambda b, j, pt: (b, 0, 0, 0))],
            out_specs=pl.BlockSpec((1, KVH, R, HD), lambda b, j, pt: (b, 0, 0, 0)),
            scratch_shapes=[pltpu.VMEM((KVH, R, 1), F32), pltpu.VMEM((KVH, R, 1), F32),
                            pltpu.VMEM((KVH, R, HD), F32)]),
        compiler_params=_cparams(("parallel", "arbitrary")),
        name="nsa_sample_sel",
    )(page_table, *([cache_t] * PAGES_PER_STEP), tail_t, q_s, sel)
    return o_cmp, o_sel


def _s3_kernel(q_ref, kw_ref, vw_ref, ocmp_ref, osel_ref, gt_ref, y_ref, *, past, S):
    KVH, G, HD = NSA_KV_HEADS, NSA_GROUP, HEAD_DIM
    R = G * S
    nk = kw_ref.shape[1]
    row = lax.broadcasted_iota(jnp.int32, (R, 1), 0)
    pos = past + (row & (S - 1))
    kpos = past - (nk - S) + lax.broadcasted_iota(jnp.int32, (1, nk), 1)
    diff = pos - kpos
    wmask = (diff >= 0) & (diff < WINDOW)
    sg = jax.nn.sigmoid(gt_ref[...])
    kw = kw_ref[0].astype(BF16)
    vw = vw_ref[0].astype(BF16)
    outs = []
    for h in range(KVH):
        qs = q_ref[0, h] * jnp.asarray(HD ** -0.5, BF16)
        s = lax.dot_general(qs, kw[:, h * HD:(h + 1) * HD], NT, preferred_element_type=F32)
        p = _masked_softmax(s, wmask)
        o_win = jnp.dot(p.astype(BF16), vw[:, h * HD:(h + 1) * HD], preferred_element_type=F32)
        oc = ocmp_ref[0, h]
        osl = osel_ref[0, h]
        for g in range(G):
            c0 = (h * G + g) * 3
            rs = slice(g * S, (g + 1) * S)
            outs.append(sg[:, c0:c0 + 1] * oc[rs] + sg[:, c0 + 1:c0 + 2] * osl[rs]
                        + sg[:, c0 + 2:c0 + 3] * o_win[rs])
    y_ref[...] = jnp.concatenate(outs, axis=-1)


def _nsa_sample_combine(q_s, kwin, vwin, o_cmp, o_sel, z, row0, gate_cb, past, S):
    B, KVH, R, HD = q_s.shape
    nk = kwin.shape[1]
    W = KVH * HD
    ospec = pl.BlockSpec((1, KVH, R, HD), lambda b: (b, 0, 0, 0))
    return pl.pallas_call(
        functools.partial(_s3_kernel, past=past, S=S),
        out_shape=jax.ShapeDtypeStruct((B * S, KVH * NSA_GROUP * HD), F32),
        grid=(B,),
        in_specs=[ospec,
                  pl.BlockSpec((1, nk, W), lambda b: (b, 0, 0)),
                  pl.BlockSpec((1, nk, W), lambda b: (b, 0, 0)),
                  ospec, ospec,
                  pl.BlockSpec((S, LANES), lambda b: (row0 // S + b, gate_cb))],
        out_specs=pl.BlockSpec((S, KVH * NSA_GROUP * HD), lambda b: (b, 0)),
        compiler_params=_cparams(("parallel",)),
        name="nsa_sample_win",
    )(q_s, kwin, vwin, o_cmp, o_sel, z)


def _out_proj_kernel(x_ref, ya_ref, yb_ref, yc_ref, yt_ref, w_ref, o_ref, *, n_head_tiles):
    i = pl.program_id(0)
    ka, kb = ya_ref.shape[1], yb_ref.shape[1]

    @pl.when(i < n_head_tiles)
    def _():
        acc = jnp.dot(ya_ref[...].astype(BF16), w_ref[0:ka, :], preferred_element_type=F32)
        acc = acc + jnp.dot(yb_ref[...].astype(BF16), w_ref[ka:ka + kb, :], preferred_element_type=F32)
        acc = acc + jnp.dot(yc_ref[...].astype(BF16), w_ref[ka + kb:, :], preferred_element_type=F32)
        o_ref[...] = x_ref[...] + acc

    @pl.when(i >= n_head_tiles)
    def _():
        o_ref[...] = x_ref[...] + jnp.dot(yt_ref[...].astype(BF16), w_ref[...], preferred_element_type=F32)


def _out_proj(x, ya, yb, yc, ytail, w, tm):
    T, D = x.shape
    TH = ya.shape[0]
    assert TH % tm == 0 and (T - TH) % tm == 0 and ytail.shape == (T - TH, D)
    nh = TH // tm

    def head(a):
        return pl.BlockSpec((tm, a.shape[1]), lambda i: (jnp.minimum(i, nh - 1), 0))

    return pl.pallas_call(
        functools.partial(_out_proj_kernel, n_head_tiles=nh),
        out_shape=jax.ShapeDtypeStruct((T, D), F32),
        grid=(T // tm,),
        in_specs=[pl.BlockSpec((tm, D), lambda i: (i, 0)),
                  head(ya), head(yb), head(yc),
                  pl.BlockSpec((tm, D), lambda i: (jnp.maximum(i - nh, 0), 0)),
                  pl.BlockSpec((D, D), lambda i: (0, 0))],
        out_specs=pl.BlockSpec((tm, D), lambda i: (i, 0)),
        compiler_params=_cparams(("parallel",)),
        name="out_proj",
    )(x, ya, yb, yc, ytail, w)


def _structural_pairs(k):
    return [(i, j) for i in range(k) for j in range(k) if (i + 1) * (j + 1) <= k]


RANK_MARK = 2.0 ** 126


def _extract_topk(cur_ref, top_ref, c, exact):
    NK, K = PEER_NKEYS, PEER_TOPK
    for k in range(K):
        cur = cur_ref[c]
        m = jnp.max(cur, axis=0)
        if exact:
            key = lax.broadcasted_iota(jnp.int32, cur.shape, 0).astype(F32)
            idx = jnp.min(jnp.where(cur == m[None], key, float(NK)), axis=0)
            hit = key == idx[None]
        else:
            hit = cur == m[None]
        cur_ref[c] = jnp.where(hit, -RANK_MARK * (1.0 + k / 32.0), cur)
        top_ref[c, k] = m


def _router_kernel(q_ref, wb_ref, n_out, e1_out, r2_out, e2_out, s_ref, cur_ref, top_ref, tmp_ref):
    H, NK, K = PEER_HEADS, PEER_NKEYS, PEER_TOPK
    Tt = q_ref.shape[0]
    half = q_ref.shape[1] // 2
    for c in range(2):
        s = lax.dot_general(wb_ref[c], q_ref[:, c * half:(c + 1) * half], NT, preferred_element_type=F32)
        s_ref[c] = s.reshape(NK, H, Tt)
        cur_ref[c] = s_ref[c]
    for c in range(2):
        _extract_topk(cur_ref, top_ref, c, exact=False)

    marked = jnp.sum(jnp.where(cur_ref[...] < -0.5 * RANK_MARK, 1.0, 0.0), axis=1)
    tied = jnp.max(jnp.where(marked != float(K), 1.0, 0.0)) > 0.0

    @pl.when(tied)
    def _():
        for c in range(2):
            cur_ref[c] = s_ref[c]
            _extract_topk(cur_ref, top_ref, c, exact=True)

    v1 = [top_ref[0, k] for k in range(K)]
    v2 = [top_ref[1, k] for k in range(K)]

    pairs = _structural_pairs(K)
    cand = [v1[i] + v2[j] for (i, j) in pairs]
    n = len(pairs)
    rank = []
    for p in range(n):
        rank.append(jnp.zeros((H, Tt), F32))
    for p in range(n):
        ip, jp = pairs[p]
        for q in range(p + 1, n):
            iq, jq = pairs[q]
            if ip <= iq and jp <= jq:
                rank[q] = rank[q] + 1.0
            else:
                b = jnp.where(cand[p] >= cand[q], 1.0, 0.0)
                rank[q] = rank[q] + b
                rank[p] = rank[p] + (1.0 - b)
    sel = [jnp.where(r < K, 1.0, 0.0) for r in rank]
    e1 = [jnp.exp(v1[i] - v1[0]) for i in range(K)]
    e2 = [jnp.exp(v2[j] - v2[0]) for j in range(K)]
    cnt = [jnp.zeros((H, Tt), F32) for _ in range(K)]
    zsum = jnp.zeros((H, Tt), F32)
    for p, (i, j) in enumerate(pairs):
        cnt[i] = cnt[i] + sel[p]
        zsum = zsum + sel[p] * (e1[i] * e2[j])
    inv_z = 1.0 / zsum

    def rank_of(c):
        cur = cur_ref[c]
        return jnp.where(cur < -0.5 * RANK_MARK, (cur * (-1.0 / RANK_MARK) - 1.0) * 32.0, float(NK))

    rk1 = rank_of(0)
    nk = jnp.zeros((NK, H, Tt), F32)
    for i in range(K):
        nk = jnp.where(rk1 == float(i), cnt[i][None], nk)
    outs = ((n_out, nk),
            (e1_out, jnp.exp(s_ref[0] - v1[0][None]) * inv_z[None]),
            (r2_out, rank_of(1)),
            (e2_out, jnp.exp(s_ref[1] - v2[0][None])))
    for ref, val in outs:
        tmp_ref[...] = val.reshape(NK * H, Tt)
        for h in range(H):
            ref[h] = tmp_ref[pl.ds(h, NK, stride=H), :].astype(ref.dtype)


def _router(q, wb, tt):
    T = q.shape[0]
    H, NK = PEER_HEADS, PEER_NKEYS
    shp = jax.ShapeDtypeStruct((H, NK, T), F32)
    shp_b = jax.ShapeDtypeStruct((H, NK, T), BF16)
    ospec = pl.BlockSpec((H, NK, tt), lambda i: (0, 0, i))
    return pl.pallas_call(
        _router_kernel,
        out_shape=[shp, shp, shp_b, shp_b],
        grid=(T // tt,),
        in_specs=[pl.BlockSpec((tt, q.shape[1]), lambda i: (i, 0)),
                  pl.BlockSpec(wb.shape, lambda i: (0, 0, 0))],
        out_specs=[ospec, ospec, ospec, ospec],
        scratch_shapes=[pltpu.VMEM((2, NK, H, tt), F32), pltpu.VMEM((2, NK, H, tt), F32),
                        pltpu.VMEM((2, PEER_TOPK, H, tt), F32), pltpu.VMEM((NK * H, tt), F32)],
        compiler_params=_cparams(("parallel",)),
        name="peer_router",
    )(q, wb)


def _gelu_tanh(x):
    return 0.5 * x * (1.0 + jnp.tanh(np.sqrt(2.0 / np.pi) * (x + 0.044715 * (x * x * x))))


PEER_ACHUNK = 4


def _peer_dense_kernel(xn_ref, u_ref, v_ref, n_ref, e1_ref, r2_ref, e2_ref, x1_ref, g_ref, o_ref, *, final_norm):
    H, NK = PEER_HEADS, PEER_NKEYS
    j = pl.program_id(1)
    na = u_ref.shape[0] // NK
    tt = xn_ref.shape[0]

    @pl.when(j == 0)
    def _():
        o_ref[...] = x1_ref[...]

    xn = xn_ref[...]
    parts = []
    for c in range(na // PEER_ACHUNK):
        rows = slice(c * PEER_ACHUNK * NK, (c + 1) * PEER_ACHUNK * NK)
        act = lax.dot_general(u_ref[rows, :], xn, NT, preferred_element_type=F32)
        gates = []
        for a in range(c * PEER_ACHUNK, (c + 1) * PEER_ACHUNK):
            row = pl.ds(j * na + a, 1)
            gate = None
            for h in range(H):
                t = jnp.where(r2_ref[h] < n_ref[h, row, :].astype(BF16),
                              e1_ref[h, row, :].astype(BF16) * e2_ref[h], jnp.zeros((), BF16))
                gate = t if gate is None else gate + t
            gates.append(gate)
        parts.append(_gelu_tanh(act).astype(BF16) * jnp.concatenate(gates, axis=0))
    wt = jnp.concatenate(parts, axis=0)
    o_ref[...] += lax.dot_general(wt, v_ref[...], TN, preferred_element_type=F32)

    if final_norm:
        @pl.when(j == pl.num_programs(1) - 1)
        def _():
            x = o_ref[...]
            o_ref[...] = x * lax.rsqrt(jnp.mean(x * x, -1, keepdims=True) + RMS_EPS) * g_ref[...]


def _peer_dense(xn, u, v, layer, nk, e1, r2, e2, x1, g, tt, te, final_norm):
    T, D = xn.shape
    E = u.shape[1]
    H, NK = PEER_HEADS, PEER_NKEYS
    once = pl.Buffered(1)
    tab = pl.BlockSpec((H, NK, tt), lambda i, j: (0, 0, i), pipeline_mode=once)
    return pl.pallas_call(
        functools.partial(_peer_dense_kernel, final_norm=final_norm),
        out_shape=jax.ShapeDtypeStruct((T, D), F32),
        grid=(T // tt, E // te),
        in_specs=[pl.BlockSpec((tt, D), lambda i, j: (i, 0), pipeline_mode=once),
                  pl.BlockSpec((None, te, D), lambda i, j: (layer, j, 0)),
                  pl.BlockSpec((None, te, D), lambda i, j: (layer, j, 0)),
                  tab, tab, tab, tab,
                  pl.BlockSpec((tt, D), lambda i, j: (i, 0), pipeline_mode=once),
                  pl.BlockSpec((1, D), lambda i, j: (0, 0))],
        out_specs=pl.BlockSpec((tt, D), lambda i, j: (i, 0)),
        compiler_params=_cparams(("parallel", "arbitrary")),
        name="peer_dense",
    )(xn, u, v, nk, e1, r2, e2, x1, g.reshape(1, D))


ROUTER_TILE = 128
PEER_T_TILE = 768
PAD_TILE = 768
Z_COL_TILE = 1280
OUT_TILE = 256
EXPERT_TILE = 1024


def kernel(x_prompt, x_sample, cache_nsa_kv, state_win_kv, state_conv, state_mlstm_C, state_mlstm_n,
           state_mlstm_m, page_table, norm1_g, w_in, conv_w, mlstm_gate_b, mlstm_norm_g, w_out, norm2_g,
           peer_wq, peer_subkeys, peer_u, peer_v, final_norm_g):
    BP, SP, D = x_prompt.shape
    BS, SS, _ = x_sample.shape
    depth = w_in.shape[0]
    KVH, G, HD = NSA_KV_HEADS, NSA_GROUP, HEAD_DIM
    H, DH = MLSTM_HEADS, MLSTM_DH
    CW = conv_w.shape[2]
    NW = KVH * G * HD
    KVW = 6 * KVH * HD
    NG = 3 * KVH * G
    MW = H * DH
    TP, TS = BP * SP, BS * SS
    T = TP + TS
    assert PAD_TILE % OUT_TILE == 0 and PAD_TILE % ROUTER_TILE == 0 and PAD_TILE % PEER_T_TILE == 0
    TPAD = -(-T // PAD_TILE) * PAD_TILE
    n_pages = page_table.shape[1]
    P = cache_nsa_kv.shape[2]
    past = n_pages * P
    wb_rows = state_win_kv.shape[2]

    c_q = 3 * CW
    c_kv = c_q + NW
    c_m = c_kv + KVW
    c_g = c_m + 4 * MW
    ZW = -(-(c_g + LANES) // Z_COL_TILE) * Z_COL_TILE
    assert CW % LANES == 0 and c_m % MW == 0 and c_g % LANES == 0 and NG + 2 * H <= LANES
    gate_cb = c_g // LANES
    gi, gf = NG, NG + H
    o_gate = 3 * CW + NW + KVW

    x = jnp.concatenate([x_prompt.reshape(TP, D), x_sample.reshape(TS, D),
                         jnp.zeros((TPAD - T, D), F32)], axis=0)
    cache_t = jnp.transpose(cache_nsa_kv, (0, 1, 3, 4, 5, 2))
    eye_h = jnp.eye(PEER_HEADS, dtype=F32)
    u_bf = peer_u.astype(BF16)
    v_bf = peer_v.astype(BF16)
    dk = peer_subkeys.shape[-1]

    p_st, s_st = [], []
    for l in range(depth):
        wi = w_in[l]
        w_perm = jnp.concatenate(
            [wi[:, :o_gate], wi[:, o_gate + NG:o_gate + NG + 4 * MW], wi[:, o_gate:o_gate + NG],
             wi[:, o_gate + NG + 4 * MW:], jnp.zeros((D, ZW - wi.shape[1]), F32)], axis=1).astype(BF16)
        z = _rms_proj(x, norm1_g[l], w_perm, PAD_TILE, Z_COL_TILE, F32)

        ya_p, conv_p = _conv(z, 0, BP, SP, jnp.zeros((BP, 2, CW), F32), conv_w[l])
        ya_s, conv_s = _conv(z, TP, BS, SS, state_conv[l], conv_w[l])

        yb_p, rows_t, win_t = _nsa_prompt(z, c_q, c_kv, gate_cb, BP, SP)
        rows_p = rows_t.transpose(0, 4, 1, 2, 3)
        win_p = win_t.transpose(0, 4, 1, 2, 3)

        zs = z[TP:T]
        q_s = zs[:, c_q:c_q + NW].reshape(BS, SS, KVH, G, HD).transpose(0, 2, 3, 1, 4)
        q_s = q_s.reshape(BS, KVH, G * SS, HD).astype(BF16)
        zkv_s = zs[:, c_kv:c_kv + KVW].reshape(BS, SS, 3, 2, KVH, HD)
        rows_s = zkv_s[:, :, :2].reshape(BS, SS, 4, KVH, HD)
        tail_t = jnp.pad(rows_s.transpose(0, 2, 3, 4, 1), ((0, 0), (0, 0), (0, 0), (0, 0), (0, P - SS)))
        o_cmp, o_sel = _nsa_sample_global(cache_t, l, page_table, tail_t, q_s, SS)
        w_all = jnp.concatenate([state_win_kv[l], zkv_s[:, :, 2]], axis=1)
        kwin = w_all[:, :, 0].reshape(BS, wb_rows + SS, KVH * HD)
        vwin = w_all[:, :, 1].reshape(BS, wb_rows + SS, KVH * HD)
        yb_s = _nsa_sample_combine(q_s, kwin, vwin, o_cmp, o_sel, z, TP, gate_cb, past, SS)
        win_s = w_all[:, SS:]

        yc_p, C_p, n_p, m_p = _mlstm(z, 0, BP, SP, c_m // MW, gate_cb, gi, gf, mlstm_gate_b[l],
                                     mlstm_norm_g[l], jnp.zeros((BP, H, DH, DH), F32),
                                     jnp.zeros((BP, H, DH), F32), jnp.zeros((BP, H), F32))
        yc_s, C_s, n_s, m_s = _mlstm(z, TP, BS, SS, c_m // MW, gate_cb, gi, gf, mlstm_gate_b[l],
                                     mlstm_norm_g[l], state_mlstm_C[l], state_mlstm_n[l], state_mlstm_m[l])

        ytail = jnp.concatenate([jnp.concatenate([ya_s, yb_s, yc_s], axis=1),
                                 jnp.zeros((TPAD - T, D), F32)], axis=0)
        x1 = _out_proj(x, ya_p, yb_p, yc_p, ytail, w_out[l].astype(BF16), OUT_TILE)

        nh = PEER_HEADS
        wq_perm = peer_wq[l].reshape(D, nh, 2, dk).transpose(0, 2, 1, 3).reshape(D, 2 * nh * dk).astype(BF16)
        q_peer, xn2 = _rms_proj(x1, norm2_g[l], wq_perm, PAD_TILE, nh * dk, BF16, emit_xn=True)
        wb = jnp.einsum('hcnd,hg->cnhgd', peer_subkeys[l], eye_h).reshape(2, PEER_NKEYS * nh, nh * dk).astype(BF16)
        nk, e1, r2, e2 = _router(q_peer, wb, ROUTER_TILE)
        x = _peer_dense(xn2, u_bf, v_bf, l, nk, e1, r2, e2, x1,
                        final_norm_g, PEER_T_TILE, EXPERT_TILE, final_norm=(l == depth - 1))

        p_st.append((rows_p, win_p, conv_p, C_p, n_p, m_p))
        s_st.append((rows_s, win_s, conv_s, C_s, n_s, m_s))

    p_rows, p_win, p_conv, p_C, p_n, p_m = [jnp.stack(a) for a in zip(*p_st)]
    s_rows, s_win, s_conv, s_C, s_n, s_m = [jnp.stack(a) for a in zip(*s_st)]
    y_prompt = x[:TP].reshape(BP, SP, D)
    y_sample = x[TP:T].reshape(BS, SS, D)
    return (y_prompt, y_sample, p_rows, p_win, p_conv, p_C, p_n, p_m, s_rows, s_win, s_conv, s_C, s_n, s_m)
```
